```python
import math
import jax, jax.numpy as jnp
from jax import lax
import numpy as np

D_MODEL = 1024
BATCH = 16
SEQ = 2048
DEPTH = 1

HG_HEADS = 4
HG_DK = 128
HG_DV = 128
HG_KEY_WIDTH = HG_HEADS * HG_DK
HG_WIDTH = HG_HEADS * HG_DV
HG_CHUNK = 64

MB_HEADS = 8
MB_DH = 64
MB_WIDTH = MB_HEADS * MB_DH
MB_BLOCK = 256
MB_TOPK = 3
MB_QCHUNK = 16

REL_BUCKETS = 32
REL_MAX_DIST = 128

N_GROUPS = 4
EXPERTS_PER_GROUP = 8
N_EXPERTS = N_GROUPS * EXPERTS_PER_GROUP
TOP_K = 2
EXPERT_HIDDEN = 512
MOE_BLOCK = 128

DN_ALPHA = (2.0 * DEPTH) ** 0.25
DN_BETA = (8.0 * DEPTH) ** -0.25
NORM_EPS = 1e-5

IN_SIZES = (HG_KEY_WIDTH, HG_KEY_WIDTH, HG_WIDTH, HG_WIDTH, MB_WIDTH, MB_WIDTH, MB_WIDTH, D_MODEL, D_MODEL)
IN_COLS = sum(IN_SIZES)

kernel_name = 'hybrid_hgrn2_moba_hmoe_block'


def layer_norm(x, g, b):
    xf = x.astype(jnp.float32)
    mu = jnp.mean(xf, axis=-1, keepdims=True)
    xc = xf - mu
    var = jnp.mean(xc * xc, axis=-1, keepdims=True)
    return (xc * lax.rsqrt(var + NORM_EPS) * g.astype(jnp.float32) + b.astype(jnp.float32)).astype(x.dtype)


def hgrn2_mixer(q, f_logit, i, g, lb, norm_g):
    B, S, _ = q.shape
    nc = S // HG_CHUNK
    f32 = jnp.float32
    lb = lb.astype(f32)
    z = f_logit.astype(f32)
    log_f = jnp.log(lb + (1.0 - lb) * jax.nn.sigmoid(z))
    k = (1.0 - lb) * jax.nn.sigmoid(-z)
    qf = jax.nn.silu(q.astype(f32))
    vf = i.astype(f32)

    def to_chunks(t, dh):
        return t.reshape(B, nc, HG_CHUNK, HG_HEADS, dh).transpose(1, 0, 3, 2, 4)

    causal = jnp.tril(jnp.ones((HG_CHUNK, HG_CHUNK), dtype=bool))

    def chunk_step(state, inp):
        qc, kc, vc, lfc = inp
        b = jnp.cumsum(lfc, axis=2)
        o_inter = jnp.einsum('bhtk,bhkv->bhtv', qc * jnp.exp(b), state)
        rel = b[:, :, :, None, :] - b[:, :, None, :, :]
        decay = jnp.exp(jnp.where(causal[:, :, None], rel, -jnp.inf))
        scores = jnp.einsum('bhtk,bhtsk,bhsk->bhts', qc, decay, kc)
        o_intra = jnp.einsum('bhts,bhsv->bhtv', scores, vc)
        b_end = b[:, :, -1:, :]
        state = (jnp.exp(b_end[:, :, 0, :])[..., None] * state
                 + jnp.einsum('bhsk,bhsv->bhkv', kc * jnp.exp(b_end - b), vc))
        return state, o_inter + o_intra

    s0 = jnp.zeros((B, HG_HEADS, HG_DK, HG_DV), f32)
    _, o = lax.scan(chunk_step, s0, (to_chunks(qf, HG_DK), to_chunks(k, HG_DK),
                                     to_chunks(vf, HG_DV), to_chunks(log_f, HG_DK)))
    o = o.transpose(1, 0, 3, 2, 4).reshape(B, S, HG_HEADS, HG_DV)
    o = o * lax.rsqrt(jnp.mean(o * o, axis=-1, keepdims=True) + NORM_EPS)
    o = o.reshape(B, S, HG_WIDTH) * norm_g.astype(f32) * jax.nn.silu(g.astype(f32))
    return o.astype(q.dtype)


def t5_bucket(dist):
    max_exact = REL_BUCKETS // 2
    d = jnp.maximum(dist, 1).astype(jnp.float32)
    log_part = max_exact + (jnp.log(d / max_exact) / math.log(REL_MAX_DIST / max_exact)
                            * (REL_BUCKETS - max_exact)).astype(jnp.int32)
    return jnp.where(dist < max_exact, dist, jnp.minimum(log_part, REL_BUCKETS - 1))


def moba_mixer(q, k, v, rel_bias):
    B, S, _ = q.shape
    f32 = jnp.float32
    nb = -(-S // MB_BLOCK)
    s_pad = nb * MB_BLOCK
    k_sel = min(MB_TOPK, nb)
    scale = MB_DH ** -0.5

    def to_heads(t):
        t = jnp.pad(t, ((0, 0), (0, s_pad - S), (0, 0)))
        return t.reshape(B, s_pad, MB_HEADS, MB_DH).transpose(0, 2, 1, 3)

    qh, kh, vh = to_heads(q), to_heads(k), to_heads(v)
    kb = kh.reshape(B, MB_HEADS, nb, MB_BLOCK, MB_DH)
    vb = vh.reshape(B, MB_HEADS, nb, MB_BLOCK, MB_DH)
    k_mean = jnp.mean(kb, axis=3)
    gate = jnp.einsum('bhsd,bhnd->bhsn', qh, k_mean).astype(f32)
    pos = jnp.arange(s_pad)
    fully_past = jnp.arange(nb)[None, :] < (pos // MB_BLOCK)[:, None]
    gate = jnp.where(fully_past, gate, -jnp.inf)
    top_val, top_idx = lax.top_k(gate, k_sel)
    top_ok = jnp.isfinite(top_val)

    rb = rel_bias.astype(f32).T
    b_ix = jnp.arange(B)[:, None, None, None]
    h_ix = jnp.arange(MB_HEADS)[None, :, None, None]

    def attend(c):
        t0 = c * MB_QCHUNK
        tq = t0 + jnp.arange(MB_QCHUNK)
        qc = lax.dynamic_slice_in_dim(qh, t0, MB_QCHUNK, axis=2)
        idx = lax.dynamic_slice_in_dim(top_idx, t0, MB_QCHUNK, axis=2)
        ok = lax.dynamic_slice_in_dim(top_ok, t0, MB_QCHUNK, axis=2)
        kg = kb[b_ix, h_ix, idx]
        vg = vb[b_ix, h_ix, idx]
        kpos = idx[..., None] * MB_BLOCK + jnp.arange(MB_BLOCK)
        bucket = t5_bucket(tq[None, None, :, None, None] - kpos)
        s_past = (jnp.einsum('bhqd,bhqjkd->bhqjk', qc, kg).astype(f32) * scale
                  + rb[h_ix[..., None], bucket])
        s_past = jnp.where(ok[..., None], s_past, -jnp.inf).reshape(B, MB_HEADS, MB_QCHUNK, k_sel * MB_BLOCK)
        j0 = (t0 // MB_BLOCK) * MB_BLOCK
        k_own = lax.dynamic_slice_in_dim(kh, j0, MB_BLOCK, axis=2)
        v_own = lax.dynamic_slice_in_dim(vh, j0, MB_BLOCK, axis=2)
        dist_own = tq[:, None] - (j0 + jnp.arange(MB_BLOCK))[None, :]
        s_own = (jnp.einsum('bhqd,bhkd->bhqk', qc, k_own).astype(f32) * scale
                 + rb[:, t5_bucket(jnp.maximum(dist_own, 0))])
        s_own = jnp.where(dist_own >= 0, s_own, -jnp.inf)
        p = jax.nn.softmax(jnp.concatenate([s_past, s_own], axis=-1), axis=-1)
        p_past = p[..., :k_sel * MB_BLOCK].reshape(B, MB_HEADS, MB_QCHUNK, k_sel, MB_BLOCK).astype(vg.dtype)
        p_own = p[..., k_sel * MB_BLOCK:].astype(v_own.dtype)
        return (jnp.einsum('bhqjk,bhqjkd->bhqd', p_past, vg)
                + jnp.einsum('bhqk,bhkd->bhqd', p_own, v_own))

    o = lax.map(attend, jnp.arange(s_pad // MB_QCHUNK))
    o = o.transpose(1, 0, 3, 2, 4).reshape(B, s_pad, MB_WIDTH)[:, :S]
    return o


def hier_moe(x, w_group, b_group, w_expert, b_expert, w_gate_up, w_down):
    B, S, D = x.shape
    n_tok = B * S
    n_asg = n_tok * TOP_K
    f32 = jnp.float32
    xt = x.reshape(n_tok, D)
    g_logits = (xt @ w_group + b_group).astype(f32)
    grp = jnp.argmax(g_logits, axis=-1)
    p_grp = jnp.take_along_axis(jax.nn.softmax(g_logits, axis=-1), grp[:, None], axis=1)
    e_logits = (xt @ w_expert + b_expert).astype(f32).reshape(n_tok, N_GROUPS, EXPERTS_PER_GROUP)
    e_in_grp = jnp.take_along_axis(e_logits, grp[:, None, None], axis=1)[:, 0]
    top_v, top_i = lax.top_k(e_in_grp, TOP_K)
    weights = (jax.nn.softmax(top_v, axis=-1) * p_grp).reshape(n_asg)
    expert = (grp[:, None] * EXPERTS_PER_GROUP + top_i).reshape(n_asg)
    token = jnp.repeat(jnp.arange(n_tok), TOP_K)
    order = jnp.argsort(expert)
    e_s, tok_s, w_s = expert[order], token[order], weights[order]
    counts = jax.ops.segment_sum(jnp.ones((n_asg,), jnp.int32), expert, num_segments=N_EXPERTS)
    starts = jnp.cumsum(counts) - counts
    padded = (counts + MOE_BLOCK - 1) // MOE_BLOCK * MOE_BLOCK
    pad_end = jnp.cumsum(padded)
    pad_start = pad_end - padded
    dest = pad_start[e_s] + jnp.arange(n_asg) - starts[e_s]
    n_blk = -(-n_asg // MOE_BLOCK) + N_EXPERTS
    n_rows = n_blk * MOE_BLOCK
    x_buf = jnp.zeros((n_rows, D), x.dtype).at[dest].set(xt[tok_s])
    blk_expert = jnp.minimum(jnp.searchsorted(pad_end, jnp.arange(n_blk) * MOE_BLOCK, side='right'),
                             N_EXPERTS - 1)

    def expert_block(args):
        xb, e = args
        gate, up = jnp.split(xb @ w_gate_up[e], 2, axis=-1)
        return (jax.nn.silu(gate) * up) @ w_down[e]

    y_buf = lax.map(expert_block, (x_buf.reshape(n_blk, MOE_BLOCK, D), blk_expert)).reshape(n_rows, D)
    y = jnp.zeros((n_tok, D), x.dtype).at[tok_s].add(y_buf[dest] * w_s[:, None].astype(x.dtype))
    return y.reshape(B, S, D)


def setup_inputs(seed: int = 0) -> dict:
    key = jax.random.key(seed)
    ks = jax.random.split(key, 20)
    f32 = jnp.float32
    L = DEPTH

    def normal(k, shape, std):
        return jax.random.normal(k, shape, f32) * std

    col_scale = jnp.concatenate([jnp.full((n,), s, f32) for n, s in
                                 zip(IN_SIZES, (1.0, 1.0, DN_BETA, 1.0, 1.0, 1.0, DN_BETA, 1.0, 1.0))])
    return {
        'x': normal(ks[0], (BATCH, SEQ, D_MODEL), 1.0),
        'w_in': normal(ks[1], (L, D_MODEL, IN_COLS), D_MODEL ** -0.5) * col_scale,
        'b_in': normal(ks[2], (L, IN_COLS), 0.02),
        'lb_logits': normal(ks[3], (L + 1, HG_KEY_WIDTH), 0.5),
        'hg_norm_g': 1.0 + normal(ks[4], (L, HG_WIDTH), 0.1),
        'rel_bias': normal(ks[5], (REL_BUCKETS, MB_HEADS), 0.5),
        'w_proj_a': normal(ks[6], (L, HG_WIDTH, D_MODEL), DN_BETA * HG_WIDTH ** -0.5),
        'w_proj_b': normal(ks[7], (L, MB_WIDTH, D_MODEL), DN_BETA * MB_WIDTH ** -0.5),
        'w_out': normal(ks[8], (L, D_MODEL, D_MODEL), DN_BETA * D_MODEL ** -0.5),
        'ln1_g': 1.0 + normal(ks[9], (L, D_MODEL), 0.1),
        'ln1_b': normal(ks[10], (L, D_MODEL), 0.02),
        'w_group': normal(ks[11], (L, D_MODEL, N_GROUPS), D_MODEL ** -0.5),
        'b_group': normal(ks[12], (L, N_GROUPS), 0.01),
        'w_expert': normal(ks[13], (L, D_MODEL, N_EXPERTS), D_MODEL ** -0.5),
        'b_expert': normal(ks[14], (L, N_EXPERTS), 0.01),
        'w_gate_up': normal(ks[15], (L, N_EXPERTS, D_MODEL, 2 * EXPERT_HIDDEN), D_MODEL ** -0.5),
        'w_down': normal(ks[16], (L, N_EXPERTS, EXPERT_HIDDEN, D_MODEL), DN_BETA * EXPERT_HIDDEN ** -0.5),
        'ln2_g': 1.0 + normal(ks[17], (L, D_MODEL), 0.1),
        'ln2_b': normal(ks[18], (L, D_MODEL), 0.02),
    }


def reference(x, w_in, b_in, lb_logits, hg_norm_g, rel_bias, w_proj_a, w_proj_b, w_out, ln1_g, ln1_b,
              w_group, b_group, w_expert, b_expert, w_gate_up, w_down, ln2_g, ln2_b):
    split_at = [int(s) for s in np.cumsum(IN_SIZES)[:-1]]
    lower_bounds = jnp.cumsum(jax.nn.softmax(lb_logits.astype(jnp.float32), axis=0), axis=0)
    for l in range(DEPTH):
        h = x
        proj = h @ w_in[l] + b_in[l]
        hg_q, hg_f, hg_i, hg_g, mb_q, mb_k, mb_v, gate_a, gate_b = jnp.split(proj, split_at, axis=-1)
        y_a = hgrn2_mixer(hg_q, hg_f, hg_i, hg_g, lower_bounds[l], hg_norm_g[l]) @ w_proj_a[l]
        y_b = moba_mixer(mb_q, mb_k, mb_v, rel_bias) @ w_proj_b[l]
        mixed = (jax.nn.sigmoid(gate_a) * y_a + jax.nn.sigmoid(gate_b) * y_b) @ w_out[l]
        x = layer_norm(DN_ALPHA * h + mixed, ln1_g[l], ln1_b[l])
        ffn = hier_moe(x, w_group[l], b_group[l], w_expert[l], b_expert[l], w_gate_up[l], w_down[l])
        x = layer_norm(DN_ALPHA * x + ffn, ln2_g[l], ln2_b[l])
    return x
```

```python
import functools
import math

import numpy as np
import jax
import jax.numpy as jnp
from jax import lax
from jax.experimental import pallas as pl
from jax.experimental.pallas import tpu as pltpu

F32 = jnp.float32
BF16 = jnp.bfloat16

HG_HEADS = 4
HG_DK = 128
HG_CHUNK = 64
MB_HEADS = 8
MB_DH = 64
MB_BLOCK = 256
MB_TOPK = 3
REL_BUCKETS = 32
REL_MAX_DIST = 128
N_GROUPS = 4
EXPERTS_PER_GROUP = 8
N_EXPERTS = N_GROUPS * EXPERTS_PER_GROUP
TOP_K = 2
EXPERT_HIDDEN = 512
MOE_BLOCK = 128
DEPTH = 1
DN_ALPHA = (2.0 * DEPTH) ** 0.25
NORM_EPS = 1e-5
LANES = 128
VMEM_LIMIT = 56 * 1024 * 1024
NEG_INF = float("-inf")


def _split2(a):
    hi = a.astype(BF16)
    lo = (a - hi.astype(F32)).astype(BF16)
    return hi, lo


def _split3(a):
    hi = a.astype(BF16)
    r = a - hi.astype(F32)
    mid = r.astype(BF16)
    lo = (r - mid.astype(F32)).astype(BF16)
    return hi, mid, lo


def _dot_nt(a, b):
    return lax.dot_general(a, b, (((1,), (1,)), ((), ())), preferred_element_type=F32)


def _dot_tn(a, b):
    return lax.dot_general(a, b, (((0,), (0,)), ((), ())), preferred_element_type=F32)


def _dot(a, b):
    return jnp.dot(a, b, preferred_element_type=F32)


def _in_proj_kernel(x_ref, w_ref, b_ref, hg_ref, mb_ref, gt_ref, *, col_chunk):
    xb = x_ref[...].astype(BF16)
    outs = ((hg_ref, 0), (mb_ref, hg_ref.shape[1]), (gt_ref, hg_ref.shape[1] + mb_ref.shape[1]))
    for o_ref, base in outs:
        for c0 in range(0, o_ref.shape[1], col_chunk):
            acc = _dot(xb, w_ref[:, base + c0:base + c0 + col_chunk])
            o_ref[:, c0:c0 + col_chunk] = acc + b_ref[:, base + c0:base + c0 + col_chunk]


def _in_proj(x2, w_bf, b_in, n_hg, n_mb, n_gt, tm):
    n, d = x2.shape
    cols = w_bf.shape[1]
    return pl.pallas_call(
        functools.partial(_in_proj_kernel, col_chunk=512),
        grid=(n // tm,),
        in_specs=[
            pl.BlockSpec((tm, d), lambda i: (i, 0)),
            pl.BlockSpec((d, cols), lambda i: (0, 0)),
            pl.BlockSpec((1, cols), lambda i: (0, 0)),
        ],
        out_specs=[
            pl.BlockSpec((tm, n_hg), lambda i: (i, 0)),
            pl.BlockSpec((tm, n_mb), lambda i: (i, 0)),
            pl.BlockSpec((tm, n_gt), lambda i: (i, 0)),
        ],
        out_shape=[
            jax.ShapeDtypeStruct((n, n_hg), F32),
            jax.ShapeDtypeStruct((n, n_mb), F32),
            jax.ShapeDtypeStruct((n, n_gt), F32),
        ],
        compiler_params=pltpu.CompilerParams(
            dimension_semantics=("parallel",), vmem_limit_bytes=VMEM_LIMIT),
        name="in_proj",
    )(x2, w_bf, b_in)


def _hgrn_tables():
    c = HG_CHUNK
    levels = []
    m = c // 2
    while m >= 1:
        levels.append(m)
        m //= 2
    stack = np.zeros((2 + len(levels), c, c), np.float32)
    masks = np.zeros((len(levels) + 1, c, c), np.float32)
    for t in range(c):
        stack[0, t, :t + 1] = 1.0
        stack[1, t, t + 1:] = 1.0
    for li, m in enumerate(levels):
        for r in range(c):
            c0 = (r // (2 * m)) * (2 * m)
            if r - c0 >= m:
                stack[2 + li, r, c0 + m:r + 1] = 1.0
                masks[li, r, c0:c0 + m] = 1.0
            else:
                stack[2 + li, r, r + 1:c0 + m] = 1.0
    masks[len(levels)] = np.eye(c, dtype=np.float32)
    return stack.reshape(-1, c), masks, len(levels)


def _hgrn_kernel(q_ref, f_ref, i_ref, g_ref, lb_ref, ng_ref, mst_ref, msk_ref, o_ref, st_ref,
                 *, n_chunks, n_levels):
    c = HG_CHUNK
    lb = lb_ref[...]
    oml = 1.0 - lb
    ng = ng_ref[...]
    st_ref[...] = jnp.zeros_like(st_ref)
    mst = mst_ref[...]

    def body(ci, carry):
        r0 = pl.multiple_of(ci * c, c)
        z = f_ref[0, pl.ds(r0, c), :]
        qr = q_ref[0, pl.ds(r0, c), :]
        v = i_ref[0, pl.ds(r0, c), :]
        g = g_ref[0, pl.ds(r0, c), :]
        lf = jnp.log(lb + oml * jax.nn.sigmoid(z))
        kk = oml * jax.nn.sigmoid(-z)
        qf = qr * jax.nn.sigmoid(qr)
        l_hi, l_mid, l_lo = _split3(lf)
        e_all = _dot(mst, l_hi) + _dot(mst, l_mid) + _dot(mst, l_lo)
        b_pre = e_all[0:c]
        b_suf = e_all[c:2 * c]
        b_end = b_pre[c - 1:c, :]
        vb = v.astype(BF16)
        st = st_ref[...]
        o = _dot_nt((qf * jnp.exp(b_pre)).astype(BF16), st.astype(BF16))
        scores = msk_ref[n_levels] * _dot_nt(qf.astype(BF16), kk.astype(BF16))
        for li in range(n_levels):
            ex = jnp.exp(e_all[(2 + li) * c:(3 + li) * c])
            sc = _dot_nt((qf * ex).astype(BF16), (kk * ex).astype(BF16))
            scores = scores + msk_ref[li] * sc
        o = o + _dot(scores.astype(BF16), vb)
        kd = (kk * jnp.exp(b_suf)).astype(BF16)
        st_ref[...] = st * jnp.exp(b_end) + _dot_tn(vb, kd)
        o = o * lax.rsqrt(jnp.mean(o * o, axis=-1, keepdims=True) + NORM_EPS)
        o_ref[0, pl.ds(r0, c), :] = o * ng * (g * jax.nn.sigmoid(g))
        return carry

    lax.fori_loop(0, n_chunks, body, 0)


def _hgrn(hg3, lb_row, ng_row):
    b, s, _ = hg3.shape
    stack, masks, n_levels = _hgrn_tables()
    mst = jnp.asarray(stack, BF16)
    msk = jnp.asarray(masks, F32)
    h = HG_HEADS

    def col(off):
        return pl.BlockSpec((1, s, HG_DK), lambda bi, hi: (bi, 0, off + hi))

    return pl.pallas_call(
        functools.partial(_hgrn_kernel, n_chunks=s // HG_CHUNK, n_levels=n_levels),
        grid=(b, h),
        in_specs=[
            col(0), col(h), col(2 * h), col(3 * h),
            pl.BlockSpec((1, HG_DK), lambda bi, hi: (0, hi)),
            pl.BlockSpec((1, HG_DK), lambda bi, hi: (0, hi)),
            pl.BlockSpec(mst.shape, lambda bi, hi: (0, 0)),
            pl.BlockSpec(msk.shape, lambda bi, hi: (0, 0, 0)),
        ],
        out_specs=pl.BlockSpec((1, s, HG_DK), lambda bi, hi: (bi, 0, hi)),
        out_shape=jax.ShapeDtypeStruct((b, s, h * HG_DK), F32),
        scratch_shapes=[pltpu.VMEM((HG_DK, HG_DK), F32)],
        compiler_params=pltpu.CompilerParams(
            dimension_semantics=("parallel", "parallel"), vmem_limit_bytes=VMEM_LIMIT),
        name="hgrn2",
    )(hg3, hg3, hg3, hg3, lb_row, ng_row, mst, msk)


def _t5_bucket_np(dist):
    max_exact = REL_BUCKETS // 2
    d = np.maximum(dist, 1).astype(np.float32)
    log_part = max_exact + (np.log(d / np.float32(max_exact)) / np.float32(math.log(REL_MAX_DIST / max_exact))
                            * np.float32(REL_BUCKETS - max_exact)).astype(np.int32)
    return np.where(dist < max_exact, dist, np.minimum(log_part, REL_BUCKETS - 1))


def _moba_kernel(q_ref, k_ref, v_ref, avg_ref, own_ref, prev_ref, far_ref, o_ref, *, n_blocks):
    blk = MB_BLOCK
    scale = MB_DH ** -0.5
    lane = lax.broadcasted_iota(jnp.int32, (blk, LANES), 1)
    avg = avg_ref[...]
    outs = []
    for hh in range(LANES // MB_DH):
        ls = slice(hh * MB_DH, (hh + 1) * MB_DH)
        k_hi, k_lo = _split2(k_ref[0, :, ls])
        k_mean = _dot(avg, k_hi) + _dot(avg, k_lo)
        km_hi, km_lo = _split2(k_mean)
        own_bias = own_ref[hh]
        prev_bias = prev_ref[hh]
        far_bias = far_ref[hh]
        o_blocks = []
        for i in range(n_blocks):
            qi = q_ref[0, i * blk:(i + 1) * blk, ls] * scale
            qb = qi.astype(BF16)
            sel = None
            if i > 0:
                q_hi, q_lo = _split2(qi)
                gate = _dot_nt(q_hi, km_hi) + _dot_nt(q_hi, km_lo) + _dot_nt(q_lo, km_hi)
                gate = jnp.where(lane < i, gate, NEG_INF)
                sel = jnp.zeros((blk, LANES), F32)
                for _ in range(min(MB_TOPK, i)):
                    mx = jnp.max(gate, axis=-1, keepdims=True)
                    first = jnp.min(jnp.where(gate == mx, lane, LANES), axis=-1, keepdims=True)
                    pick = jnp.logical_and(lane == first, mx > NEG_INF)
                    sel = jnp.where(pick, 1.0, sel)
                    gate = jnp.where(pick, NEG_INF, gate)
            kcat = k_ref[0, 0:(i + 1) * blk, ls].astype(BF16)
            s = _dot_nt(qb, kcat)
            pieces = []
            for j in range(i + 1):
                sj = s[:, j * blk:(j + 1) * blk]
                if j == i:
                    pieces.append(sj + own_bias)
                else:
                    bias = prev_bias if j == i - 1 else far_bias
                    pieces.append(jnp.where(sel[:, j:j + 1] > 0.0, sj + bias, NEG_INF))
            mx = pieces[0].max(axis=-1, keepdims=True)
            for p in pieces[1:]:
                mx = jnp.maximum(mx, p.max(axis=-1, keepdims=True))
            den = jnp.zeros((blk, 1), F32)
            acc = jnp.zeros((blk, MB_DH), F32)
            for j, p in enumerate(pieces):
                e = jnp.exp(p - mx)
                den = den + e.sum(axis=-1, keepdims=True)
                acc = acc + _dot(e.astype(BF16), v_ref[0, j * blk:(j + 1) * blk, ls].astype(BF16))
            o_blocks.append(acc / den)
        outs.append(o_blocks)
    for i in range(n_blocks):
        o_ref[0, i * blk:(i + 1) * blk, :] = jnp.concatenate([o[i] for o in outs], axis=-1)


def _moba(mb3, rel_bias):
    b, s, _ = mb3.shape
    assert s % MB_BLOCK == 0
    nb = s // MB_BLOCK
    hp = LANES // MB_DH
    n_hp = MB_HEADS // hp
    t = np.arange(MB_BLOCK)
    d_own = t[:, None] - t[None, :]
    b_own = _t5_bucket_np(np.maximum(d_own, 0))
    b_prev = _t5_bucket_np(d_own + MB_BLOCK)
    rb = rel_bias.astype(F32).T
    own = jnp.where(jnp.asarray(d_own >= 0)[None], rb[:, b_own], NEG_INF)
    prev = rb[:, b_prev]
    far = jnp.broadcast_to(rb[:, REL_BUCKETS - 1][:, None, None], (MB_HEADS, 1, MB_BLOCK))
    avg_np = np.zeros((LANES, s), np.float32)
    for j in range(nb):
        avg_np[j, j * MB_BLOCK:(j + 1) * MB_BLOCK] = 1.0 / MB_BLOCK
    avg = jnp.asarray(avg_np, BF16)

    def col(off):
        return pl.BlockSpec((1, s, LANES), lambda bi, hi: (bi, 0, off + hi))

    return pl.pallas_call(
        functools.partial(_moba_kernel, n_blocks=nb),
        grid=(b, n_hp),
        in_specs=[
            col(0), col(n_hp), col(2 * n_hp),
            pl.BlockSpec(avg.shape, lambda bi, hi: (0, 0)),
            pl.BlockSpec((hp, MB_BLOCK, MB_BLOCK), lambda bi, hi: (hi, 0, 0)),
            pl.BlockSpec((hp, MB_BLOCK, MB_BLOCK), lambda bi, hi: (hi, 0, 0)),
            pl.BlockSpec((hp, 1, MB_BLOCK), lambda bi, hi: (hi, 0, 0)),
        ],
        out_specs=pl.BlockSpec((1, s, LANES), lambda bi, hi: (bi, 0, hi)),
        out_shape=jax.ShapeDtypeStruct((b, s, MB_HEADS * MB_DH), F32),
        compiler_params=pltpu.CompilerParams(
            dimension_semantics=("parallel", "parallel"), vmem_limit_bytes=VMEM_LIMIT),
        name="moba",
    )(mb3, mb3, mb3, avg, own, prev, far)


def _layer_norm(x, g, b):
    mu = jnp.mean(x, axis=-1, keepdims=True)
    xc = x - mu
    var = jnp.mean(xc * xc, axis=-1, keepdims=True)
    return xc * lax.rsqrt(var + NORM_EPS) * g + b


def _merge_kernel(hg_ref, mb_ref, ga_ref, gb_ref, x_ref, wa_ref, wb_ref, wo_ref, g1_ref, b1_ref,
                  wr_ref, br_ref, x1_ref, route_ref):
    ya = _dot(hg_ref[...].astype(BF16), wa_ref[...])
    yb = _dot(mb_ref[...].astype(BF16), wb_ref[...])
    mixed_in = jax.nn.sigmoid(ga_ref[...]) * ya + jax.nn.sigmoid(gb_ref[...]) * yb
    mixed = _dot(mixed_in.astype(BF16), wo_ref[...])
    x1 = _layer_norm(DN_ALPHA * x_ref[...] + mixed, g1_ref[...], b1_ref[...])
    x1_ref[...] = x1
    x_hi, x_lo = _split2(x1)
    w_hi, w_lo = wr_ref[0], wr_ref[1]
    logits = _dot(x_hi, w_hi) + _dot(x_hi, w_lo) + _dot(x_lo, w_hi) + br_ref[...]
    tm = logits.shape[0]
    lane = lax.broadcasted_iota(jnp.int32, (tm, LANES), 1)
    glog = jnp.where(lane < N_GROUPS, logits, NEG_INF)
    gmax = jnp.max(glog, axis=-1, keepdims=True)
    grp = jnp.min(jnp.where(glog == gmax, lane, LANES), axis=-1, keepdims=True)
    p_grp = 1.0 / jnp.sum(jnp.exp(glog - gmax), axis=-1, keepdims=True)
    e_lo = N_GROUPS + grp * EXPERTS_PER_GROUP
    elog = jnp.where(jnp.logical_and(lane >= e_lo, lane < e_lo + EXPERTS_PER_GROUP), logits, NEG_INF)
    m1 = jnp.max(elog, axis=-1, keepdims=True)
    i1 = jnp.min(jnp.where(elog == m1, lane, LANES), axis=-1, keepdims=True)
    elog2 = jnp.where(lane == i1, NEG_INF, elog)
    m2 = jnp.max(elog2, axis=-1, keepdims=True)
    i2 = jnp.min(jnp.where(elog2 == m2, lane, LANES), axis=-1, keepdims=True)
    e2 = jnp.exp(m2 - m1)
    w1 = p_grp / (1.0 + e2)
    w2 = p_grp * e2 / (1.0 + e2)
    route = jnp.where(lane == 0, (i1 - N_GROUPS).astype(F32),
                      jnp.where(lane == 1, (i2 - N_GROUPS).astype(F32),
                                jnp.where(lane == 2, w1, jnp.where(lane == 3, w2, 0.0))))
    route_ref[...] = route


def _merge(hg_o, mb_o, gates, x2, wa, wb, wo, g1, b1, wr, br, tm):
    n, d = x2.shape
    wa_n = hg_o.shape[1]
    wb_n = mb_o.shape[1]

    def full(a):
        nd = a.ndim
        return pl.BlockSpec(a.shape, lambda i: (0,) * nd)

    return pl.pallas_call(
        _merge_kernel,
        grid=(n // tm,),
        in_specs=[
            pl.BlockSpec((tm, wa_n), lambda i: (i, 0)),
            pl.BlockSpec((tm, wb_n), lambda i: (i, 0)),
            pl.BlockSpec((tm, d), lambda i: (i, 0)),
            pl.BlockSpec((tm, d), lambda i: (i, 1)),
            pl.BlockSpec((tm, d), lambda i: (i, 0)),
            full(wa), full(wb), full(wo), full(g1), full(b1), full(wr), full(br),
        ],
        out_specs=[
            pl.BlockSpec((tm, d), lambda i: (i, 0)),
            pl.BlockSpec((tm, LANES), lambda i: (i, 0)),
        ],
        out_shape=[
            jax.ShapeDtypeStruct((n, d), F32),
            jax.ShapeDtypeStruct((n, LANES), F32),
        ],
        compiler_params=pltpu.CompilerParams(
            dimension_semantics=("parallel",), vmem_limit_bytes=VMEM_LIMIT),
        name="merge_ln1_router",
    )(hg_o, mb_o, gates, gates, x2, wa, wb, wo, g1, b1, wr, br)


def _expert_kernel(be_ref, nu_ref, idx_ref, x_hbm, wgu_ref, wd_ref, y_hbm, xbuf, ybuf, gsem, ssem,
                   *, n_tok):
    g = pl.program_id(0)
    n_used = nu_ref[0]
    slot = g % 2
    rows = MOE_BLOCK
    dump = TOP_K * n_tok

    def gather_copy(r, a, sl):
        tok = jnp.where(a >= 0, a // TOP_K, 0)
        return pltpu.make_async_copy(x_hbm.at[pl.ds(tok, 1), :], xbuf.at[sl, pl.ds(r, 1), :], gsem.at[sl])

    def scatter_copy(r, a, sl):
        dst = jnp.where(a >= 0, a, dump + sl * rows + r)
        return pltpu.make_async_copy(ybuf.at[sl, pl.ds(r, 1), :], y_hbm.at[pl.ds(dst, 1), :], ssem.at[sl])

    def start_gather(blk_rows_ref, which, sl):
        def body(r, c):
            gather_copy(r, blk_rows_ref[which, 0, r], sl).start()
            return c
        lax.fori_loop(0, rows, body, 0)

    @pl.when(g == 0)
    def _():
        start_gather(idx_ref, 0, 0)

    @pl.when(g + 1 < n_used)
    def _():
        start_gather(idx_ref, 1, 1 - slot)

    @pl.when(g < n_used)
    def _():
        def wait_g(r, c):
            gather_copy(r, idx_ref[0, 0, r], slot).wait()
            return c
        lax.fori_loop(0, rows, wait_g, 0)

        @pl.when(g >= 2)
        def _():
            def wait_s(r, c):
                scatter_copy(r, 0, slot).wait()
                return c
            lax.fori_loop(0, rows, wait_s, 0)

        xb = xbuf[slot].astype(BF16)
        gu = _dot(xb, wgu_ref[0])
        gate = gu[:, :EXPERT_HIDDEN]
        up = gu[:, EXPERT_HIDDEN:]
        hdn = (gate * jax.nn.sigmoid(gate) * up).astype(BF16)
        ybuf[slot] = _dot(hdn, wd_ref[0])

        def start_s(r, c):
            scatter_copy(r, idx_ref[0, 0, r], slot).start()
            return c
        lax.fori_loop(0, rows, start_s, 0)

    @pl.when(g == n_used - 1)
    def _():
        def wait_s(r, c):
            scatter_copy(r, 0, slot).wait()
            return c
        lax.fori_loop(0, rows, wait_s, 0)

        @pl.when(g >= 1)
        def _():
            def wait_s2(r, c):
                scatter_copy(r, 0, 1 - slot).wait()
                return c
            lax.fori_loop(0, rows, wait_s2, 0)


def _experts(x1, asg_blocks, blk_expert, n_used, wgu_bf, wd_bf):
    n_tok, d = x1.shape
    n_blk = asg_blocks.shape[0]
    idx2 = jnp.stack([asg_blocks, jnp.concatenate([asg_blocks[1:], asg_blocks[-1:]], axis=0)], axis=1)
    idx2 = idx2.reshape(n_blk * 2, 1, MOE_BLOCK)
    grid_spec = pltpu.PrefetchScalarGridSpec(
        num_scalar_prefetch=2,
        grid=(n_blk,),
        in_specs=[
            pl.BlockSpec((2, 1, MOE_BLOCK), lambda g, be, nu: (g, 0, 0), memory_space=pltpu.SMEM),
            pl.BlockSpec(memory_space=pl.ANY),
            pl.BlockSpec((1, d, 2 * EXPERT_HIDDEN), lambda g, be, nu: (be[g], 0, 0)),
            pl.BlockSpec((1, EXPERT_HIDDEN, d), lambda g, be, nu: (be[g], 0, 0)),
        ],
        out_specs=pl.BlockSpec(memory_space=pl.ANY),
        scratch_shapes=[
            pltpu.VMEM((2, MOE_BLOCK, d), F32),
            pltpu.VMEM((2, MOE_BLOCK, d), F32),
            pltpu.SemaphoreType.DMA((2,)),
            pltpu.SemaphoreType.DMA((2,)),
        ],
    )
    return pl.pallas_call(
        functools.partial(_expert_kernel, n_tok=n_tok),
        grid_spec=grid_spec,
        out_shape=jax.ShapeDtypeStruct((TOP_K * n_tok + 2 * MOE_BLOCK, d), F32),
        compiler_params=pltpu.CompilerParams(
            dimension_semantics=("arbitrary",), vmem_limit_bytes=VMEM_LIMIT),
        name="experts",
    )(blk_expert, n_used, idx2, x1, wgu_bf, wd_bf)


def _combine_kernel(y_ref, route_ref, x1_ref, g2_ref, b2_ref, o_ref):
    d = x1_ref.shape[1]
    route = route_ref[...]
    w1 = route[:, 2:3]
    w2 = route[:, 3:4]
    ffn = y_ref[:, :d] * w1 + y_ref[:, d:] * w2
    o_ref[...] = _layer_norm(DN_ALPHA * x1_ref[...] + ffn, g2_ref[...], b2_ref[...])


def _combine(y_pairs, route, x1, g2, b2, tm):
    n, d = x1.shape
    return pl.pallas_call(
        _combine_kernel,
        grid=(n // tm,),
        in_specs=[
            pl.BlockSpec((tm, TOP_K * d), lambda i: (i, 0)),
            pl.BlockSpec((tm, LANES), lambda i: (i, 0)),
            pl.BlockSpec((tm, d), lambda i: (i, 0)),
            pl.BlockSpec((1, d), lambda i: (0, 0)),
            pl.BlockSpec((1, d), lambda i: (0, 0)),
        ],
        out_specs=pl.BlockSpec((tm, d), lambda i: (i, 0)),
        out_shape=jax.ShapeDtypeStruct((n, d), F32),
        compiler_params=pltpu.CompilerParams(
            dimension_semantics=("parallel",), vmem_limit_bytes=VMEM_LIMIT),
        name="combine_ln2",
    )(y_pairs, route, x1, g2, b2)


def _dispatch_plan(route, n_tok):
    n_asg = n_tok * TOP_K
    expert = route[:, :TOP_K].astype(jnp.int32).reshape(n_asg)
    onehot = (expert[:, None] == jnp.arange(N_EXPERTS, dtype=jnp.int32)[None, :]).astype(jnp.int32)
    csum = jnp.cumsum(onehot, axis=0)
    rank = jnp.sum(csum * onehot, axis=1) - 1
    counts = csum[-1]
    padded = (counts + MOE_BLOCK - 1) // MOE_BLOCK * MOE_BLOCK
    pad_end = jnp.cumsum(padded)
    pad_start = pad_end - padded
    pos = pad_start[expert] + rank
    n_blk = -(-n_asg // MOE_BLOCK) + N_EXPERTS
    n_rows = n_blk * MOE_BLOCK
    asg_of_pos = jnp.full((n_rows,), -1, jnp.int32).at[pos].set(jnp.arange(n_asg, dtype=jnp.int32))
    blk_expert = jnp.minimum(
        jnp.searchsorted(pad_end, jnp.arange(n_blk, dtype=jnp.int32) * MOE_BLOCK, side="right"),
        N_EXPERTS - 1).astype(jnp.int32)
    n_used = (pad_end[-1] // MOE_BLOCK).astype(jnp.int32).reshape(1)
    return asg_of_pos.reshape(n_blk, MOE_BLOCK), blk_expert, n_used


def _block(x, w_in, b_in, lower_bound, hg_norm_g, rel_bias, w_proj_a, w_proj_b, w_out, ln1_g, ln1_b,
           w_group, b_group, w_expert, b_expert, w_gate_up, w_down, ln2_g, ln2_b, *, tm_proj, tm_merge, tm_comb):
    b, s, d = x.shape
    n = b * s
    n_hg = 4 * HG_HEADS * HG_DK
    n_mb = 3 * MB_HEADS * MB_DH
    n_gt = 2 * d
    x2 = x.reshape(n, d)
    hg, mb, gates = _in_proj(x2, w_in.astype(BF16), b_in.reshape(1, -1), n_hg, n_mb, n_gt, tm_proj)
    hg_o = _hgrn(hg.reshape(b, s, n_hg), lower_bound.reshape(1, -1), hg_norm_g.reshape(1, -1))
    mb_o = _moba(mb.reshape(b, s, n_mb), rel_bias)
    w_r = jnp.zeros((d, LANES), F32).at[:, :N_GROUPS].set(w_group).at[:, N_GROUPS:N_GROUPS + N_EXPERTS].set(w_expert)
    w_r_hi = w_r.astype(BF16)
    w_r_lo = (w_r - w_r_hi.astype(F32)).astype(BF16)
    b_r = jnp.zeros((1, LANES), F32).at[0, :N_GROUPS].set(b_group).at[0, N_GROUPS:N_GROUPS + N_EXPERTS].set(b_expert)
    x1, route = _merge(hg_o.reshape(n, -1), mb_o.reshape(n, -1), gates, x2,
                       w_proj_a.astype(BF16), w_proj_b.astype(BF16), w_out.astype(BF16),
                       ln1_g.reshape(1, d), ln1_b.reshape(1, d), jnp.stack([w_r_hi, w_r_lo]), b_r, tm_merge)
    asg_blocks, blk_expert, n_used = _dispatch_plan(route, n)
    y_rows = _experts(x1, asg_blocks, blk_expert, n_used, w_gate_up.astype(BF16), w_down.astype(BF16))
    y_pairs = y_rows.reshape(-1, TOP_K * d)
    out = _combine(y_pairs, route, x1, ln2_g.reshape(1, d), ln2_b.reshape(1, d), tm_comb)
    return out.reshape(b, s, d)


def kernel(x, w_in, b_in, lb_logits, hg_norm_g, rel_bias, w_proj_a, w_proj_b, w_out, ln1_g, ln1_b, w_group,
           b_group, w_expert, b_expert, w_gate_up, w_down, ln2_g, ln2_b):
    lower_bounds = jnp.cumsum(jax.nn.softmax(lb_logits.astype(F32), axis=0), axis=0)
    l = 0
    return _block(x, w_in[l], b_in[l], lower_bounds[l], hg_norm_g[l], rel_bias, w_proj_a[l], w_proj_b[l],
                  w_out[l], ln1_g[l], ln1_b[l], w_group[l], b_group[l], w_expert[l], b_expert[l],
                  w_gate_up[l], w_down[l], ln2_g[l], ln2_b[l], tm_proj=256, tm_merge=256, tm_comb=256)
```

```python
import functools
import math

import numpy as np
import jax
import jax.numpy as jnp
from jax import lax
from jax.experimental import pallas as pl
from jax.experimental.pallas import tpu as pltpu

F32 = jnp.float32
BF16 = jnp.bfloat16

HG_HEADS = 4
HG_DK = 128
HG_CHUNK = 64
MB_HEADS = 8
MB_DH = 64
MB_BLOCK = 256
MB_TOPK = 3
REL_BUCKETS = 32
REL_MAX_DIST = 128
N_GROUPS = 4
EXPERTS_PER_GROUP = 8
N_EXPERTS = N_GROUPS * EXPERTS_PER_GROUP
TOP_K = 2
EXPERT_HIDDEN = 512
MOE_TILE = 2048
MOE_ROWS = 256
MOE_PAD = 64
DEPTH = 1
DN_ALPHA = (2.0 * DEPTH) ** 0.25
NORM_EPS = 1e-5
LANES = 128
VMEM_LIMIT = 56 * 1024 * 1024
NEG_INF = float("-inf")


def _split2(a):
    hi = a.astype(BF16)
    lo = (a - hi.astype(F32)).astype(BF16)
    return hi, lo


def _split3(a):
    hi = a.astype(BF16)
    r = a - hi.astype(F32)
    mid = r.astype(BF16)
    lo = (r - mid.astype(F32)).astype(BF16)
    return hi, mid, lo


def _dot_nt(a, b):
    return lax.dot_general(a, b, (((1,), (1,)), ((), ())), preferred_element_type=F32)


def _dot_tn(a, b):
    return lax.dot_general(a, b, (((0,), (0,)), ((), ())), preferred_element_type=F32)


def _dot(a, b):
    return jnp.dot(a, b, preferred_element_type=F32)


def _in_proj_kernel(x_ref, w_ref, b_ref, hg_ref, mb_ref, gt_ref, *, col_chunk):
    xb = x_ref[...].astype(BF16)
    outs = ((hg_ref, 0), (mb_ref, hg_ref.shape[1]), (gt_ref, hg_ref.shape[1] + mb_ref.shape[1]))
    for o_ref, base in outs:
        for c0 in range(0, o_ref.shape[1], col_chunk):
            acc = _dot(xb, w_ref[:, base + c0:base + c0 + col_chunk])
            o_ref[:, c0:c0 + col_chunk] = acc + b_ref[:, base + c0:base + c0 + col_chunk]


def _in_proj(x2, w_bf, b_in, n_hg, n_mb, n_gt, tm):
    n, d = x2.shape
    cols = w_bf.shape[1]
    return pl.pallas_call(
        functools.partial(_in_proj_kernel, col_chunk=512),
        grid=(n // tm,),
        in_specs=[
            pl.BlockSpec((tm, d), lambda i: (i, 0)),
            pl.BlockSpec((d, cols), lambda i: (0, 0)),
            pl.BlockSpec((1, cols), lambda i: (0, 0)),
        ],
        out_specs=[
            pl.BlockSpec((tm, n_hg), lambda i: (i, 0)),
            pl.BlockSpec((tm, n_mb), lambda i: (i, 0)),
            pl.BlockSpec((tm, n_gt), lambda i: (i, 0)),
        ],
        out_shape=[
            jax.ShapeDtypeStruct((n, n_hg), F32),
            jax.ShapeDtypeStruct((n, n_mb), F32),
            jax.ShapeDtypeStruct((n, n_gt), F32),
        ],
        compiler_params=pltpu.CompilerParams(
            dimension_semantics=("parallel",), vmem_limit_bytes=VMEM_LIMIT),
        name="in_proj",
    )(x2, w_bf, b_in)


def _hgrn_tables():
    c = HG_CHUNK
    levels = []
    m = c // 2
    while m >= 1:
        levels.append(m)
        m //= 2
    stack = np.zeros((2 + len(levels), c, c), np.float32)
    masks = np.zeros((len(levels) + 1, c, c), np.float32)
    for t in range(c):
        stack[0, t, :t + 1] = 1.0
        stack[1, t, t + 1:] = 1.0
    for li, m in enumerate(levels):
        for r in range(c):
            c0 = (r // (2 * m)) * (2 * m)
            if r - c0 >= m:
                stack[2 + li, r, c0 + m:r + 1] = 1.0
                masks[li, r, c0:c0 + m] = 1.0
            else:
                stack[2 + li, r, r + 1:c0 + m] = 1.0
    masks[len(levels)] = np.eye(c, dtype=np.float32)
    return stack.reshape(-1, c), masks, len(levels)


def _hgrn_kernel(q_ref, f_ref, i_ref, g_ref, lb_ref, ng_ref, mst_ref, msk_ref, o_ref, st_ref,
                 *, n_chunks, n_levels):
    c = HG_CHUNK
    lb = lb_ref[...]
    oml = 1.0 - lb
    ng = ng_ref[...]
    st_ref[...] = jnp.zeros_like(st_ref)
    mst = mst_ref[...]

    def body(ci, carry):
        r0 = pl.multiple_of(ci * c, c)
        z = f_ref[0, pl.ds(r0, c), :]
        qr = q_ref[0, pl.ds(r0, c), :]
        v = i_ref[0, pl.ds(r0, c), :]
        g = g_ref[0, pl.ds(r0, c), :]
        lf = jnp.log(lb + oml * jax.nn.sigmoid(z))
        kk = oml * jax.nn.sigmoid(-z)
        qf = qr * jax.nn.sigmoid(qr)
        l_hi, l_mid, l_lo = _split3(lf)
        e_all = _dot(mst, l_hi) + _dot(mst, l_mid) + _dot(mst, l_lo)
        b_pre = e_all[0:c]
        b_suf = e_all[c:2 * c]
        b_end = b_pre[c - 1:c, :]
        vb = v.astype(BF16)
        st = st_ref[...]
        o = _dot_nt((qf * jnp.exp(b_pre)).astype(BF16), st.astype(BF16))
        scores = msk_ref[n_levels] * _dot_nt(qf.astype(BF16), kk.astype(BF16))
        for li in range(n_levels):
            ex = jnp.exp(e_all[(2 + li) * c:(3 + li) * c])
            sc = _dot_nt((qf * ex).astype(BF16), (kk * ex).astype(BF16))
            scores = scores + msk_ref[li] * sc
        o = o + _dot(scores.astype(BF16), vb)
        kd = (kk * jnp.exp(b_suf)).astype(BF16)
        st_ref[...] = st * jnp.exp(b_end) + _dot_tn(vb, kd)
        o = o * lax.rsqrt(jnp.mean(o * o, axis=-1, keepdims=True) + NORM_EPS)
        o_ref[0, pl.ds(r0, c), :] = o * ng * (g * jax.nn.sigmoid(g))
        return carry

    lax.fori_loop(0, n_chunks, body, 0)


def _hgrn(hg3, lb_row, ng_row):
    b, s, _ = hg3.shape
    stack, masks, n_levels = _hgrn_tables()
    mst = jnp.asarray(stack, BF16)
    msk = jnp.asarray(masks, F32)
    h = HG_HEADS

    def col(off):
        return pl.BlockSpec((1, s, HG_DK), lambda bi, hi: (bi, 0, off + hi))

    return pl.pallas_call(
        functools.partial(_hgrn_kernel, n_chunks=s // HG_CHUNK, n_levels=n_levels),
        grid=(b, h),
        in_specs=[
            col(0), col(h), col(2 * h), col(3 * h),
            pl.BlockSpec((1, HG_DK), lambda bi, hi: (0, hi)),
            pl.BlockSpec((1, HG_DK), lambda bi, hi: (0, hi)),
            pl.BlockSpec(mst.shape, lambda bi, hi: (0, 0)),
            pl.BlockSpec(msk.shape, lambda bi, hi: (0, 0, 0)),
        ],
        out_specs=pl.BlockSpec((1, s, HG_DK), lambda bi, hi: (bi, 0, hi)),
        out_shape=jax.ShapeDtypeStruct((b, s, h * HG_DK), F32),
        scratch_shapes=[pltpu.VMEM((HG_DK, HG_DK), F32)],
        compiler_params=pltpu.CompilerParams(
            dimension_semantics=("parallel", "parallel"), vmem_limit_bytes=VMEM_LIMIT),
        name="hgrn2",
    )(hg3, hg3, hg3, hg3, lb_row, ng_row, mst, msk)


def _t5_bucket_np(dist):
    max_exact = REL_BUCKETS // 2
    d = np.maximum(dist, 1).astype(np.float32)
    log_part = max_exact + (np.log(d / np.float32(max_exact)) / np.float32(math.log(REL_MAX_DIST / max_exact))
                            * np.float32(REL_BUCKETS - max_exact)).astype(np.int32)
    return np.where(dist < max_exact, dist, np.minimum(log_part, REL_BUCKETS - 1))


def _moba_kernel(q_ref, k_ref, v_ref, avg_ref, bkt_ref, rb_ref, o_ref, own_ref, prev_ref, *, n_blocks):
    blk = MB_BLOCK
    scale = MB_DH ** -0.5
    lane = lax.broadcasted_iota(jnp.int32, (blk, LANES), 1)
    avg = avg_ref[...]
    hp = LANES // MB_DH

    @pl.when(pl.program_id(1) == 0)
    def _():
        causal = (lax.broadcasted_iota(jnp.int32, (blk, blk), 0) >= lax.broadcasted_iota(jnp.int32, (blk, blk), 1))
        for hh in range(hp):
            head = pl.program_id(0) * hp + hh
            own_t = jnp.zeros((blk, blk), F32)
            prev_t = jnp.zeros((blk, blk), F32)
            for bk in range(REL_BUCKETS):
                val = rb_ref[bk, head]
                own_t = jnp.where(bkt_ref[0] == bk, val, own_t)
                prev_t = jnp.where(bkt_ref[1] == bk, val, prev_t)
            own_ref[hh] = jnp.where(causal, own_t, NEG_INF)
            prev_ref[hh] = prev_t

    outs = []
    for hh in range(hp):
        ls = slice(hh * MB_DH, (hh + 1) * MB_DH)
        k_hi, k_lo = _split2(k_ref[0, :, ls])
        k_mean = _dot(avg, k_hi) + _dot(avg, k_lo)
        km_hi, km_lo = _split2(k_mean)
        own_bias = own_ref[hh]
        prev_bias = prev_ref[hh]
        far_bias = rb_ref[REL_BUCKETS - 1, pl.program_id(0) * hp + hh]
        o_blocks = []
        for i in range(n_blocks):
            qi = q_ref[0, i * blk:(i + 1) * blk, ls] * scale
            qb = qi.astype(BF16)
            sel = None
            if i > 0:
                q_hi, q_lo = _split2(qi)
                gate = _dot_nt(q_hi, km_hi) + _dot_nt(q_hi, km_lo) + _dot_nt(q_lo, km_hi)
                gate = jnp.where(lane < i, gate, NEG_INF)
                sel = jnp.zeros((blk, LANES), F32)
                for _ in range(min(MB_TOPK, i)):
                    mx = jnp.max(gate, axis=-1, keepdims=True)
                    first = jnp.min(jnp.where(gate == mx, lane, LANES), axis=-1, keepdims=True)
                    pick = jnp.logical_and(lane == first, mx > NEG_INF)
                    sel = jnp.where(pick, 1.0, sel)
                    gate = jnp.where(pick, NEG_INF, gate)
            kcat = k_ref[0, 0:(i + 1) * blk, ls].astype(BF16)
            s = _dot_nt(qb, kcat)
            pieces = []
            for j in range(i + 1):
                sj = s[:, j * blk:(j + 1) * blk]
                if j == i:
                    pieces.append(sj + own_bias)
                else:
                    bias = prev_bias if j == i - 1 else far_bias
                    pieces.append(jnp.where(sel[:, j:j + 1] > 0.0, sj + bias, NEG_INF))
            mx = pieces[0].max(axis=-1, keepdims=True)
            for p in pieces[1:]:
                mx = jnp.maximum(mx, p.max(axis=-1, keepdims=True))
            den = jnp.zeros((blk, 1), F32)
            acc = jnp.zeros((blk, MB_DH), F32)
            for j, p in enumerate(pieces):
                e = jnp.exp(p - mx)
                den = den + e.sum(axis=-1, keepdims=True)
                acc = acc + _dot(e.astype(BF16), v_ref[0, j * blk:(j + 1) * blk, ls].astype(BF16))
            o_blocks.append(acc / den)
        outs.append(o_blocks)
    for i in range(n_blocks):
        o_ref[0, i * blk:(i + 1) * blk, :] = jnp.concatenate([o[i] for o in outs], axis=-1)


def _moba(mb3, rel_bias):
    b, s, _ = mb3.shape
    assert s % MB_BLOCK == 0
    nb = s // MB_BLOCK
    hp = LANES // MB_DH
    n_hp = MB_HEADS // hp
    t = np.arange(MB_BLOCK)
    d_own = t[:, None] - t[None, :]
    buckets = np.stack([_t5_bucket_np(np.maximum(d_own, 0)), _t5_bucket_np(d_own + MB_BLOCK)]).astype(np.int32)
    assert _t5_bucket_np(np.array([MB_BLOCK + 1]))[0] == REL_BUCKETS - 1
    bkt = jnp.asarray(buckets)
    avg_np = np.zeros((LANES, s), np.float32)
    for j in range(nb):
        avg_np[j, j * MB_BLOCK:(j + 1) * MB_BLOCK] = 1.0 / MB_BLOCK
    avg = jnp.asarray(avg_np, BF16)

    def col(off):
        return pl.BlockSpec((1, s, LANES), lambda hi, bi: (bi, 0, off + hi))

    return pl.pallas_call(
        functools.partial(_moba_kernel, n_blocks=nb),
        grid=(n_hp, b),
        in_specs=[
            col(0), col(n_hp), col(2 * n_hp),
            pl.BlockSpec(avg.shape, lambda hi, bi: (0, 0)),
            pl.BlockSpec(bkt.shape, lambda hi, bi: (0, 0, 0)),
            pl.BlockSpec(memory_space=pltpu.SMEM),
        ],
        out_specs=pl.BlockSpec((1, s, LANES), lambda hi, bi: (bi, 0, hi)),
        out_shape=jax.ShapeDtypeStruct((b, s, MB_HEADS * MB_DH), F32),
        scratch_shapes=[pltpu.VMEM((hp, MB_BLOCK, MB_BLOCK), F32), pltpu.VMEM((hp, MB_BLOCK, MB_BLOCK), F32)],
        compiler_params=pltpu.CompilerParams(
            dimension_semantics=("parallel", "arbitrary"), vmem_limit_bytes=VMEM_LIMIT),
        name="moba",
    )(mb3, mb3, mb3, avg, bkt, rel_bias.astype(F32))


def _layer_norm(x, g, b):
    mu = jnp.mean(x, axis=-1, keepdims=True)
    xc = x - mu
    var = jnp.mean(xc * xc, axis=-1, keepdims=True)
    return xc * lax.rsqrt(var + NORM_EPS) * g + b


def _merge_kernel(hg_ref, mb_ref, ga_ref, gb_ref, x_ref, wa_ref, wb_ref, wo_ref, g1_ref, b1_ref,
                  wr_ref, br_ref, tri_ref, x1_ref, route_ref, cnt_ref):
    ya = _dot(hg_ref[...].astype(BF16), wa_ref[...])
    yb = _dot(mb_ref[...].astype(BF16), wb_ref[...])
    mixed_in = jax.nn.sigmoid(ga_ref[...]) * ya + jax.nn.sigmoid(gb_ref[...]) * yb
    mixed = _dot(mixed_in.astype(BF16), wo_ref[...])
    x1 = _layer_norm(DN_ALPHA * x_ref[...] + mixed, g1_ref[...], b1_ref[...])
    x1_ref[...] = x1
    x_hi, x_lo = _split2(x1)
    w_hi, w_lo = wr_ref[0], wr_ref[1]
    logits = _dot(x_hi, w_hi) + _dot(x_hi, w_lo) + _dot(x_lo, w_hi) + br_ref[...]
    tm = logits.shape[0]
    lane = lax.broadcasted_iota(jnp.int32, (tm, LANES), 1)
    glog = jnp.where(lane < N_GROUPS, logits, NEG_INF)
    gmax = jnp.max(glog, axis=-1, keepdims=True)
    grp = jnp.min(jnp.where(glog == gmax, lane, LANES), axis=-1, keepdims=True)
    p_grp = 1.0 / jnp.sum(jnp.exp(glog - gmax), axis=-1, keepdims=True)
    e_lo = N_GROUPS + grp * EXPERTS_PER_GROUP
    elog = jnp.where(jnp.logical_and(lane >= e_lo, lane < e_lo + EXPERTS_PER_GROUP), logits, NEG_INF)
    m1 = jnp.max(elog, axis=-1, keepdims=True)
    i1 = jnp.min(jnp.where(elog == m1, lane, LANES), axis=-1, keepdims=True)
    elog2 = jnp.where(lane == i1, NEG_INF, elog)
    m2 = jnp.max(elog2, axis=-1, keepdims=True)
    i2 = jnp.min(jnp.where(elog2 == m2, lane, LANES), axis=-1, keepdims=True)
    e2 = jnp.exp(m2 - m1)
    w1 = p_grp / (1.0 + e2)
    w2 = p_grp * e2 / (1.0 + e2)
    oh1 = jnp.where(lane == i1 - N_GROUPS, 1.0, 0.0)
    oh2 = jnp.where(lane == i2 - N_GROUPS, 1.0, 0.0)
    both = oh1 + oh2
    before = _dot(tri_ref[...], both.astype(BF16))
    r1 = jnp.sum(before * oh1, axis=-1, keepdims=True)
    r2 = jnp.sum(before * oh2, axis=-1, keepdims=True)
    cnt_ref[0] = jnp.sum(both, axis=0, keepdims=True)
    cols = ((i1 - N_GROUPS).astype(F32), (i2 - N_GROUPS).astype(F32), w1, w2, r1, r2)
    route = jnp.zeros((tm, LANES), F32)
    for li, col in enumerate(cols):
        route = jnp.where(lane == li, col, route)
    route_ref[...] = route


def _merge(hg_o, mb_o, gates, x2, wa, wb, wo, g1, b1, wr, br, tm):
    n, d = x2.shape
    wa_n = hg_o.shape[1]
    wb_n = mb_o.shape[1]
    tri = jnp.asarray(np.tril(np.ones((tm, tm), np.float32), -1), BF16)

    def full(a):
        nd = a.ndim
        return pl.BlockSpec(a.shape, lambda i: (0,) * nd)

    return pl.pallas_call(
        _merge_kernel,
        grid=(n // tm,),
        in_specs=[
            pl.BlockSpec((tm, wa_n), lambda i: (i, 0)),
            pl.BlockSpec((tm, wb_n), lambda i: (i, 0)),
            pl.BlockSpec((tm, d), lambda i: (i, 0)),
            pl.BlockSpec((tm, d), lambda i: (i, 1)),
            pl.BlockSpec((tm, d), lambda i: (i, 0)),
            full(wa), full(wb), full(wo), full(g1), full(b1), full(wr), full(br), full(tri),
        ],
        out_specs=[
            pl.BlockSpec((tm, d), lambda i: (i, 0)),
            pl.BlockSpec((tm, LANES), lambda i: (i, 0)),
            pl.BlockSpec((1, 1, LANES), lambda i: (i, 0, 0)),
        ],
        out_shape=[
            jax.ShapeDtypeStruct((n, d), F32),
            jax.ShapeDtypeStruct((n, LANES), F32),
            jax.ShapeDtypeStruct((n // tm, 1, LANES), F32),
        ],
        compiler_params=pltpu.CompilerParams(
            dimension_semantics=("parallel",), vmem_limit_bytes=VMEM_LIMIT),
        name="merge_ln1_router",
    )(hg_o, mb_o, gates, gates, x2, wa, wb, wo, g1, b1, wr, br, tri)


def _moe_kernel(cnt_ref, base_ref, sbase_ref, re_ref, rr_ref, rw_ref, x1_hbm, wgu_ref, wd_ref, g2_ref, b2_ref,
                out_hbm, y_ref, xs_ref, sem, *, sub_tokens, ln_rows):
    ti = pl.program_id(0)
    e = pl.program_id(1)
    n_tiles = pl.num_programs(0)
    tile, d = y_ref.shape
    n_sub = tile // sub_tokens

    def out_copy(tidx):
        return pltpu.make_async_copy(y_ref, out_hbm.at[pl.ds(tidx * tile, tile), :], sem.at[1])

    def position(a):
        sub = lax.div(a, jnp.int32(TOP_K * sub_tokens))
        return sbase_ref[(ti * n_sub + sub) * N_EXPERTS + re_ref[0, 0, a]] + rr_ref[0, 0, a]

    @pl.when(e == 0)
    def _():
        @pl.when(ti > 0)
        def _():
            out_copy(ti - 1).wait()

        load = pltpu.make_async_copy(x1_hbm.at[pl.ds(ti * tile, tile), :], y_ref, sem.at[0])
        load.start()

        @pl.when(ti == 0)
        def _():
            def zero(c, carry):
                r0 = pl.multiple_of(c * ln_rows, ln_rows)
                xs_ref[pl.ds(r0, ln_rows), :] = jnp.zeros((ln_rows, d), F32)
                return carry
            lax.fori_loop(0, xs_ref.shape[0] // ln_rows, zero, 0)

        load.wait()

        def sort_rows(tok, carry):
            row = y_ref[pl.ds(tok, 1), :]
            xs_ref[pl.ds(position(TOP_K * tok), 1), :] = row
            xs_ref[pl.ds(position(TOP_K * tok + 1), 1), :] = row
            return carry
        lax.fori_loop(0, tile, sort_rows, 0, unroll=4)

        def scale(c, carry):
            r0 = pl.multiple_of(c * ln_rows, ln_rows)
            y_ref[pl.ds(r0, ln_rows), :] = DN_ALPHA * y_ref[pl.ds(r0, ln_rows), :]
            return carry
        lax.fori_loop(0, tile // ln_rows, scale, 0)

    cnt = cnt_ref[ti * N_EXPERTS + e]
    base = base_ref[ti * N_EXPERTS + e]

    def expert_block(b, carry):
        r0 = pl.multiple_of(base + b * MOE_ROWS, MOE_PAD)
        xb = xs_ref[pl.ds(r0, MOE_ROWS), :].astype(BF16)
        gu = _dot(xb, wgu_ref[0])
        gate = gu[:, :EXPERT_HIDDEN]
        up = gu[:, EXPERT_HIDDEN:]
        hdn = (gate * jax.nn.sigmoid(gate) * up).astype(BF16)
        ys = _dot(hdn, wd_ref[0])
        n_valid = cnt - b * MOE_ROWS
        for q in range(MOE_ROWS // MOE_PAD):
            @pl.when(n_valid > q * MOE_PAD)
            def _():
                xs_ref[pl.ds(r0 + q * MOE_PAD, MOE_PAD), :] = ys[q * MOE_PAD:(q + 1) * MOE_PAD]
        return carry
    lax.fori_loop(0, lax.div(cnt + MOE_ROWS - 1, jnp.int32(MOE_ROWS)), expert_block, 0)

    @pl.when(e == N_EXPERTS - 1)
    def _():
        def combine(tok, carry):
            a = TOP_K * tok
            y_ref[pl.ds(tok, 1), :] = (y_ref[pl.ds(tok, 1), :]
                                       + rw_ref[0, 0, a] * xs_ref[pl.ds(position(a), 1), :]
                                       + rw_ref[0, 0, a + 1] * xs_ref[pl.ds(position(a + 1), 1), :])
            return carry
        lax.fori_loop(0, tile, combine, 0, unroll=4)

        def norm(c, carry):
            r0 = pl.multiple_of(c * ln_rows, ln_rows)
            y_ref[pl.ds(r0, ln_rows), :] = _layer_norm(y_ref[pl.ds(r0, ln_rows), :], g2_ref[...], b2_ref[...])
            return carry
        lax.fori_loop(0, tile // ln_rows, norm, 0)
        out_copy(ti).start()

        @pl.when(ti == n_tiles - 1)
        def _():
            out_copy(ti).wait()


def _moe(x1, route, cnt_sub, wgu_bf, wd_bf, g2, b2, sub_tokens):
    n, d = x1.shape
    tile = min(MOE_TILE, n)
    assert n % tile == 0 and tile % sub_tokens == 0
    n_tiles = n // tile
    n_sub = tile // sub_tokens
    ln_rows = min(256, tile)
    c_sub = cnt_sub[:, 0, :N_EXPERTS].astype(jnp.int32).reshape(n_tiles, n_sub, N_EXPERTS)
    cnt = c_sub.sum(axis=1)
    padded = (cnt + MOE_PAD - 1) // MOE_PAD * MOE_PAD
    base = jnp.cumsum(padded, axis=1) - padded
    sbase = base[:, None, :] + jnp.cumsum(c_sub, axis=1) - c_sub
    r_exp = route[:, 0:TOP_K].astype(jnp.int32).reshape(n_tiles, 1, TOP_K * tile)
    r_wgt = route[:, TOP_K:2 * TOP_K].reshape(n_tiles, 1, TOP_K * tile)
    r_rank = route[:, 2 * TOP_K:3 * TOP_K].astype(jnp.int32).reshape(n_tiles, 1, TOP_K * tile)
    xs_rows = TOP_K * tile + N_EXPERTS * MOE_PAD + MOE_ROWS
    xs_rows = -(-xs_rows // ln_rows) * ln_rows

    def smem_spec():
        return pl.BlockSpec((1, 1, TOP_K * tile), lambda t, e, *_: (t, 0, 0), memory_space=pltpu.SMEM)

    grid_spec = pltpu.PrefetchScalarGridSpec(
        num_scalar_prefetch=3,
        grid=(n_tiles, N_EXPERTS),
        in_specs=[
            smem_spec(), smem_spec(), smem_spec(),
            pl.BlockSpec(memory_space=pl.ANY),
            pl.BlockSpec((1, d, 2 * EXPERT_HIDDEN), lambda t, e, *_: (e, 0, 0)),
            pl.BlockSpec((1, EXPERT_HIDDEN, d), lambda t, e, *_: (e, 0, 0)),
            pl.BlockSpec((1, d), lambda t, e, *_: (0, 0)),
            pl.BlockSpec((1, d), lambda t, e, *_: (0, 0)),
        ],
        out_specs=pl.BlockSpec(memory_space=pl.ANY),
        scratch_shapes=[
            pltpu.VMEM((tile, d), F32),
            pltpu.VMEM((xs_rows, d), F32),
            pltpu.SemaphoreType.DMA((2,)),
        ],
    )
    return pl.pallas_call(
        functools.partial(_moe_kernel, sub_tokens=sub_tokens, ln_rows=ln_rows),
        grid_spec=grid_spec,
        out_shape=jax.ShapeDtypeStruct((n, d), F32),
        compiler_params=pltpu.CompilerParams(
            dimension_semantics=("arbitrary", "arbitrary"), vmem_limit_bytes=VMEM_LIMIT),
        name="moe_ln2",
    )(cnt.reshape(-1), base.reshape(-1), sbase.reshape(-1), r_exp, r_rank, r_wgt, x1, wgu_bf, wd_bf, g2, b2)


def _block(x, w_in, b_in, lower_bound, hg_norm_g, rel_bias, w_proj_a, w_proj_b, w_out, ln1_g, ln1_b,
           w_group, b_group, w_expert, b_expert, w_gate_up, w_down, ln2_g, ln2_b, *, tm_proj, tm_merge):
    b, s, d = x.shape
    n = b * s
    n_hg = 4 * HG_HEADS * HG_DK
    n_mb = 3 * MB_HEADS * MB_DH
    n_gt = 2 * d
    x2 = x.reshape(n, d)
    hg, mb, gates = _in_proj(x2, w_in.astype(BF16), b_in.reshape(1, -1), n_hg, n_mb, n_gt, tm_proj)
    hg_o = _hgrn(hg.reshape(b, s, n_hg), lower_bound.reshape(1, -1), hg_norm_g.reshape(1, -1))
    mb_o = _moba(mb.reshape(b, s, n_mb), rel_bias)
    w_r = jnp.zeros((d, LANES), F32).at[:, :N_GROUPS].set(w_group).at[:, N_GROUPS:N_GROUPS + N_EXPERTS].set(w_expert)
    w_r_hi = w_r.astype(BF16)
    w_r_lo = (w_r - w_r_hi.astype(F32)).astype(BF16)
    b_r = jnp.zeros((1, LANES), F32).at[0, :N_GROUPS].set(b_group).at[0, N_GROUPS:N_GROUPS + N_EXPERTS].set(b_expert)
    x1, route, cnt_sub = _merge(hg_o.reshape(n, -1), mb_o.reshape(n, -1), gates, x2,
                                w_proj_a.astype(BF16), w_proj_b.astype(BF16), w_out.astype(BF16),
                                ln1_g.reshape(1, d), ln1_b.reshape(1, d), jnp.stack([w_r_hi, w_r_lo]), b_r, tm_merge)
    out = _moe(x1, route, cnt_sub, w_gate_up.astype(BF16), w_down.astype(BF16),
               ln2_g.reshape(1, d), ln2_b.reshape(1, d), tm_merge)
    return out.reshape(b, s, d)


def kernel(x, w_in, b_in, lb_logits, hg_norm_g, rel_bias, w_proj_a, w_proj_b, w_out, ln1_g, ln1_b, w_group,
           b_group, w_expert, b_expert, w_gate_up, w_down, ln2_g, ln2_b):
    lower_bounds = jnp.cumsum(jax.nn.softmax(lb_logits.astype(F32), axis=0), axis=0)
    l = 0
    return _block(x, w_in[l], b_in[l], lower_bounds[l], hg_norm_g[l], rel_bias, w_proj_a[l], w_proj_b[l],
                  w_out[l], ln1_g[l], ln1_b[l], w_group[l], b_group[l], w_expert[l], b_expert[l],
                  w_gate_up[l], w_down[l], ln2_g[l], ln2_b[l], tm_proj=256, tm_merge=256)
```

```python
import functools
import math

import numpy as np
import jax
import jax.numpy as jnp
from jax import lax
from jax.experimental import pallas as pl
from jax.experimental.pallas import tpu as pltpu

F32 = jnp.float32
BF16 = jnp.bfloat16

HG_HEADS = 4
HG_DK = 128
HG_CHUNK = 64
MB_HEADS = 8
MB_DH = 64
MB_BLOCK = 256
MB_TOPK = 3
GATE_ROWS = 16
REL_BUCKETS = 32
REL_MAX_DIST = 128
N_GROUPS = 4
EXPERTS_PER_GROUP = 8
N_EXPERTS = N_GROUPS * EXPERTS_PER_GROUP
TOP_K = 2
EXPERT_HIDDEN = 512
MOE_TILE = 2048
MOE_ROWS = 256
MOE_PAD = 64
DEPTH = 1
DN_ALPHA = (2.0 * DEPTH) ** 0.25
NORM_EPS = 1e-5
LANES = 128
SUBLANES = 8
VMEM_LIMIT = 56 * 1024 * 1024
NEG_INF = float("-inf")


def _split2(a):
    hi = a.astype(BF16)
    lo = (a - hi.astype(F32)).astype(BF16)
    return hi, lo


def _split3(a):
    hi = a.astype(BF16)
    r = a - hi.astype(F32)
    mid = r.astype(BF16)
    lo = (r - mid.astype(F32)).astype(BF16)
    return hi, mid, lo


def _dot_nt(a, b):
    return lax.dot_general(a, b, (((1,), (1,)), ((), ())), preferred_element_type=F32)


def _dot_tn(a, b):
    return lax.dot_general(a, b, (((0,), (0,)), ((), ())), preferred_element_type=F32)


def _dot(a, b):
    return jnp.dot(a, b, preferred_element_type=F32)


def _in_proj_kernel(x_ref, w_ref, b_ref, hg_ref, mb_ref, gt_ref, *, col_chunk):
    xb = x_ref[...].astype(BF16)
    outs = ((hg_ref, 0), (mb_ref, hg_ref.shape[1]), (gt_ref, hg_ref.shape[1] + mb_ref.shape[1]))
    for o_ref, base in outs:
        for c0 in range(0, o_ref.shape[1], col_chunk):
            acc = _dot(xb, w_ref[:, base + c0:base + c0 + col_chunk])
            o_ref[:, c0:c0 + col_chunk] = acc + b_ref[:, base + c0:base + c0 + col_chunk]


def _in_proj(x2, w_bf, b_in, n_hg, n_mb, n_gt, tm):
    n, d = x2.shape
    cols = w_bf.shape[1]
    return pl.pallas_call(
        functools.partial(_in_proj_kernel, col_chunk=512),
        grid=(n // tm,),
        in_specs=[
            pl.BlockSpec((tm, d), lambda i: (i, 0)),
            pl.BlockSpec((d, cols), lambda i: (0, 0)),
            pl.BlockSpec((1, cols), lambda i: (0, 0)),
        ],
        out_specs=[
            pl.BlockSpec((tm, n_hg), lambda i: (i, 0)),
            pl.BlockSpec((tm, n_mb), lambda i: (i, 0)),
            pl.BlockSpec((tm, n_gt), lambda i: (i, 0)),
        ],
        out_shape=[
            jax.ShapeDtypeStruct((n, n_hg), F32),
            jax.ShapeDtypeStruct((n, n_mb), F32),
            jax.ShapeDtypeStruct((n, n_gt), F32),
        ],
        compiler_params=pltpu.CompilerParams(
            dimension_semantics=("parallel",), vmem_limit_bytes=VMEM_LIMIT),
        name="in_proj",
    )(x2, w_bf, b_in)


def _hgrn_tables():
    c = HG_CHUNK
    levels = [c >> (i + 1) for i in range(int(math.log2(c)))]
    masks = np.zeros((len(levels) + 1, c, c), np.float32)
    for li, m in enumerate(levels):
        for r in range(c):
            c0 = (r // (2 * m)) * (2 * m)
            if r - c0 >= m:
                masks[li, r, c0:c0 + m] = 1.0
    masks[len(levels)] = np.eye(c, dtype=np.float32)
    return np.tril(np.ones((c, c), np.float32)), masks, tuple(levels)


def _level_ref_rows(p, m):
    c, w = p.shape
    if 2 * m > SUBLANES:
        parts = [jnp.broadcast_to(p[c0 + m - 1:c0 + m, :], (2 * m, w)) for c0 in range(0, c, 2 * m)]
        return parts[0] if len(parts) == 1 else jnp.concatenate(parts, axis=0)
    p3 = p.reshape(c // SUBLANES, SUBLANES, w)
    sub = lax.broadcasted_iota(jnp.int32, p3.shape, 1)
    out = None
    for c0 in range(0, SUBLANES, 2 * m):
        b = jnp.broadcast_to(p3[:, c0 + m - 1:c0 + m, :], p3.shape)
        out = b if out is None else jnp.where(sub >= c0, b, out)
    return out.reshape(c, w)


def _hgrn_kernel(q_ref, f_ref, i_ref, g_ref, lb_ref, ng_ref, tril_ref, msk_ref, o_ref,
                 *, n_chunks, levels, group):
    c = HG_CHUNK
    dk = HG_DK
    assert n_chunks % group == 0
    lb = jnp.concatenate([lb_ref[...]] * group, axis=1)
    oml = 1.0 - lb
    ng = ng_ref[...]
    tril = tril_ref[...]
    n_levels = len(levels)

    def load(ref, r0):
        return jnp.concatenate([ref[0, pl.ds(r0 + u * c, c), :] for u in range(group)], axis=1)

    def lanes(a, u):
        return a[:, u * dk:(u + 1) * dk]

    def body(gi, st):
        r0 = pl.multiple_of(gi * (group * c), group * c)
        z = load(f_ref, r0)
        qr = load(q_ref, r0)
        lf = jnp.log(lb + oml * jax.nn.sigmoid(z))
        kk = oml * jax.nn.sigmoid(-z)
        qf = qr * jax.nn.sigmoid(qr)
        l_hi, l_mid, l_lo = _split3(lf)
        p = _dot(tril, l_hi) + _dot(tril, l_mid) + _dot(tril, l_lo)
        b_end = p[c - 1:c, :]
        qb = (qf * jnp.exp(p)).astype(BF16)
        kd = (kk * jnp.exp(b_end - p)).astype(BF16)
        dec = jnp.exp(b_end)
        qh = qf.astype(BF16)
        kh = kk.astype(BF16)
        scores = [msk_ref[n_levels] * _dot_nt(lanes(qh, u), lanes(kh, u)) for u in range(group)]
        for li, m in enumerate(levels):
            ex = jnp.exp(-jnp.abs(p - _level_ref_rows(p, m)))
            ql = (qf * ex).astype(BF16)
            kl = (kk * ex).astype(BF16)
            for u in range(group):
                scores[u] = scores[u] + msk_ref[li] * _dot_nt(lanes(ql, u), lanes(kl, u))
        for u in range(group):
            rows = pl.ds(r0 + u * c, c)
            vb = i_ref[0, rows, :].astype(BF16)
            g = g_ref[0, rows, :]
            o = _dot_nt(lanes(qb, u), st.astype(BF16)) + _dot(scores[u].astype(BF16), vb)
            st = st * lanes(dec, u) + _dot_tn(vb, lanes(kd, u))
            o = o * lax.rsqrt(jnp.mean(o * o, axis=-1, keepdims=True) + NORM_EPS)
            o_ref[0, rows, :] = o * ng * (g * jax.nn.sigmoid(g))
        return st

    lax.fori_loop(0, n_chunks // group, body, jnp.zeros((HG_DK, HG_DK), F32))


def _hgrn(hg3, lb_row, ng_row):
    b, s, _ = hg3.shape
    tril, masks, levels = _hgrn_tables()
    mst = jnp.asarray(tril, BF16)
    msk = jnp.asarray(masks, F32)
    h = HG_HEADS

    def col(off):
        return pl.BlockSpec((1, s, HG_DK), lambda bi, hi: (bi, 0, off + hi))

    return pl.pallas_call(
        functools.partial(_hgrn_kernel, n_chunks=s // HG_CHUNK, levels=levels, group=4),
        grid=(b, h),
        in_specs=[
            col(0), col(h), col(2 * h), col(3 * h),
            pl.BlockSpec((1, HG_DK), lambda bi, hi: (0, hi)),
            pl.BlockSpec((1, HG_DK), lambda bi, hi: (0, hi)),
            pl.BlockSpec(mst.shape, lambda bi, hi: (0, 0)),
            pl.BlockSpec(msk.shape, lambda bi, hi: (0, 0, 0)),
        ],
        out_specs=pl.BlockSpec((1, s, HG_DK), lambda bi, hi: (bi, 0, hi)),
        out_shape=jax.ShapeDtypeStruct((b, s, h * HG_DK), F32),
        compiler_params=pltpu.CompilerParams(
            dimension_semantics=("parallel", "parallel"), vmem_limit_bytes=VMEM_LIMIT),
        name="hgrn2",
    )(hg3, hg3, hg3, hg3, lb_row, ng_row, mst, msk)


def _t5_bucket_np(dist):
    max_exact = REL_BUCKETS // 2
    d = np.maximum(dist, 1).astype(np.float32)
    log_part = max_exact + (np.log(d / np.float32(max_exact)) / np.float32(math.log(REL_MAX_DIST / max_exact))
                            * np.float32(REL_BUCKETS - max_exact)).astype(np.int32)
    return np.where(dist < max_exact, dist, np.minimum(log_part, REL_BUCKETS - 1))


def _moba_kernel(q_ref, k_ref, v_ref, avg_ref, bkt_ref, rb_ref, o_ref, own_ref, prev_ref, *, n_blocks):
    blk = MB_BLOCK
    scale = MB_DH ** -0.5
    avg = avg_ref[...]
    hp = LANES // MB_DH

    @pl.when(pl.program_id(1) == 0)
    def _():
        causal = (lax.broadcasted_iota(jnp.int32, (blk, blk), 0) <= lax.broadcasted_iota(jnp.int32, (blk, blk), 1))
        for hh in range(hp):
            head = pl.program_id(0) * hp + hh
            own_t = jnp.zeros((blk, blk), F32)
            prev_t = jnp.zeros((blk, blk), F32)
            for bk in range(REL_BUCKETS):
                val = rb_ref[bk, head]
                own_t = jnp.where(bkt_ref[0] == bk, val, own_t)
                prev_t = jnp.where(bkt_ref[1] == bk, val, prev_t)
            own_ref[hh] = jnp.where(causal, own_t, NEG_INF)
            prev_ref[hh] = prev_t

    grow = lax.broadcasted_iota(jnp.int32, (GATE_ROWS, blk), 0)
    vt_all = v_ref[0].T
    outs = []
    for hh in range(hp):
        ls = slice(hh * MB_DH, (hh + 1) * MB_DH)
        k_h = k_ref[0, :, ls]
        k_hi, k_lo = _split2(k_h)
        k_mean = (_dot(avg, k_hi) + _dot(avg, k_lo))[:GATE_ROWS]
        km_hi, km_lo = _split2(k_mean)
        kb = k_h.astype(BF16)
        vt = vt_all[hh * MB_DH:(hh + 1) * MB_DH, :].astype(BF16)
        own_bias = own_ref[hh]
        prev_bias = prev_ref[hh]
        far_bias = rb_ref[REL_BUCKETS - 1, pl.program_id(0) * hp + hh]
        o_blocks = []
        for i in range(n_blocks):
            qi = q_ref[0, i * blk:(i + 1) * blk, ls] * scale
            st = _dot_nt(kb[:(i + 1) * blk], qi.astype(BF16))
            sel = None
            if i > 0:
                q_hi, q_lo = _split2(qi)
                gate = _dot_nt(km_hi, q_hi) + _dot_nt(km_lo, q_hi) + _dot_nt(km_hi, q_lo)
                gate = jnp.where(grow < i, gate, NEG_INF)
                rank = jnp.zeros((GATE_ROWS, blk), F32)
                if i > MB_TOPK:
                    for j2 in range(i):
                        gj = jnp.broadcast_to(gate[j2:j2 + 1, :], (GATE_ROWS, blk))
                        tie = jnp.where(grow > j2, 1.0, 0.0)
                        rank = rank + jnp.where(gj > gate, 1.0, jnp.where(gj == gate, tie, 0.0))
                sel = jnp.where(rank < MB_TOPK, gate, NEG_INF) > NEG_INF
            pieces = []
            for j in range(i + 1):
                sj = st[j * blk:(j + 1) * blk, :]
                if j == i:
                    pieces.append(sj + own_bias)
                elif j == i - 1:
                    pieces.append(sj + prev_bias + jnp.where(sel[j:j + 1, :], 0.0, NEG_INF))
                else:
                    pieces.append(sj + jnp.where(sel[j:j + 1, :], far_bias, NEG_INF))

            def fold(a):
                return a.reshape(blk // SUBLANES, SUBLANES, blk)

            mx8 = fold(pieces[0]).max(axis=0)
            for p in pieces[1:]:
                mx8 = jnp.maximum(mx8, fold(p).max(axis=0))
            mx = mx8.max(axis=0, keepdims=True)
            den8 = jnp.zeros((SUBLANES, blk), F32)
            acc = jnp.zeros((MB_DH, blk), F32)
            for j, p in enumerate(pieces):
                e = jnp.exp(p - mx)
                den8 = den8 + fold(e).sum(axis=0)
                acc = acc + _dot(vt[:, j * blk:(j + 1) * blk], e.astype(BF16))
            o_blocks.append(acc / den8.sum(axis=0, keepdims=True))
        outs.append(o_blocks)
    for i in range(n_blocks):
        o_ref[0, i * blk:(i + 1) * blk, :] = jnp.concatenate([o[i] for o in outs], axis=0).T


def _moba(mb3, rel_bias):
    b, s, _ = mb3.shape
    assert s % MB_BLOCK == 0
    nb = s // MB_BLOCK
    hp = LANES // MB_DH
    n_hp = MB_HEADS // hp
    t = np.arange(MB_BLOCK)
    d_own = t[:, None] - t[None, :]
    buckets = np.stack([_t5_bucket_np(np.maximum(d_own, 0)), _t5_bucket_np(d_own + MB_BLOCK)]).astype(np.int32)
    assert _t5_bucket_np(np.array([MB_BLOCK + 1]))[0] == REL_BUCKETS - 1
    assert nb <= GATE_ROWS
    bkt = jnp.asarray(buckets.transpose(0, 2, 1))
    avg_np = np.zeros((LANES, s), np.float32)
    for j in range(nb):
        avg_np[j, j * MB_BLOCK:(j + 1) * MB_BLOCK] = 1.0 / MB_BLOCK
    avg = jnp.asarray(avg_np, BF16)

    def col(off):
        return pl.BlockSpec((1, s, LANES), lambda hi, bi: (bi, 0, off + hi))

    return pl.pallas_call(
        functools.partial(_moba_kernel, n_blocks=nb),
        grid=(n_hp, b),
        in_specs=[
            col(0), col(n_hp), col(2 * n_hp),
            pl.BlockSpec(avg.shape, lambda hi, bi: (0, 0)),
            pl.BlockSpec(bkt.shape, lambda hi, bi: (0, 0, 0)),
            pl.BlockSpec(memory_space=pltpu.SMEM),
        ],
        out_specs=pl.BlockSpec((1, s, LANES), lambda hi, bi: (bi, 0, hi)),
        out_shape=jax.ShapeDtypeStruct((b, s, MB_HEADS * MB_DH), F32),
        scratch_shapes=[pltpu.VMEM((hp, MB_BLOCK, MB_BLOCK), F32), pltpu.VMEM((hp, MB_BLOCK, MB_BLOCK), F32)],
        compiler_params=pltpu.CompilerParams(
            dimension_semantics=("parallel", "arbitrary"), vmem_limit_bytes=VMEM_LIMIT),
        name="moba",
    )(mb3, mb3, mb3, avg, bkt, rel_bias.astype(F32))


def _layer_norm(x, g, b):
    mu = jnp.mean(x, axis=-1, keepdims=True)
    xc = x - mu
    var = jnp.mean(xc * xc, axis=-1, keepdims=True)
    return xc * lax.rsqrt(var + NORM_EPS) * g + b


def _merge_kernel(hg_ref, mb_ref, ga_ref, gb_ref, x_ref, wa_ref, wb_ref, wo_ref, g1_ref, b1_ref,
                  wr_ref, br_ref, tri_ref, x1_ref, route_ref, cnt_ref):
    ya = _dot(hg_ref[...].astype(BF16), wa_ref[...])
    yb = _dot(mb_ref[...].astype(BF16), wb_ref[...])
    mixed_in = jax.nn.sigmoid(ga_ref[...]) * ya + jax.nn.sigmoid(gb_ref[...]) * yb
    mixed = _dot(mixed_in.astype(BF16), wo_ref[...])
    x1 = _layer_norm(DN_ALPHA * x_ref[...] + mixed, g1_ref[...], b1_ref[...])
    x1_ref[...] = x1
    x_hi, x_lo = _split2(x1)
    w_hi, w_lo = wr_ref[0], wr_ref[1]
    logits = _dot(x_hi, w_hi) + _dot(x_hi, w_lo) + _dot(x_lo, w_hi) + br_ref[...]
    tm = logits.shape[0]
    lane = lax.broadcasted_iota(jnp.int32, (tm, LANES), 1)
    glog = jnp.where(lane < N_GROUPS, logits, NEG_INF)
    gmax = jnp.max(glog, axis=-1, keepdims=True)
    grp = jnp.min(jnp.where(glog == gmax, lane, LANES), axis=-1, keepdims=True)
    p_grp = 1.0 / jnp.sum(jnp.exp(glog - gmax), axis=-1, keepdims=True)
    e_lo = N_GROUPS + grp * EXPERTS_PER_GROUP
    elog = jnp.where(jnp.logical_and(lane >= e_lo, lane < e_lo + EXPERTS_PER_GROUP), logits, NEG_INF)
    m1 = jnp.max(elog, axis=-1, keepdims=True)
    i1 = jnp.min(jnp.where(elog == m1, lane, LANES), axis=-1, keepdims=True)
    elog2 = jnp.where(lane == i1, NEG_INF, elog)
    m2 = jnp.max(elog2, axis=-1, keepdims=True)
    i2 = jnp.min(jnp.where(elog2 == m2, lane, LANES), axis=-1, keepdims=True)
    e2 = jnp.exp(m2 - m1)
    w1 = p_grp / (1.0 + e2)
    w2 = p_grp * e2 / (1.0 + e2)
    oh1 = jnp.where(lane == i1 - N_GROUPS, 1.0, 0.0)
    oh2 = jnp.where(lane == i2 - N_GROUPS, 1.0, 0.0)
    both = oh1 + oh2
    before = _dot(tri_ref[...], both.astype(BF16))
    r1 = jnp.sum(before * oh1, axis=-1, keepdims=True)
    r2 = jnp.sum(before * oh2, axis=-1, keepdims=True)
    cnt_ref[0] = jnp.sum(both, axis=0, keepdims=True)
    cols = ((i1 - N_GROUPS).astype(F32), (i2 - N_GROUPS).astype(F32), w1, w2, r1, r2)
    route = jnp.zeros((tm, LANES), F32)
    for li, col in enumerate(cols):
        route = jnp.where(lane == li, col, route)
    route_ref[...] = route


def _merge(hg_o, mb_o, gates, x2, wa, wb, wo, g1, b1, wr, br, tm):
    n, d = x2.shape
    wa_n = hg_o.shape[1]
    wb_n = mb_o.shape[1]
    tri = jnp.asarray(np.tril(np.ones((tm, tm), np.float32), -1), BF16)

    def full(a):
        nd = a.ndim
        return pl.BlockSpec(a.shape, lambda i: (0,) * nd)

    return pl.pallas_call(
        _merge_kernel,
        grid=(n // tm,),
        in_specs=[
            pl.BlockSpec((tm, wa_n), lambda i: (i, 0)),
            pl.BlockSpec((tm, wb_n), lambda i: (i, 0)),
            pl.BlockSpec((tm, d), lambda i: (i, 0)),
            pl.BlockSpec((tm, d), lambda i: (i, 1)),
            pl.BlockSpec((tm, d), lambda i: (i, 0)),
            full(wa), full(wb), full(wo), full(g1), full(b1), full(wr), full(br), full(tri),
        ],
        out_specs=[
            pl.BlockSpec((tm, d), lambda i: (i, 0)),
            pl.BlockSpec((tm, LANES), lambda i: (i, 0)),
            pl.BlockSpec((1, 1, LANES), lambda i: (i, 0, 0)),
        ],
        out_shape=[
            jax.ShapeDtypeStruct((n, d), F32),
            jax.ShapeDtypeStruct((n, LANES), F32),
            jax.ShapeDtypeStruct((n // tm, 1, LANES), F32),
        ],
        compiler_params=pltpu.CompilerParams(
            dimension_semantics=("parallel",), vmem_limit_bytes=VMEM_LIMIT),
        name="merge_ln1_router",
    )(hg_o, mb_o, gates, gates, x2, wa, wb, wo, g1, b1, wr, br, tri)


def _moe_kernel(cnt_ref, base_ref, sbase_ref, re_ref, rr_ref, rw_ref, x1_hbm, wgu_ref, wd_ref, g2_ref, b2_ref,
                out_hbm, y_ref, xs_ref, sem, *, sub_tokens, ln_rows):
    ti = pl.program_id(0)
    e = pl.program_id(1)
    n_tiles = pl.num_programs(0)
    tile, d = y_ref.shape
    n_sub = tile // sub_tokens

    def out_copy(tidx):
        return pltpu.make_async_copy(y_ref, out_hbm.at[pl.ds(tidx * tile, tile), :], sem.at[1])

    def position(a):
        sub = lax.div(a, jnp.int32(TOP_K * sub_tokens))
        return sbase_ref[(ti * n_sub + sub) * N_EXPERTS + re_ref[0, 0, a]] + rr_ref[0, 0, a]

    @pl.when(e == 0)
    def _():
        @pl.when(ti > 0)
        def _():
            out_copy(ti - 1).wait()

        load = pltpu.make_async_copy(x1_hbm.at[pl.ds(ti * tile, tile), :], y_ref, sem.at[0])
        load.start()

        @pl.when(ti == 0)
        def _():
            def zero(c, carry):
                r0 = pl.multiple_of(c * ln_rows, ln_rows)
                xs_ref[pl.ds(r0, ln_rows), :] = jnp.zeros((ln_rows, d), F32)
                return carry
            lax.fori_loop(0, xs_ref.shape[0] // ln_rows, zero, 0)

        load.wait()

        def sort_rows(tok, carry):
            row = y_ref[pl.ds(tok, 1), :]
            xs_ref[pl.ds(position(TOP_K * tok), 1), :] = row
            xs_ref[pl.ds(position(TOP_K * tok + 1), 1), :] = row
            return carry
        lax.fori_loop(0, tile, sort_rows, 0, unroll=4)

        def scale(c, carry):
            r0 = pl.multiple_of(c * ln_rows, ln_rows)
            y_ref[pl.ds(r0, ln_rows), :] = DN_ALPHA * y_ref[pl.ds(r0, ln_rows), :]
            return carry
        lax.fori_loop(0, tile // ln_rows, scale, 0)

    cnt = cnt_ref[ti * N_EXPERTS + e]
    base = base_ref[ti * N_EXPERTS + e]

    def expert_block(b, carry):
        r0 = pl.multiple_of(base + b * MOE_ROWS, MOE_PAD)
        xb = xs_ref[pl.ds(r0, MOE_ROWS), :].astype(BF16)
        gu = _dot(xb, wgu_ref[0])
        gate = gu[:, :EXPERT_HIDDEN]
        up = gu[:, EXPERT_HIDDEN:]
        hdn = (gate * jax.nn.sigmoid(gate) * up).astype(BF16)
        ys = _dot(hdn, wd_ref[0])
        n_valid = cnt - b * MOE_ROWS
        for q in range(MOE_ROWS // MOE_PAD):
            @pl.when(n_valid > q * MOE_PAD)
            def _():
                xs_ref[pl.ds(r0 + q * MOE_PAD, MOE_PAD), :] = ys[q * MOE_PAD:(q + 1) * MOE_PAD]
        return carry
    lax.fori_loop(0, lax.div(cnt + MOE_ROWS - 1, jnp.int32(MOE_ROWS)), expert_block, 0)

    @pl.when(e == N_EXPERTS - 1)
    def _():
        def combine(tok, carry):
            a = TOP_K * tok
            y_ref[pl.ds(tok, 1), :] = (y_ref[pl.ds(tok, 1), :]
                                       + rw_ref[0, 0, a] * xs_ref[pl.ds(position(a), 1), :]
                                       + rw_ref[0, 0, a + 1] * xs_ref[pl.ds(position(a + 1), 1), :])
            return carry
        lax.fori_loop(0, tile, combine, 0, unroll=4)

        def norm(c, carry):
            r0 = pl.multiple_of(c * ln_rows, ln_rows)
            y_ref[pl.ds(r0, ln_rows), :] = _layer_norm(y_ref[pl.ds(r0, ln_rows), :], g2_ref[...], b2_ref[...])
            return carry
        lax.fori_loop(0, tile // ln_rows, norm, 0)
        out_copy(ti).start()

        @pl.when(ti == n_tiles - 1)
        def _():
            out_copy(ti).wait()


def _moe(x1, route, cnt_sub, wgu_bf, wd_bf, g2, b2, sub_tokens):
    n, d = x1.shape
    tile = min(MOE_TILE, n)
    assert n % tile == 0 and tile % sub_tokens == 0
    n_tiles = n // tile
    n_sub = tile // sub_tokens
    ln_rows = min(256, tile)
    c_sub = cnt_sub[:, 0, :N_EXPERTS].astype(jnp.int32).reshape(n_tiles, n_sub, N_EXPERTS)
    cnt = c_sub.sum(axis=1)
    padded = (cnt + MOE_PAD - 1) // MOE_PAD * MOE_PAD
    base = jnp.cumsum(padded, axis=1) - padded
    sbase = base[:, None, :] + jnp.cumsum(c_sub, axis=1) - c_sub
    r_exp = route[:, 0:TOP_K].astype(jnp.int32).reshape(n_tiles, 1, TOP_K * tile)
    r_wgt = route[:, TOP_K:2 * TOP_K].reshape(n_tiles, 1, TOP_K * tile)
    r_rank = route[:, 2 * TOP_K:3 * TOP_K].astype(jnp.int32).reshape(n_tiles, 1, TOP_K * tile)
    xs_rows = TOP_K * tile + N_EXPERTS * MOE_PAD + MOE_ROWS
    xs_rows = -(-xs_rows // ln_rows) * ln_rows

    def smem_spec():
        return pl.BlockSpec((1, 1, TOP_K * tile), lambda t, e, *_: (t, 0, 0), memory_space=pltpu.SMEM)

    grid_spec = pltpu.PrefetchScalarGridSpec(
        num_scalar_prefetch=3,
        grid=(n_tiles, N_EXPERTS),
        in_specs=[
            smem_spec(), smem_spec(), smem_spec(),
            pl.BlockSpec(memory_space=pl.ANY),
            pl.BlockSpec((1, d, 2 * EXPERT_HIDDEN), lambda t, e, *_: (e, 0, 0)),
            pl.BlockSpec((1, EXPERT_HIDDEN, d), lambda t, e, *_: (e, 0, 0)),
            pl.BlockSpec((1, d), lambda t, e, *_: (0, 0)),
            pl.BlockSpec((1, d), lambda t, e, *_: (0, 0)),
        ],
        out_specs=pl.BlockSpec(memory_space=pl.ANY),
        scratch_shapes=[
            pltpu.VMEM((tile, d), F32),
            pltpu.VMEM((xs_rows, d), F32),
            pltpu.SemaphoreType.DMA((2,)),
        ],
    )
    return pl.pallas_call(
        functools.partial(_moe_kernel, sub_tokens=sub_tokens, ln_rows=ln_rows),
        grid_spec=grid_spec,
        out_shape=jax.ShapeDtypeStruct((n, d), F32),
        compiler_params=pltpu.CompilerParams(
            dimension_semantics=("arbitrary", "arbitrary"), vmem_limit_bytes=VMEM_LIMIT),
        name="moe_ln2",
    )(cnt.reshape(-1), base.reshape(-1), sbase.reshape(-1), r_exp, r_rank, r_wgt, x1, wgu_bf, wd_bf, g2, b2)


def _block(x, w_in, b_in, lower_bound, hg_norm_g, rel_bias, w_proj_a, w_proj_b, w_out, ln1_g, ln1_b,
           w_group, b_group, w_expert, b_expert, w_gate_up, w_down, ln2_g, ln2_b, *, tm_proj, tm_merge):
    b, s, d = x.shape
    n = b * s
    n_hg = 4 * HG_HEADS * HG_DK
    n_mb = 3 * MB_HEADS * MB_DH
    n_gt = 2 * d
    x2 = x.reshape(n, d)
    hg, mb, gates = _in_proj(x2, w_in.astype(BF16), b_in.reshape(1, -1), n_hg, n_mb, n_gt, tm_proj)
    hg_o = _hgrn(hg.reshape(b, s, n_hg), lower_bound.reshape(1, -1), hg_norm_g.reshape(1, -1))
    mb_o = _moba(mb.reshape(b, s, n_mb), rel_bias)
    w_r = jnp.zeros((d, LANES), F32).at[:, :N_GROUPS].set(w_group).at[:, N_GROUPS:N_GROUPS + N_EXPERTS].set(w_expert)
    w_r_hi = w_r.astype(BF16)
    w_r_lo = (w_r - w_r_hi.astype(F32)).astype(BF16)
    b_r = jnp.zeros((1, LANES), F32).at[0, :N_GROUPS].set(b_group).at[0, N_GROUPS:N_GROUPS + N_EXPERTS].set(b_expert)
    x1, route, cnt_sub = _merge(hg_o.reshape(n, -1), mb_o.reshape(n, -1), gates, x2,
                                w_proj_a.astype(BF16), w_proj_b.astype(BF16), w_out.astype(BF16),
                                ln1_g.reshape(1, d), ln1_b.reshape(1, d), jnp.stack([w_r_hi, w_r_lo]), b_r, tm_merge)
    out = _moe(x1, route, cnt_sub, w_gate_up.astype(BF16), w_down.astype(BF16),
               ln2_g.reshape(1, d), ln2_b.reshape(1, d), tm_merge)
    return out.reshape(b, s, d)


def kernel(x, w_in, b_in, lb_logits, hg_norm_g, rel_bias, w_proj_a, w_proj_b, w_out, ln1_g, ln1_b, w_group,
           b_group, w_expert, b_expert, w_gate_up, w_down, ln2_g, ln2_b):
    lower_bounds = jnp.cumsum(jax.nn.softmax(lb_logits.astype(F32), axis=0), axis=0)
    l = 0
    return _block(x, w_in[l], b_in[l], lower_bounds[l], hg_norm_g[l], rel_bias, w_proj_a[l], w_proj_b[l],
                  w_out[l], ln1_g[l], ln1_b[l], w_group[l], b_group[l], w_expert[l], b_expert[l],
                  w_gate_up[l], w_down[l], ln2_g[l], ln2_b[l], tm_proj=256, tm_merge=256)
```

```python
import functools
import math

import numpy as np
import jax
import jax.numpy as jnp
from jax import lax
from jax.experimental import pallas as pl
from jax.experimental.pallas import tpu as pltpu

F32 = jnp.float32
BF16 = jnp.bfloat16

HG_HEADS = 4
HG_DK = 128
HG_CHUNK = 64
MB_HEADS = 8
MB_DH = 64
MB_BLOCK = 256
MB_TOPK = 3
GATE_ROWS = 16
REL_BUCKETS = 32
REL_MAX_DIST = 128
N_GROUPS = 4
EXPERTS_PER_GROUP = 8
N_EXPERTS = N_GROUPS * EXPERTS_PER_GROUP
TOP_K = 2
EXPERT_HIDDEN = 512
MOE_TILE = 2048
MOE_ROWS = 160
MOE_PAD = 32
DEPTH = 1
DN_ALPHA = (2.0 * DEPTH) ** 0.25
NORM_EPS = 1e-5
LANES = 128
SUBLANES = 8
VMEM_LIMIT = 56 * 1024 * 1024
NEG_INF = float("-inf")


def _split2(a):
    hi = a.astype(BF16)
    lo = (a - hi.astype(F32)).astype(BF16)
    return hi, lo


def _split3(a):
    hi = a.astype(BF16)
    r = a - hi.astype(F32)
    mid = r.astype(BF16)
    lo = (r - mid.astype(F32)).astype(BF16)
    return hi, mid, lo


def _dot_nt(a, b):
    return lax.dot_general(a, b, (((1,), (1,)), ((), ())), preferred_element_type=F32)


def _dot_tn(a, b):
    return lax.dot_general(a, b, (((0,), (0,)), ((), ())), preferred_element_type=F32)


def _dot(a, b):
    return jnp.dot(a, b, preferred_element_type=F32)


def _in_proj_kernel(x_ref, w_ref, b_ref, hg_ref, mb_ref, gt_ref, *, col_chunk):
    xb = x_ref[...].astype(BF16)
    outs = ((hg_ref, 0), (mb_ref, hg_ref.shape[1]), (gt_ref, hg_ref.shape[1] + mb_ref.shape[1]))
    for o_ref, base in outs:
        for c0 in range(0, o_ref.shape[1], col_chunk):
            acc = _dot(xb, w_ref[:, base + c0:base + c0 + col_chunk])
            o_ref[:, c0:c0 + col_chunk] = acc + b_ref[:, base + c0:base + c0 + col_chunk]


def _in_proj(x2, w_bf, b_in, n_hg, n_mb, n_gt, tm):
    n, d = x2.shape
    cols = w_bf.shape[1]
    return pl.pallas_call(
        functools.partial(_in_proj_kernel, col_chunk=512),
        grid=(n // tm,),
        in_specs=[
            pl.BlockSpec((tm, d), lambda i: (i, 0)),
            pl.BlockSpec((d, cols), lambda i: (0, 0)),
            pl.BlockSpec((1, cols), lambda i: (0, 0)),
        ],
        out_specs=[
            pl.BlockSpec((tm, n_hg), lambda i: (i, 0)),
            pl.BlockSpec((tm, n_mb), lambda i: (i, 0)),
            pl.BlockSpec((tm, n_gt), lambda i: (i, 0)),
        ],
        out_shape=[
            jax.ShapeDtypeStruct((n, n_hg), F32),
            jax.ShapeDtypeStruct((n, n_mb), F32),
            jax.ShapeDtypeStruct((n, n_gt), F32),
        ],
        compiler_params=pltpu.CompilerParams(
            dimension_semantics=("parallel",), vmem_limit_bytes=VMEM_LIMIT),
        name="in_proj",
    )(x2, w_bf, b_in)


def _hgrn_tables():
    c = HG_CHUNK
    levels = [c >> (i + 1) for i in range(int(math.log2(c)))]
    masks = np.zeros((len(levels) + 1, c, c), np.float32)
    for li, m in enumerate(levels):
        for r in range(c):
            c0 = (r // (2 * m)) * (2 * m)
            if r - c0 >= m:
                masks[li, r, c0:c0 + m] = 1.0
    masks[len(levels)] = np.eye(c, dtype=np.float32)
    return np.tril(np.ones((c, c), np.float32)), masks, tuple(levels)


def _level_ref_rows(p, m):
    c, w = p.shape
    if 2 * m > SUBLANES:
        parts = [jnp.broadcast_to(p[c0 + m - 1:c0 + m, :], (2 * m, w)) for c0 in range(0, c, 2 * m)]
        return parts[0] if len(parts) == 1 else jnp.concatenate(parts, axis=0)
    p3 = p.reshape(c // SUBLANES, SUBLANES, w)
    sub = lax.broadcasted_iota(jnp.int32, p3.shape, 1)
    out = None
    for c0 in range(0, SUBLANES, 2 * m):
        b = jnp.broadcast_to(p3[:, c0 + m - 1:c0 + m, :], p3.shape)
        out = b if out is None else jnp.where(sub >= c0, b, out)
    return out.reshape(c, w)


def _hgrn_kernel(q_ref, f_ref, i_ref, g_ref, lb_ref, ng_ref, tril_ref, msk_ref, o_ref,
                 *, n_chunks, levels, group):
    c = HG_CHUNK
    dk = HG_DK
    assert n_chunks % group == 0
    lb = jnp.concatenate([lb_ref[...]] * group, axis=1)
    oml = 1.0 - lb
    ng = ng_ref[...]
    tril = tril_ref[...]
    n_levels = len(levels)

    def load(ref, r0):
        return jnp.concatenate([ref[0, pl.ds(r0 + u * c, c), :] for u in range(group)], axis=1)

    def lanes(a, u):
        return a[:, u * dk:(u + 1) * dk]

    def body(gi, st):
        r0 = pl.multiple_of(gi * (group * c), group * c)
        z = load(f_ref, r0)
        qr = load(q_ref, r0)
        lf = jnp.log(lb + oml * jax.nn.sigmoid(z))
        kk = oml * jax.nn.sigmoid(-z)
        qf = qr * jax.nn.sigmoid(qr)
        l_hi, l_mid, l_lo = _split3(lf)
        p = _dot(tril, l_hi) + _dot(tril, l_mid) + _dot(tril, l_lo)
        b_end = p[c - 1:c, :]
        qb = (qf * jnp.exp(p)).astype(BF16)
        kd = (kk * jnp.exp(b_end - p)).astype(BF16)
        dec = jnp.exp(b_end)
        qh = qf.astype(BF16)
        kh = kk.astype(BF16)
        scores = [msk_ref[n_levels] * _dot_nt(lanes(qh, u), lanes(kh, u)) for u in range(group)]
        for li, m in enumerate(levels):
            ex = jnp.exp(-jnp.abs(p - _level_ref_rows(p, m)))
            ql = (qf * ex).astype(BF16)
            kl = (kk * ex).astype(BF16)
            for u in range(group):
                scores[u] = scores[u] + msk_ref[li] * _dot_nt(lanes(ql, u), lanes(kl, u))
        for u in range(group):
            rows = pl.ds(r0 + u * c, c)
            vb = i_ref[0, rows, :].astype(BF16)
            g = g_ref[0, rows, :]
            o = _dot_nt(lanes(qb, u), st.astype(BF16)) + _dot(scores[u].astype(BF16), vb)
            st = st * lanes(dec, u) + _dot_tn(vb, lanes(kd, u))
            o = o * lax.rsqrt(jnp.mean(o * o, axis=-1, keepdims=True) + NORM_EPS)
            o_ref[0, rows, :] = o * ng * (g * jax.nn.sigmoid(g))
        return st

    lax.fori_loop(0, n_chunks // group, body, jnp.zeros((HG_DK, HG_DK), F32))


def _hgrn(hg3, lb_row, ng_row):
    b, s, _ = hg3.shape
    tril, masks, levels = _hgrn_tables()
    mst = jnp.asarray(tril, BF16)
    msk = jnp.asarray(masks, F32)
    h = HG_HEADS

    def col(off):
        return pl.BlockSpec((1, s, HG_DK), lambda bi, hi: (bi, 0, off + hi))

    return pl.pallas_call(
        functools.partial(_hgrn_kernel, n_chunks=s // HG_CHUNK, levels=levels, group=4),
        grid=(b, h),
        in_specs=[
            col(0), col(h), col(2 * h), col(3 * h),
            pl.BlockSpec((1, HG_DK), lambda bi, hi: (0, hi)),
            pl.BlockSpec((1, HG_DK), lambda bi, hi: (0, hi)),
            pl.BlockSpec(mst.shape, lambda bi, hi: (0, 0)),
            pl.BlockSpec(msk.shape, lambda bi, hi: (0, 0, 0)),
        ],
        out_specs=pl.BlockSpec((1, s, HG_DK), lambda bi, hi: (bi, 0, hi)),
        out_shape=jax.ShapeDtypeStruct((b, s, h * HG_DK), F32),
        compiler_params=pltpu.CompilerParams(
            dimension_semantics=("parallel", "parallel"), vmem_limit_bytes=VMEM_LIMIT),
        name="hgrn2",
    )(hg3, hg3, hg3, hg3, lb_row, ng_row, mst, msk)


def _t5_bucket_np(dist):
    max_exact = REL_BUCKETS // 2
    d = np.maximum(dist, 1).astype(np.float32)
    log_part = max_exact + (np.log(d / np.float32(max_exact)) / np.float32(math.log(REL_MAX_DIST / max_exact))
                            * np.float32(REL_BUCKETS - max_exact)).astype(np.int32)
    return np.where(dist < max_exact, dist, np.minimum(log_part, REL_BUCKETS - 1))


def _moba_kernel(q_ref, k_ref, v_ref, avg_ref, bkt_ref, rb_ref, o_ref, own_ref, prev_ref, *, n_blocks):
    blk = MB_BLOCK
    scale = MB_DH ** -0.5
    avg = avg_ref[...]
    hp = LANES // MB_DH

    @pl.when(pl.program_id(1) == 0)
    def _():
        causal = (lax.broadcasted_iota(jnp.int32, (blk, blk), 0) <= lax.broadcasted_iota(jnp.int32, (blk, blk), 1))
        for hh in range(hp):
            head = pl.program_id(0) * hp + hh
            own_t = jnp.zeros((blk, blk), F32)
            prev_t = jnp.zeros((blk, blk), F32)
            for bk in range(REL_BUCKETS):
                val = rb_ref[bk, head]
                own_t = jnp.where(bkt_ref[0] == bk, val, own_t)
                prev_t = jnp.where(bkt_ref[1] == bk, val, prev_t)
            own_ref[hh] = jnp.where(causal, own_t, NEG_INF)
            prev_ref[hh] = prev_t

    grow = lax.broadcasted_iota(jnp.int32, (GATE_ROWS, blk), 0)
    vt_all = v_ref[0].T
    outs = []
    for hh in range(hp):
        ls = slice(hh * MB_DH, (hh + 1) * MB_DH)
        k_h = k_ref[0, :, ls]
        k_hi, k_lo = _split2(k_h)
        k_mean = (_dot(avg, k_hi) + _dot(avg, k_lo))[:GATE_ROWS]
        km_hi, km_lo = _split2(k_mean)
        kb = k_h.astype(BF16)
        vt = vt_all[hh * MB_DH:(hh + 1) * MB_DH, :].astype(BF16)
        own_bias = own_ref[hh]
        prev_bias = prev_ref[hh]
        far_bias = rb_ref[REL_BUCKETS - 1, pl.program_id(0) * hp + hh]
        o_blocks = []
        for i in range(n_blocks):
            qi = q_ref[0, i * blk:(i + 1) * blk, ls] * scale
            st = _dot_nt(kb[:(i + 1) * blk], qi.astype(BF16))
            sel = None
            if i > 0:
                q_hi, q_lo = _split2(qi)
                gate = _dot_nt(km_hi, q_hi) + _dot_nt(km_lo, q_hi) + _dot_nt(km_hi, q_lo)
                gate = jnp.where(grow < i, gate, NEG_INF)
                rank = jnp.zeros((GATE_ROWS, blk), F32)
                if i > MB_TOPK:
                    for j2 in range(i):
                        gj = jnp.broadcast_to(gate[j2:j2 + 1, :], (GATE_ROWS, blk))
                        tie = jnp.where(grow > j2, 1.0, 0.0)
                        rank = rank + jnp.where(gj > gate, 1.0, jnp.where(gj == gate, tie, 0.0))
                sel = jnp.where(rank < MB_TOPK, gate, NEG_INF) > NEG_INF
            pieces = []
            for j in range(i + 1):
                sj = st[j * blk:(j + 1) * blk, :]
                if j == i:
                    pieces.append(sj + own_bias)
                elif j == i - 1:
                    pieces.append(sj + prev_bias + jnp.where(sel[j:j + 1, :], 0.0, NEG_INF))
                else:
                    pieces.append(sj + jnp.where(sel[j:j + 1, :], far_bias, NEG_INF))

            def fold(a):
                return a.reshape(blk // SUBLANES, SUBLANES, blk)

            mx8 = fold(pieces[0]).max(axis=0)
            for p in pieces[1:]:
                mx8 = jnp.maximum(mx8, fold(p).max(axis=0))
            mx = mx8.max(axis=0, keepdims=True)
            den8 = jnp.zeros((SUBLANES, blk), F32)
            acc = jnp.zeros((MB_DH, blk), F32)
            for j, p in enumerate(pieces):
                e = jnp.exp(p - mx)
                den8 = den8 + fold(e).sum(axis=0)
                acc = acc + _dot(vt[:, j * blk:(j + 1) * blk], e.astype(BF16))
            o_blocks.append(acc / den8.sum(axis=0, keepdims=True))
        outs.append(o_blocks)
    for i in range(n_blocks):
        o_ref[0, i * blk:(i + 1) * blk, :] = jnp.concatenate([o[i] for o in outs], axis=0).T


def _moba(mb3, rel_bias):
    b, s, _ = mb3.shape
    assert s % MB_BLOCK == 0
    nb = s // MB_BLOCK
    hp = LANES // MB_DH
    n_hp = MB_HEADS // hp
    t = np.arange(MB_BLOCK)
    d_own = t[:, None] - t[None, :]
    buckets = np.stack([_t5_bucket_np(np.maximum(d_own, 0)), _t5_bucket_np(d_own + MB_BLOCK)]).astype(np.int32)
    assert _t5_bucket_np(np.array([MB_BLOCK + 1]))[0] == REL_BUCKETS - 1
    assert nb <= GATE_ROWS
    bkt = jnp.asarray(buckets.transpose(0, 2, 1))
    avg_np = np.zeros((LANES, s), np.float32)
    for j in range(nb):
        avg_np[j, j * MB_BLOCK:(j + 1) * MB_BLOCK] = 1.0 / MB_BLOCK
    avg = jnp.asarray(avg_np, BF16)

    def col(off):
        return pl.BlockSpec((1, s, LANES), lambda hi, bi: (bi, 0, off + hi))

    return pl.pallas_call(
        functools.partial(_moba_kernel, n_blocks=nb),
        grid=(n_hp, b),
        in_specs=[
            col(0), col(n_hp), col(2 * n_hp),
            pl.BlockSpec(avg.shape, lambda hi, bi: (0, 0)),
            pl.BlockSpec(bkt.shape, lambda hi, bi: (0, 0, 0)),
            pl.BlockSpec(memory_space=pltpu.SMEM),
        ],
        out_specs=pl.BlockSpec((1, s, LANES), lambda hi, bi: (bi, 0, hi)),
        out_shape=jax.ShapeDtypeStruct((b, s, MB_HEADS * MB_DH), F32),
        scratch_shapes=[pltpu.VMEM((hp, MB_BLOCK, MB_BLOCK), F32), pltpu.VMEM((hp, MB_BLOCK, MB_BLOCK), F32)],
        compiler_params=pltpu.CompilerParams(
            dimension_semantics=("parallel", "arbitrary"), vmem_limit_bytes=VMEM_LIMIT),
        name="moba",
    )(mb3, mb3, mb3, avg, bkt, rel_bias.astype(F32))


def _layer_norm(x, g, b):
    mu = jnp.mean(x, axis=-1, keepdims=True)
    xc = x - mu
    var = jnp.mean(xc * xc, axis=-1, keepdims=True)
    return xc * lax.rsqrt(var + NORM_EPS) * g + b


def _merge_kernel(hg_ref, mb_ref, ga_ref, gb_ref, x_ref, wa_ref, wb_ref, wo_ref, g1_ref, b1_ref,
                  wr_ref, br_ref, tri_ref, x1_ref, route_ref, cnt_ref, run_ref, *, blocks_per_moe_tile):
    ya = _dot(hg_ref[...].astype(BF16), wa_ref[...])
    yb = _dot(mb_ref[...].astype(BF16), wb_ref[...])
    mixed_in = jax.nn.sigmoid(ga_ref[...]) * ya + jax.nn.sigmoid(gb_ref[...]) * yb
    mixed = _dot(mixed_in.astype(BF16), wo_ref[...])
    x1 = _layer_norm(DN_ALPHA * x_ref[...] + mixed, g1_ref[...], b1_ref[...])
    x1_ref[...] = x1
    x_hi, x_lo = _split2(x1)
    w_hi, w_lo = wr_ref[0], wr_ref[1]
    logits = _dot(x_hi, w_hi) + _dot(x_hi, w_lo) + _dot(x_lo, w_hi) + br_ref[...]
    tm = logits.shape[0]
    lane = lax.broadcasted_iota(jnp.int32, (tm, LANES), 1)
    glog = jnp.where(lane < N_GROUPS, logits, NEG_INF)
    gmax = jnp.max(glog, axis=-1, keepdims=True)
    grp = jnp.min(jnp.where(glog == gmax, lane, LANES), axis=-1, keepdims=True)
    p_grp = 1.0 / jnp.sum(jnp.exp(glog - gmax), axis=-1, keepdims=True)
    e_lo = N_GROUPS + grp * EXPERTS_PER_GROUP
    elog = jnp.where(jnp.logical_and(lane >= e_lo, lane < e_lo + EXPERTS_PER_GROUP), logits, NEG_INF)
    m1 = jnp.max(elog, axis=-1, keepdims=True)
    i1 = jnp.min(jnp.where(elog == m1, lane, LANES), axis=-1, keepdims=True)
    elog2 = jnp.where(lane == i1, NEG_INF, elog)
    m2 = jnp.max(elog2, axis=-1, keepdims=True)
    i2 = jnp.min(jnp.where(elog2 == m2, lane, LANES), axis=-1, keepdims=True)
    e2 = jnp.exp(m2 - m1)
    w1 = p_grp / (1.0 + e2)
    w2 = p_grp * e2 / (1.0 + e2)
    @pl.when(pl.program_id(0) % blocks_per_moe_tile == 0)
    def _():
        run_ref[...] = jnp.zeros_like(run_ref)

    oh1 = jnp.where(lane == i1 - N_GROUPS, 1.0, 0.0)
    oh2 = jnp.where(lane == i2 - N_GROUPS, 1.0, 0.0)
    both = oh1 + oh2
    before = _dot(tri_ref[...], both.astype(BF16)) + run_ref[...]
    r1 = jnp.sum(before * oh1, axis=-1, keepdims=True)
    r2 = jnp.sum(before * oh2, axis=-1, keepdims=True)
    run_ref[...] = run_ref[...] + jnp.sum(both, axis=0, keepdims=True)
    cnt_ref[0] = run_ref[...]
    cols = ((i1 - N_GROUPS).astype(F32), (i2 - N_GROUPS).astype(F32), w1, w2, r1, r2)
    route = jnp.zeros((tm, LANES), F32)
    for li, col in enumerate(cols):
        route = jnp.where(lane == li, col, route)
    route_ref[...] = route


def _merge(hg_o, mb_o, gates, x2, wa, wb, wo, g1, b1, wr, br, tm, moe_tile):
    n, d = x2.shape
    wa_n = hg_o.shape[1]
    wb_n = mb_o.shape[1]
    assert moe_tile % tm == 0
    tri = jnp.asarray(np.tril(np.ones((tm, tm), np.float32), -1), BF16)

    def full(a):
        nd = a.ndim
        return pl.BlockSpec(a.shape, lambda i: (0,) * nd)

    return pl.pallas_call(
        functools.partial(_merge_kernel, blocks_per_moe_tile=moe_tile // tm),
        grid=(n // tm,),
        in_specs=[
            pl.BlockSpec((tm, wa_n), lambda i: (i, 0)),
            pl.BlockSpec((tm, wb_n), lambda i: (i, 0)),
            pl.BlockSpec((tm, d), lambda i: (i, 0)),
            pl.BlockSpec((tm, d), lambda i: (i, 1)),
            pl.BlockSpec((tm, d), lambda i: (i, 0)),
            full(wa), full(wb), full(wo), full(g1), full(b1), full(wr), full(br), full(tri),
        ],
        out_specs=[
            pl.BlockSpec((tm, d), lambda i: (i, 0)),
            pl.BlockSpec((tm, LANES), lambda i: (i, 0)),
            pl.BlockSpec((1, 1, LANES), lambda i: (i, 0, 0)),
        ],
        out_shape=[
            jax.ShapeDtypeStruct((n, d), F32),
            jax.ShapeDtypeStruct((n, LANES), F32),
            jax.ShapeDtypeStruct((n // tm, 1, LANES), F32),
        ],
        scratch_shapes=[pltpu.VMEM((1, LANES), F32)],
        compiler_params=pltpu.CompilerParams(
            dimension_semantics=("arbitrary",), vmem_limit_bytes=VMEM_LIMIT),
        name="merge_ln1_router",
    )(hg_o, mb_o, gates, gates, x2, wa, wb, wo, g1, b1, wr, br, tri)


def _moe_kernel(cnt_ref, base_ref, pos_ref, rw_ref, x1_hbm, wgu_ref, wd_ref, g2_ref, b2_ref,
                out_hbm, y_ref, xs_ref, sem, *, ln_rows):
    ti = pl.program_id(0)
    e = pl.program_id(1)
    n_tiles = pl.num_programs(0)
    tile, d = y_ref.shape

    def out_copy(tidx):
        return pltpu.make_async_copy(y_ref, out_hbm.at[pl.ds(tidx * tile, tile), :], sem.at[1])

    def position(a):
        return pos_ref[0, 0, a]

    @pl.when(e == 0)
    def _():
        @pl.when(ti > 0)
        def _():
            out_copy(ti - 1).wait()

        load = pltpu.make_async_copy(x1_hbm.at[pl.ds(ti * tile, tile), :], y_ref, sem.at[0])
        load.start()

        @pl.when(ti == 0)
        def _():
            def zero(c, carry):
                r0 = pl.multiple_of(c * ln_rows, ln_rows)
                xs_ref[pl.ds(r0, ln_rows), :] = jnp.zeros((ln_rows, d), F32)
                return carry
            lax.fori_loop(0, xs_ref.shape[0] // ln_rows, zero, 0)

        load.wait()

        def sort_rows(grp, carry):
            t0 = pl.multiple_of(grp * SUBLANES, SUBLANES)
            for k in range(SUBLANES):
                row = y_ref[pl.ds(t0 + k, 1), :]
                xs_ref[pl.ds(position(TOP_K * (t0 + k)), 1), :] = row
                xs_ref[pl.ds(position(TOP_K * (t0 + k) + 1), 1), :] = row
            return carry
        lax.fori_loop(0, tile // SUBLANES, sort_rows, 0)

        def scale(c, carry):
            r0 = pl.multiple_of(c * ln_rows, ln_rows)
            y_ref[pl.ds(r0, ln_rows), :] = DN_ALPHA * y_ref[pl.ds(r0, ln_rows), :]
            return carry
        lax.fori_loop(0, tile // ln_rows, scale, 0)

    cnt = cnt_ref[ti * N_EXPERTS + e]
    base = base_ref[ti * N_EXPERTS + e]

    def expert_block(b, carry):
        r0 = pl.multiple_of(base + b * MOE_ROWS, MOE_PAD)
        xb = xs_ref[pl.ds(r0, MOE_ROWS), :].astype(BF16)
        gu = _dot(xb, wgu_ref[0])
        gate = gu[:, :EXPERT_HIDDEN]
        up = gu[:, EXPERT_HIDDEN:]
        hdn = (gate * jax.nn.sigmoid(gate) * up).astype(BF16)
        ys = _dot(hdn, wd_ref[0])
        n_valid = cnt - b * MOE_ROWS
        for q in range(MOE_ROWS // MOE_PAD):
            @pl.when(n_valid > q * MOE_PAD)
            def _():
                xs_ref[pl.ds(r0 + q * MOE_PAD, MOE_PAD), :] = ys[q * MOE_PAD:(q + 1) * MOE_PAD]
        return carry
    lax.fori_loop(0, lax.div(cnt + MOE_ROWS - 1, jnp.int32(MOE_ROWS)), expert_block, 0)

    @pl.when(e == N_EXPERTS - 1)
    def _():
        def combine(grp, carry):
            t0 = pl.multiple_of(grp * SUBLANES, SUBLANES)
            for k in range(SUBLANES):
                a = TOP_K * (t0 + k)
                y_ref[pl.ds(t0 + k, 1), :] = (y_ref[pl.ds(t0 + k, 1), :]
                                              + rw_ref[0, 0, a] * xs_ref[pl.ds(position(a), 1), :]
                                              + rw_ref[0, 0, a + 1] * xs_ref[pl.ds(position(a + 1), 1), :])
            return carry
        lax.fori_loop(0, tile // SUBLANES, combine, 0)

        def norm(c, carry):
            r0 = pl.multiple_of(c * ln_rows, ln_rows)
            y_ref[pl.ds(r0, ln_rows), :] = _layer_norm(y_ref[pl.ds(r0, ln_rows), :], g2_ref[...], b2_ref[...])
            return carry
        lax.fori_loop(0, tile // ln_rows, norm, 0)
        out_copy(ti).start()

        @pl.when(ti == n_tiles - 1)
        def _():
            out_copy(ti).wait()


def _moe_tile(n):
    return min(MOE_TILE, n)


def _moe(x1, route, cnt_run, wgu_bf, wd_bf, g2, b2):
    n, d = x1.shape
    tile = _moe_tile(n)
    assert n % tile == 0
    n_tiles = n // tile
    ln_rows = min(256, tile)
    cnt = cnt_run.reshape(n_tiles, -1, LANES)[:, -1, :N_EXPERTS].astype(jnp.int32)
    padded = (cnt + MOE_PAD - 1) // MOE_PAD * MOE_PAD
    base = jnp.cumsum(padded, axis=1) - padded
    r_exp = route[:, 0:TOP_K].astype(jnp.int32).reshape(n_tiles, TOP_K * tile)
    r_rank = route[:, 2 * TOP_K:3 * TOP_K].astype(jnp.int32).reshape(n_tiles, TOP_K * tile)
    seg_start = jnp.sum(jnp.where(r_exp[:, :, None] == jnp.arange(N_EXPERTS, dtype=jnp.int32),
                                  base[:, None, :], 0), axis=-1)
    pos = (seg_start + r_rank).reshape(n_tiles, 1, TOP_K * tile)
    r_wgt = route[:, TOP_K:2 * TOP_K].reshape(n_tiles, 1, TOP_K * tile)
    xs_rows = TOP_K * tile + N_EXPERTS * MOE_PAD + MOE_ROWS
    xs_rows = -(-xs_rows // ln_rows) * ln_rows

    def smem_spec():
        return pl.BlockSpec((1, 1, TOP_K * tile), lambda t, e, *_: (t, 0, 0), memory_space=pltpu.SMEM)

    grid_spec = pltpu.PrefetchScalarGridSpec(
        num_scalar_prefetch=2,
        grid=(n_tiles, N_EXPERTS),
        in_specs=[
            smem_spec(), smem_spec(),
            pl.BlockSpec(memory_space=pl.ANY),
            pl.BlockSpec((1, d, 2 * EXPERT_HIDDEN), lambda t, e, *_: (e, 0, 0)),
            pl.BlockSpec((1, EXPERT_HIDDEN, d), lambda t, e, *_: (e, 0, 0)),
            pl.BlockSpec((1, d), lambda t, e, *_: (0, 0)),
            pl.BlockSpec((1, d), lambda t, e, *_: (0, 0)),
        ],
        out_specs=pl.BlockSpec(memory_space=pl.ANY),
        scratch_shapes=[
            pltpu.VMEM((tile, d), F32),
            pltpu.VMEM((xs_rows, d), F32),
            pltpu.SemaphoreType.DMA((2,)),
        ],
    )
    return pl.pallas_call(
        functools.partial(_moe_kernel, ln_rows=ln_rows),
        grid_spec=grid_spec,
        out_shape=jax.ShapeDtypeStruct((n, d), F32),
        compiler_params=pltpu.CompilerParams(
            dimension_semantics=("arbitrary", "arbitrary"), vmem_limit_bytes=VMEM_LIMIT),
        name="moe_ln2",
    )(cnt.reshape(-1), base.reshape(-1), pos, r_wgt, x1, wgu_bf, wd_bf, g2, b2)


def _block(x, w_in, b_in, lower_bound, hg_norm_g, rel_bias, w_proj_a, w_proj_b, w_out, ln1_g, ln1_b,
           w_group, b_group, w_expert, b_expert, w_gate_up, w_down, ln2_g, ln2_b, *, tm_proj, tm_merge):
    b, s, d = x.shape
    n = b * s
    n_hg = 4 * HG_HEADS * HG_DK
    n_mb = 3 * MB_HEADS * MB_DH
    n_gt = 2 * d
    x2 = x.reshape(n, d)
    hg, mb, gates = _in_proj(x2, w_in.astype(BF16), b_in.reshape(1, -1), n_hg, n_mb, n_gt, tm_proj)
    hg_o = _hgrn(hg.reshape(b, s, n_hg), lower_bound.reshape(1, -1), hg_norm_g.reshape(1, -1))
    mb_o = _moba(mb.reshape(b, s, n_mb), rel_bias)
    w_r = jnp.zeros((d, LANES), F32).at[:, :N_GROUPS].set(w_group).at[:, N_GROUPS:N_GROUPS + N_EXPERTS].set(w_expert)
    w_r_hi = w_r.astype(BF16)
    w_r_lo = (w_r - w_r_hi.astype(F32)).astype(BF16)
    b_r = jnp.zeros((1, LANES), F32).at[0, :N_GROUPS].set(b_group).at[0, N_GROUPS:N_GROUPS + N_EXPERTS].set(b_expert)
    tm_merge = min(tm_merge, n)
    x1, route, cnt_run = _merge(hg_o.reshape(n, -1), mb_o.reshape(n, -1), gates, x2,
                                w_proj_a.astype(BF16), w_proj_b.astype(BF16), w_out.astype(BF16),
                                ln1_g.reshape(1, d), ln1_b.reshape(1, d), jnp.stack([w_r_hi, w_r_lo]), b_r,
                                tm_merge, _moe_tile(n))
    out = _moe(x1, route, cnt_run, w_gate_up.astype(BF16), w_down.astype(BF16),
               ln2_g.reshape(1, d), ln2_b.reshape(1, d))
    return out.reshape(b, s, d)


def kernel(x, w_in, b_in, lb_logits, hg_norm_g, rel_bias, w_proj_a, w_proj_b, w_out, ln1_g, ln1_b, w_group,
           b_group, w_expert, b_expert, w_gate_up, w_down, ln2_g, ln2_b):
    lower_bounds = jnp.cumsum(jax.nn.softmax(lb_logits.astype(F32), axis=0), axis=0)
    l = 0
    return _block(x, w_in[l], b_in[l], lower_bounds[l], hg_norm_g[l], rel_bias, w_proj_a[l], w_proj_b[l],
                  w_out[l], ln1_g[l], ln1_b[l], w_group[l], b_group[l], w_expert[l], b_expert[l],
                  w_gate_up[l], w_down[l], ln2_g[l], ln2_b[l], tm_proj=256, tm_merge=512)
```

```python
import functools
import math

import numpy as np
import jax
import jax.numpy as jnp
from jax import lax
from jax.experimental import pallas as pl
from jax.experimental.pallas import tpu as pltpu

F32 = jnp.float32
BF16 = jnp.bfloat16

HG_HEADS = 4
HG_DK = 128
HG_CHUNK = 64
MB_HEADS = 8
MB_DH = 64
MB_BLOCK = 256
MB_TOPK = 3
GATE_ROWS = 16
REL_BUCKETS = 32
REL_MAX_DIST = 128
N_GROUPS = 4
EXPERTS_PER_GROUP = 8
N_EXPERTS = N_GROUPS * EXPERTS_PER_GROUP
TOP_K = 2
EXPERT_HIDDEN = 512
MOE_TILE = 1024
SEG_PAD = 16
SEG_BIG = 64
EXP_ROWS = 256
DEPTH = 1
DN_ALPHA = (2.0 * DEPTH) ** 0.25
NORM_EPS = 1e-5
LANES = 128
SUBLANES = 8
VMEM_LIMIT = 56 * 1024 * 1024
NEG_INF = float("-inf")


def _split2(a):
    hi = a.astype(BF16)
    lo = (a - hi.astype(F32)).astype(BF16)
    return hi, lo


def _split3(a):
    hi = a.astype(BF16)
    r = a - hi.astype(F32)
    mid = r.astype(BF16)
    lo = (r - mid.astype(F32)).astype(BF16)
    return hi, mid, lo


def _dot_nt(a, b):
    return lax.dot_general(a, b, (((1,), (1,)), ((), ())), preferred_element_type=F32)


def _dot_tn(a, b):
    return lax.dot_general(a, b, (((0,), (0,)), ((), ())), preferred_element_type=F32)


def _dot(a, b):
    return jnp.dot(a, b, preferred_element_type=F32)


def _in_proj_kernel(x_ref, w_ref, b_ref, hg_ref, mb_ref, gt_ref, *, col_chunk):
    xb = x_ref[...].astype(BF16)
    outs = ((hg_ref, 0), (mb_ref, hg_ref.shape[1]), (gt_ref, hg_ref.shape[1] + mb_ref.shape[1]))
    for o_ref, base in outs:
        for c0 in range(0, o_ref.shape[1], col_chunk):
            acc = _dot(xb, w_ref[:, base + c0:base + c0 + col_chunk])
            o_ref[:, c0:c0 + col_chunk] = acc + b_ref[:, base + c0:base + c0 + col_chunk]


def _in_proj(x2, w_bf, b_in, n_hg, n_mb, n_gt, tm):
    n, d = x2.shape
    cols = w_bf.shape[1]
    return pl.pallas_call(
        functools.partial(_in_proj_kernel, col_chunk=512),
        grid=(n // tm,),
        in_specs=[
            pl.BlockSpec((tm, d), lambda i: (i, 0)),
            pl.BlockSpec((d, cols), lambda i: (0, 0)),
            pl.BlockSpec((1, cols), lambda i: (0, 0)),
        ],
        out_specs=[
            pl.BlockSpec((tm, n_hg), lambda i: (i, 0)),
            pl.BlockSpec((tm, n_mb), lambda i: (i, 0)),
            pl.BlockSpec((tm, n_gt), lambda i: (i, 0)),
        ],
        out_shape=[
            jax.ShapeDtypeStruct((n, n_hg), F32),
            jax.ShapeDtypeStruct((n, n_mb), F32),
            jax.ShapeDtypeStruct((n, n_gt), F32),
        ],
        compiler_params=pltpu.CompilerParams(
            dimension_semantics=("parallel",), vmem_limit_bytes=VMEM_LIMIT),
        name="in_proj",
    )(x2, w_bf, b_in)


def _hgrn_tables():
    c = HG_CHUNK
    levels = [c >> (i + 1) for i in range(int(math.log2(c)))]
    masks = np.zeros((len(levels) + 1, c, c), np.float32)
    for li, m in enumerate(levels):
        for r in range(c):
            c0 = (r // (2 * m)) * (2 * m)
            if r - c0 >= m:
                masks[li, r, c0:c0 + m] = 1.0
    masks[len(levels)] = np.eye(c, dtype=np.float32)
    return np.tril(np.ones((c, c), np.float32)), masks, tuple(levels)


def _level_ref_rows(p, m):
    c, w = p.shape
    if 2 * m > SUBLANES:
        parts = [jnp.broadcast_to(p[c0 + m - 1:c0 + m, :], (2 * m, w)) for c0 in range(0, c, 2 * m)]
        return parts[0] if len(parts) == 1 else jnp.concatenate(parts, axis=0)
    p3 = p.reshape(c // SUBLANES, SUBLANES, w)
    sub = lax.broadcasted_iota(jnp.int32, p3.shape, 1)
    out = None
    for c0 in range(0, SUBLANES, 2 * m):
        b = jnp.broadcast_to(p3[:, c0 + m - 1:c0 + m, :], p3.shape)
        out = b if out is None else jnp.where(sub >= c0, b, out)
    return out.reshape(c, w)


def _hgrn_kernel(q_ref, f_ref, i_ref, g_ref, lb_ref, ng_ref, tril_ref, msk_ref, o_ref,
                 *, n_chunks, levels, group):
    c = HG_CHUNK
    dk = HG_DK
    assert n_chunks % group == 0
    lb = jnp.concatenate([lb_ref[...]] * group, axis=1)
    oml = 1.0 - lb
    ng = ng_ref[...]
    tril = tril_ref[...]
    n_levels = len(levels)

    def load(ref, r0):
        return jnp.concatenate([ref[0, pl.ds(r0 + u * c, c), :] for u in range(group)], axis=1)

    def lanes(a, u):
        return a[:, u * dk:(u + 1) * dk]

    def body(gi, st):
        r0 = pl.multiple_of(gi * (group * c), group * c)
        z = load(f_ref, r0)
        qr = load(q_ref, r0)
        lf = jnp.log(lb + oml * jax.nn.sigmoid(z))
        kk = oml * jax.nn.sigmoid(-z)
        qf = qr * jax.nn.sigmoid(qr)
        l_hi, l_mid, l_lo = _split3(lf)
        p = _dot(tril, l_hi) + _dot(tril, l_mid) + _dot(tril, l_lo)
        b_end = p[c - 1:c, :]
        qb = (qf * jnp.exp(p)).astype(BF16)
        kd = (kk * jnp.exp(b_end - p)).astype(BF16)
        dec = jnp.exp(b_end)
        qh = qf.astype(BF16)
        kh = kk.astype(BF16)
        scores = [msk_ref[n_levels] * _dot_nt(lanes(qh, u), lanes(kh, u)) for u in range(group)]
        for li, m in enumerate(levels):
            ex = jnp.exp(-jnp.abs(p - _level_ref_rows(p, m)))
            ql = (qf * ex).astype(BF16)
            kl = (kk * ex).astype(BF16)
            for u in range(group):
                scores[u] = scores[u] + msk_ref[li] * _dot_nt(lanes(ql, u), lanes(kl, u))
        for u in range(group):
            rows = pl.ds(r0 + u * c, c)
            vb = i_ref[0, rows, :].astype(BF16)
            g = g_ref[0, rows, :]
            o = _dot_nt(lanes(qb, u), st.astype(BF16)) + _dot(scores[u].astype(BF16), vb)
            st = st * lanes(dec, u) + _dot_tn(vb, lanes(kd, u))
            o = o * lax.rsqrt(jnp.mean(o * o, axis=-1, keepdims=True) + NORM_EPS)
            o_ref[0, rows, :] = o * ng * (g * jax.nn.sigmoid(g))
        return st

    lax.fori_loop(0, n_chunks // group, body, jnp.zeros((HG_DK, HG_DK), F32))


def _hgrn(hg3, lb_row, ng_row):
    b, s, _ = hg3.shape
    tril, masks, levels = _hgrn_tables()
    mst = jnp.asarray(tril, BF16)
    msk = jnp.asarray(masks, F32)
    h = HG_HEADS

    def col(off):
        return pl.BlockSpec((1, s, HG_DK), lambda bi, hi: (bi, 0, off + hi))

    return pl.pallas_call(
        functools.partial(_hgrn_kernel, n_chunks=s // HG_CHUNK, levels=levels, group=4),
        grid=(b, h),
        in_specs=[
            col(0), col(h), col(2 * h), col(3 * h),
            pl.BlockSpec((1, HG_DK), lambda bi, hi: (0, hi)),
            pl.BlockSpec((1, HG_DK), lambda bi, hi: (0, hi)),
            pl.BlockSpec(mst.shape, lambda bi, hi: (0, 0)),
            pl.BlockSpec(msk.shape, lambda bi, hi: (0, 0, 0)),
        ],
        out_specs=pl.BlockSpec((1, s, HG_DK), lambda bi, hi: (bi, 0, hi)),
        out_shape=jax.ShapeDtypeStruct((b, s, h * HG_DK), F32),
        compiler_params=pltpu.CompilerParams(
            dimension_semantics=("parallel", "parallel"), vmem_limit_bytes=VMEM_LIMIT),
        name="hgrn2",
    )(hg3, hg3, hg3, hg3, lb_row, ng_row, mst, msk)


def _t5_bucket_np(dist):
    max_exact = REL_BUCKETS // 2
    d = np.maximum(dist, 1).astype(np.float32)
    log_part = max_exact + (np.log(d / np.float32(max_exact)) / np.float32(math.log(REL_MAX_DIST / max_exact))
                            * np.float32(REL_BUCKETS - max_exact)).astype(np.int32)
    return np.where(dist < max_exact, dist, np.minimum(log_part, REL_BUCKETS - 1))


def _moba_kernel(q_ref, k_ref, v_ref, avg_ref, bkt_ref, rb_ref, o_ref, own_ref, prev_ref, *, n_blocks):
    blk = MB_BLOCK
    scale = MB_DH ** -0.5
    avg = avg_ref[...]
    hp = LANES // MB_DH

    @pl.when(pl.program_id(1) == 0)
    def _():
        causal = (lax.broadcasted_iota(jnp.int32, (blk, blk), 0) <= lax.broadcasted_iota(jnp.int32, (blk, blk), 1))
        for hh in range(hp):
            head = pl.program_id(0) * hp + hh
            own_t = jnp.zeros((blk, blk), F32)
            prev_t = jnp.zeros((blk, blk), F32)
            for bk in range(REL_BUCKETS):
                val = rb_ref[bk, head]
                own_t = jnp.where(bkt_ref[0] == bk, val, own_t)
                prev_t = jnp.where(bkt_ref[1] == bk, val, prev_t)
            own_ref[hh] = jnp.where(causal, own_t, NEG_INF)
            prev_ref[hh] = prev_t

    grow = lax.broadcasted_iota(jnp.int32, (GATE_ROWS, blk), 0)
    vt_all = v_ref[0].T
    outs = []
    for hh in range(hp):
        ls = slice(hh * MB_DH, (hh + 1) * MB_DH)
        k_h = k_ref[0, :, ls]
        k_hi, k_lo = _split2(k_h)
        k_mean = (_dot(avg, k_hi) + _dot(avg, k_lo))[:GATE_ROWS]
        km_hi, km_lo = _split2(k_mean)
        kb = k_h.astype(BF16)
        vt = vt_all[hh * MB_DH:(hh + 1) * MB_DH, :].astype(BF16)
        own_bias = own_ref[hh]
        prev_bias = prev_ref[hh]
        far_bias = rb_ref[REL_BUCKETS - 1, pl.program_id(0) * hp + hh]
        o_blocks = []
        for i in range(n_blocks):
            qi = q_ref[0, i * blk:(i + 1) * blk, ls] * scale
            st = _dot_nt(kb[:(i + 1) * blk], qi.astype(BF16))
            sel = None
            if i > 0:
                q_hi, q_lo = _split2(qi)
                gate = _dot_nt(km_hi, q_hi) + _dot_nt(km_lo, q_hi) + _dot_nt(km_hi, q_lo)
                gate = jnp.where(grow < i, gate, NEG_INF)
                rank = jnp.zeros((GATE_ROWS, blk), F32)
                if i > MB_TOPK:
                    for j2 in range(i):
                        gj = jnp.broadcast_to(gate[j2:j2 + 1, :], (GATE_ROWS, blk))
                        tie = jnp.where(grow > j2, 1.0, 0.0)
                        rank = rank + jnp.where(gj > gate, 1.0, jnp.where(gj == gate, tie, 0.0))
                sel = jnp.where(rank < MB_TOPK, gate, NEG_INF) > NEG_INF
            pieces = []
            for j in range(i + 1):
                sj = st[j * blk:(j + 1) * blk, :]
                if j == i:
                    pieces.append(sj + own_bias)
                elif j == i - 1:
                    pieces.append(sj + prev_bias + jnp.where(sel[j:j + 1, :], 0.0, NEG_INF))
                else:
                    pieces.append(sj + jnp.where(sel[j:j + 1, :], far_bias, NEG_INF))

            def fold(a):
                return a.reshape(blk // SUBLANES, SUBLANES, blk)

            mx8 = fold(pieces[0]).max(axis=0)
            for p in pieces[1:]:
                mx8 = jnp.maximum(mx8, fold(p).max(axis=0))
            mx = mx8.max(axis=0, keepdims=True)
            den8 = jnp.zeros((SUBLANES, blk), F32)
            acc = jnp.zeros((MB_DH, blk), F32)
            for j, p in enumerate(pieces):
                e = jnp.exp(p - mx)
                den8 = den8 + fold(e).sum(axis=0)
                acc = acc + _dot(vt[:, j * blk:(j + 1) * blk], e.astype(BF16))
            o_blocks.append(acc / den8.sum(axis=0, keepdims=True))
        outs.append(o_blocks)
    for i in range(n_blocks):
        o_ref[0, i * blk:(i + 1) * blk, :] = jnp.concatenate([o[i] for o in outs], axis=0).T


def _moba(mb3, rel_bias):
    b, s, _ = mb3.shape
    assert s % MB_BLOCK == 0
    nb = s // MB_BLOCK
    hp = LANES // MB_DH
    n_hp = MB_HEADS // hp
    t = np.arange(MB_BLOCK)
    d_own = t[:, None] - t[None, :]
    buckets = np.stack([_t5_bucket_np(np.maximum(d_own, 0)), _t5_bucket_np(d_own + MB_BLOCK)]).astype(np.int32)
    assert _t5_bucket_np(np.array([MB_BLOCK + 1]))[0] == REL_BUCKETS - 1
    assert nb <= GATE_ROWS
    bkt = jnp.asarray(buckets.transpose(0, 2, 1))
    avg_np = np.zeros((LANES, s), np.float32)
    for j in range(nb):
        avg_np[j, j * MB_BLOCK:(j + 1) * MB_BLOCK] = 1.0 / MB_BLOCK
    avg = jnp.asarray(avg_np, BF16)

    def col(off):
        return pl.BlockSpec((1, s, LANES), lambda hi, bi: (bi, 0, off + hi))

    return pl.pallas_call(
        functools.partial(_moba_kernel, n_blocks=nb),
        grid=(n_hp, b),
        in_specs=[
            col(0), col(n_hp), col(2 * n_hp),
            pl.BlockSpec(avg.shape, lambda hi, bi: (0, 0)),
            pl.BlockSpec(bkt.shape, lambda hi, bi: (0, 0, 0)),
            pl.BlockSpec(memory_space=pltpu.SMEM),
        ],
        out_specs=pl.BlockSpec((1, s, LANES), lambda hi, bi: (bi, 0, hi)),
        out_shape=jax.ShapeDtypeStruct((b, s, MB_HEADS * MB_DH), F32),
        scratch_shapes=[pltpu.VMEM((hp, MB_BLOCK, MB_BLOCK), F32), pltpu.VMEM((hp, MB_BLOCK, MB_BLOCK), F32)],
        compiler_params=pltpu.CompilerParams(
            dimension_semantics=("parallel", "arbitrary"), vmem_limit_bytes=VMEM_LIMIT),
        name="moba",
    )(mb3, mb3, mb3, avg, bkt, rel_bias.astype(F32))


def _layer_norm(x, g, b):
    mu = jnp.mean(x, axis=-1, keepdims=True)
    xc = x - mu
    var = jnp.mean(xc * xc, axis=-1, keepdims=True)
    return xc * lax.rsqrt(var + NORM_EPS) * g + b


def _merge_kernel(hg_ref, mb_ref, ga_ref, gb_ref, x_ref, wa_ref, wb_ref, wo_ref, g1_ref, b1_ref,
                  wr_ref, br_ref, tri_ref, x1_ref, route_ref, cnt_ref, run_ref, *, blocks_per_moe_tile):
    ya = _dot(hg_ref[...].astype(BF16), wa_ref[...])
    yb = _dot(mb_ref[...].astype(BF16), wb_ref[...])
    mixed_in = jax.nn.sigmoid(ga_ref[...]) * ya + jax.nn.sigmoid(gb_ref[...]) * yb
    mixed = _dot(mixed_in.astype(BF16), wo_ref[...])
    x1 = _layer_norm(DN_ALPHA * x_ref[...] + mixed, g1_ref[...], b1_ref[...])
    x1_ref[...] = x1
    x_hi, x_lo = _split2(x1)
    w_hi, w_lo = wr_ref[0], wr_ref[1]
    logits = _dot(x_hi, w_hi) + _dot(x_hi, w_lo) + _dot(x_lo, w_hi) + br_ref[...]
    tm = logits.shape[0]
    lane = lax.broadcasted_iota(jnp.int32, (tm, LANES), 1)
    glog = jnp.where(lane < N_GROUPS, logits, NEG_INF)
    gmax = jnp.max(glog, axis=-1, keepdims=True)
    grp = jnp.min(jnp.where(glog == gmax, lane, LANES), axis=-1, keepdims=True)
    p_grp = 1.0 / jnp.sum(jnp.exp(glog - gmax), axis=-1, keepdims=True)
    e_lo = N_GROUPS + grp * EXPERTS_PER_GROUP
    elog = jnp.where(jnp.logical_and(lane >= e_lo, lane < e_lo + EXPERTS_PER_GROUP), logits, NEG_INF)
    m1 = jnp.max(elog, axis=-1, keepdims=True)
    i1 = jnp.min(jnp.where(elog == m1, lane, LANES), axis=-1, keepdims=True)
    elog2 = jnp.where(lane == i1, NEG_INF, elog)
    m2 = jnp.max(elog2, axis=-1, keepdims=True)
    i2 = jnp.min(jnp.where(elog2 == m2, lane, LANES), axis=-1, keepdims=True)
    e2 = jnp.exp(m2 - m1)
    w1 = p_grp / (1.0 + e2)
    w2 = p_grp * e2 / (1.0 + e2)
    @pl.when(pl.program_id(0) % blocks_per_moe_tile == 0)
    def _():
        run_ref[...] = jnp.zeros_like(run_ref)

    oh1 = jnp.where(lane == i1 - N_GROUPS, 1.0, 0.0)
    oh2 = jnp.where(lane == i2 - N_GROUPS, 1.0, 0.0)
    both = oh1 + oh2
    before = _dot(tri_ref[...], both.astype(BF16)) + run_ref[...]
    r1 = jnp.sum(before * oh1, axis=-1, keepdims=True)
    r2 = jnp.sum(before * oh2, axis=-1, keepdims=True)
    run_ref[...] = run_ref[...] + jnp.sum(both, axis=0, keepdims=True)
    cnt_ref[0] = run_ref[...]
    cols = ((i1 - N_GROUPS).astype(F32), (i2 - N_GROUPS).astype(F32), w1, w2, r1, r2)
    route = jnp.zeros((tm, LANES), F32)
    for li, col in enumerate(cols):
        route = jnp.where(lane == li, col, route)
    route_ref[...] = route


def _merge(hg_o, mb_o, gates, x2, wa, wb, wo, g1, b1, wr, br, tm, moe_tile):
    n, d = x2.shape
    wa_n = hg_o.shape[1]
    wb_n = mb_o.shape[1]
    assert moe_tile % tm == 0
    tri = jnp.asarray(np.tril(np.ones((tm, tm), np.float32), -1), BF16)

    def full(a):
        nd = a.ndim
        return pl.BlockSpec(a.shape, lambda i: (0,) * nd)

    return pl.pallas_call(
        functools.partial(_merge_kernel, blocks_per_moe_tile=moe_tile // tm),
        grid=(n // tm,),
        in_specs=[
            pl.BlockSpec((tm, wa_n), lambda i: (i, 0)),
            pl.BlockSpec((tm, wb_n), lambda i: (i, 0)),
            pl.BlockSpec((tm, d), lambda i: (i, 0)),
            pl.BlockSpec((tm, d), lambda i: (i, 1)),
            pl.BlockSpec((tm, d), lambda i: (i, 0)),
            full(wa), full(wb), full(wo), full(g1), full(b1), full(wr), full(br), full(tri),
        ],
        out_specs=[
            pl.BlockSpec((tm, d), lambda i: (i, 0)),
            pl.BlockSpec((tm, LANES), lambda i: (i, 0)),
            pl.BlockSpec((1, 1, LANES), lambda i: (i, 0, 0)),
        ],
        out_shape=[
            jax.ShapeDtypeStruct((n, d), F32),
            jax.ShapeDtypeStruct((n, LANES), F32),
            jax.ShapeDtypeStruct((n // tm, 1, LANES), F32),
        ],
        scratch_shapes=[pltpu.VMEM((1, LANES), F32)],
        compiler_params=pltpu.CompilerParams(
            dimension_semantics=("arbitrary",), vmem_limit_bytes=VMEM_LIMIT),
        name="merge_ln1_router",
    )(hg_o, mb_o, gates, gates, x2, wa, wb, wo, g1, b1, wr, br, tri)


def _segment_copies(n_small, make_copy, act):
    per_big = SEG_BIG // SEG_PAD
    n_big = n_small // per_big

    def big(c, carry):
        act(make_copy(c * SEG_BIG, SEG_BIG))
        return carry
    lax.fori_loop(0, n_big, big, 0)

    def small(c, carry):
        act(make_copy(n_big * SEG_BIG + c * SEG_PAD, SEG_PAD))
        return carry
    lax.fori_loop(0, n_small - n_big * per_big, small, 0)


def _wait_copies(n_big, n_small, make_copy):
    def big(c, carry):
        make_copy(0, SEG_BIG).wait()
        return carry
    lax.fori_loop(0, n_big, big, 0)

    def small(c, carry):
        make_copy(0, SEG_PAD).wait()
        return carry
    lax.fori_loop(0, n_small, small, 0)


def _dispatch_kernel(lbase_ref, gseg_ref, nseg_ref, tot_ref, gap_ref, pos_ref, x1_ref, xs_hbm,
                     xs_ref, xb_ref, zero_ref, sem, *, cv_rows):
    ti = pl.program_id(0)
    n_tiles = pl.num_programs(0)
    tile, d = x1_ref.shape

    def out_copy(t, ex):
        def make(off, size):
            src = pl.multiple_of(lbase_ref[t * N_EXPERTS + ex] + off, SEG_PAD)
            dst = pl.multiple_of(gseg_ref[t * N_EXPERTS + ex] + off, SEG_PAD)
            return pltpu.make_async_copy(xb_ref.at[pl.ds(src, size), :], xs_hbm.at[pl.ds(dst, size), :], sem.at[0])
        return make

    @pl.when(ti == 0)
    def _():
        def zero(c, carry):
            r0 = pl.multiple_of(c * cv_rows, cv_rows)
            xs_ref[pl.ds(r0, cv_rows), :] = jnp.zeros((cv_rows, d), F32)
            return carry
        lax.fori_loop(0, xs_ref.shape[0] // cv_rows, zero, 0)
        zero_ref[...] = jnp.zeros_like(zero_ref)

    def sort_rows(grp, carry):
        t0 = pl.multiple_of(grp * SUBLANES, SUBLANES)
        for k in range(SUBLANES):
            row = x1_ref[pl.ds(t0 + k, 1), :]
            xs_ref[pl.ds(pos_ref[0, 0, TOP_K * (t0 + k)], 1), :] = row
            xs_ref[pl.ds(pos_ref[0, 0, TOP_K * (t0 + k) + 1], 1), :] = row
        return carry
    lax.fori_loop(0, tile // SUBLANES, sort_rows, 0)

    @pl.when(ti > 0)
    def _():
        _wait_copies(tot_ref[2 * (ti - 1)], tot_ref[2 * (ti - 1) + 1], out_copy(0, 0))

    def convert(c, carry):
        r0 = pl.multiple_of(c * cv_rows, cv_rows)
        xb_ref[pl.ds(r0, cv_rows), :] = xs_ref[pl.ds(r0, cv_rows), :].astype(BF16)
        return carry
    lax.fori_loop(0, xs_ref.shape[0] // cv_rows, convert, 0)

    def send(ex, carry):
        _segment_copies(nseg_ref[ti * N_EXPERTS + ex], out_copy(ti, ex), lambda cp: cp.start())
        return carry
    lax.fori_loop(0, N_EXPERTS, send, 0)

    @pl.when(ti == n_tiles - 1)
    def _():
        _wait_copies(tot_ref[2 * ti], tot_ref[2 * ti + 1], out_copy(0, 0))

        def fill_copy(row):
            return pltpu.make_async_copy(zero_ref, xs_hbm.at[pl.ds(pl.multiple_of(row, SEG_PAD), SEG_PAD), :],
                                         sem.at[1])

        def fill(ex, carry):
            def one(c, carry2):
                fill_copy(gap_ref[2 * ex] + c * SEG_PAD).start()
                return carry2
            lax.fori_loop(0, gap_ref[2 * ex + 1], one, 0)
            return carry
        lax.fori_loop(0, N_EXPERTS, fill, 0)

        def fill_wait(ex, carry):
            def one(c, carry2):
                fill_copy(0).wait()
                return carry2
            lax.fori_loop(0, gap_ref[2 * ex + 1], one, 0)
            return carry
        lax.fori_loop(0, N_EXPERTS, fill_wait, 0)


def _expert_kernel(be_ref, nu_ref, x_ref, wgu_ref, wd_ref, y_ref):
    @pl.when(pl.program_id(0) < nu_ref[0])
    def _():
        gu = _dot(x_ref[...], wgu_ref[0])
        gate = gu[:, :EXPERT_HIDDEN]
        up = gu[:, EXPERT_HIDDEN:]
        hdn = (gate * jax.nn.sigmoid(gate) * up).astype(BF16)
        y_ref[...] = _dot(hdn, wd_ref[0])


def _combine_kernel(lbase_ref, gseg_ref, nseg_ref, tot_ref, pos_ref, rw_ref, x1_ref, ys_hbm, g2_ref, b2_ref,
                    o_ref, xs_ref, sem, *, ln_rows):
    ti = pl.program_id(0)
    n_tiles = pl.num_programs(0)
    tile, d = x1_ref.shape
    slot = ti % 2

    def in_copy(t, ex, sl):
        def make(off, size):
            src = pl.multiple_of(gseg_ref[t * N_EXPERTS + ex] + off, SEG_PAD)
            dst = pl.multiple_of(lbase_ref[t * N_EXPERTS + ex] + off, SEG_PAD)
            return pltpu.make_async_copy(ys_hbm.at[pl.ds(src, size), :], xs_ref.at[sl, pl.ds(dst, size), :],
                                         sem.at[sl])
        return make

    def fetch(t, sl):
        def one(ex, carry):
            _segment_copies(nseg_ref[t * N_EXPERTS + ex], in_copy(t, ex, sl), lambda cp: cp.start())
            return carry
        lax.fori_loop(0, N_EXPERTS, one, 0)

    @pl.when(ti == 0)
    def _():
        fetch(0, 0)

    @pl.when(ti + 1 < n_tiles)
    def _():
        fetch(ti + 1, 1 - slot)

    _wait_copies(tot_ref[2 * ti], tot_ref[2 * ti + 1], in_copy(0, 0, slot))

    def combine(grp, carry):
        t0 = pl.multiple_of(grp * SUBLANES, SUBLANES)
        for k in range(SUBLANES):
            a = TOP_K * (t0 + k)
            o_ref[pl.ds(t0 + k, 1), :] = (DN_ALPHA * x1_ref[pl.ds(t0 + k, 1), :]
                                          + rw_ref[0, 0, a] * xs_ref[slot, pl.ds(pos_ref[0, 0, a], 1), :]
                                          + rw_ref[0, 0, a + 1] * xs_ref[slot, pl.ds(pos_ref[0, 0, a + 1], 1), :])
        return carry
    lax.fori_loop(0, tile // SUBLANES, combine, 0)

    def norm(c, carry):
        r0 = pl.multiple_of(c * ln_rows, ln_rows)
        o_ref[pl.ds(r0, ln_rows), :] = _layer_norm(o_ref[pl.ds(r0, ln_rows), :], g2_ref[...], b2_ref[...])
        return carry
    lax.fori_loop(0, tile // ln_rows, norm, 0)


def _moe_tile(n):
    return min(MOE_TILE, n)


def _round_up(a, m):
    return (a + m - 1) // m * m


def _moe(x1, route, cnt_run, wgu_bf, wd_bf, g2, b2):
    n, d = x1.shape
    tile = _moe_tile(n)
    assert n % tile == 0
    n_tiles = n // tile
    i32 = jnp.int32
    cnt = cnt_run.reshape(n_tiles, -1, LANES)[:, -1, :N_EXPERTS].astype(i32)
    seg = _round_up(cnt, SEG_PAD)
    lbase = jnp.cumsum(seg, axis=1) - seg
    used = jnp.sum(seg, axis=0)
    region = _round_up(used, EXP_ROWS)
    e_start = jnp.cumsum(region) - region
    gseg = e_start[None, :] + jnp.cumsum(seg, axis=0) - seg
    nseg = seg // SEG_PAD
    per_big = SEG_BIG // SEG_PAD
    tot = jnp.stack([jnp.sum(nseg // per_big, axis=1), jnp.sum(nseg % per_big, axis=1)], axis=1)
    gap = jnp.stack([e_start + used, (region - used) // SEG_PAD], axis=1)
    rows_max = _round_up(TOP_K * n + n_tiles * N_EXPERTS * (SEG_PAD - 1) + N_EXPERTS * (EXP_ROWS - 1), EXP_ROWS)
    n_blocks = rows_max // EXP_ROWS
    n_used = (jnp.sum(region) // EXP_ROWS).astype(i32).reshape(1)
    blk_expert = jnp.minimum(jnp.searchsorted(jnp.cumsum(region), jnp.arange(n_blocks, dtype=i32) * EXP_ROWS,
                                              side="right"), N_EXPERTS - 1).astype(i32)
    r_exp = route[:, 0:TOP_K].astype(i32).reshape(n_tiles, TOP_K * tile)
    r_rank = route[:, 2 * TOP_K:3 * TOP_K].astype(i32).reshape(n_tiles, TOP_K * tile)
    seg_start = jnp.sum(jnp.where(r_exp[:, :, None] == jnp.arange(N_EXPERTS, dtype=i32), lbase[:, None, :], 0),
                        axis=-1)
    pos = (seg_start + r_rank).reshape(n_tiles, 1, TOP_K * tile)
    r_wgt = route[:, TOP_K:2 * TOP_K].reshape(n_tiles, 1, TOP_K * tile)
    local_rows = _round_up(TOP_K * tile + N_EXPERTS * (SEG_PAD - 1), 256)
    flat = lambda a: a.reshape(-1).astype(i32)

    def smem_spec():
        return pl.BlockSpec((1, 1, TOP_K * tile), lambda t, *_: (t, 0, 0), memory_space=pltpu.SMEM)

    xs_hbm = pl.pallas_call(
        functools.partial(_dispatch_kernel, cv_rows=256),
        grid_spec=pltpu.PrefetchScalarGridSpec(
            num_scalar_prefetch=5,
            grid=(n_tiles,),
            in_specs=[smem_spec(), pl.BlockSpec((tile, d), lambda t, *_: (t, 0))],
            out_specs=pl.BlockSpec(memory_space=pl.ANY),
            scratch_shapes=[
                pltpu.VMEM((local_rows, d), F32),
                pltpu.VMEM((local_rows, d), BF16),
                pltpu.VMEM((SEG_PAD, d), BF16),
                pltpu.SemaphoreType.DMA((2,)),
            ],
        ),
        out_shape=jax.ShapeDtypeStruct((rows_max, d), BF16),
        compiler_params=pltpu.CompilerParams(dimension_semantics=("arbitrary",), vmem_limit_bytes=VMEM_LIMIT),
        name="moe_dispatch",
    )(flat(lbase), flat(gseg), flat(nseg), flat(tot), flat(gap), pos, x1)

    def blk(i, be, nu):
        return jnp.minimum(i, nu[0] - 1)

    ys_hbm = pl.pallas_call(
        _expert_kernel,
        grid_spec=pltpu.PrefetchScalarGridSpec(
            num_scalar_prefetch=2,
            grid=(n_blocks,),
            in_specs=[
                pl.BlockSpec((EXP_ROWS, d), lambda i, be, nu: (blk(i, be, nu), 0)),
                pl.BlockSpec((1, d, 2 * EXPERT_HIDDEN), lambda i, be, nu: (be[blk(i, be, nu)], 0, 0)),
                pl.BlockSpec((1, EXPERT_HIDDEN, d), lambda i, be, nu: (be[blk(i, be, nu)], 0, 0)),
            ],
            out_specs=pl.BlockSpec((EXP_ROWS, d), lambda i, be, nu: (blk(i, be, nu), 0)),
        ),
        out_shape=jax.ShapeDtypeStruct((rows_max, d), F32),
        compiler_params=pltpu.CompilerParams(dimension_semantics=("arbitrary",), vmem_limit_bytes=VMEM_LIMIT),
        name="moe_experts",
    )(blk_expert, n_used, xs_hbm, wgu_bf, wd_bf)

    return pl.pallas_call(
        functools.partial(_combine_kernel, ln_rows=min(256, tile)),
        grid_spec=pltpu.PrefetchScalarGridSpec(
            num_scalar_prefetch=4,
            grid=(n_tiles,),
            in_specs=[
                smem_spec(), smem_spec(),
                pl.BlockSpec((tile, d), lambda t, *_: (t, 0)),
                pl.BlockSpec(memory_space=pl.ANY),
                pl.BlockSpec((1, d), lambda t, *_: (0, 0)),
                pl.BlockSpec((1, d), lambda t, *_: (0, 0)),
            ],
            out_specs=pl.BlockSpec((tile, d), lambda t, *_: (t, 0)),
            scratch_shapes=[pltpu.VMEM((2, local_rows, d), F32), pltpu.SemaphoreType.DMA((2,))],
        ),
        out_shape=jax.ShapeDtypeStruct((n, d), F32),
        compiler_params=pltpu.CompilerParams(dimension_semantics=("arbitrary",), vmem_limit_bytes=VMEM_LIMIT),
        name="moe_combine_ln2",
    )(flat(lbase), flat(gseg), flat(nseg), flat(tot), pos, r_wgt, x1, ys_hbm, g2, b2)


def _block(x, w_in, b_in, lower_bound, hg_norm_g, rel_bias, w_proj_a, w_proj_b, w_out, ln1_g, ln1_b,
           w_group, b_group, w_expert, b_expert, w_gate_up, w_down, ln2_g, ln2_b, *, tm_proj, tm_merge):
    b, s, d = x.shape
    n = b * s
    n_hg = 4 * HG_HEADS * HG_DK
    n_mb = 3 * MB_HEADS * MB_DH
    n_gt = 2 * d
    x2 = x.reshape(n, d)
    hg, mb, gates = _in_proj(x2, w_in.astype(BF16), b_in.reshape(1, -1), n_hg, n_mb, n_gt, tm_proj)
    hg_o = _hgrn(hg.reshape(b, s, n_hg), lower_bound.reshape(1, -1), hg_norm_g.reshape(1, -1))
    mb_o = _moba(mb.reshape(b, s, n_mb), rel_bias)
    w_r = jnp.zeros((d, LANES), F32).at[:, :N_GROUPS].set(w_group).at[:, N_GROUPS:N_GROUPS + N_EXPERTS].set(w_expert)
    w_r_hi = w_r.astype(BF16)
    w_r_lo = (w_r - w_r_hi.astype(F32)).astype(BF16)
    b_r = jnp.zeros((1, LANES), F32).at[0, :N_GROUPS].set(b_group).at[0, N_GROUPS:N_GROUPS + N_EXPERTS].set(b_expert)
    tm_merge = min(tm_merge, n)
    x1, route, cnt_run = _merge(hg_o.reshape(n, -1), mb_o.reshape(n, -1), gates, x2,
                                w_proj_a.astype(BF16), w_proj_b.astype(BF16), w_out.astype(BF16),
                                ln1_g.reshape(1, d), ln1_b.reshape(1, d), jnp.stack([w_r_hi, w_r_lo]), b_r,
                                tm_merge, _moe_tile(n))
    out = _moe(x1, route, cnt_run, w_gate_up.astype(BF16), w_down.astype(BF16),
               ln2_g.reshape(1, d), ln2_b.reshape(1, d))
    return out.reshape(b, s, d)


def kernel(x, w_in, b_in, lb_logits, hg_norm_g, rel_bias, w_proj_a, w_proj_b, w_out, ln1_g, ln1_b, w_group,
           b_group, w_expert, b_expert, w_gate_up, w_down, ln2_g, ln2_b):
    lower_bounds = jnp.cumsum(jax.nn.softmax(lb_logits.astype(F32), axis=0), axis=0)
    l = 0
    return _block(x, w_in[l], b_in[l], lower_bounds[l], hg_norm_g[l], rel_bias, w_proj_a[l], w_proj_b[l],
                  w_out[l], ln1_g[l], ln1_b[l], w_group[l], b_group[l], w_expert[l], b_expert[l],
                  w_gate_up[l], w_down[l], ln2_g[l], ln2_b[l], tm_proj=256, tm_merge=512)
```

```python
import functools
import math

import numpy as np
import jax
import jax.numpy as jnp
from jax import lax
from jax.experimental import pallas as pl
from jax.experimental.pallas import tpu as pltpu

F32 = jnp.float32
BF16 = jnp.bfloat16

HG_HEADS = 4
HG_DK = 128
HG_CHUNK = 128
HG_GROUP = 4
MB_HEADS = 8
MB_DH = 64
MB_BLOCK = 256
MB_TOPK = 3
DEN_ROWS = 16
GATE_ROWS = 16
REL_BUCKETS = 32
REL_MAX_DIST = 128
N_GROUPS = 4
EXPERTS_PER_GROUP = 8
N_EXPERTS = N_GROUPS * EXPERTS_PER_GROUP
TOP_K = 2
EXPERT_HIDDEN = 512
MOE_TILE = 1024
SEG_PAD = 16
SEG_BIG = 64
EXP_ROWS = 256
DEPTH = 1
DN_ALPHA = (2.0 * DEPTH) ** 0.25
NORM_EPS = 1e-5
LANES = 128
SUBLANES = 8
VMEM_LIMIT = 56 * 1024 * 1024
NEG_INF = float("-inf")
LOG2_E = 1.4426950408889634


def _split2(a):
    hi = a.astype(BF16)
    lo = (a - hi.astype(F32)).astype(BF16)
    return hi, lo


def _split3(a):
    hi = a.astype(BF16)
    r = a - hi.astype(F32)
    mid = r.astype(BF16)
    lo = (r - mid.astype(F32)).astype(BF16)
    return hi, mid, lo


def _dot_nt(a, b):
    return lax.dot_general(a, b, (((1,), (1,)), ((), ())), preferred_element_type=F32)


def _dot_tn(a, b):
    return lax.dot_general(a, b, (((0,), (0,)), ((), ())), preferred_element_type=F32)


def _dot(a, b):
    return jnp.dot(a, b, preferred_element_type=F32)


def _in_proj_kernel(x_ref, w_ref, b_ref, hg_ref, mb_ref, gt_ref, *, col_chunk):
    xb = x_ref[...].astype(BF16)
    outs = ((hg_ref, 0), (mb_ref, hg_ref.shape[1]), (gt_ref, hg_ref.shape[1] + mb_ref.shape[1]))
    for o_ref, base in outs:
        for c0 in range(0, o_ref.shape[1], col_chunk):
            acc = _dot(xb, w_ref[:, base + c0:base + c0 + col_chunk])
            o_ref[:, c0:c0 + col_chunk] = acc + b_ref[:, base + c0:base + c0 + col_chunk]


def _in_proj(x2, w_bf, b_in, n_hg, n_mb, n_gt, tm):
    n, d = x2.shape
    cols = w_bf.shape[1]
    return pl.pallas_call(
        functools.partial(_in_proj_kernel, col_chunk=512),
        grid=(n // tm,),
        in_specs=[
            pl.BlockSpec((tm, d), lambda i: (i, 0)),
            pl.BlockSpec((d, cols), lambda i: (0, 0)),
            pl.BlockSpec((1, cols), lambda i: (0, 0)),
        ],
        out_specs=[
            pl.BlockSpec((tm, n_hg), lambda i: (i, 0)),
            pl.BlockSpec((tm, n_mb), lambda i: (i, 0)),
            pl.BlockSpec((tm, n_gt), lambda i: (i, 0)),
        ],
        out_shape=[
            jax.ShapeDtypeStruct((n, n_hg), F32),
            jax.ShapeDtypeStruct((n, n_mb), F32),
            jax.ShapeDtypeStruct((n, n_gt), F32),
        ],
        compiler_params=pltpu.CompilerParams(
            dimension_semantics=("parallel",), vmem_limit_bytes=VMEM_LIMIT),
        name="in_proj",
    )(x2, w_bf, b_in)


def _hgrn_tables():
    c = HG_CHUNK
    levels = [c >> (i + 1) for i in range(int(math.log2(c)))]
    masks = np.zeros((len(levels) + 1, c, c), np.float32)
    for li, m in enumerate(levels):
        for r in range(c):
            c0 = (r // (2 * m)) * (2 * m)
            if r - c0 >= m:
                masks[li, r, c0:c0 + m] = 1.0
    masks[len(levels)] = np.eye(c, dtype=np.float32)
    return np.tril(np.ones((c, c), np.float32)), masks, tuple(levels)


def _level_ref_rows(p, m):
    c, w = p.shape
    if 2 * m > SUBLANES:
        parts = [jnp.broadcast_to(p[c0 + m - 1:c0 + m, :], (2 * m, w)) for c0 in range(0, c, 2 * m)]
        return parts[0] if len(parts) == 1 else jnp.concatenate(parts, axis=0)
    p3 = p.reshape(c // SUBLANES, SUBLANES, w)
    sub = lax.broadcasted_iota(jnp.int32, p3.shape, 1)
    out = None
    for c0 in range(0, SUBLANES, 2 * m):
        b = jnp.broadcast_to(p3[:, c0 + m - 1:c0 + m, :], p3.shape)
        out = b if out is None else jnp.where(sub >= c0, b, out)
    return out.reshape(c, w)


def _hgrn_kernel(q_ref, f_ref, i_ref, g_ref, lb_ref, ng_ref, tril_ref, msk_ref, o_ref,
                 *, n_chunks, levels, group):
    c = HG_CHUNK
    dk = HG_DK
    assert n_chunks % group == 0
    lb = jnp.concatenate([lb_ref[...]] * group, axis=1)
    oml = 1.0 - lb
    ng = ng_ref[...]
    tril = tril_ref[...]
    n_levels = len(levels)

    def load(ref, r0):
        return jnp.concatenate([ref[0, pl.ds(r0 + u * c, c), :] for u in range(group)], axis=1)

    def lanes(a, u):
        return a[:, u * dk:(u + 1) * dk]

    def body(gi, st):
        r0 = pl.multiple_of(gi * (group * c), group * c)
        z = load(f_ref, r0)
        qr = load(q_ref, r0)
        lf = jnp.log(lb + oml * jax.nn.sigmoid(z))
        kk = oml * jax.nn.sigmoid(-z)
        qf = qr * jax.nn.sigmoid(qr)
        l_hi, l_mid, l_lo = _split3(lf)
        p = _dot(tril, l_hi) + _dot(tril, l_mid) + _dot(tril, l_lo)
        b_end = p[c - 1:c, :]
        qb = (qf * jnp.exp(p)).astype(BF16)
        kd = (kk * jnp.exp(b_end - p)).astype(BF16)
        dec = jnp.exp(b_end)
        qh = qf.astype(BF16)
        kh = kk.astype(BF16)
        scores = [msk_ref[n_levels] * _dot_nt(lanes(qh, u), lanes(kh, u)) for u in range(group)]
        row = lax.broadcasted_iota(jnp.int32, p.shape, 0)
        for li, m in enumerate(levels):
            ex = jnp.exp2(jnp.abs(p - _level_ref_rows(p, m)) * (-LOG2_E))
            qk = (jnp.where((row & (2 * m - 1)) >= m, qf, kk) * ex).astype(BF16)
            for u in range(group):
                scores[u] = scores[u] + msk_ref[li] * _dot_nt(lanes(qk, u), lanes(qk, u))
        for u in range(group):
            rows = pl.ds(r0 + u * c, c)
            vb = i_ref[0, rows, :].astype(BF16)
            g = g_ref[0, rows, :]
            o = _dot_nt(lanes(qb, u), st.astype(BF16)) + _dot(scores[u].astype(BF16), vb)
            st = st * lanes(dec, u) + _dot_tn(vb, lanes(kd, u))
            o = o * lax.rsqrt(jnp.mean(o * o, axis=-1, keepdims=True) + NORM_EPS)
            o_ref[0, rows, :] = o * ng * (g * jax.nn.sigmoid(g))
        return st

    lax.fori_loop(0, n_chunks // group, body, jnp.zeros((HG_DK, HG_DK), F32))


def _hgrn(hg3, lb_row, ng_row):
    b, s, _ = hg3.shape
    tril, masks, levels = _hgrn_tables()
    mst = jnp.asarray(tril, BF16)
    msk = jnp.asarray(masks, F32)
    h = HG_HEADS

    def col(off):
        return pl.BlockSpec((1, s, HG_DK), lambda bi, hi: (bi, 0, off + hi))

    return pl.pallas_call(
        functools.partial(_hgrn_kernel, n_chunks=s // HG_CHUNK, levels=levels,
                          group=math.gcd(s // HG_CHUNK, HG_GROUP)),
        grid=(b, h),
        in_specs=[
            col(0), col(h), col(2 * h), col(3 * h),
            pl.BlockSpec((1, HG_DK), lambda bi, hi: (0, hi)),
            pl.BlockSpec((1, HG_DK), lambda bi, hi: (0, hi)),
            pl.BlockSpec(mst.shape, lambda bi, hi: (0, 0)),
            pl.BlockSpec(msk.shape, lambda bi, hi: (0, 0, 0)),
        ],
        out_specs=pl.BlockSpec((1, s, HG_DK), lambda bi, hi: (bi, 0, hi)),
        out_shape=jax.ShapeDtypeStruct((b, s, h * HG_DK), F32),
        compiler_params=pltpu.CompilerParams(
            dimension_semantics=("parallel", "parallel"), vmem_limit_bytes=VMEM_LIMIT),
        name="hgrn2",
    )(hg3, hg3, hg3, hg3, lb_row, ng_row, mst, msk)


def _t5_bucket_np(dist):
    max_exact = REL_BUCKETS // 2
    d = np.maximum(dist, 1).astype(np.float32)
    log_part = max_exact + (np.log(d / np.float32(max_exact)) / np.float32(math.log(REL_MAX_DIST / max_exact))
                            * np.float32(REL_BUCKETS - max_exact)).astype(np.int32)
    return np.where(dist < max_exact, dist, np.minimum(log_part, REL_BUCKETS - 1))


def _moba_kernel(q_ref, k_ref, v_ref, avg_ref, bkt_ref, rb_ref, o_ref, own_ref, prev_ref, *, n_blocks):
    blk = MB_BLOCK
    scale = MB_DH ** -0.5 * LOG2_E
    avg = avg_ref[...]
    hp = LANES // MB_DH

    @pl.when(pl.program_id(1) == 0)
    def _():
        causal = (lax.broadcasted_iota(jnp.int32, (blk, blk), 0) <= lax.broadcasted_iota(jnp.int32, (blk, blk), 1))
        for hh in range(hp):
            head = pl.program_id(0) * hp + hh
            own_t = jnp.zeros((blk, blk), F32)
            prev_t = jnp.zeros((blk, blk), F32)
            for bk in range(REL_BUCKETS):
                val = rb_ref[bk, head] * LOG2_E
                own_t = jnp.where(bkt_ref[0] == bk, val, own_t)
                prev_t = jnp.where(bkt_ref[1] == bk, val, prev_t)
            own_ref[hh] = jnp.where(causal, own_t, NEG_INF)
            prev_ref[hh] = prev_t

    grow = lax.broadcasted_iota(jnp.int32, (GATE_ROWS, blk), 0)
    vt_all = v_ref[0].T
    outs = []
    for hh in range(hp):
        ls = slice(hh * MB_DH, (hh + 1) * MB_DH)
        k_h = k_ref[0, :, ls]
        k_hi, k_lo = _split2(k_h)
        k_mean = (_dot(avg, k_hi) + _dot(avg, k_lo))[:GATE_ROWS]
        km_hi, km_lo = _split2(k_mean)
        kb = k_h.astype(BF16)
        vt = jnp.concatenate([vt_all[hh * MB_DH:(hh + 1) * MB_DH, :], jnp.ones((DEN_ROWS, vt_all.shape[1]), F32)],
                             axis=0).astype(BF16)
        own_bias = own_ref[hh]
        prev_bias = prev_ref[hh]
        far_bias = rb_ref[REL_BUCKETS - 1, pl.program_id(0) * hp + hh] * LOG2_E
        o_blocks = []
        for i in range(n_blocks):
            qi = q_ref[0, i * blk:(i + 1) * blk, ls] * scale
            st = _dot_nt(kb[:(i + 1) * blk], qi.astype(BF16))
            sel = None
            if i > 0:
                q_hi, q_lo = _split2(qi)
                gate = _dot_nt(km_hi, q_hi) + _dot_nt(km_lo, q_hi) + _dot_nt(km_hi, q_lo)
                gate = jnp.where(grow < i, gate, NEG_INF)
                rank = jnp.zeros((GATE_ROWS, blk), F32)
                if i > MB_TOPK:
                    for j2 in range(i):
                        gj = jnp.broadcast_to(gate[j2:j2 + 1, :], (GATE_ROWS, blk))
                        tie = jnp.where(grow > j2, 1.0, 0.0)
                        rank = rank + jnp.where(gj > gate, 1.0, jnp.where(gj == gate, tie, 0.0))
                sel = jnp.where(rank < MB_TOPK, gate, NEG_INF) > NEG_INF
            pieces = []
            for j in range(i + 1):
                sj = st[j * blk:(j + 1) * blk, :]
                if j == i:
                    pieces.append(sj + own_bias)
                elif j == i - 1:
                    pieces.append(sj + prev_bias + jnp.where(sel[j:j + 1, :], 0.0, NEG_INF))
                else:
                    pieces.append(sj + jnp.where(sel[j:j + 1, :], far_bias, NEG_INF))

            def fold(a):
                return a.reshape(blk // SUBLANES, SUBLANES, blk)

            mx8 = fold(pieces[0]).max(axis=0)
            for p in pieces[1:]:
                mx8 = jnp.maximum(mx8, fold(p).max(axis=0))
            mx = mx8.max(axis=0, keepdims=True)
            acc = jnp.zeros((MB_DH + DEN_ROWS, blk), F32)
            for j, p in enumerate(pieces):
                acc = acc + _dot(vt[:, j * blk:(j + 1) * blk], jnp.exp2(p - mx).astype(BF16))
            o_blocks.append(acc[:MB_DH] / acc[MB_DH:MB_DH + 1])
        outs.append(o_blocks)
    for i in range(n_blocks):
        o_ref[0, i * blk:(i + 1) * blk, :] = jnp.concatenate([o[i] for o in outs], axis=0).T


def _moba(mb3, rel_bias):
    b, s, _ = mb3.shape
    assert s % MB_BLOCK == 0
    nb = s // MB_BLOCK
    hp = LANES // MB_DH
    n_hp = MB_HEADS // hp
    t = np.arange(MB_BLOCK)
    d_own = t[:, None] - t[None, :]
    buckets = np.stack([_t5_bucket_np(np.maximum(d_own, 0)), _t5_bucket_np(d_own + MB_BLOCK)]).astype(np.int32)
    assert _t5_bucket_np(np.array([MB_BLOCK + 1]))[0] == REL_BUCKETS - 1
    assert nb <= GATE_ROWS
    bkt = jnp.asarray(buckets.transpose(0, 2, 1))
    avg_np = np.zeros((LANES, s), np.float32)
    for j in range(nb):
        avg_np[j, j * MB_BLOCK:(j + 1) * MB_BLOCK] = 1.0 / MB_BLOCK
    avg = jnp.asarray(avg_np, BF16)

    def col(off):
        return pl.BlockSpec((1, s, LANES), lambda hi, bi: (bi, 0, off + hi))

    return pl.pallas_call(
        functools.partial(_moba_kernel, n_blocks=nb),
        grid=(n_hp, b),
        in_specs=[
            col(0), col(n_hp), col(2 * n_hp),
            pl.BlockSpec(avg.shape, lambda hi, bi: (0, 0)),
            pl.BlockSpec(bkt.shape, lambda hi, bi: (0, 0, 0)),
            pl.BlockSpec(memory_space=pltpu.SMEM),
        ],
        out_specs=pl.BlockSpec((1, s, LANES), lambda hi, bi: (bi, 0, hi)),
        out_shape=jax.ShapeDtypeStruct((b, s, MB_HEADS * MB_DH), F32),
        scratch_shapes=[pltpu.VMEM((hp, MB_BLOCK, MB_BLOCK), F32), pltpu.VMEM((hp, MB_BLOCK, MB_BLOCK), F32)],
        compiler_params=pltpu.CompilerParams(
            dimension_semantics=("parallel", "arbitrary"), vmem_limit_bytes=VMEM_LIMIT),
        name="moba",
    )(mb3, mb3, mb3, avg, bkt, rel_bias.astype(F32))


def _layer_norm(x, g, b):
    mu = jnp.mean(x, axis=-1, keepdims=True)
    xc = x - mu
    var = jnp.mean(xc * xc, axis=-1, keepdims=True)
    return xc * lax.rsqrt(var + NORM_EPS) * g + b


def _merge_kernel(hg_ref, mb_ref, ga_ref, gb_ref, x_ref, wa_ref, wb_ref, wo_ref, g1_ref, b1_ref,
                  wr_ref, br_ref, tri_ref, x1_ref, route_ref, cnt_ref, run_ref, *, blocks_per_moe_tile):
    ya = _dot(hg_ref[...].astype(BF16), wa_ref[...])
    yb = _dot(mb_ref[...].astype(BF16), wb_ref[...])
    mixed_in = jax.nn.sigmoid(ga_ref[...]) * ya + jax.nn.sigmoid(gb_ref[...]) * yb
    mixed = _dot(mixed_in.astype(BF16), wo_ref[...])
    x1 = _layer_norm(DN_ALPHA * x_ref[...] + mixed, g1_ref[...], b1_ref[...])
    x1_ref[...] = x1
    x_hi, x_lo = _split2(x1)
    w_hi, w_lo = wr_ref[0], wr_ref[1]
    logits = _dot(x_hi, w_hi) + _dot(x_hi, w_lo) + _dot(x_lo, w_hi) + br_ref[...]
    tm = logits.shape[0]
    lane = lax.broadcasted_iota(jnp.int32, (tm, LANES), 1)
    glog = jnp.where(lane < N_GROUPS, logits, NEG_INF)
    gmax = jnp.max(glog, axis=-1, keepdims=True)
    grp = jnp.min(jnp.where(glog == gmax, lane, LANES), axis=-1, keepdims=True)
    p_grp = 1.0 / jnp.sum(jnp.exp(glog - gmax), axis=-1, keepdims=True)
    e_lo = N_GROUPS + grp * EXPERTS_PER_GROUP
    elog = jnp.where(jnp.logical_and(lane >= e_lo, lane < e_lo + EXPERTS_PER_GROUP), logits, NEG_INF)
    m1 = jnp.max(elog, axis=-1, keepdims=True)
    i1 = jnp.min(jnp.where(elog == m1, lane, LANES), axis=-1, keepdims=True)
    elog2 = jnp.where(lane == i1, NEG_INF, elog)
    m2 = jnp.max(elog2, axis=-1, keepdims=True)
    i2 = jnp.min(jnp.where(elog2 == m2, lane, LANES), axis=-1, keepdims=True)
    e2 = jnp.exp(m2 - m1)
    w1 = p_grp / (1.0 + e2)
    w2 = p_grp * e2 / (1.0 + e2)
    @pl.when(pl.program_id(0) % blocks_per_moe_tile == 0)
    def _():
        run_ref[...] = jnp.zeros_like(run_ref)

    oh1 = jnp.where(lane == i1 - N_GROUPS, 1.0, 0.0)
    oh2 = jnp.where(lane == i2 - N_GROUPS, 1.0, 0.0)
    both = oh1 + oh2
    before = _dot(tri_ref[...], both.astype(BF16)) + run_ref[...]
    r1 = jnp.sum(before * oh1, axis=-1, keepdims=True)
    r2 = jnp.sum(before * oh2, axis=-1, keepdims=True)
    run_ref[...] = run_ref[...] + jnp.sum(both, axis=0, keepdims=True)
    cnt_ref[0] = run_ref[...]
    cols = ((i1 - N_GROUPS).astype(F32), (i2 - N_GROUPS).astype(F32), w1, w2, r1, r2)
    route = jnp.zeros((tm, LANES), F32)
    for li, col in enumerate(cols):
        route = jnp.where(lane == li, col, route)
    route_ref[...] = route


def _merge(hg_o, mb_o, gates, x2, wa, wb, wo, g1, b1, wr, br, tm, moe_tile):
    n, d = x2.shape
    wa_n = hg_o.shape[1]
    wb_n = mb_o.shape[1]
    assert moe_tile % tm == 0
    tri = jnp.asarray(np.tril(np.ones((tm, tm), np.float32), -1), BF16)

    def full(a):
        nd = a.ndim
        return pl.BlockSpec(a.shape, lambda i: (0,) * nd)

    return pl.pallas_call(
        functools.partial(_merge_kernel, blocks_per_moe_tile=moe_tile // tm),
        grid=(n // tm,),
        in_specs=[
            pl.BlockSpec((tm, wa_n), lambda i: (i, 0)),
            pl.BlockSpec((tm, wb_n), lambda i: (i, 0)),
            pl.BlockSpec((tm, d), lambda i: (i, 0)),
            pl.BlockSpec((tm, d), lambda i: (i, 1)),
            pl.BlockSpec((tm, d), lambda i: (i, 0)),
            full(wa), full(wb), full(wo), full(g1), full(b1), full(wr), full(br), full(tri),
        ],
        out_specs=[
            pl.BlockSpec((tm, d), lambda i: (i, 0)),
            pl.BlockSpec((tm, LANES), lambda i: (i, 0)),
            pl.BlockSpec((1, 1, LANES), lambda i: (i, 0, 0)),
        ],
        out_shape=[
            jax.ShapeDtypeStruct((n, d), F32),
            jax.ShapeDtypeStruct((n, LANES), F32),
            jax.ShapeDtypeStruct((n // tm, 1, LANES), F32),
        ],
        scratch_shapes=[pltpu.VMEM((1, LANES), F32)],
        compiler_params=pltpu.CompilerParams(
            dimension_semantics=("arbitrary",), vmem_limit_bytes=VMEM_LIMIT),
        name="merge_ln1_router",
    )(hg_o, mb_o, gates, gates, x2, wa, wb, wo, g1, b1, wr, br, tri)


def _segment_copies(n_small, make_copy, act):
    per_big = SEG_BIG // SEG_PAD
    n_big = n_small // per_big

    def big(c, carry):
        act(make_copy(c * SEG_BIG, SEG_BIG))
        return carry
    lax.fori_loop(0, n_big, big, 0)

    def small(c, carry):
        act(make_copy(n_big * SEG_BIG + c * SEG_PAD, SEG_PAD))
        return carry
    lax.fori_loop(0, n_small - n_big * per_big, small, 0)


def _wait_copies(n_big, n_small, make_copy):
    def big(c, carry):
        make_copy(0, SEG_BIG).wait()
        return carry
    lax.fori_loop(0, n_big, big, 0)

    def small(c, carry):
        make_copy(0, SEG_PAD).wait()
        return carry
    lax.fori_loop(0, n_small, small, 0)


def _dispatch_kernel(lbase_ref, gseg_ref, nseg_ref, tot_ref, gap_ref, pos_ref, x1_ref, xs_hbm,
                     xs_ref, xb_ref, zero_ref, sem, *, cv_rows):
    ti = pl.program_id(0)
    n_tiles = pl.num_programs(0)
    tile, d = x1_ref.shape

    def out_copy(t, ex):
        def make(off, size):
            src = pl.multiple_of(lbase_ref[t * N_EXPERTS + ex] + off, SEG_PAD)
            dst = pl.multiple_of(gseg_ref[t * N_EXPERTS + ex] + off, SEG_PAD)
            return pltpu.make_async_copy(xb_ref.at[pl.ds(src, size), :], xs_hbm.at[pl.ds(dst, size), :], sem.at[0])
        return make

    @pl.when(ti == 0)
    def _():
        def zero(c, carry):
            r0 = pl.multiple_of(c * cv_rows, cv_rows)
            xs_ref[pl.ds(r0, cv_rows), :] = jnp.zeros((cv_rows, d), F32)
            return carry
        lax.fori_loop(0, xs_ref.shape[0] // cv_rows, zero, 0)
        zero_ref[...] = jnp.zeros_like(zero_ref)

    def sort_rows(grp, carry):
        t0 = pl.multiple_of(grp * SUBLANES, SUBLANES)
        for k in range(SUBLANES):
            row = x1_ref[pl.ds(t0 + k, 1), :]
            xs_ref[pl.ds(pos_ref[0, 0, TOP_K * (t0 + k)], 1), :] = row
            xs_ref[pl.ds(pos_ref[0, 0, TOP_K * (t0 + k) + 1], 1), :] = row
        return carry
    lax.fori_loop(0, tile // SUBLANES, sort_rows, 0)

    @pl.when(ti > 0)
    def _():
        _wait_copies(tot_ref[2 * (ti - 1)], tot_ref[2 * (ti - 1) + 1], out_copy(0, 0))

    def convert(c, carry):
        r0 = pl.multiple_of(c * cv_rows, cv_rows)
        xb_ref[pl.ds(r0, cv_rows), :] = xs_ref[pl.ds(r0, cv_rows), :].astype(BF16)
        return carry
    lax.fori_loop(0, xs_ref.shape[0] // cv_rows, convert, 0)

    def send(ex, carry):
        _segment_copies(nseg_ref[ti * N_EXPERTS + ex], out_copy(ti, ex), lambda cp: cp.start())
        return carry
    lax.fori_loop(0, N_EXPERTS, send, 0)

    @pl.when(ti == n_tiles - 1)
    def _():
        _wait_copies(tot_ref[2 * ti], tot_ref[2 * ti + 1], out_copy(0, 0))

        def fill_copy(row):
            return pltpu.make_async_copy(zero_ref, xs_hbm.at[pl.ds(pl.multiple_of(row, SEG_PAD), SEG_PAD), :],
                                         sem.at[1])

        def fill(ex, carry):
            def one(c, carry2):
                fill_copy(gap_ref[2 * ex] + c * SEG_PAD).start()
                return carry2
            lax.fori_loop(0, gap_ref[2 * ex + 1], one, 0)
            return carry
        lax.fori_loop(0, N_EXPERTS, fill, 0)

        def fill_wait(ex, carry):
            def one(c, carry2):
                fill_copy(0).wait()
                return carry2
            lax.fori_loop(0, gap_ref[2 * ex + 1], one, 0)
            return carry
        lax.fori_loop(0, N_EXPERTS, fill_wait, 0)


def _expert_kernel(be_ref, nu_ref, x_ref, wgu_ref, wd_ref, y_ref):
    @pl.when(pl.program_id(0) < nu_ref[0])
    def _():
        gu = _dot(x_ref[...], wgu_ref[0])
        gate = gu[:, :EXPERT_HIDDEN]
        up = gu[:, EXPERT_HIDDEN:]
        hdn = (gate * jax.nn.sigmoid(gate) * up).astype(BF16)
        y_ref[...] = _dot(hdn, wd_ref[0])


def _combine_kernel(lbase_ref, gseg_ref, nseg_ref, tot_ref, pos_ref, rw_ref, x1_ref, ys_hbm, g2_ref, b2_ref,
                    o_ref, xs_ref, sem, *, ln_rows):
    ti = pl.program_id(0)
    n_tiles = pl.num_programs(0)
    tile, d = x1_ref.shape
    slot = ti % 2

    def in_copy(t, ex, sl):
        def make(off, size):
            src = pl.multiple_of(gseg_ref[t * N_EXPERTS + ex] + off, SEG_PAD)
            dst = pl.multiple_of(lbase_ref[t * N_EXPERTS + ex] + off, SEG_PAD)
            return pltpu.make_async_copy(ys_hbm.at[pl.ds(src, size), :], xs_ref.at[sl, pl.ds(dst, size), :],
                                         sem.at[sl])
        return make

    def fetch(t, sl):
        def one(ex, carry):
            _segment_copies(nseg_ref[t * N_EXPERTS + ex], in_copy(t, ex, sl), lambda cp: cp.start())
            return carry
        lax.fori_loop(0, N_EXPERTS, one, 0)

    @pl.when(ti == 0)
    def _():
        fetch(0, 0)

    @pl.when(ti + 1 < n_tiles)
    def _():
        fetch(ti + 1, 1 - slot)

    _wait_copies(tot_ref[2 * ti], tot_ref[2 * ti + 1], in_copy(0, 0, slot))

    def combine(grp, carry):
        t0 = pl.multiple_of(grp * SUBLANES, SUBLANES)
        for k in range(SUBLANES):
            a = TOP_K * (t0 + k)
            o_ref[pl.ds(t0 + k, 1), :] = (DN_ALPHA * x1_ref[pl.ds(t0 + k, 1), :]
                                          + rw_ref[0, 0, a] * xs_ref[slot, pl.ds(pos_ref[0, 0, a], 1), :]
                                          + rw_ref[0, 0, a + 1] * xs_ref[slot, pl.ds(pos_ref[0, 0, a + 1], 1), :])
        return carry
    lax.fori_loop(0, tile // SUBLANES, combine, 0)

    def norm(c, carry):
        r0 = pl.multiple_of(c * ln_rows, ln_rows)
        o_ref[pl.ds(r0, ln_rows), :] = _layer_norm(o_ref[pl.ds(r0, ln_rows), :], g2_ref[...], b2_ref[...])
        return carry
    lax.fori_loop(0, tile // ln_rows, norm, 0)


def _moe_tile(n):
    return min(MOE_TILE, n)


def _round_up(a, m):
    return (a + m - 1) // m * m


def _moe(x1, route, cnt_run, wgu_bf, wd_bf, g2, b2):
    n, d = x1.shape
    tile = _moe_tile(n)
    assert n % tile == 0
    n_tiles = n // tile
    i32 = jnp.int32
    cnt = cnt_run.reshape(n_tiles, -1, LANES)[:, -1, :N_EXPERTS].astype(i32)
    seg = _round_up(cnt, SEG_PAD)
    lbase = jnp.cumsum(seg, axis=1) - seg
    used = jnp.sum(seg, axis=0)
    region = _round_up(used, EXP_ROWS)
    e_start = jnp.cumsum(region) - region
    gseg = e_start[None, :] + jnp.cumsum(seg, axis=0) - seg
    nseg = seg // SEG_PAD
    per_big = SEG_BIG // SEG_PAD
    tot = jnp.stack([jnp.sum(nseg // per_big, axis=1), jnp.sum(nseg % per_big, axis=1)], axis=1)
    gap = jnp.stack([e_start + used, (region - used) // SEG_PAD], axis=1)
    rows_max = _round_up(TOP_K * n + n_tiles * N_EXPERTS * (SEG_PAD - 1) + N_EXPERTS * (EXP_ROWS - 1), EXP_ROWS)
    n_blocks = rows_max // EXP_ROWS
    n_used = (jnp.sum(region) // EXP_ROWS).astype(i32).reshape(1)
    blk_row = jnp.arange(n_blocks, dtype=i32)[:, None] * EXP_ROWS
    blk_expert = jnp.minimum(jnp.sum((blk_row >= jnp.cumsum(region)[None, :]).astype(i32), axis=1), N_EXPERTS - 1)
    r_exp = route[:, 0:TOP_K].astype(i32).reshape(n_tiles, TOP_K * tile)
    r_rank = route[:, 2 * TOP_K:3 * TOP_K].astype(i32).reshape(n_tiles, TOP_K * tile)
    seg_start = jnp.sum(jnp.where(r_exp[:, :, None] == jnp.arange(N_EXPERTS, dtype=i32), lbase[:, None, :], 0),
                        axis=-1)
    pos = (seg_start + r_rank).reshape(n_tiles, 1, TOP_K * tile)
    r_wgt = route[:, TOP_K:2 * TOP_K].reshape(n_tiles, 1, TOP_K * tile)
    local_rows = _round_up(TOP_K * tile + N_EXPERTS * (SEG_PAD - 1), 256)
    flat = lambda a: a.reshape(-1).astype(i32)

    def smem_spec():
        return pl.BlockSpec((1, 1, TOP_K * tile), lambda t, *_: (t, 0, 0), memory_space=pltpu.SMEM)

    xs_hbm = pl.pallas_call(
        functools.partial(_dispatch_kernel, cv_rows=256),
        grid_spec=pltpu.PrefetchScalarGridSpec(
            num_scalar_prefetch=5,
            grid=(n_tiles,),
            in_specs=[smem_spec(), pl.BlockSpec((tile, d), lambda t, *_: (t, 0))],
            out_specs=pl.BlockSpec(memory_space=pl.ANY),
            scratch_shapes=[
                pltpu.VMEM((local_rows, d), F32),
                pltpu.VMEM((local_rows, d), BF16),
                pltpu.VMEM((SEG_PAD, d), BF16),
                pltpu.SemaphoreType.DMA((2,)),
            ],
        ),
        out_shape=jax.ShapeDtypeStruct((rows_max, d), BF16),
        compiler_params=pltpu.CompilerParams(dimension_semantics=("arbitrary",), vmem_limit_bytes=VMEM_LIMIT),
        name="moe_dispatch",
    )(flat(lbase), flat(gseg), flat(nseg), flat(tot), flat(gap), pos, x1)

    def blk(i, be, nu):
        return jnp.minimum(i, nu[0] - 1)

    ys_hbm = pl.pallas_call(
        _expert_kernel,
        grid_spec=pltpu.PrefetchScalarGridSpec(
            num_scalar_prefetch=2,
            grid=(n_blocks,),
            in_specs=[
                pl.BlockSpec((EXP_ROWS, d), lambda i, be, nu: (blk(i, be, nu), 0)),
                pl.BlockSpec((1, d, 2 * EXPERT_HIDDEN), lambda i, be, nu: (be[blk(i, be, nu)], 0, 0)),
                pl.BlockSpec((1, EXPERT_HIDDEN, d), lambda i, be, nu: (be[blk(i, be, nu)], 0, 0)),
            ],
            out_specs=pl.BlockSpec((EXP_ROWS, d), lambda i, be, nu: (blk(i, be, nu), 0)),
        ),
        out_shape=jax.ShapeDtypeStruct((rows_max, d), F32),
        compiler_params=pltpu.CompilerParams(dimension_semantics=("arbitrary",), vmem_limit_bytes=VMEM_LIMIT),
        name="moe_experts",
    )(blk_expert, n_used, xs_hbm, wgu_bf, wd_bf)

    return pl.pallas_call(
        functools.partial(_combine_kernel, ln_rows=min(256, tile)),
        grid_spec=pltpu.PrefetchScalarGridSpec(
            num_scalar_prefetch=4,
            grid=(n_tiles,),
            in_specs=[
                smem_spec(), smem_spec(),
                pl.BlockSpec((tile, d), lambda t, *_: (t, 0)),
                pl.BlockSpec(memory_space=pl.ANY),
                pl.BlockSpec((1, d), lambda t, *_: (0, 0)),
                pl.BlockSpec((1, d), lambda t, *_: (0, 0)),
            ],
            out_specs=pl.BlockSpec((tile, d), lambda t, *_: (t, 0)),
            scratch_shapes=[pltpu.VMEM((2, local_rows, d), F32), pltpu.SemaphoreType.DMA((2,))],
        ),
        out_shape=jax.ShapeDtypeStruct((n, d), F32),
        compiler_params=pltpu.CompilerParams(dimension_semantics=("arbitrary",), vmem_limit_bytes=VMEM_LIMIT),
        name="moe_combine_ln2",
    )(flat(lbase), flat(gseg), flat(nseg), flat(tot), pos, r_wgt, x1, ys_hbm, g2, b2)


def _block(x, w_in, b_in, lower_bound, hg_norm_g, rel_bias, w_proj_a, w_proj_b, w_out, ln1_g, ln1_b,
           w_group, b_group, w_expert, b_expert, w_gate_up, w_down, ln2_g, ln2_b, *, tm_proj, tm_merge):
    b, s, d = x.shape
    n = b * s
    n_hg = 4 * HG_HEADS * HG_DK
    n_mb = 3 * MB_HEADS * MB_DH
    n_gt = 2 * d
    x2 = x.reshape(n, d)
    hg, mb, gates = _in_proj(x2, w_in.astype(BF16), b_in.reshape(1, -1), n_hg, n_mb, n_gt, tm_proj)
    hg_o = _hgrn(hg.reshape(b, s, n_hg), lower_bound.reshape(1, -1), hg_norm_g.reshape(1, -1))
    mb_o = _moba(mb.reshape(b, s, n_mb), rel_bias)
    w_r = jnp.zeros((d, LANES), F32).at[:, :N_GROUPS].set(w_group).at[:, N_GROUPS:N_GROUPS + N_EXPERTS].set(w_expert)
    w_r_hi = w_r.astype(BF16)
    w_r_lo = (w_r - w_r_hi.astype(F32)).astype(BF16)
    b_r = jnp.zeros((1, LANES), F32).at[0, :N_GROUPS].set(b_group).at[0, N_GROUPS:N_GROUPS + N_EXPERTS].set(b_expert)
    tm_merge = min(tm_merge, n)
    x1, route, cnt_run = _merge(hg_o.reshape(n, -1), mb_o.reshape(n, -1), gates, x2,
                                w_proj_a.astype(BF16), w_proj_b.astype(BF16), w_out.astype(BF16),
                                ln1_g.reshape(1, d), ln1_b.reshape(1, d), jnp.stack([w_r_hi, w_r_lo]), b_r,
                                tm_merge, _moe_tile(n))
    out = _moe(x1, route, cnt_run, w_gate_up.astype(BF16), w_down.astype(BF16),
               ln2_g.reshape(1, d), ln2_b.reshape(1, d))
    return out.reshape(b, s, d)


def kernel(x, w_in, b_in, lb_logits, hg_norm_g, rel_bias, w_proj_a, w_proj_b, w_out, ln1_g, ln1_b, w_group,
           b_group, w_expert, b_expert, w_gate_up, w_down, ln2_g, ln2_b):
    lower_bounds = jnp.cumsum(jax.nn.softmax(lb_logits.astype(F32), axis=0), axis=0)
    l = 0
    return _block(x, w_in[l], b_in[l], lower_bounds[l], hg_norm_g[l], rel_bias, w_proj_a[l], w_proj_b[l],
                  w_out[l], ln1_g[l], ln1_b[l], w_group[l], b_group[l], w_expert[l], b_expert[l],
                  w_gate_up[l], w_down[l], ln2_g[l], ln2_b[l], tm_proj=256, tm_merge=512)
```

```python
import functools
import math

import numpy as np
import jax
import jax.numpy as jnp
from jax import lax
from jax.experimental import pallas as pl
from jax.experimental.pallas import tpu as pltpu

F32 = jnp.float32
BF16 = jnp.bfloat16

HG_HEADS = 4
HG_DK = 128
HG_CHUNK = 128
HG_GROUP = 4
MB_HEADS = 8
MB_DH = 64
MB_BLOCK = 256
MB_TOPK = 3
PIPE_AHEAD = 2
DEN_ROWS = 16
GATE_ROWS = 16
REL_BUCKETS = 32
REL_MAX_DIST = 128
N_GROUPS = 4
EXPERTS_PER_GROUP = 8
N_EXPERTS = N_GROUPS * EXPERTS_PER_GROUP
TOP_K = 2
EXPERT_HIDDEN = 512
MOE_TILE = 1024
SEG_PAD = 16
SEG_BIG = 64
EXP_ROWS = 512
DEPTH = 1
DN_ALPHA = (2.0 * DEPTH) ** 0.25
NORM_EPS = 1e-5
LANES = 128
SUBLANES = 8
VMEM_LIMIT = 56 * 1024 * 1024
NEG_INF = float("-inf")
LOG2_E = 1.4426950408889634


def _split2(a):
    hi = a.astype(BF16)
    lo = (a - hi.astype(F32)).astype(BF16)
    return hi, lo


def _split3(a):
    hi = a.astype(BF16)
    r = a - hi.astype(F32)
    mid = r.astype(BF16)
    lo = (r - mid.astype(F32)).astype(BF16)
    return hi, mid, lo


def _dot_nt(a, b):
    return lax.dot_general(a, b, (((1,), (1,)), ((), ())), preferred_element_type=F32)


def _dot_tn(a, b):
    return lax.dot_general(a, b, (((0,), (0,)), ((), ())), preferred_element_type=F32)


def _dot(a, b):
    return jnp.dot(a, b, preferred_element_type=F32)


def _in_proj_kernel(x_ref, w_ref, b_ref, hg_ref, mb_ref, gt_ref, *, col_chunk):
    xb = x_ref[...].astype(BF16)
    outs = ((hg_ref, 0), (mb_ref, hg_ref.shape[1]), (gt_ref, hg_ref.shape[1] + mb_ref.shape[1]))
    for o_ref, base in outs:
        for c0 in range(0, o_ref.shape[1], col_chunk):
            acc = _dot(xb, w_ref[:, base + c0:base + c0 + col_chunk])
            o_ref[:, c0:c0 + col_chunk] = acc + b_ref[:, base + c0:base + c0 + col_chunk]


def _in_proj(x2, w_bf, b_in, n_hg, n_mb, n_gt, tm):
    n, d = x2.shape
    cols = w_bf.shape[1]
    return pl.pallas_call(
        functools.partial(_in_proj_kernel, col_chunk=512),
        grid=(n // tm,),
        in_specs=[
            pl.BlockSpec((tm, d), lambda i: (i, 0)),
            pl.BlockSpec((d, cols), lambda i: (0, 0)),
            pl.BlockSpec((1, cols), lambda i: (0, 0)),
        ],
        out_specs=[
            pl.BlockSpec((tm, n_hg), lambda i: (i, 0)),
            pl.BlockSpec((tm, n_mb), lambda i: (i, 0)),
            pl.BlockSpec((tm, n_gt), lambda i: (i, 0)),
        ],
        out_shape=[
            jax.ShapeDtypeStruct((n, n_hg), F32),
            jax.ShapeDtypeStruct((n, n_mb), F32),
            jax.ShapeDtypeStruct((n, n_gt), F32),
        ],
        compiler_params=pltpu.CompilerParams(
            dimension_semantics=("parallel",), vmem_limit_bytes=VMEM_LIMIT),
        name="in_proj",
    )(x2, w_bf, b_in)


def _hgrn_tables():
    c = HG_CHUNK
    levels = [c >> (i + 1) for i in range(int(math.log2(c)))]
    masks = np.zeros((len(levels) + 1, c, c), np.float32)
    for li, m in enumerate(levels):
        for r in range(c):
            c0 = (r // (2 * m)) * (2 * m)
            if r - c0 >= m:
                masks[li, r, c0:c0 + m] = 1.0
    masks[len(levels)] = np.eye(c, dtype=np.float32)
    return np.tril(np.ones((c, c), np.float32)), masks, tuple(levels)


def _level_ref_rows(p, m):
    c, w = p.shape
    if 2 * m > SUBLANES:
        parts = [jnp.broadcast_to(p[c0 + m - 1:c0 + m, :], (2 * m, w)) for c0 in range(0, c, 2 * m)]
        return parts[0] if len(parts) == 1 else jnp.concatenate(parts, axis=0)
    p3 = p.reshape(c // SUBLANES, SUBLANES, w)
    sub = lax.broadcasted_iota(jnp.int32, p3.shape, 1)
    out = None
    for c0 in range(0, SUBLANES, 2 * m):
        b = jnp.broadcast_to(p3[:, c0 + m - 1:c0 + m, :], p3.shape)
        out = b if out is None else jnp.where(sub >= c0, b, out)
    return out.reshape(c, w)


def _hgrn_kernel(q_ref, f_ref, i_ref, g_ref, lb_ref, ng_ref, tril_ref, msk_ref, o_ref,
                 *, n_chunks, levels, group):
    c = HG_CHUNK
    dk = HG_DK
    assert n_chunks % group == 0
    lb = jnp.concatenate([lb_ref[...]] * group, axis=1)
    oml = 1.0 - lb
    ng = ng_ref[...]
    tril = tril_ref[...]
    n_levels = len(levels)

    def load(ref, r0):
        return jnp.concatenate([ref[0, pl.ds(r0 + u * c, c), :] for u in range(group)], axis=1)

    def lanes(a, u):
        return a[:, u * dk:(u + 1) * dk]

    def body(gi, st):
        r0 = pl.multiple_of(gi * (group * c), group * c)
        z = load(f_ref, r0)
        qr = load(q_ref, r0)
        lf = jnp.log(lb + oml * jax.nn.sigmoid(z))
        kk = oml * jax.nn.sigmoid(-z)
        qf = qr * jax.nn.sigmoid(qr)
        l_hi, l_mid, l_lo = _split3(lf)
        p = _dot(tril, l_hi) + _dot(tril, l_mid) + _dot(tril, l_lo)
        b_end = p[c - 1:c, :]
        qb = (qf * jnp.exp(p)).astype(BF16)
        kd = (kk * jnp.exp(b_end - p)).astype(BF16)
        dec = jnp.exp(b_end)
        qh = qf.astype(BF16)
        kh = kk.astype(BF16)
        scores = [msk_ref[n_levels] * _dot_nt(lanes(qh, u), lanes(kh, u)) for u in range(group)]
        row = lax.broadcasted_iota(jnp.int32, p.shape, 0)
        for li, m in enumerate(levels):
            ex = jnp.exp2(jnp.abs(p - _level_ref_rows(p, m)) * (-LOG2_E))
            qk = (jnp.where((row & (2 * m - 1)) >= m, qf, kk) * ex).astype(BF16)
            for u in range(group):
                scores[u] = scores[u] + msk_ref[li] * _dot_nt(lanes(qk, u), lanes(qk, u))
        for u in range(group):
            rows = pl.ds(r0 + u * c, c)
            vb = i_ref[0, rows, :].astype(BF16)
            g = g_ref[0, rows, :]
            o = _dot_nt(lanes(qb, u), st.astype(BF16)) + _dot(scores[u].astype(BF16), vb)
            st = st * lanes(dec, u) + _dot_tn(vb, lanes(kd, u))
            o = o * lax.rsqrt(jnp.mean(o * o, axis=-1, keepdims=True) + NORM_EPS)
            o_ref[0, rows, :] = o * ng * (g * jax.nn.sigmoid(g))
        return st

    lax.fori_loop(0, n_chunks // group, body, jnp.zeros((HG_DK, HG_DK), F32))


def _hgrn(hg3, lb_row, ng_row):
    b, s, _ = hg3.shape
    tril, masks, levels = _hgrn_tables()
    mst = jnp.asarray(tril, BF16)
    msk = jnp.asarray(masks, F32)
    h = HG_HEADS

    def col(off):
        return pl.BlockSpec((1, s, HG_DK), lambda bi, hi: (bi, 0, off + hi))

    return pl.pallas_call(
        functools.partial(_hgrn_kernel, n_chunks=s // HG_CHUNK, levels=levels,
                          group=math.gcd(s // HG_CHUNK, HG_GROUP)),
        grid=(b, h),
        in_specs=[
            col(0), col(h), col(2 * h), col(3 * h),
            pl.BlockSpec((1, HG_DK), lambda bi, hi: (0, hi)),
            pl.BlockSpec((1, HG_DK), lambda bi, hi: (0, hi)),
            pl.BlockSpec(mst.shape, lambda bi, hi: (0, 0)),
            pl.BlockSpec(msk.shape, lambda bi, hi: (0, 0, 0)),
        ],
        out_specs=pl.BlockSpec((1, s, HG_DK), lambda bi, hi: (bi, 0, hi)),
        out_shape=jax.ShapeDtypeStruct((b, s, h * HG_DK), F32),
        compiler_params=pltpu.CompilerParams(
            dimension_semantics=("parallel", "parallel"), vmem_limit_bytes=VMEM_LIMIT),
        name="hgrn2",
    )(hg3, hg3, hg3, hg3, lb_row, ng_row, mst, msk)


def _t5_bucket_np(dist):
    max_exact = REL_BUCKETS // 2
    d = np.maximum(dist, 1).astype(np.float32)
    log_part = max_exact + (np.log(d / np.float32(max_exact)) / np.float32(math.log(REL_MAX_DIST / max_exact))
                            * np.float32(REL_BUCKETS - max_exact)).astype(np.int32)
    return np.where(dist < max_exact, dist, np.minimum(log_part, REL_BUCKETS - 1))


def _moba_kernel(q_ref, k_ref, v_ref, avg_ref, bkt_ref, rb_ref, o_ref, own_ref, prev_ref, *, n_blocks):
    blk = MB_BLOCK
    scale = MB_DH ** -0.5 * LOG2_E
    avg = avg_ref[...]
    hp = LANES // MB_DH

    @pl.when(pl.program_id(1) == 0)
    def _():
        causal = (lax.broadcasted_iota(jnp.int32, (blk, blk), 0) <= lax.broadcasted_iota(jnp.int32, (blk, blk), 1))
        for hh in range(hp):
            head = pl.program_id(0) * hp + hh
            own_t = jnp.zeros((blk, blk), F32)
            prev_t = jnp.zeros((blk, blk), F32)
            for bk in range(REL_BUCKETS):
                val = rb_ref[bk, head] * LOG2_E
                own_t = jnp.where(bkt_ref[0] == bk, val, own_t)
                prev_t = jnp.where(bkt_ref[1] == bk, val, prev_t)
            own_ref[hh] = jnp.where(causal, own_t, NEG_INF)
            prev_ref[hh] = prev_t

    grow = lax.broadcasted_iota(jnp.int32, (GATE_ROWS, blk), 0)
    vt_all = v_ref[0].T

    def fold(a):
        return a.reshape(blk // SUBLANES, SUBLANES, blk)

    heads = []
    for hh in range(hp):
        ls = slice(hh * MB_DH, (hh + 1) * MB_DH)
        k_h = k_ref[0, :, ls]
        k_hi, k_lo = _split2(k_h)
        k_mean = (_dot(avg, k_hi) + _dot(avg, k_lo))[:GATE_ROWS]
        vt = jnp.concatenate([vt_all[hh * MB_DH:(hh + 1) * MB_DH, :], jnp.ones((DEN_ROWS, vt_all.shape[1]), F32)],
                             axis=0).astype(BF16)
        far_bias = rb_ref[REL_BUCKETS - 1, pl.program_id(0) * hp + hh] * LOG2_E
        heads.append((ls, _split2(k_mean), k_h.astype(BF16), vt, far_bias))

    def logits(hh, i):
        ls, (km_hi, km_lo), kb, _, far_bias = heads[hh]
        qi = q_ref[0, i * blk:(i + 1) * blk, ls] * scale
        st = _dot_nt(kb[:(i + 1) * blk], qi.astype(BF16))
        sel = None
        if i > 0:
            q_hi, q_lo = _split2(qi)
            gate = _dot_nt(km_hi, q_hi) + _dot_nt(km_lo, q_hi) + _dot_nt(km_hi, q_lo)
            gate = jnp.where(grow < i, gate, NEG_INF)
            rank = jnp.zeros((GATE_ROWS, blk), F32)
            if i > MB_TOPK:
                for j2 in range(i):
                    gj = jnp.broadcast_to(gate[j2:j2 + 1, :], (GATE_ROWS, blk))
                    tie = jnp.where(grow > j2, 1.0, 0.0)
                    rank = rank + jnp.where(gj > gate, 1.0, jnp.where(gj == gate, tie, 0.0))
            sel = jnp.where(rank < MB_TOPK, gate, NEG_INF) > NEG_INF
        pieces = []
        for j in range(i + 1):
            sj = st[j * blk:(j + 1) * blk, :]
            if j == i:
                pieces.append(sj + own_ref[hh])
            elif j == i - 1:
                pieces.append(sj + prev_ref[hh] + jnp.where(sel[j:j + 1, :], 0.0, NEG_INF))
            else:
                pieces.append(sj + jnp.where(sel[j:j + 1, :], far_bias, NEG_INF))
        mx8 = fold(pieces[0]).max(axis=0)
        for p in pieces[1:]:
            mx8 = jnp.maximum(mx8, fold(p).max(axis=0))
        return pieces, mx8.max(axis=0, keepdims=True)

    def attend(hh, pieces, mx):
        vt = heads[hh][3]
        acc = jnp.zeros((MB_DH + DEN_ROWS, blk), F32)
        for j, p in enumerate(pieces):
            acc = acc + _dot(vt[:, j * blk:(j + 1) * blk], jnp.exp2(p - mx).astype(BF16))
        return acc[:MB_DH] / acc[MB_DH:MB_DH + 1]

    items = [(i, hh) for i in range(n_blocks) for hh in range(hp)]
    staged = [logits(hh, i) for i, hh in items[:PIPE_AHEAD]]
    done = {}
    for n, (i, hh) in enumerate(items):
        if n + PIPE_AHEAD < len(items):
            staged.append(logits(items[n + PIPE_AHEAD][1], items[n + PIPE_AHEAD][0]))
        done[hh] = attend(hh, *staged.pop(0))
        if hh == hp - 1:
            o_ref[0, i * blk:(i + 1) * blk, :] = jnp.concatenate([done[h] for h in range(hp)], axis=0).T


def _moba(mb3, rel_bias):
    b, s, _ = mb3.shape
    assert s % MB_BLOCK == 0
    nb = s // MB_BLOCK
    hp = LANES // MB_DH
    n_hp = MB_HEADS // hp
    t = np.arange(MB_BLOCK)
    d_own = t[:, None] - t[None, :]
    buckets = np.stack([_t5_bucket_np(np.maximum(d_own, 0)), _t5_bucket_np(d_own + MB_BLOCK)]).astype(np.int32)
    assert _t5_bucket_np(np.array([MB_BLOCK + 1]))[0] == REL_BUCKETS - 1
    assert nb <= GATE_ROWS
    bkt = jnp.asarray(buckets.transpose(0, 2, 1))
    avg_np = np.zeros((LANES, s), np.float32)
    for j in range(nb):
        avg_np[j, j * MB_BLOCK:(j + 1) * MB_BLOCK] = 1.0 / MB_BLOCK
    avg = jnp.asarray(avg_np, BF16)

    def col(off):
        return pl.BlockSpec((1, s, LANES), lambda hi, bi: (bi, 0, off + hi))

    return pl.pallas_call(
        functools.partial(_moba_kernel, n_blocks=nb),
        grid=(n_hp, b),
        in_specs=[
            col(0), col(n_hp), col(2 * n_hp),
            pl.BlockSpec(avg.shape, lambda hi, bi: (0, 0)),
            pl.BlockSpec(bkt.shape, lambda hi, bi: (0, 0, 0)),
            pl.BlockSpec(memory_space=pltpu.SMEM),
        ],
        out_specs=pl.BlockSpec((1, s, LANES), lambda hi, bi: (bi, 0, hi)),
        out_shape=jax.ShapeDtypeStruct((b, s, MB_HEADS * MB_DH), F32),
        scratch_shapes=[pltpu.VMEM((hp, MB_BLOCK, MB_BLOCK), F32), pltpu.VMEM((hp, MB_BLOCK, MB_BLOCK), F32)],
        compiler_params=pltpu.CompilerParams(
            dimension_semantics=("parallel", "arbitrary"), vmem_limit_bytes=VMEM_LIMIT),
        name="moba",
    )(mb3, mb3, mb3, avg, bkt, rel_bias.astype(F32))


def _layer_norm(x, g, b):
    mu = jnp.mean(x, axis=-1, keepdims=True)
    xc = x - mu
    var = jnp.mean(xc * xc, axis=-1, keepdims=True)
    return xc * lax.rsqrt(var + NORM_EPS) * g + b


def _merge_kernel(hg_ref, mb_ref, ga_ref, gb_ref, x_ref, wa_ref, wb_ref, wo_ref, g1_ref, b1_ref,
                  wr_ref, br_ref, tri_ref, x1_ref, route_ref, cnt_ref, run_ref, *, blocks_per_moe_tile, parts):
    tm = x_ref.shape[0]
    pr = tm // parts
    lane = lax.broadcasted_iota(jnp.int32, (pr, LANES), 1)
    w_hi, w_lo = wr_ref[0], wr_ref[1]

    @pl.when(pl.program_id(0) % blocks_per_moe_tile == 0)
    def _():
        run_ref[...] = jnp.zeros_like(run_ref)

    def mix(rows):
        ya = _dot(hg_ref[rows, :].astype(BF16), wa_ref[...])
        yb = _dot(mb_ref[rows, :].astype(BF16), wb_ref[...])
        mixed_in = jax.nn.sigmoid(ga_ref[rows, :]) * ya + jax.nn.sigmoid(gb_ref[rows, :]) * yb
        mixed = _dot(mixed_in.astype(BF16), wo_ref[...])
        x1 = _layer_norm(DN_ALPHA * x_ref[rows, :] + mixed, g1_ref[...], b1_ref[...])
        x1_ref[rows, :] = x1
        return x1

    def route_rows(rows, x1, run):
        x_hi, x_lo = _split2(x1)
        logits = _dot(x_hi, w_hi) + _dot(x_hi, w_lo) + _dot(x_lo, w_hi) + br_ref[...]
        glog = jnp.where(lane < N_GROUPS, logits, NEG_INF)
        gmax = jnp.max(glog, axis=-1, keepdims=True)
        grp = jnp.min(jnp.where(glog == gmax, lane, LANES), axis=-1, keepdims=True)
        p_grp = 1.0 / jnp.sum(jnp.exp(glog - gmax), axis=-1, keepdims=True)
        e_lo = N_GROUPS + grp * EXPERTS_PER_GROUP
        elog = jnp.where(jnp.logical_and(lane >= e_lo, lane < e_lo + EXPERTS_PER_GROUP), logits, NEG_INF)
        m1 = jnp.max(elog, axis=-1, keepdims=True)
        i1 = jnp.min(jnp.where(elog == m1, lane, LANES), axis=-1, keepdims=True)
        elog2 = jnp.where(lane == i1, NEG_INF, elog)
        m2 = jnp.max(elog2, axis=-1, keepdims=True)
        i2 = jnp.min(jnp.where(elog2 == m2, lane, LANES), axis=-1, keepdims=True)
        e2 = jnp.exp(m2 - m1)
        w1 = p_grp / (1.0 + e2)
        w2 = p_grp * e2 / (1.0 + e2)
        oh1 = jnp.where(lane == i1 - N_GROUPS, 1.0, 0.0)
        oh2 = jnp.where(lane == i2 - N_GROUPS, 1.0, 0.0)
        both = oh1 + oh2
        before = _dot(tri_ref[...], both.astype(BF16)) + run
        r1 = jnp.sum(before * oh1, axis=-1, keepdims=True)
        r2 = jnp.sum(before * oh2, axis=-1, keepdims=True)
        cols = ((i1 - N_GROUPS).astype(F32), (i2 - N_GROUPS).astype(F32), w1, w2, r1, r2)
        route = jnp.zeros((pr, LANES), F32)
        for li, col in enumerate(cols):
            route = jnp.where(lane == li, col, route)
        route_ref[rows, :] = route
        return run + jnp.sum(both, axis=0, keepdims=True)

    slabs = [slice(h * pr, (h + 1) * pr) for h in range(parts)]
    mixed = [mix(rows) for rows in slabs]
    run = run_ref[...]
    for rows, x1 in zip(slabs, mixed):
        run = route_rows(rows, x1, run)
    run_ref[...] = run
    cnt_ref[0] = run


def _merge(hg_o, mb_o, gates, x2, wa, wb, wo, g1, b1, wr, br, tm, moe_tile):
    n, d = x2.shape
    wa_n = hg_o.shape[1]
    wb_n = mb_o.shape[1]
    assert moe_tile % tm == 0
    parts = 1
    tri = jnp.asarray(np.tril(np.ones((tm // parts, tm // parts), np.float32), -1), BF16)

    def full(a):
        nd = a.ndim
        return pl.BlockSpec(a.shape, lambda i: (0,) * nd)

    return pl.pallas_call(
        functools.partial(_merge_kernel, blocks_per_moe_tile=moe_tile // tm, parts=parts),
        grid=(n // tm,),
        in_specs=[
            pl.BlockSpec((tm, wa_n), lambda i: (i, 0)),
            pl.BlockSpec((tm, wb_n), lambda i: (i, 0)),
            pl.BlockSpec((tm, d), lambda i: (i, 0)),
            pl.BlockSpec((tm, d), lambda i: (i, 1)),
            pl.BlockSpec((tm, d), lambda i: (i, 0)),
            full(wa), full(wb), full(wo), full(g1), full(b1), full(wr), full(br), full(tri),
        ],
        out_specs=[
            pl.BlockSpec((tm, d), lambda i: (i, 0)),
            pl.BlockSpec((tm, LANES), lambda i: (i, 0)),
            pl.BlockSpec((1, 1, LANES), lambda i: (i, 0, 0)),
        ],
        out_shape=[
            jax.ShapeDtypeStruct((n, d), F32),
            jax.ShapeDtypeStruct((n, LANES), F32),
            jax.ShapeDtypeStruct((n // tm, 1, LANES), F32),
        ],
        scratch_shapes=[pltpu.VMEM((1, LANES), F32)],
        compiler_params=pltpu.CompilerParams(
            dimension_semantics=("arbitrary",), vmem_limit_bytes=VMEM_LIMIT),
        name="merge_ln1_router",
    )(hg_o, mb_o, gates, gates, x2, wa, wb, wo, g1, b1, wr, br, tri)


def _segment_copies(n_small, make_copy, act):
    per_big = SEG_BIG // SEG_PAD
    n_big = n_small // per_big

    def big(c, carry):
        act(make_copy(c * SEG_BIG, SEG_BIG))
        return carry
    lax.fori_loop(0, n_big, big, 0)

    def small(c, carry):
        act(make_copy(n_big * SEG_BIG + c * SEG_PAD, SEG_PAD))
        return carry
    lax.fori_loop(0, n_small - n_big * per_big, small, 0)


def _wait_copies(n_big, n_small, make_copy):
    def big(c, carry):
        make_copy(0, SEG_BIG).wait()
        return carry
    lax.fori_loop(0, n_big, big, 0)

    def small(c, carry):
        make_copy(0, SEG_PAD).wait()
        return carry
    lax.fori_loop(0, n_small, small, 0)


def _dispatch_kernel(lbase_ref, gseg_ref, nseg_ref, tot_ref, gap_ref, pos_ref, x1_ref, xs_hbm,
                     xs_ref, xb_ref, zero_ref, sem, *, cv_rows):
    ti = pl.program_id(0)
    n_tiles = pl.num_programs(0)
    tile, d = x1_ref.shape

    def out_copy(t, ex):
        def make(off, size):
            src = pl.multiple_of(lbase_ref[t * N_EXPERTS + ex] + off, SEG_PAD)
            dst = pl.multiple_of(gseg_ref[t * N_EXPERTS + ex] + off, SEG_PAD)
            return pltpu.make_async_copy(xb_ref.at[pl.ds(src, size), :], xs_hbm.at[pl.ds(dst, size), :], sem.at[0])
        return make

    @pl.when(ti == 0)
    def _():
        def zero(c, carry):
            r0 = pl.multiple_of(c * cv_rows, cv_rows)
            xs_ref[pl.ds(r0, cv_rows), :] = jnp.zeros((cv_rows, d), F32)
            return carry
        lax.fori_loop(0, xs_ref.shape[0] // cv_rows, zero, 0)
        zero_ref[...] = jnp.zeros_like(zero_ref)

    def sort_rows(grp, carry):
        t0 = pl.multiple_of(grp * SUBLANES, SUBLANES)
        for k in range(SUBLANES):
            row = x1_ref[pl.ds(t0 + k, 1), :]
            xs_ref[pl.ds(pos_ref[0, 0, TOP_K * (t0 + k)], 1), :] = row
            xs_ref[pl.ds(pos_ref[0, 0, TOP_K * (t0 + k) + 1], 1), :] = row
        return carry
    lax.fori_loop(0, tile // SUBLANES, sort_rows, 0)

    @pl.when(ti > 0)
    def _():
        _wait_copies(tot_ref[2 * (ti - 1)], tot_ref[2 * (ti - 1) + 1], out_copy(0, 0))

    def convert(c, carry):
        r0 = pl.multiple_of(c * cv_rows, cv_rows)
        xb_ref[pl.ds(r0, cv_rows), :] = xs_ref[pl.ds(r0, cv_rows), :].astype(BF16)
        return carry
    lax.fori_loop(0, xs_ref.shape[0] // cv_rows, convert, 0)

    def send(ex, carry):
        _segment_copies(nseg_ref[ti * N_EXPERTS + ex], out_copy(ti, ex), lambda cp: cp.start())
        return carry
    lax.fori_loop(0, N_EXPERTS, send, 0)

    @pl.when(ti == n_tiles - 1)
    def _():
        _wait_copies(tot_ref[2 * ti], tot_ref[2 * ti + 1], out_copy(0, 0))

        def fill_copy(row):
            return pltpu.make_async_copy(zero_ref, xs_hbm.at[pl.ds(pl.multiple_of(row, SEG_PAD), SEG_PAD), :],
                                         sem.at[1])

        def fill(ex, carry):
            def one(c, carry2):
                fill_copy(gap_ref[2 * ex] + c * SEG_PAD).start()
                return carry2
            lax.fori_loop(0, gap_ref[2 * ex + 1], one, 0)
            return carry
        lax.fori_loop(0, N_EXPERTS, fill, 0)

        def fill_wait(ex, carry):
            def one(c, carry2):
                fill_copy(0).wait()
                return carry2
            lax.fori_loop(0, gap_ref[2 * ex + 1], one, 0)
            return carry
        lax.fori_loop(0, N_EXPERTS, fill_wait, 0)


def _expert_kernel(be_ref, nu_ref, x_ref, wgu_ref, wd_ref, y_ref, wgu_bf, wd_bf, *, cast_rows):
    i = pl.program_id(0)

    @pl.when(i < nu_ref[0])
    def _():
        @pl.when(jnp.logical_or(i == 0, be_ref[i] != be_ref[jnp.maximum(i - 1, 0)]))
        def _():
            for r0 in range(0, wgu_bf.shape[0], cast_rows):
                wgu_bf[r0:r0 + cast_rows, :] = wgu_ref[0, r0:r0 + cast_rows, :].astype(BF16)
            for r0 in range(0, wd_bf.shape[0], cast_rows):
                wd_bf[r0:r0 + cast_rows, :] = wd_ref[0, r0:r0 + cast_rows, :].astype(BF16)

        gu = _dot(x_ref[...], wgu_bf[...])
        gate = gu[:, :EXPERT_HIDDEN]
        up = gu[:, EXPERT_HIDDEN:]
        hdn = (gate * jax.nn.sigmoid(gate) * up).astype(BF16)
        y_ref[...] = _dot(hdn, wd_bf[...])


def _combine_kernel(lbase_ref, gseg_ref, nseg_ref, tot_ref, pos_ref, rw_ref, x1_ref, ys_hbm, g2_ref, b2_ref,
                    o_ref, xs_ref, sem, *, ln_rows):
    ti = pl.program_id(0)
    n_tiles = pl.num_programs(0)
    tile, d = x1_ref.shape
    slot = ti % 2

    def in_copy(t, ex, sl):
        def make(off, size):
            src = pl.multiple_of(gseg_ref[t * N_EXPERTS + ex] + off, SEG_PAD)
            dst = pl.multiple_of(lbase_ref[t * N_EXPERTS + ex] + off, SEG_PAD)
            return pltpu.make_async_copy(ys_hbm.at[pl.ds(src, size), :], xs_ref.at[sl, pl.ds(dst, size), :],
                                         sem.at[sl])
        return make

    def fetch(t, sl):
        def one(ex, carry):
            _segment_copies(nseg_ref[t * N_EXPERTS + ex], in_copy(t, ex, sl), lambda cp: cp.start())
            return carry
        lax.fori_loop(0, N_EXPERTS, one, 0)

    @pl.when(ti == 0)
    def _():
        fetch(0, 0)

    @pl.when(ti + 1 < n_tiles)
    def _():
        fetch(ti + 1, 1 - slot)

    _wait_copies(tot_ref[2 * ti], tot_ref[2 * ti + 1], in_copy(0, 0, slot))

    def combine(grp, carry):
        t0 = pl.multiple_of(grp * SUBLANES, SUBLANES)
        for k in range(SUBLANES):
            a = TOP_K * (t0 + k)
            o_ref[pl.ds(t0 + k, 1), :] = (DN_ALPHA * x1_ref[pl.ds(t0 + k, 1), :]
                                          + rw_ref[0, 0, a] * xs_ref[slot, pl.ds(pos_ref[0, 0, a], 1), :]
                                          + rw_ref[0, 0, a + 1] * xs_ref[slot, pl.ds(pos_ref[0, 0, a + 1], 1), :])
        return carry
    lax.fori_loop(0, tile // SUBLANES, combine, 0)

    def norm(c, carry):
        r0 = pl.multiple_of(c * ln_rows, ln_rows)
        o_ref[pl.ds(r0, ln_rows), :] = _layer_norm(o_ref[pl.ds(r0, ln_rows), :], g2_ref[...], b2_ref[...])
        return carry
    lax.fori_loop(0, tile // ln_rows, norm, 0)


def _moe_tile(n):
    return min(MOE_TILE, n)


def _round_up(a, m):
    return (a + m - 1) // m * m


def _moe(x1, route, cnt_run, w_gate_up, w_down, g2, b2):
    n, d = x1.shape
    tile = _moe_tile(n)
    assert n % tile == 0
    n_tiles = n // tile
    i32 = jnp.int32
    cnt = cnt_run.reshape(n_tiles, -1, LANES)[:, -1, :N_EXPERTS].astype(i32)
    seg = _round_up(cnt, SEG_PAD)
    lbase = jnp.cumsum(seg, axis=1) - seg
    used = jnp.sum(seg, axis=0)
    region = _round_up(used, EXP_ROWS)
    e_start = jnp.cumsum(region) - region
    gseg = e_start[None, :] + jnp.cumsum(seg, axis=0) - seg
    nseg = seg // SEG_PAD
    per_big = SEG_BIG // SEG_PAD
    tot = jnp.stack([jnp.sum(nseg // per_big, axis=1), jnp.sum(nseg % per_big, axis=1)], axis=1)
    gap = jnp.stack([e_start + used, (region - used) // SEG_PAD], axis=1)
    rows_max = _round_up(TOP_K * n + n_tiles * N_EXPERTS * (SEG_PAD - 1) + N_EXPERTS * (EXP_ROWS - 1), EXP_ROWS)
    n_blocks = rows_max // EXP_ROWS
    n_used = (jnp.sum(region) // EXP_ROWS).astype(i32).reshape(1)
    blk_row = jnp.arange(n_blocks, dtype=i32)[:, None] * EXP_ROWS
    blk_expert = jnp.minimum(jnp.sum((blk_row >= jnp.cumsum(region)[None, :]).astype(i32), axis=1), N_EXPERTS - 1)
    r_exp = route[:, 0:TOP_K].astype(i32).reshape(n_tiles, TOP_K * tile)
    r_rank = route[:, 2 * TOP_K:3 * TOP_K].astype(i32).reshape(n_tiles, TOP_K * tile)
    seg_start = jnp.sum(jnp.where(r_exp[:, :, None] == jnp.arange(N_EXPERTS, dtype=i32), lbase[:, None, :], 0),
                        axis=-1)
    pos = (seg_start + r_rank).reshape(n_tiles, 1, TOP_K * tile)
    r_wgt = route[:, TOP_K:2 * TOP_K].reshape(n_tiles, 1, TOP_K * tile)
    local_rows = _round_up(TOP_K * tile + N_EXPERTS * (SEG_PAD - 1), 256)
    flat = lambda a: a.reshape(-1).astype(i32)

    def smem_spec():
        return pl.BlockSpec((1, 1, TOP_K * tile), lambda t, *_: (t, 0, 0), memory_space=pltpu.SMEM)

    xs_hbm = pl.pallas_call(
        functools.partial(_dispatch_kernel, cv_rows=256),
        grid_spec=pltpu.PrefetchScalarGridSpec(
            num_scalar_prefetch=5,
            grid=(n_tiles,),
            in_specs=[smem_spec(), pl.BlockSpec((tile, d), lambda t, *_: (t, 0))],
            out_specs=pl.BlockSpec(memory_space=pl.ANY),
            scratch_shapes=[
                pltpu.VMEM((local_rows, d), F32),
                pltpu.VMEM((local_rows, d), BF16),
                pltpu.VMEM((SEG_PAD, d), BF16),
                pltpu.SemaphoreType.DMA((2,)),
            ],
        ),
        out_shape=jax.ShapeDtypeStruct((rows_max, d), BF16),
        compiler_params=pltpu.CompilerParams(dimension_semantics=("arbitrary",), vmem_limit_bytes=VMEM_LIMIT),
        name="moe_dispatch",
    )(flat(lbase), flat(gseg), flat(nseg), flat(tot), flat(gap), pos, x1)

    def blk(i, be, nu):
        return jnp.minimum(i, nu[0] - 1)

    ys_hbm = pl.pallas_call(
        functools.partial(_expert_kernel, cast_rows=256),
        grid_spec=pltpu.PrefetchScalarGridSpec(
            num_scalar_prefetch=2,
            grid=(n_blocks,),
            in_specs=[
                pl.BlockSpec((EXP_ROWS, d), lambda i, be, nu: (blk(i, be, nu), 0)),
                pl.BlockSpec((1, d, 2 * EXPERT_HIDDEN), lambda i, be, nu: (be[blk(i, be, nu)], 0, 0)),
                pl.BlockSpec((1, EXPERT_HIDDEN, d), lambda i, be, nu: (be[blk(i, be, nu)], 0, 0)),
            ],
            out_specs=pl.BlockSpec((EXP_ROWS, d), lambda i, be, nu: (blk(i, be, nu), 0)),
            scratch_shapes=[pltpu.VMEM((d, 2 * EXPERT_HIDDEN), BF16), pltpu.VMEM((EXPERT_HIDDEN, d), BF16)],
        ),
        out_shape=jax.ShapeDtypeStruct((rows_max, d), F32),
        compiler_params=pltpu.CompilerParams(dimension_semantics=("arbitrary",), vmem_limit_bytes=VMEM_LIMIT),
        name="moe_experts",
    )(blk_expert, n_used, xs_hbm, w_gate_up, w_down)

    return pl.pallas_call(
        functools.partial(_combine_kernel, ln_rows=min(256, tile)),
        grid_spec=pltpu.PrefetchScalarGridSpec(
            num_scalar_prefetch=4,
            grid=(n_tiles,),
            in_specs=[
                smem_spec(), smem_spec(),
                pl.BlockSpec((tile, d), lambda t, *_: (t, 0)),
                pl.BlockSpec(memory_space=pl.ANY),
                pl.BlockSpec((1, d), lambda t, *_: (0, 0)),
                pl.BlockSpec((1, d), lambda t, *_: (0, 0)),
            ],
            out_specs=pl.BlockSpec((tile, d), lambda t, *_: (t, 0)),
            scratch_shapes=[pltpu.VMEM((2, local_rows, d), F32), pltpu.SemaphoreType.DMA((2,))],
        ),
        out_shape=jax.ShapeDtypeStruct((n, d), F32),
        compiler_params=pltpu.CompilerParams(dimension_semantics=("arbitrary",), vmem_limit_bytes=VMEM_LIMIT),
        name="moe_combine_ln2",
    )(flat(lbase), flat(gseg), flat(nseg), flat(tot), pos, r_wgt, x1, ys_hbm, g2, b2)


def _block(x, w_in, b_in, lower_bound, hg_norm_g, rel_bias, w_proj_a, w_proj_b, w_out, ln1_g, ln1_b,
           w_group, b_group, w_expert, b_expert, w_gate_up, w_down, ln2_g, ln2_b, *, tm_proj, tm_merge):
    b, s, d = x.shape
    n = b * s
    n_hg = 4 * HG_HEADS * HG_DK
    n_mb = 3 * MB_HEADS * MB_DH
    n_gt = 2 * d
    x2 = x.reshape(n, d)
    hg, mb, gates = _in_proj(x2, w_in.astype(BF16), b_in.reshape(1, -1), n_hg, n_mb, n_gt, tm_proj)
    hg_o = _hgrn(hg.reshape(b, s, n_hg), lower_bound.reshape(1, -1), hg_norm_g.reshape(1, -1))
    mb_o = _moba(mb.reshape(b, s, n_mb), rel_bias)
    w_r = jnp.zeros((d, LANES), F32).at[:, :N_GROUPS].set(w_group).at[:, N_GROUPS:N_GROUPS + N_EXPERTS].set(w_expert)
    w_r_hi = w_r.astype(BF16)
    w_r_lo = (w_r - w_r_hi.astype(F32)).astype(BF16)
    b_r = jnp.zeros((1, LANES), F32).at[0, :N_GROUPS].set(b_group).at[0, N_GROUPS:N_GROUPS + N_EXPERTS].set(b_expert)
    tm_merge = min(tm_merge, n)
    x1, route, cnt_run = _merge(hg_o.reshape(n, -1), mb_o.reshape(n, -1), gates, x2,
                                w_proj_a.astype(BF16), w_proj_b.astype(BF16), w_out.astype(BF16),
                                ln1_g.reshape(1, d), ln1_b.reshape(1, d), jnp.stack([w_r_hi, w_r_lo]), b_r,
                                tm_merge, _moe_tile(n))
    out = _moe(x1, route, cnt_run, w_gate_up, w_down, ln2_g.reshape(1, d), ln2_b.reshape(1, d))
    return out.reshape(b, s, d)


def kernel(x, w_in, b_in, lb_logits, hg_norm_g, rel_bias, w_proj_a, w_proj_b, w_out, ln1_g, ln1_b, w_group,
           b_group, w_expert, b_expert, w_gate_up, w_down, ln2_g, ln2_b):
    lower_bounds = jnp.cumsum(jax.nn.softmax(lb_logits.astype(F32), axis=0), axis=0)
    l = 0
    return _block(x, w_in[l], b_in[l], lower_bounds[l], hg_norm_g[l], rel_bias, w_proj_a[l], w_proj_b[l],
                  w_out[l], ln1_g[l], ln1_b[l], w_group[l], b_group[l], w_expert[l], b_expert[l],
                  w_gate_up[l], w_down[l], ln2_g[l], ln2_b[l], tm_proj=256, tm_merge=512)
```

```python
import functools
import math

import numpy as np
import jax
import jax.numpy as jnp
from jax import lax
from jax.experimental import pallas as pl
from jax.experimental.pallas import tpu as pltpu

F32 = jnp.float32
BF16 = jnp.bfloat16

HG_HEADS = 4
HG_DK = 128
HG_CHUNK = 128
HG_GROUP = 4
MB_HEADS = 8
MB_DH = 64
MB_BLOCK = 256
MB_TOPK = 3
PIPE_AHEAD = 2
DEN_ROWS = 16
GATE_ROWS = 16
REL_BUCKETS = 32
REL_MAX_DIST = 128
N_GROUPS = 4
EXPERTS_PER_GROUP = 8
N_EXPERTS = N_GROUPS * EXPERTS_PER_GROUP
TOP_K = 2
EXPERT_HIDDEN = 512
MOE_TILE = 1024
SEG_PAD = 16
SEG_BIG = 64
EXP_ROWS = 512
DEPTH = 1
DN_ALPHA = (2.0 * DEPTH) ** 0.25
NORM_EPS = 1e-5
LANES = 128
SUBLANES = 8
VMEM_LIMIT = 56 * 1024 * 1024
NEG_INF = float("-inf")
LOG2_E = 1.4426950408889634


def _split2(a):
    hi = a.astype(BF16)
    lo = (a - hi.astype(F32)).astype(BF16)
    return hi, lo


def _split3(a):
    hi = a.astype(BF16)
    r = a - hi.astype(F32)
    mid = r.astype(BF16)
    lo = (r - mid.astype(F32)).astype(BF16)
    return hi, mid, lo


def _dot_nt(a, b):
    return lax.dot_general(a, b, (((1,), (1,)), ((), ())), preferred_element_type=F32)


def _dot_tn(a, b):
    return lax.dot_general(a, b, (((0,), (0,)), ((), ())), preferred_element_type=F32)


def _dot(a, b):
    return jnp.dot(a, b, preferred_element_type=F32)


def _in_proj_kernel(x_ref, w_ref, b_ref, hg_ref, mb_ref, gt_ref, *, col_chunk):
    xb = x_ref[...].astype(BF16)
    outs = ((hg_ref, 0), (mb_ref, hg_ref.shape[1]), (gt_ref, hg_ref.shape[1] + mb_ref.shape[1]))
    for o_ref, base in outs:
        for c0 in range(0, o_ref.shape[1], col_chunk):
            acc = _dot(xb, w_ref[:, base + c0:base + c0 + col_chunk])
            o_ref[:, c0:c0 + col_chunk] = acc + b_ref[:, base + c0:base + c0 + col_chunk]


def _in_proj(x2, w_bf, b_in, n_hg, n_mb, n_gt, tm):
    n, d = x2.shape
    cols = w_bf.shape[1]
    return pl.pallas_call(
        functools.partial(_in_proj_kernel, col_chunk=512),
        grid=(n // tm,),
        in_specs=[
            pl.BlockSpec((tm, d), lambda i: (i, 0)),
            pl.BlockSpec((d, cols), lambda i: (0, 0)),
            pl.BlockSpec((1, cols), lambda i: (0, 0)),
        ],
        out_specs=[
            pl.BlockSpec((tm, n_hg), lambda i: (i, 0)),
            pl.BlockSpec((tm, n_mb), lambda i: (i, 0)),
            pl.BlockSpec((tm, n_gt), lambda i: (i, 0)),
        ],
        out_shape=[
            jax.ShapeDtypeStruct((n, n_hg), F32),
            jax.ShapeDtypeStruct((n, n_mb), F32),
            jax.ShapeDtypeStruct((n, n_gt), F32),
        ],
        compiler_params=pltpu.CompilerParams(
            dimension_semantics=("parallel",), vmem_limit_bytes=VMEM_LIMIT),
        name="in_proj",
    )(x2, w_bf, b_in)


def _hgrn_tables():
    c = HG_CHUNK
    levels = [c >> (i + 1) for i in range(int(math.log2(c)))]
    masks = np.zeros((len(levels) + 1, c, c), np.float32)
    for li, m in enumerate(levels):
        for r in range(c):
            c0 = (r // (2 * m)) * (2 * m)
            if r - c0 >= m:
                masks[li, r, c0:c0 + m] = 1.0
    masks[len(levels)] = np.eye(c, dtype=np.float32)
    return np.tril(np.ones((c, c), np.float32)), masks, tuple(levels)


def _level_ref_rows(p, m):
    c, w = p.shape
    if 2 * m > SUBLANES:
        parts = [jnp.broadcast_to(p[c0 + m - 1:c0 + m, :], (2 * m, w)) for c0 in range(0, c, 2 * m)]
        return parts[0] if len(parts) == 1 else jnp.concatenate(parts, axis=0)
    p3 = p.reshape(c // SUBLANES, SUBLANES, w)
    sub = lax.broadcasted_iota(jnp.int32, p3.shape, 1)
    out = None
    for c0 in range(0, SUBLANES, 2 * m):
        b = jnp.broadcast_to(p3[:, c0 + m - 1:c0 + m, :], p3.shape)
        out = b if out is None else jnp.where(sub >= c0, b, out)
    return out.reshape(c, w)


def _hgrn_kernel(q_ref, f_ref, i_ref, g_ref, lb_ref, ng_ref, tril_ref, msk_ref, o_ref,
                 *, n_chunks, levels, group):
    c = HG_CHUNK
    dk = HG_DK
    assert n_chunks % group == 0
    lb = jnp.concatenate([lb_ref[...]] * group, axis=1)
    oml = 1.0 - lb
    ng = ng_ref[...]
    tril = tril_ref[...]
    n_levels = len(levels)

    def load(ref, r0):
        return jnp.concatenate([ref[0, pl.ds(r0 + u * c, c), :] for u in range(group)], axis=1)

    def lanes(a, u):
        return a[:, u * dk:(u + 1) * dk]

    def intra(r0):
        z = load(f_ref, r0)
        qr = load(q_ref, r0)
        lf = jnp.log(lb + oml * jax.nn.sigmoid(z))
        kk = oml * jax.nn.sigmoid(-z)
        qf = qr * jax.nn.sigmoid(qr)
        l_hi, l_mid, l_lo = _split3(lf)
        p = _dot(tril, l_hi) + _dot(tril, l_mid) + _dot(tril, l_lo)
        b_end = p[c - 1:c, :]
        qb = (qf * jnp.exp(p)).astype(BF16)
        kd = (kk * jnp.exp(b_end - p)).astype(BF16)
        dec = jnp.exp(b_end)
        qh = qf.astype(BF16)
        kh = kk.astype(BF16)
        scores = [msk_ref[n_levels] * _dot_nt(lanes(qh, u), lanes(kh, u)) for u in range(group)]
        row = lax.broadcasted_iota(jnp.int32, p.shape, 0)
        for li, m in enumerate(levels):
            ex = jnp.exp2(jnp.abs(p - _level_ref_rows(p, m)) * (-LOG2_E))
            qk = (jnp.where((row & (2 * m - 1)) >= m, qf, kk) * ex).astype(BF16)
            for u in range(group):
                scores[u] = scores[u] + msk_ref[li] * _dot_nt(lanes(qk, u), lanes(qk, u))
        vbs = [i_ref[0, pl.ds(r0 + u * c, c), :].astype(BF16) for u in range(group)]
        o_intra = [_dot(scores[u].astype(BF16), vbs[u]) for u in range(group)]
        return qb, kd, dec, vbs, o_intra

    def chain(r0, staged, st):
        qb, kd, dec, vbs, o_intra = staged
        for u in range(group):
            rows = pl.ds(r0 + u * c, c)
            g = g_ref[0, rows, :]
            o = _dot_nt(lanes(qb, u), st.astype(BF16)) + o_intra[u]
            st = st * lanes(dec, u) + _dot_tn(vbs[u], lanes(kd, u))
            o = o * lax.rsqrt(jnp.mean(o * o, axis=-1, keepdims=True) + NORM_EPS)
            o_ref[0, rows, :] = o * ng * (g * jax.nn.sigmoid(g))
        return st

    n_groups = n_chunks // group
    st = jnp.zeros((HG_DK, HG_DK), F32)
    staged = intra(0)
    for gi in range(n_groups):
        nxt = intra((gi + 1) * group * c) if gi + 1 < n_groups else None
        st = chain(gi * group * c, staged, st)
        staged = nxt


def _hgrn(hg3, lb_row, ng_row):
    b, s, _ = hg3.shape
    tril, masks, levels = _hgrn_tables()
    mst = jnp.asarray(tril, BF16)
    msk = jnp.asarray(masks, F32)
    h = HG_HEADS

    def col(off):
        return pl.BlockSpec((1, s, HG_DK), lambda bi, hi: (bi, 0, off + hi))

    return pl.pallas_call(
        functools.partial(_hgrn_kernel, n_chunks=s // HG_CHUNK, levels=levels,
                          group=math.gcd(s // HG_CHUNK, HG_GROUP)),
        grid=(b, h),
        in_specs=[
            col(0), col(h), col(2 * h), col(3 * h),
            pl.BlockSpec((1, HG_DK), lambda bi, hi: (0, hi)),
            pl.BlockSpec((1, HG_DK), lambda bi, hi: (0, hi)),
            pl.BlockSpec(mst.shape, lambda bi, hi: (0, 0)),
            pl.BlockSpec(msk.shape, lambda bi, hi: (0, 0, 0)),
        ],
        out_specs=pl.BlockSpec((1, s, HG_DK), lambda bi, hi: (bi, 0, hi)),
        out_shape=jax.ShapeDtypeStruct((b, s, h * HG_DK), F32),
        compiler_params=pltpu.CompilerParams(
            dimension_semantics=("parallel", "parallel"), vmem_limit_bytes=VMEM_LIMIT),
        name="hgrn2",
    )(hg3, hg3, hg3, hg3, lb_row, ng_row, mst, msk)


def _t5_bucket_np(dist):
    max_exact = REL_BUCKETS // 2
    d = np.maximum(dist, 1).astype(np.float32)
    log_part = max_exact + (np.log(d / np.float32(max_exact)) / np.float32(math.log(REL_MAX_DIST / max_exact))
                            * np.float32(REL_BUCKETS - max_exact)).astype(np.int32)
    return np.where(dist < max_exact, dist, np.minimum(log_part, REL_BUCKETS - 1))


def _moba_kernel(q_ref, k_ref, v_ref, avg_ref, bkt_ref, rb_ref, o_ref, own_ref, prev_ref, *, n_blocks):
    blk = MB_BLOCK
    scale = MB_DH ** -0.5 * LOG2_E
    avg = avg_ref[...]
    hp = LANES // MB_DH

    @pl.when(pl.program_id(1) == 0)
    def _():
        causal = (lax.broadcasted_iota(jnp.int32, (blk, blk), 0) <= lax.broadcasted_iota(jnp.int32, (blk, blk), 1))
        for hh in range(hp):
            head = pl.program_id(0) * hp + hh
            own_t = jnp.zeros((blk, blk), F32)
            prev_t = jnp.zeros((blk, blk), F32)
            for bk in range(REL_BUCKETS):
                val = rb_ref[bk, head] * LOG2_E
                own_t = jnp.where(bkt_ref[0] == bk, val, own_t)
                prev_t = jnp.where(bkt_ref[1] == bk, val, prev_t)
            own_ref[hh] = jnp.where(causal, own_t, NEG_INF)
            prev_ref[hh] = prev_t

    grow = lax.broadcasted_iota(jnp.int32, (GATE_ROWS, blk), 0)
    vt_all = v_ref[0].T

    def fold(a):
        return a.reshape(blk // SUBLANES, SUBLANES, blk)

    heads = []
    for hh in range(hp):
        ls = slice(hh * MB_DH, (hh + 1) * MB_DH)
        k_h = k_ref[0, :, ls]
        k_hi, k_lo = _split2(k_h)
        k_mean = (_dot(avg, k_hi) + _dot(avg, k_lo))[:GATE_ROWS]
        vt = jnp.concatenate([vt_all[hh * MB_DH:(hh + 1) * MB_DH, :], jnp.ones((DEN_ROWS, vt_all.shape[1]), F32)],
                             axis=0).astype(BF16)
        far_bias = rb_ref[REL_BUCKETS - 1, pl.program_id(0) * hp + hh] * LOG2_E
        heads.append((ls, _split2(k_mean), k_h.astype(BF16), vt, far_bias))

    def logits(hh, i):
        ls, (km_hi, km_lo), kb, _, far_bias = heads[hh]
        qi = q_ref[0, i * blk:(i + 1) * blk, ls] * scale
        st = _dot_nt(kb[:(i + 1) * blk], qi.astype(BF16))
        sel = None
        if i > 0:
            q_hi, q_lo = _split2(qi)
            gate = _dot_nt(km_hi, q_hi) + _dot_nt(km_lo, q_hi) + _dot_nt(km_hi, q_lo)
            gate = jnp.where(grow < i, gate, NEG_INF)
            rank = jnp.zeros((GATE_ROWS, blk), F32)
            if i > MB_TOPK:
                for j2 in range(i):
                    gj = jnp.broadcast_to(gate[j2:j2 + 1, :], (GATE_ROWS, blk))
                    tie = jnp.where(grow > j2, 1.0, 0.0)
                    rank = rank + jnp.where(gj > gate, 1.0, jnp.where(gj == gate, tie, 0.0))
            sel = jnp.where(rank < MB_TOPK, gate, NEG_INF) > NEG_INF
        pieces = []
        for j in range(i + 1):
            sj = st[j * blk:(j + 1) * blk, :]
            if j == i:
                pieces.append(sj + own_ref[hh])
            elif j == i - 1:
                pieces.append(sj + prev_ref[hh] + jnp.where(sel[j:j + 1, :], 0.0, NEG_INF))
            else:
                pieces.append(sj + jnp.where(sel[j:j + 1, :], far_bias, NEG_INF))
        mx8 = fold(pieces[0]).max(axis=0)
        for p in pieces[1:]:
            mx8 = jnp.maximum(mx8, fold(p).max(axis=0))
        return pieces, mx8.max(axis=0, keepdims=True)

    def attend(hh, pieces, mx):
        vt = heads[hh][3]
        acc = jnp.zeros((MB_DH + DEN_ROWS, blk), F32)
        for j, p in enumerate(pieces):
            acc = acc + _dot(vt[:, j * blk:(j + 1) * blk], jnp.exp2(p - mx).astype(BF16))
        return acc[:MB_DH] / acc[MB_DH:MB_DH + 1]

    items = [(i, hh) for i in range(n_blocks) for hh in range(hp)]
    staged = [logits(hh, i) for i, hh in items[:PIPE_AHEAD]]
    done = {}
    for n, (i, hh) in enumerate(items):
        if n + PIPE_AHEAD < len(items):
            staged.append(logits(items[n + PIPE_AHEAD][1], items[n + PIPE_AHEAD][0]))
        done[hh] = attend(hh, *staged.pop(0))
        if hh == hp - 1:
            o_ref[0, i * blk:(i + 1) * blk, :] = jnp.concatenate([done[h] for h in range(hp)], axis=0).T


def _moba(mb3, rel_bias):
    b, s, _ = mb3.shape
    assert s % MB_BLOCK == 0
    nb = s // MB_BLOCK
    hp = LANES // MB_DH
    n_hp = MB_HEADS // hp
    t = np.arange(MB_BLOCK)
    d_own = t[:, None] - t[None, :]
    buckets = np.stack([_t5_bucket_np(np.maximum(d_own, 0)), _t5_bucket_np(d_own + MB_BLOCK)]).astype(np.int32)
    assert _t5_bucket_np(np.array([MB_BLOCK + 1]))[0] == REL_BUCKETS - 1
    assert nb <= GATE_ROWS
    bkt = jnp.asarray(buckets.transpose(0, 2, 1))
    avg_np = np.zeros((LANES, s), np.float32)
    for j in range(nb):
        avg_np[j, j * MB_BLOCK:(j + 1) * MB_BLOCK] = 1.0 / MB_BLOCK
    avg = jnp.asarray(avg_np, BF16)

    def col(off):
        return pl.BlockSpec((1, s, LANES), lambda hi, bi: (bi, 0, off + hi))

    return pl.pallas_call(
        functools.partial(_moba_kernel, n_blocks=nb),
        grid=(n_hp, b),
        in_specs=[
            col(0), col(n_hp), col(2 * n_hp),
            pl.BlockSpec(avg.shape, lambda hi, bi: (0, 0)),
            pl.BlockSpec(bkt.shape, lambda hi, bi: (0, 0, 0)),
            pl.BlockSpec(memory_space=pltpu.SMEM),
        ],
        out_specs=pl.BlockSpec((1, s, LANES), lambda hi, bi: (bi, 0, hi)),
        out_shape=jax.ShapeDtypeStruct((b, s, MB_HEADS * MB_DH), F32),
        scratch_shapes=[pltpu.VMEM((hp, MB_BLOCK, MB_BLOCK), F32), pltpu.VMEM((hp, MB_BLOCK, MB_BLOCK), F32)],
        compiler_params=pltpu.CompilerParams(
            dimension_semantics=("parallel", "arbitrary"), vmem_limit_bytes=VMEM_LIMIT),
        name="moba",
    )(mb3, mb3, mb3, avg, bkt, rel_bias.astype(F32))


def _layer_norm(x, g, b):
    mu = jnp.mean(x, axis=-1, keepdims=True)
    xc = x - mu
    var = jnp.mean(xc * xc, axis=-1, keepdims=True)
    return xc * lax.rsqrt(var + NORM_EPS) * g + b


def _merge_kernel(hg_ref, mb_ref, ga_ref, gb_ref, x_ref, wa_ref, wb_ref, wo_ref, g1_ref, b1_ref,
                  wr_ref, br_ref, tri_ref, x1_ref, route_ref, cnt_ref, run_ref, *, blocks_per_moe_tile, parts):
    tm = x_ref.shape[0]
    pr = tm // parts
    lane = lax.broadcasted_iota(jnp.int32, (pr, LANES), 1)
    w_hi, w_lo = wr_ref[0], wr_ref[1]

    @pl.when(pl.program_id(0) % blocks_per_moe_tile == 0)
    def _():
        run_ref[...] = jnp.zeros_like(run_ref)

    def mix(rows):
        ya = _dot(hg_ref[rows, :].astype(BF16), wa_ref[...])
        yb = _dot(mb_ref[rows, :].astype(BF16), wb_ref[...])
        mixed_in = jax.nn.sigmoid(ga_ref[rows, :]) * ya + jax.nn.sigmoid(gb_ref[rows, :]) * yb
        mixed = _dot(mixed_in.astype(BF16), wo_ref[...])
        x1 = _layer_norm(DN_ALPHA * x_ref[rows, :] + mixed, g1_ref[...], b1_ref[...])
        x1_ref[rows, :] = x1
        return x1

    def route_rows(rows, x1, run):
        x_hi, x_lo = _split2(x1)
        logits = _dot(x_hi, w_hi) + _dot(x_hi, w_lo) + _dot(x_lo, w_hi) + br_ref[...]
        glog = jnp.where(lane < N_GROUPS, logits, NEG_INF)
        gmax = jnp.max(glog, axis=-1, keepdims=True)
        grp = jnp.min(jnp.where(glog == gmax, lane, LANES), axis=-1, keepdims=True)
        p_grp = 1.0 / jnp.sum(jnp.exp(glog - gmax), axis=-1, keepdims=True)
        e_lo = N_GROUPS + grp * EXPERTS_PER_GROUP
        elog = jnp.where(jnp.logical_and(lane >= e_lo, lane < e_lo + EXPERTS_PER_GROUP), logits, NEG_INF)
        m1 = jnp.max(elog, axis=-1, keepdims=True)
        i1 = jnp.min(jnp.where(elog == m1, lane, LANES), axis=-1, keepdims=True)
        elog2 = jnp.where(lane == i1, NEG_INF, elog)
        m2 = jnp.max(elog2, axis=-1, keepdims=True)
        i2 = jnp.min(jnp.where(elog2 == m2, lane, LANES), axis=-1, keepdims=True)
        e2 = jnp.exp(m2 - m1)
        w1 = p_grp / (1.0 + e2)
        w2 = p_grp * e2 / (1.0 + e2)
        oh1 = jnp.where(lane == i1 - N_GROUPS, 1.0, 0.0)
        oh2 = jnp.where(lane == i2 - N_GROUPS, 1.0, 0.0)
        both = oh1 + oh2
        before = _dot(tri_ref[...], both.astype(BF16)) + run
        r1 = jnp.sum(before * oh1, axis=-1, keepdims=True)
        r2 = jnp.sum(before * oh2, axis=-1, keepdims=True)
        cols = ((i1 - N_GROUPS).astype(F32), (i2 - N_GROUPS).astype(F32), w1, w2, r1, r2)
        route = jnp.zeros((pr, LANES), F32)
        for li, col in enumerate(cols):
            route = jnp.where(lane == li, col, route)
        route_ref[rows, :] = route
        return run + jnp.sum(both, axis=0, keepdims=True)

    slabs = [slice(h * pr, (h + 1) * pr) for h in range(parts)]
    mixed = [mix(rows) for rows in slabs]
    run = run_ref[...]
    for rows, x1 in zip(slabs, mixed):
        run = route_rows(rows, x1, run)
    run_ref[...] = run
    cnt_ref[0] = run


def _merge(hg_o, mb_o, gates, x2, wa, wb, wo, g1, b1, wr, br, tm, moe_tile):
    n, d = x2.shape
    wa_n = hg_o.shape[1]
    wb_n = mb_o.shape[1]
    assert moe_tile % tm == 0
    parts = 1
    tri = jnp.asarray(np.tril(np.ones((tm // parts, tm // parts), np.float32), -1), BF16)

    def full(a):
        nd = a.ndim
        return pl.BlockSpec(a.shape, lambda i: (0,) * nd)

    return pl.pallas_call(
        functools.partial(_merge_kernel, blocks_per_moe_tile=moe_tile // tm, parts=parts),
        grid=(n // tm,),
        in_specs=[
            pl.BlockSpec((tm, wa_n), lambda i: (i, 0)),
            pl.BlockSpec((tm, wb_n), lambda i: (i, 0)),
            pl.BlockSpec((tm, d), lambda i: (i, 0)),
            pl.BlockSpec((tm, d), lambda i: (i, 1)),
            pl.BlockSpec((tm, d), lambda i: (i, 0)),
            full(wa), full(wb), full(wo), full(g1), full(b1), full(wr), full(br), full(tri),
        ],
        out_specs=[
            pl.BlockSpec((tm, d), lambda i: (i, 0)),
            pl.BlockSpec((tm, LANES), lambda i: (i, 0)),
            pl.BlockSpec((1, 1, LANES), lambda i: (i, 0, 0)),
        ],
        out_shape=[
            jax.ShapeDtypeStruct((n, d), F32),
            jax.ShapeDtypeStruct((n, LANES), F32),
            jax.ShapeDtypeStruct((n // tm, 1, LANES), F32),
        ],
        scratch_shapes=[pltpu.VMEM((1, LANES), F32)],
        compiler_params=pltpu.CompilerParams(
            dimension_semantics=("arbitrary",), vmem_limit_bytes=VMEM_LIMIT),
        name="merge_ln1_router",
    )(hg_o, mb_o, gates, gates, x2, wa, wb, wo, g1, b1, wr, br, tri)


def _segment_copies(n_small, make_copy, act):
    per_big = SEG_BIG // SEG_PAD
    n_big = n_small // per_big

    def big(c, carry):
        act(make_copy(c * SEG_BIG, SEG_BIG))
        return carry
    lax.fori_loop(0, n_big, big, 0)

    def small(c, carry):
        act(make_copy(n_big * SEG_BIG + c * SEG_PAD, SEG_PAD))
        return carry
    lax.fori_loop(0, n_small - n_big * per_big, small, 0)


def _wait_copies(n_big, n_small, make_copy):
    def big(c, carry):
        make_copy(0, SEG_BIG).wait()
        return carry
    lax.fori_loop(0, n_big, big, 0)

    def small(c, carry):
        make_copy(0, SEG_PAD).wait()
        return carry
    lax.fori_loop(0, n_small, small, 0)


def _dispatch_kernel(lbase_ref, gseg_ref, nseg_ref, tot_ref, gap_ref, pos_ref, x1_ref, xs_hbm,
                     xs_ref, xb_ref, zero_ref, sem, *, cv_rows):
    ti = pl.program_id(0)
    n_tiles = pl.num_programs(0)
    groups, _, d = x1_ref.shape

    def out_copy(t, ex):
        def make(off, size):
            src = pl.multiple_of(lbase_ref[t * N_EXPERTS + ex] + off, SEG_PAD)
            dst = pl.multiple_of(gseg_ref[t * N_EXPERTS + ex] + off, SEG_PAD)
            return pltpu.make_async_copy(xb_ref.at[pl.ds(src, size), :], xs_hbm.at[pl.ds(dst, size), :], sem.at[0])
        return make

    @pl.when(ti == 0)
    def _():
        def zero(c, carry):
            r0 = pl.multiple_of(c * cv_rows, cv_rows)
            xs_ref[pl.ds(r0, cv_rows), :] = jnp.zeros((cv_rows, d), F32)
            return carry
        lax.fori_loop(0, xs_ref.shape[0] // cv_rows, zero, 0)
        zero_ref[...] = jnp.zeros_like(zero_ref)

    def sort_rows(grp, carry):
        a0 = grp * (TOP_K * SUBLANES)
        for k in range(SUBLANES):
            row = x1_ref[grp, pl.ds(k, 1), :]
            xs_ref[pl.ds(pos_ref[0, 0, a0 + TOP_K * k], 1), :] = row
            xs_ref[pl.ds(pos_ref[0, 0, a0 + TOP_K * k + 1], 1), :] = row
        return carry
    lax.fori_loop(0, groups, sort_rows, 0)

    @pl.when(ti > 0)
    def _():
        _wait_copies(tot_ref[2 * (ti - 1)], tot_ref[2 * (ti - 1) + 1], out_copy(0, 0))

    def convert(c, carry):
        r0 = pl.multiple_of(c * cv_rows, cv_rows)
        xb_ref[pl.ds(r0, cv_rows), :] = xs_ref[pl.ds(r0, cv_rows), :].astype(BF16)
        return carry
    lax.fori_loop(0, xs_ref.shape[0] // cv_rows, convert, 0)

    def send(ex, carry):
        _segment_copies(nseg_ref[ti * N_EXPERTS + ex], out_copy(ti, ex), lambda cp: cp.start())
        return carry
    lax.fori_loop(0, N_EXPERTS, send, 0)

    @pl.when(ti == n_tiles - 1)
    def _():
        _wait_copies(tot_ref[2 * ti], tot_ref[2 * ti + 1], out_copy(0, 0))

        def fill_copy(row):
            return pltpu.make_async_copy(zero_ref, xs_hbm.at[pl.ds(pl.multiple_of(row, SEG_PAD), SEG_PAD), :],
                                         sem.at[1])

        def fill(ex, carry):
            def one(c, carry2):
                fill_copy(gap_ref[2 * ex] + c * SEG_PAD).start()
                return carry2
            lax.fori_loop(0, gap_ref[2 * ex + 1], one, 0)
            return carry
        lax.fori_loop(0, N_EXPERTS, fill, 0)

        def fill_wait(ex, carry):
            def one(c, carry2):
                fill_copy(0).wait()
                return carry2
            lax.fori_loop(0, gap_ref[2 * ex + 1], one, 0)
            return carry
        lax.fori_loop(0, N_EXPERTS, fill_wait, 0)


def _expert_kernel(be_ref, nu_ref, x_ref, wgu_ref, wd_ref, y_ref, wgu_bf, wd_bf, *, cast_rows):
    i = pl.program_id(0)

    @pl.when(i < nu_ref[0])
    def _():
        @pl.when(jnp.logical_or(i == 0, be_ref[i] != be_ref[jnp.maximum(i - 1, 0)]))
        def _():
            for r0 in range(0, wgu_bf.shape[0], cast_rows):
                wgu_bf[r0:r0 + cast_rows, :] = wgu_ref[0, r0:r0 + cast_rows, :].astype(BF16)
            for r0 in range(0, wd_bf.shape[0], cast_rows):
                wd_bf[r0:r0 + cast_rows, :] = wd_ref[0, r0:r0 + cast_rows, :].astype(BF16)

        gu = _dot(x_ref[...], wgu_bf[...])
        gate = gu[:, :EXPERT_HIDDEN]
        up = gu[:, EXPERT_HIDDEN:]
        hdn = (gate * jax.nn.sigmoid(gate) * up).astype(BF16)
        y_ref[...] = _dot(hdn, wd_bf[...])


def _combine_kernel(lbase_ref, gseg_ref, nseg_ref, tot_ref, pos_ref, rw_ref, x1_ref, ys_hbm, g2_ref, b2_ref,
                    o_ref, xs_ref, sem, *, ln_rows):
    ti = pl.program_id(0)
    n_tiles = pl.num_programs(0)
    groups, _, d = x1_ref.shape
    slot = ti % 2

    def in_copy(t, ex, sl):
        def make(off, size):
            src = pl.multiple_of(gseg_ref[t * N_EXPERTS + ex] + off, SEG_PAD)
            dst = pl.multiple_of(lbase_ref[t * N_EXPERTS + ex] + off, SEG_PAD)
            return pltpu.make_async_copy(ys_hbm.at[pl.ds(src, size), :], xs_ref.at[sl, pl.ds(dst, size), :],
                                         sem.at[sl])
        return make

    def fetch(t, sl):
        def one(ex, carry):
            _segment_copies(nseg_ref[t * N_EXPERTS + ex], in_copy(t, ex, sl), lambda cp: cp.start())
            return carry
        lax.fori_loop(0, N_EXPERTS, one, 0)

    @pl.when(ti == 0)
    def _():
        fetch(0, 0)

    @pl.when(ti + 1 < n_tiles)
    def _():
        fetch(ti + 1, 1 - slot)

    _wait_copies(tot_ref[2 * ti], tot_ref[2 * ti + 1], in_copy(0, 0, slot))

    def combine(grp, carry):
        a0 = grp * (TOP_K * SUBLANES)
        for k in range(SUBLANES):
            a = a0 + TOP_K * k
            o_ref[grp, pl.ds(k, 1), :] = (DN_ALPHA * x1_ref[grp, pl.ds(k, 1), :]
                                          + rw_ref[0, 0, a] * xs_ref[slot, pl.ds(pos_ref[0, 0, a], 1), :]
                                          + rw_ref[0, 0, a + 1] * xs_ref[slot, pl.ds(pos_ref[0, 0, a + 1], 1), :])
        return carry
    lax.fori_loop(0, groups, combine, 0)

    ln_groups = ln_rows // SUBLANES

    def norm(c, carry):
        g0 = pl.multiple_of(c * ln_groups, ln_groups)
        rows = o_ref[pl.ds(g0, ln_groups), :, :].reshape(ln_rows, d)
        o_ref[pl.ds(g0, ln_groups), :, :] = _layer_norm(rows, g2_ref[...], b2_ref[...]).reshape(ln_groups, SUBLANES, d)
        return carry
    lax.fori_loop(0, groups // ln_groups, norm, 0)


def _moe_tile(n):
    return min(MOE_TILE, n)


def _round_up(a, m):
    return (a + m - 1) // m * m


def _moe(x1, route, cnt_run, w_gate_up, w_down, g2, b2):
    n, d = x1.shape
    tile = _moe_tile(n)
    assert n % tile == 0
    n_tiles = n // tile
    i32 = jnp.int32
    cnt = cnt_run.reshape(n_tiles, -1, LANES)[:, -1, :N_EXPERTS].astype(i32)
    seg = _round_up(cnt, SEG_PAD)
    lbase = jnp.cumsum(seg, axis=1) - seg
    used = jnp.sum(seg, axis=0)
    region = _round_up(used, EXP_ROWS)
    e_start = jnp.cumsum(region) - region
    gseg = e_start[None, :] + jnp.cumsum(seg, axis=0) - seg
    nseg = seg // SEG_PAD
    per_big = SEG_BIG // SEG_PAD
    tot = jnp.stack([jnp.sum(nseg // per_big, axis=1), jnp.sum(nseg % per_big, axis=1)], axis=1)
    gap = jnp.stack([e_start + used, (region - used) // SEG_PAD], axis=1)
    rows_max = _round_up(TOP_K * n + n_tiles * N_EXPERTS * (SEG_PAD - 1) + N_EXPERTS * (EXP_ROWS - 1), EXP_ROWS)
    n_blocks = rows_max // EXP_ROWS
    n_used = (jnp.sum(region) // EXP_ROWS).astype(i32).reshape(1)
    blk_row = jnp.arange(n_blocks, dtype=i32)[:, None] * EXP_ROWS
    blk_expert = jnp.minimum(jnp.sum((blk_row >= jnp.cumsum(region)[None, :]).astype(i32), axis=1), N_EXPERTS - 1)
    r_exp = route[:, 0:TOP_K].astype(i32).reshape(n_tiles, TOP_K * tile)
    r_rank = route[:, 2 * TOP_K:3 * TOP_K].astype(i32).reshape(n_tiles, TOP_K * tile)
    seg_start = jnp.sum(jnp.where(r_exp[:, :, None] == jnp.arange(N_EXPERTS, dtype=i32), lbase[:, None, :], 0),
                        axis=-1)
    pos = (seg_start + r_rank).reshape(n_tiles, 1, TOP_K * tile)
    r_wgt = route[:, TOP_K:2 * TOP_K].reshape(n_tiles, 1, TOP_K * tile)
    local_rows = _round_up(TOP_K * tile + N_EXPERTS * (SEG_PAD - 1), 256)
    flat = lambda a: a.reshape(-1).astype(i32)

    def smem_spec():
        return pl.BlockSpec((1, 1, TOP_K * tile), lambda t, *_: (t, 0, 0), memory_space=pltpu.SMEM)

    x1_groups = x1.reshape(n // SUBLANES, SUBLANES, d)
    row_groups_spec = pl.BlockSpec((tile // SUBLANES, SUBLANES, d), lambda t, *_: (t, 0, 0))

    xs_hbm = pl.pallas_call(
        functools.partial(_dispatch_kernel, cv_rows=256),
        grid_spec=pltpu.PrefetchScalarGridSpec(
            num_scalar_prefetch=5,
            grid=(n_tiles,),
            in_specs=[smem_spec(), row_groups_spec],
            out_specs=pl.BlockSpec(memory_space=pl.ANY),
            scratch_shapes=[
                pltpu.VMEM((local_rows, d), F32),
                pltpu.VMEM((local_rows, d), BF16),
                pltpu.VMEM((SEG_PAD, d), BF16),
                pltpu.SemaphoreType.DMA((2,)),
            ],
        ),
        out_shape=jax.ShapeDtypeStruct((rows_max, d), BF16),
        compiler_params=pltpu.CompilerParams(dimension_semantics=("arbitrary",), vmem_limit_bytes=VMEM_LIMIT),
        name="moe_dispatch",
    )(flat(lbase), flat(gseg), flat(nseg), flat(tot), flat(gap), pos, x1_groups)

    def blk(i, be, nu):
        return jnp.minimum(i, nu[0] - 1)

    ys_hbm = pl.pallas_call(
        functools.partial(_expert_kernel, cast_rows=256),
        grid_spec=pltpu.PrefetchScalarGridSpec(
            num_scalar_prefetch=2,
            grid=(n_blocks,),
            in_specs=[
                pl.BlockSpec((EXP_ROWS, d), lambda i, be, nu: (blk(i, be, nu), 0)),
                pl.BlockSpec((1, d, 2 * EXPERT_HIDDEN), lambda i, be, nu: (be[blk(i, be, nu)], 0, 0)),
                pl.BlockSpec((1, EXPERT_HIDDEN, d), lambda i, be, nu: (be[blk(i, be, nu)], 0, 0)),
            ],
            out_specs=pl.BlockSpec((EXP_ROWS, d), lambda i, be, nu: (blk(i, be, nu), 0)),
            scratch_shapes=[pltpu.VMEM((d, 2 * EXPERT_HIDDEN), BF16), pltpu.VMEM((EXPERT_HIDDEN, d), BF16)],
        ),
        out_shape=jax.ShapeDtypeStruct((rows_max, d), F32),
        compiler_params=pltpu.CompilerParams(dimension_semantics=("arbitrary",), vmem_limit_bytes=VMEM_LIMIT),
        name="moe_experts",
    )(blk_expert, n_used, xs_hbm, w_gate_up, w_down)

    return pl.pallas_call(
        functools.partial(_combine_kernel, ln_rows=min(256, tile)),
        grid_spec=pltpu.PrefetchScalarGridSpec(
            num_scalar_prefetch=4,
            grid=(n_tiles,),
            in_specs=[
                smem_spec(), smem_spec(),
                row_groups_spec,
                pl.BlockSpec(memory_space=pl.ANY),
                pl.BlockSpec((1, d), lambda t, *_: (0, 0)),
                pl.BlockSpec((1, d), lambda t, *_: (0, 0)),
            ],
            out_specs=row_groups_spec,
            scratch_shapes=[pltpu.VMEM((2, local_rows, d), F32), pltpu.SemaphoreType.DMA((2,))],
        ),
        out_shape=jax.ShapeDtypeStruct(x1_groups.shape, F32),
        compiler_params=pltpu.CompilerParams(dimension_semantics=("arbitrary",), vmem_limit_bytes=VMEM_LIMIT),
        name="moe_combine_ln2",
    )(flat(lbase), flat(gseg), flat(nseg), flat(tot), pos, r_wgt, x1_groups, ys_hbm, g2, b2).reshape(n, d)


def _block(x, w_in, b_in, lower_bound, hg_norm_g, rel_bias, w_proj_a, w_proj_b, w_out, ln1_g, ln1_b,
           w_group, b_group, w_expert, b_expert, w_gate_up, w_down, ln2_g, ln2_b, *, tm_proj, tm_merge):
    b, s, d = x.shape
    n = b * s
    n_hg = 4 * HG_HEADS * HG_DK
    n_mb = 3 * MB_HEADS * MB_DH
    n_gt = 2 * d
    x2 = x.reshape(n, d)
    hg, mb, gates = _in_proj(x2, w_in.astype(BF16), b_in.reshape(1, -1), n_hg, n_mb, n_gt, tm_proj)
    hg_o = _hgrn(hg.reshape(b, s, n_hg), lower_bound.reshape(1, -1), hg_norm_g.reshape(1, -1))
    mb_o = _moba(mb.reshape(b, s, n_mb), rel_bias)
    w_r = jnp.zeros((d, LANES), F32).at[:, :N_GROUPS].set(w_group).at[:, N_GROUPS:N_GROUPS + N_EXPERTS].set(w_expert)
    w_r_hi = w_r.astype(BF16)
    w_r_lo = (w_r - w_r_hi.astype(F32)).astype(BF16)
    b_r = jnp.zeros((1, LANES), F32).at[0, :N_GROUPS].set(b_group).at[0, N_GROUPS:N_GROUPS + N_EXPERTS].set(b_expert)
    tm_merge = min(tm_merge, n)
    x1, route, cnt_run = _merge(hg_o.reshape(n, -1), mb_o.reshape(n, -1), gates, x2,
                                w_proj_a.astype(BF16), w_proj_b.astype(BF16), w_out.astype(BF16),
                                ln1_g.reshape(1, d), ln1_b.reshape(1, d), jnp.stack([w_r_hi, w_r_lo]), b_r,
                                tm_merge, _moe_tile(n))
    out = _moe(x1, route, cnt_run, w_gate_up, w_down, ln2_g.reshape(1, d), ln2_b.reshape(1, d))
    return out.reshape(b, s, d)


def kernel(x, w_in, b_in, lb_logits, hg_norm_g, rel_bias, w_proj_a, w_proj_b, w_out, ln1_g, ln1_b, w_group,
           b_group, w_expert, b_expert, w_gate_up, w_down, ln2_g, ln2_b):
    lower_bounds = jnp.cumsum(jax.nn.softmax(lb_logits.astype(F32), axis=0), axis=0)
    l = 0
    return _block(x, w_in[l], b_in[l], lower_bounds[l], hg_norm_g[l], rel_bias, w_proj_a[l], w_proj_b[l],
                  w_out[l], ln1_g[l], ln1_b[l], w_group[l], b_group[l], w_expert[l], b_expert[l],
                  w_gate_up[l], w_down[l], ln2_g[l], ln2_b[l], tm_proj=256, tm_merge=512)
```

```python
import functools
import math

import numpy as np
import jax
import jax.numpy as jnp
from jax import lax
from jax.experimental import pallas as pl
from jax.experimental.pallas import tpu as pltpu

F32 = jnp.float32
BF16 = jnp.bfloat16

HG_HEADS = 4
HG_DK = 128
HG_CHUNK = 128
HG_GROUP = 4
MB_HEADS = 8
MB_DH = 64
MB_BLOCK = 256
MB_TOPK = 3
PIPE_AHEAD = 2
DEN_ROWS = 16
GATE_ROWS = 16
REL_BUCKETS = 32
REL_MAX_DIST = 128
N_GROUPS = 4
EXPERTS_PER_GROUP = 8
N_EXPERTS = N_GROUPS * EXPERTS_PER_GROUP
TOP_K = 2
EXPERT_HIDDEN = 512
MOE_TILE = 1024
SEG_PAD = 16
SEG_BIG = 64
EXP_ROWS = 512
DEPTH = 1
DN_ALPHA = (2.0 * DEPTH) ** 0.25
NORM_EPS = 1e-5
LANES = 128
SUBLANES = 8
VMEM_LIMIT = 56 * 1024 * 1024
NEG_INF = float("-inf")
LOG2_E = 1.4426950408889634


def _split2(a):
    hi = a.astype(BF16)
    lo = (a - hi.astype(F32)).astype(BF16)
    return hi, lo


def _split3(a):
    hi = a.astype(BF16)
    r = a - hi.astype(F32)
    mid = r.astype(BF16)
    lo = (r - mid.astype(F32)).astype(BF16)
    return hi, mid, lo


def _dot_nt(a, b):
    return lax.dot_general(a, b, (((1,), (1,)), ((), ())), preferred_element_type=F32)


def _dot_tn(a, b):
    return lax.dot_general(a, b, (((0,), (0,)), ((), ())), preferred_element_type=F32)


def _dot(a, b):
    return jnp.dot(a, b, preferred_element_type=F32)


def _in_proj_kernel(x_ref, w_ref, b_ref, hg_ref, mb_ref, gt_ref, *, col_chunk):
    xb = x_ref[...].astype(BF16)
    outs = ((hg_ref, 0), (mb_ref, hg_ref.shape[1]), (gt_ref, hg_ref.shape[1] + mb_ref.shape[1]))
    for o_ref, base in outs:
        for c0 in range(0, o_ref.shape[1], col_chunk):
            acc = _dot(xb, w_ref[:, base + c0:base + c0 + col_chunk])
            o_ref[:, c0:c0 + col_chunk] = acc + b_ref[:, base + c0:base + c0 + col_chunk]


def _in_proj(x2, w_bf, b_in, n_hg, n_mb, n_gt, tm):
    n, d = x2.shape
    cols = w_bf.shape[1]
    return pl.pallas_call(
        functools.partial(_in_proj_kernel, col_chunk=512),
        grid=(n // tm,),
        in_specs=[
            pl.BlockSpec((tm, d), lambda i: (i, 0)),
            pl.BlockSpec((d, cols), lambda i: (0, 0), pipeline_mode=pl.Buffered(1)),
            pl.BlockSpec((1, cols), lambda i: (0, 0)),
        ],
        out_specs=[
            pl.BlockSpec((tm, n_hg), lambda i: (i, 0)),
            pl.BlockSpec((tm, n_mb), lambda i: (i, 0)),
            pl.BlockSpec((tm, n_gt), lambda i: (i, 0)),
        ],
        out_shape=[
            jax.ShapeDtypeStruct((n, n_hg), F32),
            jax.ShapeDtypeStruct((n, n_mb), F32),
            jax.ShapeDtypeStruct((n, n_gt), F32),
        ],
        compiler_params=pltpu.CompilerParams(
            dimension_semantics=("parallel",), vmem_limit_bytes=VMEM_LIMIT),
        name="in_proj",
    )(x2, w_bf, b_in)


def _hgrn_tables():
    c = HG_CHUNK
    levels = [c >> (i + 1) for i in range(int(math.log2(c)))]
    masks = np.zeros((len(levels) + 1, c, c), np.float32)
    for li, m in enumerate(levels):
        for r in range(c):
            c0 = (r // (2 * m)) * (2 * m)
            if r - c0 >= m:
                masks[li, r, c0:c0 + m] = 1.0
    masks[len(levels)] = np.eye(c, dtype=np.float32)
    return np.tril(np.ones((c, c), np.float32)), masks, tuple(levels)


def _level_ref_rows(p, m):
    c, w = p.shape
    if 2 * m > SUBLANES:
        parts = [jnp.broadcast_to(p[c0 + m - 1:c0 + m, :], (2 * m, w)) for c0 in range(0, c, 2 * m)]
        return parts[0] if len(parts) == 1 else jnp.concatenate(parts, axis=0)
    p3 = p.reshape(c // SUBLANES, SUBLANES, w)
    sub = lax.broadcasted_iota(jnp.int32, p3.shape, 1)
    out = None
    for c0 in range(0, SUBLANES, 2 * m):
        b = jnp.broadcast_to(p3[:, c0 + m - 1:c0 + m, :], p3.shape)
        out = b if out is None else jnp.where(sub >= c0, b, out)
    return out.reshape(c, w)


def _hgrn_kernel(q_ref, f_ref, i_ref, g_ref, lb_ref, ng_ref, tril_ref, msk_ref, o_ref,
                 *, n_chunks, levels, group):
    c = HG_CHUNK
    dk = HG_DK
    assert n_chunks % group == 0
    lb = jnp.concatenate([lb_ref[...]] * group, axis=1)
    oml = 1.0 - lb
    ng = ng_ref[...]
    tril = tril_ref[...]
    n_levels = len(levels)

    def load(ref, r0):
        return jnp.concatenate([ref[0, pl.ds(r0 + u * c, c), :] for u in range(group)], axis=1)

    def lanes(a, u):
        return a[:, u * dk:(u + 1) * dk]

    def intra(r0):
        z = load(f_ref, r0)
        qr = load(q_ref, r0)
        lf = jnp.log(lb + oml * jax.nn.sigmoid(z))
        kk = oml * jax.nn.sigmoid(-z)
        qf = qr * jax.nn.sigmoid(qr)
        l_hi, l_mid, l_lo = _split3(lf)
        p = _dot(tril, l_hi) + _dot(tril, l_mid) + _dot(tril, l_lo)
        b_end = p[c - 1:c, :]
        qb = (qf * jnp.exp(p)).astype(BF16)
        kd = (kk * jnp.exp(b_end - p)).astype(BF16)
        dec = jnp.exp(b_end)
        qh = qf.astype(BF16)
        kh = kk.astype(BF16)
        scores = [msk_ref[n_levels] * _dot_nt(lanes(qh, u), lanes(kh, u)) for u in range(group)]
        row = lax.broadcasted_iota(jnp.int32, p.shape, 0)
        for li, m in enumerate(levels):
            ex = jnp.exp2(jnp.abs(p - _level_ref_rows(p, m)) * (-LOG2_E))
            qk = (jnp.where((row & (2 * m - 1)) >= m, qf, kk) * ex).astype(BF16)
            for u in range(group):
                scores[u] = scores[u] + msk_ref[li] * _dot_nt(lanes(qk, u), lanes(qk, u))
        vbs = [i_ref[0, pl.ds(r0 + u * c, c), :].astype(BF16) for u in range(group)]
        o_intra = [_dot(scores[u].astype(BF16), vbs[u]) for u in range(group)]
        return qb, kd, dec, vbs, o_intra

    def chain(r0, staged, st):
        qb, kd, dec, vbs, o_intra = staged
        for u in range(group):
            rows = pl.ds(r0 + u * c, c)
            g = g_ref[0, rows, :]
            o = _dot_nt(lanes(qb, u), st.astype(BF16)) + o_intra[u]
            st = st * lanes(dec, u) + _dot_tn(vbs[u], lanes(kd, u))
            o = o * lax.rsqrt(jnp.mean(o * o, axis=-1, keepdims=True) + NORM_EPS)
            o_ref[0, rows, :] = o * ng * (g * jax.nn.sigmoid(g))
        return st

    n_groups = n_chunks // group
    st = jnp.zeros((HG_DK, HG_DK), F32)
    staged = intra(0)
    for gi in range(n_groups):
        nxt = intra((gi + 1) * group * c) if gi + 1 < n_groups else None
        st = chain(gi * group * c, staged, st)
        staged = nxt


def _hgrn(hg3, lb_row, ng_row):
    b, s, _ = hg3.shape
    tril, masks, levels = _hgrn_tables()
    mst = jnp.asarray(tril, BF16)
    msk = jnp.asarray(masks, F32)
    h = HG_HEADS

    def col(off):
        return pl.BlockSpec((1, s, HG_DK), lambda bi, hi: (bi, 0, off + hi))

    return pl.pallas_call(
        functools.partial(_hgrn_kernel, n_chunks=s // HG_CHUNK, levels=levels,
                          group=math.gcd(s // HG_CHUNK, HG_GROUP)),
        grid=(b, h),
        in_specs=[
            col(0), col(h), col(2 * h), col(3 * h),
            pl.BlockSpec((1, HG_DK), lambda bi, hi: (0, hi)),
            pl.BlockSpec((1, HG_DK), lambda bi, hi: (0, hi)),
            pl.BlockSpec(mst.shape, lambda bi, hi: (0, 0)),
            pl.BlockSpec(msk.shape, lambda bi, hi: (0, 0, 0)),
        ],
        out_specs=pl.BlockSpec((1, s, HG_DK), lambda bi, hi: (bi, 0, hi)),
        out_shape=jax.ShapeDtypeStruct((b, s, h * HG_DK), F32),
        compiler_params=pltpu.CompilerParams(
            dimension_semantics=("parallel", "parallel"), vmem_limit_bytes=VMEM_LIMIT),
        name="hgrn2",
    )(hg3, hg3, hg3, hg3, lb_row, ng_row, mst, msk)


def _t5_bucket_np(dist):
    max_exact = REL_BUCKETS // 2
    d = np.maximum(dist, 1).astype(np.float32)
    log_part = max_exact + (np.log(d / np.float32(max_exact)) / np.float32(math.log(REL_MAX_DIST / max_exact))
                            * np.float32(REL_BUCKETS - max_exact)).astype(np.int32)
    return np.where(dist < max_exact, dist, np.minimum(log_part, REL_BUCKETS - 1))


def _moba_kernel(q_ref, k_ref, v_ref, avg_ref, bkt_ref, rb_ref, o_ref, own_ref, prev_ref, *, n_blocks):
    blk = MB_BLOCK
    scale = MB_DH ** -0.5 * LOG2_E
    avg = avg_ref[...]
    hp = LANES // MB_DH

    @pl.when(pl.program_id(1) == 0)
    def _():
        causal = (lax.broadcasted_iota(jnp.int32, (blk, blk), 0) <= lax.broadcasted_iota(jnp.int32, (blk, blk), 1))
        for hh in range(hp):
            head = pl.program_id(0) * hp + hh
            own_t = jnp.zeros((blk, blk), F32)
            prev_t = jnp.zeros((blk, blk), F32)
            for bk in range(REL_BUCKETS):
                val = rb_ref[bk, head] * LOG2_E
                own_t = jnp.where(bkt_ref[0] == bk, val, own_t)
                prev_t = jnp.where(bkt_ref[1] == bk, val, prev_t)
            own_ref[hh] = jnp.where(causal, own_t, NEG_INF)
            prev_ref[hh] = prev_t

    grow = lax.broadcasted_iota(jnp.int32, (GATE_ROWS, blk), 0)
    vt_all = v_ref[0].T

    def fold(a):
        return a.reshape(blk // SUBLANES, SUBLANES, blk)

    heads = []
    for hh in range(hp):
        ls = slice(hh * MB_DH, (hh + 1) * MB_DH)
        k_h = k_ref[0, :, ls]
        k_hi, k_lo = _split2(k_h)
        k_mean = (_dot(avg, k_hi) + _dot(avg, k_lo))[:GATE_ROWS]
        vt = jnp.concatenate([vt_all[hh * MB_DH:(hh + 1) * MB_DH, :], jnp.ones((DEN_ROWS, vt_all.shape[1]), F32)],
                             axis=0).astype(BF16)
        far_bias = rb_ref[REL_BUCKETS - 1, pl.program_id(0) * hp + hh] * LOG2_E
        heads.append((ls, _split2(k_mean), k_h.astype(BF16), vt, far_bias))

    def logits(hh, i):
        ls, (km_hi, km_lo), kb, _, far_bias = heads[hh]
        qi = q_ref[0, i * blk:(i + 1) * blk, ls] * scale
        st = _dot_nt(kb[:(i + 1) * blk], qi.astype(BF16))
        sel = None
        if i > 0:
            q_hi, q_lo = _split2(qi)
            gate = _dot_nt(km_hi, q_hi) + _dot_nt(km_lo, q_hi) + _dot_nt(km_hi, q_lo)
            gate = jnp.where(grow < i, gate, NEG_INF)
            rank = jnp.zeros((GATE_ROWS, blk), F32)
            if i > MB_TOPK:
                for j2 in range(i):
                    gj = jnp.broadcast_to(gate[j2:j2 + 1, :], (GATE_ROWS, blk))
                    tie = jnp.where(grow > j2, 1.0, 0.0)
                    rank = rank + jnp.where(gj > gate, 1.0, jnp.where(gj == gate, tie, 0.0))
            sel = jnp.where(rank < MB_TOPK, gate, NEG_INF) > NEG_INF
        pieces = []
        for j in range(i + 1):
            sj = st[j * blk:(j + 1) * blk, :]
            if j == i:
                pieces.append(sj + own_ref[hh])
            elif j == i - 1:
                pieces.append(sj + prev_ref[hh] + jnp.where(sel[j:j + 1, :], 0.0, NEG_INF))
            else:
                pieces.append(sj + jnp.where(sel[j:j + 1, :], far_bias, NEG_INF))
        mx8 = fold(pieces[0]).max(axis=0)
        for p in pieces[1:]:
            mx8 = jnp.maximum(mx8, fold(p).max(axis=0))
        return pieces, mx8.max(axis=0, keepdims=True)

    def attend(hh, pieces, mx):
        vt = heads[hh][3]
        acc = jnp.zeros((MB_DH + DEN_ROWS, blk), F32)
        for j, p in enumerate(pieces):
            acc = acc + _dot(vt[:, j * blk:(j + 1) * blk], jnp.exp2(p - mx).astype(BF16))
        return acc[:MB_DH] / acc[MB_DH:MB_DH + 1]

    items = [(i, hh) for i in range(n_blocks) for hh in range(hp)]
    staged = [logits(hh, i) for i, hh in items[:PIPE_AHEAD]]
    done = {}
    for n, (i, hh) in enumerate(items):
        if n + PIPE_AHEAD < len(items):
            staged.append(logits(items[n + PIPE_AHEAD][1], items[n + PIPE_AHEAD][0]))
        done[hh] = attend(hh, *staged.pop(0))
        if hh == hp - 1:
            o_ref[0, i * blk:(i + 1) * blk, :] = jnp.concatenate([done[h] for h in range(hp)], axis=0).T


def _moba(mb3, rel_bias):
    b, s, _ = mb3.shape
    assert s % MB_BLOCK == 0
    nb = s // MB_BLOCK
    hp = LANES // MB_DH
    n_hp = MB_HEADS // hp
    t = np.arange(MB_BLOCK)
    d_own = t[:, None] - t[None, :]
    buckets = np.stack([_t5_bucket_np(np.maximum(d_own, 0)), _t5_bucket_np(d_own + MB_BLOCK)]).astype(np.int32)
    assert _t5_bucket_np(np.array([MB_BLOCK + 1]))[0] == REL_BUCKETS - 1
    assert nb <= GATE_ROWS
    bkt = jnp.asarray(buckets.transpose(0, 2, 1))
    avg_np = np.zeros((LANES, s), np.float32)
    for j in range(nb):
        avg_np[j, j * MB_BLOCK:(j + 1) * MB_BLOCK] = 1.0 / MB_BLOCK
    avg = jnp.asarray(avg_np, BF16)

    def col(off):
        return pl.BlockSpec((1, s, LANES), lambda hi, bi: (bi, 0, off + hi))

    return pl.pallas_call(
        functools.partial(_moba_kernel, n_blocks=nb),
        grid=(n_hp, b),
        in_specs=[
            col(0), col(n_hp), col(2 * n_hp),
            pl.BlockSpec(avg.shape, lambda hi, bi: (0, 0)),
            pl.BlockSpec(bkt.shape, lambda hi, bi: (0, 0, 0)),
            pl.BlockSpec(memory_space=pltpu.SMEM),
        ],
        out_specs=pl.BlockSpec((1, s, LANES), lambda hi, bi: (bi, 0, hi)),
        out_shape=jax.ShapeDtypeStruct((b, s, MB_HEADS * MB_DH), F32),
        scratch_shapes=[pltpu.VMEM((hp, MB_BLOCK, MB_BLOCK), F32), pltpu.VMEM((hp, MB_BLOCK, MB_BLOCK), F32)],
        compiler_params=pltpu.CompilerParams(
            dimension_semantics=("parallel", "arbitrary"), vmem_limit_bytes=VMEM_LIMIT),
        name="moba",
    )(mb3, mb3, mb3, avg, bkt, rel_bias.astype(F32))


def _layer_norm(x, g, b):
    mu = jnp.mean(x, axis=-1, keepdims=True)
    xc = x - mu
    var = jnp.mean(xc * xc, axis=-1, keepdims=True)
    return xc * lax.rsqrt(var + NORM_EPS) * g + b


def _merge_kernel(hg_ref, mb_ref, ga_ref, gb_ref, x_ref, wa_ref, wb_ref, wo_ref, g1_ref, b1_ref,
                  wr_ref, br_ref, tri_ref, x1_ref, route_ref, cnt_ref, run_ref, *, blocks_per_moe_tile, parts):
    tm = x_ref.shape[0]
    pr = tm // parts
    lane = lax.broadcasted_iota(jnp.int32, (pr, LANES), 1)

    @pl.when(pl.program_id(0) % blocks_per_moe_tile == 0)
    def _():
        run_ref[...] = jnp.zeros_like(run_ref)

    def mix(rows):
        ya = _dot(hg_ref[rows, :].astype(BF16), wa_ref[...])
        yb = _dot(mb_ref[rows, :].astype(BF16), wb_ref[...])
        mixed_in = jax.nn.sigmoid(ga_ref[rows, :]) * ya + jax.nn.sigmoid(gb_ref[rows, :]) * yb
        mixed = _dot(mixed_in.astype(BF16), wo_ref[...])
        x1 = _layer_norm(DN_ALPHA * x_ref[rows, :] + mixed, g1_ref[...], b1_ref[...])
        x1_ref[rows, :] = x1
        return x1

    def route_rows(rows, x1, run):
        x_hi, x_lo = _split2(x1)
        w_hi, w_lo = wr_ref[0], wr_ref[1]
        logits = _dot(x_hi, w_hi) + _dot(x_hi, w_lo) + _dot(x_lo, w_hi) + br_ref[...]
        glog = jnp.where(lane < N_GROUPS, logits, NEG_INF)
        gmax = jnp.max(glog, axis=-1, keepdims=True)
        grp = jnp.min(jnp.where(glog == gmax, lane, LANES), axis=-1, keepdims=True)
        p_grp = 1.0 / jnp.sum(jnp.exp(glog - gmax), axis=-1, keepdims=True)
        e_lo = N_GROUPS + grp * EXPERTS_PER_GROUP
        elog = jnp.where(jnp.logical_and(lane >= e_lo, lane < e_lo + EXPERTS_PER_GROUP), logits, NEG_INF)
        m1 = jnp.max(elog, axis=-1, keepdims=True)
        i1 = jnp.min(jnp.where(elog == m1, lane, LANES), axis=-1, keepdims=True)
        elog2 = jnp.where(lane == i1, NEG_INF, elog)
        m2 = jnp.max(elog2, axis=-1, keepdims=True)
        i2 = jnp.min(jnp.where(elog2 == m2, lane, LANES), axis=-1, keepdims=True)
        e2 = jnp.exp(m2 - m1)
        w1 = p_grp / (1.0 + e2)
        w2 = p_grp * e2 / (1.0 + e2)
        oh1 = jnp.where(lane == i1 - N_GROUPS, 1.0, 0.0)
        oh2 = jnp.where(lane == i2 - N_GROUPS, 1.0, 0.0)
        both = oh1 + oh2
        before = _dot(tri_ref[...], both.astype(BF16)) + run
        r1 = jnp.sum(before * oh1, axis=-1, keepdims=True)
        r2 = jnp.sum(before * oh2, axis=-1, keepdims=True)
        cols = ((i1 - N_GROUPS).astype(F32), (i2 - N_GROUPS).astype(F32), w1, w2, r1, r2)
        route = jnp.zeros((pr, LANES), F32)
        for li, col in enumerate(cols):
            route = jnp.where(lane == li, col, route)
        route_ref[rows, :] = route
        return run + jnp.sum(both, axis=0, keepdims=True)

    slabs = [slice(h * pr, (h + 1) * pr) for h in range(parts)]
    mixed = [mix(rows) for rows in slabs]
    run = run_ref[...]
    for rows, x1 in zip(slabs, mixed):
        run = route_rows(rows, x1, run)
    run_ref[...] = run
    cnt_ref[0] = run


def _merge(hg_o, mb_o, gates, x2, wa, wb, wo, g1, b1, wr, br, tm, moe_tile):
    n, d = x2.shape
    wa_n = hg_o.shape[1]
    wb_n = mb_o.shape[1]
    assert moe_tile % tm == 0
    parts = 1
    tri = jnp.asarray(np.tril(np.ones((tm // parts, tm // parts), np.float32), -1), BF16)

    def full(a):
        nd = a.ndim
        return pl.BlockSpec(a.shape, lambda i: (0,) * nd)

    return pl.pallas_call(
        functools.partial(_merge_kernel, blocks_per_moe_tile=moe_tile // tm, parts=parts),
        grid=(n // tm,),
        in_specs=[
            pl.BlockSpec((tm, wa_n), lambda i: (i, 0)),
            pl.BlockSpec((tm, wb_n), lambda i: (i, 0)),
            pl.BlockSpec((tm, d), lambda i: (i, 0)),
            pl.BlockSpec((tm, d), lambda i: (i, 1)),
            pl.BlockSpec((tm, d), lambda i: (i, 0)),
            full(wa), full(wb), full(wo), full(g1), full(b1), full(wr), full(br), full(tri),
        ],
        out_specs=[
            pl.BlockSpec((tm, d), lambda i: (i, 0)),
            pl.BlockSpec((tm, LANES), lambda i: (i, 0)),
            pl.BlockSpec((1, 1, LANES), lambda i: (i, 0, 0)),
        ],
        out_shape=[
            jax.ShapeDtypeStruct((n, d), F32),
            jax.ShapeDtypeStruct((n, LANES), F32),
            jax.ShapeDtypeStruct((n // tm, 1, LANES), F32),
        ],
        scratch_shapes=[pltpu.VMEM((1, LANES), F32)],
        compiler_params=pltpu.CompilerParams(
            dimension_semantics=("arbitrary",), vmem_limit_bytes=VMEM_LIMIT),
        name="merge_ln1_router",
    )(hg_o, mb_o, gates, gates, x2, wa, wb, wo, g1, b1, wr, br, tri)


def _segment_copies(n_small, make_copy, act):
    per_big = SEG_BIG // SEG_PAD
    n_big = n_small // per_big

    def big(c, carry):
        act(make_copy(c * SEG_BIG, SEG_BIG))
        return carry
    lax.fori_loop(0, n_big, big, 0)

    def small(c, carry):
        act(make_copy(n_big * SEG_BIG + c * SEG_PAD, SEG_PAD))
        return carry
    lax.fori_loop(0, n_small - n_big * per_big, small, 0)


def _wait_copies(n_big, n_small, make_copy):
    def big(c, carry):
        make_copy(0, SEG_BIG).wait()
        return carry
    lax.fori_loop(0, n_big, big, 0)

    def small(c, carry):
        make_copy(0, SEG_PAD).wait()
        return carry
    lax.fori_loop(0, n_small, small, 0)


def _dispatch_kernel(lbase_ref, gseg_ref, nseg_ref, tot_ref, gap_ref, pos_ref, x1_ref, xs_hbm,
                     xs_ref, xb_ref, zero_ref, sem, *, cv_rows):
    ti = pl.program_id(0)
    n_tiles = pl.num_programs(0)
    groups, _, d = x1_ref.shape

    def out_copy(t, ex):
        def make(off, size):
            src = pl.multiple_of(lbase_ref[t * N_EXPERTS + ex] + off, SEG_PAD)
            dst = pl.multiple_of(gseg_ref[t * N_EXPERTS + ex] + off, SEG_PAD)
            return pltpu.make_async_copy(xb_ref.at[pl.ds(src, size), :], xs_hbm.at[pl.ds(dst, size), :], sem.at[0])
        return make

    @pl.when(ti == 0)
    def _():
        def zero(c, carry):
            r0 = pl.multiple_of(c * cv_rows, cv_rows)
            xs_ref[pl.ds(r0, cv_rows), :] = jnp.zeros((cv_rows, d), F32)
            return carry
        lax.fori_loop(0, xs_ref.shape[0] // cv_rows, zero, 0)
        zero_ref[...] = jnp.zeros_like(zero_ref)

    def sort_rows(grp, carry):
        a0 = grp * (TOP_K * SUBLANES)
        for k in range(SUBLANES):
            row = x1_ref[grp, pl.ds(k, 1), :]
            xs_ref[pl.ds(pos_ref[0, 0, a0 + TOP_K * k], 1), :] = row
            xs_ref[pl.ds(pos_ref[0, 0, a0 + TOP_K * k + 1], 1), :] = row
        return carry
    lax.fori_loop(0, groups, sort_rows, 0)

    @pl.when(ti > 0)
    def _():
        _wait_copies(tot_ref[2 * (ti - 1)], tot_ref[2 * (ti - 1) + 1], out_copy(0, 0))

    def convert(c, carry):
        r0 = pl.multiple_of(c * cv_rows, cv_rows)
        xb_ref[pl.ds(r0, cv_rows), :] = xs_ref[pl.ds(r0, cv_rows), :].astype(BF16)
        return carry
    lax.fori_loop(0, xs_ref.shape[0] // cv_rows, convert, 0)

    def send(ex, carry):
        _segment_copies(nseg_ref[ti * N_EXPERTS + ex], out_copy(ti, ex), lambda cp: cp.start())
        return carry
    lax.fori_loop(0, N_EXPERTS, send, 0)

    @pl.when(ti == n_tiles - 1)
    def _():
        _wait_copies(tot_ref[2 * ti], tot_ref[2 * ti + 1], out_copy(0, 0))

        def fill_copy(row):
            return pltpu.make_async_copy(zero_ref, xs_hbm.at[pl.ds(pl.multiple_of(row, SEG_PAD), SEG_PAD), :],
                                         sem.at[1])

        def fill(ex, carry):
            def one(c, carry2):
                fill_copy(gap_ref[2 * ex] + c * SEG_PAD).start()
                return carry2
            lax.fori_loop(0, gap_ref[2 * ex + 1], one, 0)
            return carry
        lax.fori_loop(0, N_EXPERTS, fill, 0)

        def fill_wait(ex, carry):
            def one(c, carry2):
                fill_copy(0).wait()
                return carry2
            lax.fori_loop(0, gap_ref[2 * ex + 1], one, 0)
            return carry
        lax.fori_loop(0, N_EXPERTS, fill_wait, 0)


def _expert_kernel(be_ref, nu_ref, x_ref, wgu_ref, wd_ref, y_ref, wgu_bf, wd_bf, *, cast_rows):
    i = pl.program_id(0)

    @pl.when(i < nu_ref[0])
    def _():
        @pl.when(jnp.logical_or(i == 0, be_ref[i] != be_ref[jnp.maximum(i - 1, 0)]))
        def _():
            for r0 in range(0, wgu_bf.shape[0], cast_rows):
                wgu_bf[r0:r0 + cast_rows, :] = wgu_ref[0, r0:r0 + cast_rows, :].astype(BF16)
            for r0 in range(0, wd_bf.shape[0], cast_rows):
                wd_bf[r0:r0 + cast_rows, :] = wd_ref[0, r0:r0 + cast_rows, :].astype(BF16)

        gu = _dot(x_ref[...], wgu_bf[...])
        gate = gu[:, :EXPERT_HIDDEN]
        up = gu[:, EXPERT_HIDDEN:]
        hdn = (gate * jax.nn.sigmoid(gate) * up).astype(BF16)
        y_ref[...] = _dot(hdn, wd_bf[...])


def _combine_kernel(lbase_ref, gseg_ref, nseg_ref, tot_ref, pos_ref, rw_ref, x1_ref, ys_hbm, g2_ref, b2_ref,
                    o_ref, xs_ref, sem, *, ln_rows):
    ti = pl.program_id(0)
    n_tiles = pl.num_programs(0)
    groups, _, d = x1_ref.shape
    slot = ti % 2

    def in_copy(t, ex, sl):
        def make(off, size):
            src = pl.multiple_of(gseg_ref[t * N_EXPERTS + ex] + off, SEG_PAD)
            dst = pl.multiple_of(lbase_ref[t * N_EXPERTS + ex] + off, SEG_PAD)
            return pltpu.make_async_copy(ys_hbm.at[pl.ds(src, size), :], xs_ref.at[sl, pl.ds(dst, size), :],
                                         sem.at[sl])
        return make

    def fetch(t, sl):
        def one(ex, carry):
            _segment_copies(nseg_ref[t * N_EXPERTS + ex], in_copy(t, ex, sl), lambda cp: cp.start())
            return carry
        lax.fori_loop(0, N_EXPERTS, one, 0)

    @pl.when(ti == 0)
    def _():
        fetch(0, 0)

    @pl.when(ti + 1 < n_tiles)
    def _():
        fetch(ti + 1, 1 - slot)

    _wait_copies(tot_ref[2 * ti], tot_ref[2 * ti + 1], in_copy(0, 0, slot))

    def combine(grp, carry):
        a0 = grp * (TOP_K * SUBLANES)
        for k in range(SUBLANES):
            a = a0 + TOP_K * k
            o_ref[grp, pl.ds(k, 1), :] = (DN_ALPHA * x1_ref[grp, pl.ds(k, 1), :]
                                          + rw_ref[0, 0, a] * xs_ref[slot, pl.ds(pos_ref[0, 0, a], 1), :]
                                          + rw_ref[0, 0, a + 1] * xs_ref[slot, pl.ds(pos_ref[0, 0, a + 1], 1), :])
        return carry
    lax.fori_loop(0, groups, combine, 0)

    ln_groups = ln_rows // SUBLANES

    def norm(c, carry):
        g0 = pl.multiple_of(c * ln_groups, ln_groups)
        rows = o_ref[pl.ds(g0, ln_groups), :, :].reshape(ln_rows, d)
        o_ref[pl.ds(g0, ln_groups), :, :] = _layer_norm(rows, g2_ref[...], b2_ref[...]).reshape(ln_groups, SUBLANES, d)
        return carry
    lax.fori_loop(0, groups // ln_groups, norm, 0)


def _moe_tile(n):
    return min(MOE_TILE, n)


def _round_up(a, m):
    return (a + m - 1) // m * m


def _moe(x1, route, cnt_run, w_gate_up, w_down, g2, b2):
    n, d = x1.shape
    tile = _moe_tile(n)
    assert n % tile == 0
    n_tiles = n // tile
    i32 = jnp.int32
    cnt = cnt_run.reshape(n_tiles, -1, LANES)[:, -1, :N_EXPERTS].astype(i32)
    seg = _round_up(cnt, SEG_PAD)
    lbase = jnp.cumsum(seg, axis=1) - seg
    used = jnp.sum(seg, axis=0)
    region = _round_up(used, EXP_ROWS)
    e_start = jnp.cumsum(region) - region
    gseg = e_start[None, :] + jnp.cumsum(seg, axis=0) - seg
    nseg = seg // SEG_PAD
    per_big = SEG_BIG // SEG_PAD
    tot = jnp.stack([jnp.sum(nseg // per_big, axis=1), jnp.sum(nseg % per_big, axis=1)], axis=1)
    gap = jnp.stack([e_start + used, (region - used) // SEG_PAD], axis=1)
    rows_max = _round_up(TOP_K * n + n_tiles * N_EXPERTS * (SEG_PAD - 1) + N_EXPERTS * (EXP_ROWS - 1), EXP_ROWS)
    n_blocks = rows_max // EXP_ROWS
    n_used = (jnp.sum(region) // EXP_ROWS).astype(i32).reshape(1)
    blk_row = jnp.arange(n_blocks, dtype=i32)[:, None] * EXP_ROWS
    blk_expert = jnp.minimum(jnp.sum((blk_row >= jnp.cumsum(region)[None, :]).astype(i32), axis=1), N_EXPERTS - 1)
    r_exp = route[:, 0:TOP_K].astype(i32).reshape(n_tiles, TOP_K * tile)
    r_rank = route[:, 2 * TOP_K:3 * TOP_K].astype(i32).reshape(n_tiles, TOP_K * tile)
    seg_start = jnp.sum(jnp.where(r_exp[:, :, None] == jnp.arange(N_EXPERTS, dtype=i32), lbase[:, None, :], 0),
                        axis=-1)
    pos = (seg_start + r_rank).reshape(n_tiles, 1, TOP_K * tile)
    r_wgt = route[:, TOP_K:2 * TOP_K].reshape(n_tiles, 1, TOP_K * tile)
    local_rows = _round_up(TOP_K * tile + N_EXPERTS * (SEG_PAD - 1), 256)
    flat = lambda a: a.reshape(-1).astype(i32)

    def smem_spec():
        return pl.BlockSpec((1, 1, TOP_K * tile), lambda t, *_: (t, 0, 0), memory_space=pltpu.SMEM)

    x1_groups = x1.reshape(n // SUBLANES, SUBLANES, d)
    row_groups_spec = pl.BlockSpec((tile // SUBLANES, SUBLANES, d), lambda t, *_: (t, 0, 0))

    xs_hbm = pl.pallas_call(
        functools.partial(_dispatch_kernel, cv_rows=256),
        grid_spec=pltpu.PrefetchScalarGridSpec(
            num_scalar_prefetch=5,
            grid=(n_tiles,),
            in_specs=[smem_spec(), row_groups_spec],
            out_specs=pl.BlockSpec(memory_space=pl.ANY),
            scratch_shapes=[
                pltpu.VMEM((local_rows, d), F32),
                pltpu.VMEM((local_rows, d), BF16),
                pltpu.VMEM((SEG_PAD, d), BF16),
                pltpu.SemaphoreType.DMA((2,)),
            ],
        ),
        out_shape=jax.ShapeDtypeStruct((rows_max, d), BF16),
        compiler_params=pltpu.CompilerParams(dimension_semantics=("arbitrary",), vmem_limit_bytes=VMEM_LIMIT),
        name="moe_dispatch",
    )(flat(lbase), flat(gseg), flat(nseg), flat(tot), flat(gap), pos, x1_groups)

    def blk(i, be, nu):
        return jnp.minimum(i, nu[0] - 1)

    ys_hbm = pl.pallas_call(
        functools.partial(_expert_kernel, cast_rows=256),
        grid_spec=pltpu.PrefetchScalarGridSpec(
            num_scalar_prefetch=2,
            grid=(n_blocks,),
            in_specs=[
                pl.BlockSpec((EXP_ROWS, d), lambda i, be, nu: (blk(i, be, nu), 0)),
                pl.BlockSpec((1, d, 2 * EXPERT_HIDDEN), lambda i, be, nu: (be[blk(i, be, nu)], 0, 0)),
                pl.BlockSpec((1, EXPERT_HIDDEN, d), lambda i, be, nu: (be[blk(i, be, nu)], 0, 0)),
            ],
            out_specs=pl.BlockSpec((EXP_ROWS, d), lambda i, be, nu: (blk(i, be, nu), 0)),
            scratch_shapes=[pltpu.VMEM((d, 2 * EXPERT_HIDDEN), BF16), pltpu.VMEM((EXPERT_HIDDEN, d), BF16)],
        ),
        out_shape=jax.ShapeDtypeStruct((rows_max, d), F32),
        compiler_params=pltpu.CompilerParams(dimension_semantics=("arbitrary",), vmem_limit_bytes=VMEM_LIMIT),
        name="moe_experts",
    )(blk_expert, n_used, xs_hbm, w_gate_up, w_down)

    return pl.pallas_call(
        functools.partial(_combine_kernel, ln_rows=min(256, tile)),
        grid_spec=pltpu.PrefetchScalarGridSpec(
            num_scalar_prefetch=4,
            grid=(n_tiles,),
            in_specs=[
                smem_spec(), smem_spec(),
                row_groups_spec,
                pl.BlockSpec(memory_space=pl.ANY),
                pl.BlockSpec((1, d), lambda t, *_: (0, 0)),
                pl.BlockSpec((1, d), lambda t, *_: (0, 0)),
            ],
            out_specs=row_groups_spec,
            scratch_shapes=[pltpu.VMEM((2, local_rows, d), F32), pltpu.SemaphoreType.DMA((2,))],
        ),
        out_shape=jax.ShapeDtypeStruct(x1_groups.shape, F32),
        compiler_params=pltpu.CompilerParams(dimension_semantics=("arbitrary",), vmem_limit_bytes=VMEM_LIMIT),
        name="moe_combine_ln2",
    )(flat(lbase), flat(gseg), flat(nseg), flat(tot), pos, r_wgt, x1_groups, ys_hbm, g2, b2).reshape(n, d)


def _block(x, w_in, b_in, lower_bound, hg_norm_g, rel_bias, w_proj_a, w_proj_b, w_out, ln1_g, ln1_b,
           w_group, b_group, w_expert, b_expert, w_gate_up, w_down, ln2_g, ln2_b, *, tm_proj, tm_merge):
    b, s, d = x.shape
    n = b * s
    n_hg = 4 * HG_HEADS * HG_DK
    n_mb = 3 * MB_HEADS * MB_DH
    n_gt = 2 * d
    x2 = x.reshape(n, d)
    hg, mb, gates = _in_proj(x2, w_in.astype(BF16), b_in.reshape(1, -1), n_hg, n_mb, n_gt, tm_proj)
    hg_o = _hgrn(hg.reshape(b, s, n_hg), lower_bound.reshape(1, -1), hg_norm_g.reshape(1, -1))
    mb_o = _moba(mb.reshape(b, s, n_mb), rel_bias)
    w_r = jnp.zeros((d, LANES), F32).at[:, :N_GROUPS].set(w_group).at[:, N_GROUPS:N_GROUPS + N_EXPERTS].set(w_expert)
    b_r = jnp.zeros((1, LANES), F32).at[0, :N_GROUPS].set(b_group).at[0, N_GROUPS:N_GROUPS + N_EXPERTS].set(b_expert)
    tm_merge = min(tm_merge, n)
    x1, route, cnt_run = _merge(hg_o.reshape(n, -1), mb_o.reshape(n, -1), gates, x2,
                                w_proj_a.astype(BF16), w_proj_b.astype(BF16), w_out.astype(BF16),
                                ln1_g.reshape(1, d), ln1_b.reshape(1, d), jnp.stack(_split2(w_r)), b_r,
                                tm_merge, _moe_tile(n))
    out = _moe(x1, route, cnt_run, w_gate_up, w_down, ln2_g.reshape(1, d), ln2_b.reshape(1, d))
    return out.reshape(b, s, d)


def kernel(x, w_in, b_in, lb_logits, hg_norm_g, rel_bias, w_proj_a, w_proj_b, w_out, ln1_g, ln1_b, w_group,
           b_group, w_expert, b_expert, w_gate_up, w_down, ln2_g, ln2_b):
    lower_bounds = jnp.cumsum(jax.nn.softmax(lb_logits.astype(F32), axis=0), axis=0)
    l = 0
    return _block(x, w_in[l], b_in[l], lower_bounds[l], hg_norm_g[l], rel_bias, w_proj_a[l], w_proj_b[l],
                  w_out[l], ln1_g[l], ln1_b[l], w_group[l], b_group[l], w_expert[l], b_expert[l],
                  w_gate_up[l], w_down[l], ln2_g[l], ln2_b[l], tm_proj=512, tm_merge=512)
```

```python
import functools
import math

import numpy as np
import jax
import jax.numpy as jnp
from jax import lax
from jax.experimental import pallas as pl
from jax.experimental.pallas import tpu as pltpu

F32 = jnp.float32
BF16 = jnp.bfloat16

HG_HEADS = 4
HG_DK = 128
HG_CHUNK = 128
HG_GROUP = 4
MB_HEADS = 8
MB_DH = 64
MB_BLOCK = 256
MB_TOPK = 3
PIPE_AHEAD = 2
DEN_ROWS = 16
ROUTE_FIELDS = 8
GATE_ROWS = 16
REL_BUCKETS = 32
REL_MAX_DIST = 128
N_GROUPS = 4
EXPERTS_PER_GROUP = 8
N_EXPERTS = N_GROUPS * EXPERTS_PER_GROUP
TOP_K = 2
EXPERT_HIDDEN = 512
MOE_TILE = 1024
SEG_PAD = 16
SEG_BIG = 64
EXP_ROWS = 512
DEPTH = 1
DN_ALPHA = (2.0 * DEPTH) ** 0.25
NORM_EPS = 1e-5
LANES = 128
SUBLANES = 8
VMEM_LIMIT = 56 * 1024 * 1024
NEG_INF = float("-inf")
LOG2_E = 1.4426950408889634


def _split2(a):
    hi = a.astype(BF16)
    lo = (a - hi.astype(F32)).astype(BF16)
    return hi, lo


def _split3(a):
    hi = a.astype(BF16)
    r = a - hi.astype(F32)
    mid = r.astype(BF16)
    lo = (r - mid.astype(F32)).astype(BF16)
    return hi, mid, lo


def _dot_nt(a, b):
    return lax.dot_general(a, b, (((1,), (1,)), ((), ())), preferred_element_type=F32)


def _dot_tn(a, b):
    return lax.dot_general(a, b, (((0,), (0,)), ((), ())), preferred_element_type=F32)


def _dot(a, b):
    return jnp.dot(a, b, preferred_element_type=F32)


def _in_proj_kernel(x_ref, w_ref, b_ref, hg_ref, mb_ref, gt_ref, *, col_chunk):
    xb = x_ref[...].astype(BF16)
    outs = ((hg_ref, 0), (mb_ref, hg_ref.shape[1]), (gt_ref, hg_ref.shape[1] + mb_ref.shape[1]))
    for o_ref, base in outs:
        for c0 in range(0, o_ref.shape[1], col_chunk):
            acc = _dot(xb, w_ref[:, base + c0:base + c0 + col_chunk])
            o_ref[:, c0:c0 + col_chunk] = acc + b_ref[:, base + c0:base + c0 + col_chunk]


def _in_proj(x2, w_bf, b_in, n_hg, n_mb, n_gt, tm):
    n, d = x2.shape
    cols = w_bf.shape[1]
    return pl.pallas_call(
        functools.partial(_in_proj_kernel, col_chunk=512),
        grid=(n // tm,),
        in_specs=[
            pl.BlockSpec((tm, d), lambda i: (i, 0)),
            pl.BlockSpec((d, cols), lambda i: (0, 0), pipeline_mode=pl.Buffered(1)),
            pl.BlockSpec((1, cols), lambda i: (0, 0)),
        ],
        out_specs=[
            pl.BlockSpec((tm, n_hg), lambda i: (i, 0)),
            pl.BlockSpec((tm, n_mb), lambda i: (i, 0)),
            pl.BlockSpec((tm, n_gt), lambda i: (i, 0)),
        ],
        out_shape=[
            jax.ShapeDtypeStruct((n, n_hg), F32),
            jax.ShapeDtypeStruct((n, n_mb), F32),
            jax.ShapeDtypeStruct((n, n_gt), F32),
        ],
        compiler_params=pltpu.CompilerParams(
            dimension_semantics=("parallel",), vmem_limit_bytes=VMEM_LIMIT),
        name="in_proj",
    )(x2, w_bf, b_in)


def _hgrn_tables():
    c = HG_CHUNK
    levels = [c >> (i + 1) for i in range(int(math.log2(c)))]
    masks = np.zeros((len(levels) + 1, c, c), np.float32)
    for li, m in enumerate(levels):
        for r in range(c):
            c0 = (r // (2 * m)) * (2 * m)
            if r - c0 >= m:
                masks[li, r, c0:c0 + m] = 1.0
    masks[len(levels)] = np.eye(c, dtype=np.float32)
    return np.tril(np.ones((c, c), np.float32)), masks, tuple(levels)


def _level_ref_rows(p, m):
    c, w = p.shape
    if 2 * m > SUBLANES:
        parts = [jnp.broadcast_to(p[c0 + m - 1:c0 + m, :], (2 * m, w)) for c0 in range(0, c, 2 * m)]
        return parts[0] if len(parts) == 1 else jnp.concatenate(parts, axis=0)
    p3 = p.reshape(c // SUBLANES, SUBLANES, w)
    sub = lax.broadcasted_iota(jnp.int32, p3.shape, 1)
    out = None
    for c0 in range(0, SUBLANES, 2 * m):
        b = jnp.broadcast_to(p3[:, c0 + m - 1:c0 + m, :], p3.shape)
        out = b if out is None else jnp.where(sub >= c0, b, out)
    return out.reshape(c, w)


def _hgrn_kernel(q_ref, f_ref, i_ref, g_ref, lb_ref, ng_ref, tril_ref, msk_ref, o_ref,
                 *, n_chunks, levels, group):
    c = HG_CHUNK
    dk = HG_DK
    assert n_chunks % group == 0
    lb = jnp.concatenate([lb_ref[...]] * group, axis=1)
    oml = 1.0 - lb
    ng = ng_ref[...]
    tril = tril_ref[...]
    n_levels = len(levels)

    def load(ref, r0):
        return jnp.concatenate([ref[0, pl.ds(r0 + u * c, c), :] for u in range(group)], axis=1)

    def lanes(a, u):
        return a[:, u * dk:(u + 1) * dk]

    def intra(r0):
        z = load(f_ref, r0)
        qr = load(q_ref, r0)
        lf = jnp.log(lb + oml * jax.nn.sigmoid(z))
        kk = oml * jax.nn.sigmoid(-z)
        qf = qr * jax.nn.sigmoid(qr)
        l_hi, l_mid, l_lo = _split3(lf)
        p = _dot(tril, l_hi) + _dot(tril, l_mid) + _dot(tril, l_lo)
        b_end = p[c - 1:c, :]
        qb = (qf * jnp.exp(p)).astype(BF16)
        kd = (kk * jnp.exp(b_end - p)).astype(BF16)
        dec = jnp.exp(b_end)
        qh = qf.astype(BF16)
        kh = kk.astype(BF16)
        scores = [msk_ref[n_levels] * _dot_nt(lanes(qh, u), lanes(kh, u)) for u in range(group)]
        row = lax.broadcasted_iota(jnp.int32, p.shape, 0)
        for li, m in enumerate(levels):
            ex = jnp.exp2(jnp.abs(p - _level_ref_rows(p, m)) * (-LOG2_E))
            qk = (jnp.where((row & (2 * m - 1)) >= m, qf, kk) * ex).astype(BF16)
            for u in range(group):
                scores[u] = scores[u] + msk_ref[li] * _dot_nt(lanes(qk, u), lanes(qk, u))
        vbs = [i_ref[0, pl.ds(r0 + u * c, c), :].astype(BF16) for u in range(group)]
        o_intra = [_dot(scores[u].astype(BF16), vbs[u]) for u in range(group)]
        return qb, kd, dec, vbs, o_intra

    def chain(r0, staged, st):
        qb, kd, dec, vbs, o_intra = staged
        for u in range(group):
            rows = pl.ds(r0 + u * c, c)
            g = g_ref[0, rows, :]
            o = _dot_nt(lanes(qb, u), st.astype(BF16)) + o_intra[u]
            st = st * lanes(dec, u) + _dot_tn(vbs[u], lanes(kd, u))
            o = o * lax.rsqrt(jnp.mean(o * o, axis=-1, keepdims=True) + NORM_EPS)
            o_ref[0, rows, :] = o * ng * (g * jax.nn.sigmoid(g))
        return st

    n_groups = n_chunks // group
    st = jnp.zeros((HG_DK, HG_DK), F32)
    staged = intra(0)
    for gi in range(n_groups):
        nxt = intra((gi + 1) * group * c) if gi + 1 < n_groups else None
        st = chain(gi * group * c, staged, st)
        staged = nxt


def _hgrn(hg3, lb_row, ng_row):
    b, s, _ = hg3.shape
    tril, masks, levels = _hgrn_tables()
    mst = jnp.asarray(tril, BF16)
    msk = jnp.asarray(masks, F32)
    h = HG_HEADS

    def col(off):
        return pl.BlockSpec((1, s, HG_DK), lambda bi, hi: (bi, 0, off + hi))

    return pl.pallas_call(
        functools.partial(_hgrn_kernel, n_chunks=s // HG_CHUNK, levels=levels,
                          group=math.gcd(s // HG_CHUNK, HG_GROUP)),
        grid=(b, h),
        in_specs=[
            col(0), col(h), col(2 * h), col(3 * h),
            pl.BlockSpec((1, HG_DK), lambda bi, hi: (0, hi)),
            pl.BlockSpec((1, HG_DK), lambda bi, hi: (0, hi)),
            pl.BlockSpec(mst.shape, lambda bi, hi: (0, 0)),
            pl.BlockSpec(msk.shape, lambda bi, hi: (0, 0, 0)),
        ],
        out_specs=pl.BlockSpec((1, s, HG_DK), lambda bi, hi: (bi, 0, hi)),
        out_shape=jax.ShapeDtypeStruct((b, s, h * HG_DK), F32),
        compiler_params=pltpu.CompilerParams(
            dimension_semantics=("parallel", "parallel"), vmem_limit_bytes=VMEM_LIMIT),
        name="hgrn2",
    )(hg3, hg3, hg3, hg3, lb_row, ng_row, mst, msk)


def _t5_bucket_np(dist):
    max_exact = REL_BUCKETS // 2
    d = np.maximum(dist, 1).astype(np.float32)
    log_part = max_exact + (np.log(d / np.float32(max_exact)) / np.float32(math.log(REL_MAX_DIST / max_exact))
                            * np.float32(REL_BUCKETS - max_exact)).astype(np.int32)
    return np.where(dist < max_exact, dist, np.minimum(log_part, REL_BUCKETS - 1))


def _moba_kernel(q_ref, k_ref, v_ref, avg_ref, bkt_ref, rb_ref, o_ref, own_ref, prev_ref, *, n_blocks):
    blk = MB_BLOCK
    scale = MB_DH ** -0.5 * LOG2_E
    avg = avg_ref[...]
    hp = LANES // MB_DH

    @pl.when(pl.program_id(1) == 0)
    def _():
        causal = (lax.broadcasted_iota(jnp.int32, (blk, blk), 0) <= lax.broadcasted_iota(jnp.int32, (blk, blk), 1))
        for hh in range(hp):
            head = pl.program_id(0) * hp + hh
            own_t = jnp.zeros((blk, blk), F32)
            prev_t = jnp.zeros((blk, blk), F32)
            for bk in range(REL_BUCKETS):
                val = rb_ref[bk, head] * LOG2_E
                own_t = jnp.where(bkt_ref[0] == bk, val, own_t)
                prev_t = jnp.where(bkt_ref[1] == bk, val, prev_t)
            own_ref[hh] = jnp.where(causal, own_t, NEG_INF)
            prev_ref[hh] = prev_t

    grow = lax.broadcasted_iota(jnp.int32, (GATE_ROWS, blk), 0)
    vt_all = v_ref[0].T

    def fold(a):
        return a.reshape(blk // SUBLANES, SUBLANES, blk)

    heads = []
    for hh in range(hp):
        ls = slice(hh * MB_DH, (hh + 1) * MB_DH)
        k_h = k_ref[0, :, ls]
        k_hi, k_lo = _split2(k_h)
        k_mean = (_dot(avg, k_hi) + _dot(avg, k_lo))[:GATE_ROWS]
        vt = jnp.concatenate([vt_all[hh * MB_DH:(hh + 1) * MB_DH, :], jnp.ones((DEN_ROWS, vt_all.shape[1]), F32)],
                             axis=0).astype(BF16)
        far_bias = rb_ref[REL_BUCKETS - 1, pl.program_id(0) * hp + hh] * LOG2_E
        heads.append((ls, _split2(k_mean), k_h.astype(BF16), vt, far_bias))

    def logits(hh, i):
        ls, (km_hi, km_lo), kb, _, far_bias = heads[hh]
        qi = q_ref[0, i * blk:(i + 1) * blk, ls] * scale
        st = _dot_nt(kb[:(i + 1) * blk], qi.astype(BF16))
        sel = None
        if i > 0:
            q_hi, q_lo = _split2(qi)
            gate = _dot_nt(km_hi, q_hi) + _dot_nt(km_lo, q_hi) + _dot_nt(km_hi, q_lo)
            gate = jnp.where(grow < i, gate, NEG_INF)
            rank = jnp.zeros((GATE_ROWS, blk), F32)
            if i > MB_TOPK:
                for j2 in range(i):
                    gj = jnp.broadcast_to(gate[j2:j2 + 1, :], (GATE_ROWS, blk))
                    tie = jnp.where(grow > j2, 1.0, 0.0)
                    rank = rank + jnp.where(gj > gate, 1.0, jnp.where(gj == gate, tie, 0.0))
            sel = jnp.where(rank < MB_TOPK, gate, NEG_INF) > NEG_INF
        pieces = []
        for j in range(i + 1):
            sj = st[j * blk:(j + 1) * blk, :]
            if j == i:
                pieces.append(sj + own_ref[hh])
            elif j == i - 1:
                pieces.append(sj + prev_ref[hh] + jnp.where(sel[j:j + 1, :], 0.0, NEG_INF))
            else:
                pieces.append(sj + jnp.where(sel[j:j + 1, :], far_bias, NEG_INF))
        mx8 = fold(pieces[0]).max(axis=0)
        for p in pieces[1:]:
            mx8 = jnp.maximum(mx8, fold(p).max(axis=0))
        return pieces, mx8.max(axis=0, keepdims=True)

    def attend(hh, pieces, mx):
        vt = heads[hh][3]
        acc = jnp.zeros((MB_DH + DEN_ROWS, blk), F32)
        for j, p in enumerate(pieces):
            acc = acc + _dot(vt[:, j * blk:(j + 1) * blk], jnp.exp2(p - mx).astype(BF16))
        return acc[:MB_DH] / acc[MB_DH:MB_DH + 1]

    items = [(i, hh) for i in range(n_blocks) for hh in range(hp)]
    staged = [logits(hh, i) for i, hh in items[:PIPE_AHEAD]]
    done = {}
    for n, (i, hh) in enumerate(items):
        if n + PIPE_AHEAD < len(items):
            staged.append(logits(items[n + PIPE_AHEAD][1], items[n + PIPE_AHEAD][0]))
        done[hh] = attend(hh, *staged.pop(0))
        if hh == hp - 1:
            o_ref[0, i * blk:(i + 1) * blk, :] = jnp.concatenate([done[h] for h in range(hp)], axis=0).T


def _moba(mb3, rel_bias):
    b, s, _ = mb3.shape
    assert s % MB_BLOCK == 0
    nb = s // MB_BLOCK
    hp = LANES // MB_DH
    n_hp = MB_HEADS // hp
    t = np.arange(MB_BLOCK)
    d_own = t[:, None] - t[None, :]
    buckets = np.stack([_t5_bucket_np(np.maximum(d_own, 0)), _t5_bucket_np(d_own + MB_BLOCK)]).astype(np.int32)
    assert _t5_bucket_np(np.array([MB_BLOCK + 1]))[0] == REL_BUCKETS - 1
    assert nb <= GATE_ROWS
    bkt = jnp.asarray(buckets.transpose(0, 2, 1))
    avg_np = np.zeros((LANES, s), np.float32)
    for j in range(nb):
        avg_np[j, j * MB_BLOCK:(j + 1) * MB_BLOCK] = 1.0 / MB_BLOCK
    avg = jnp.asarray(avg_np, BF16)

    def col(off):
        return pl.BlockSpec((1, s, LANES), lambda hi, bi: (bi, 0, off + hi))

    return pl.pallas_call(
        functools.partial(_moba_kernel, n_blocks=nb),
        grid=(n_hp, b),
        in_specs=[
            col(0), col(n_hp), col(2 * n_hp),
            pl.BlockSpec(avg.shape, lambda hi, bi: (0, 0)),
            pl.BlockSpec(bkt.shape, lambda hi, bi: (0, 0, 0)),
            pl.BlockSpec(memory_space=pltpu.SMEM),
        ],
        out_specs=pl.BlockSpec((1, s, LANES), lambda hi, bi: (bi, 0, hi)),
        out_shape=jax.ShapeDtypeStruct((b, s, MB_HEADS * MB_DH), F32),
        scratch_shapes=[pltpu.VMEM((hp, MB_BLOCK, MB_BLOCK), F32), pltpu.VMEM((hp, MB_BLOCK, MB_BLOCK), F32)],
        compiler_params=pltpu.CompilerParams(
            dimension_semantics=("parallel", "arbitrary"), vmem_limit_bytes=VMEM_LIMIT),
        name="moba",
    )(mb3, mb3, mb3, avg, bkt, rel_bias.astype(F32))


def _layer_norm(x, g, b):
    mu = jnp.mean(x, axis=-1, keepdims=True)
    xc = x - mu
    var = jnp.mean(xc * xc, axis=-1, keepdims=True)
    return xc * lax.rsqrt(var + NORM_EPS) * g + b


def _merge_kernel(hg_ref, mb_ref, ga_ref, gb_ref, x_ref, wa_ref, wb_ref, wo_ref, g1_ref, b1_ref,
                  wr_ref, br_ref, tri_ref, x1_ref, route_ref, cnt_ref, run_ref, *, blocks_per_moe_tile, parts):
    tm = x_ref.shape[0]
    pr = tm // parts
    lane = lax.broadcasted_iota(jnp.int32, (pr, LANES), 1)

    @pl.when(pl.program_id(0) % blocks_per_moe_tile == 0)
    def _():
        run_ref[...] = jnp.zeros_like(run_ref)

    def mix(rows):
        ya = _dot(hg_ref[rows, :].astype(BF16), wa_ref[...])
        yb = _dot(mb_ref[rows, :].astype(BF16), wb_ref[...])
        mixed_in = jax.nn.sigmoid(ga_ref[rows, :]) * ya + jax.nn.sigmoid(gb_ref[rows, :]) * yb
        mixed = _dot(mixed_in.astype(BF16), wo_ref[...])
        x1 = _layer_norm(DN_ALPHA * x_ref[rows, :] + mixed, g1_ref[...], b1_ref[...])
        x1_ref[rows, :] = x1
        return x1

    def route_rows(rows, x1, run):
        x_hi, x_lo = _split2(x1)
        w_hi, w_lo = wr_ref[0], wr_ref[1]
        logits = _dot(x_hi, w_hi) + _dot(x_hi, w_lo) + _dot(x_lo, w_hi) + br_ref[...]
        glog = jnp.where(lane < N_GROUPS, logits, NEG_INF)
        gmax = jnp.max(glog, axis=-1, keepdims=True)
        grp = jnp.min(jnp.where(glog == gmax, lane, LANES), axis=-1, keepdims=True)
        p_grp = 1.0 / jnp.sum(jnp.exp(glog - gmax), axis=-1, keepdims=True)
        e_lo = N_GROUPS + grp * EXPERTS_PER_GROUP
        elog = jnp.where(jnp.logical_and(lane >= e_lo, lane < e_lo + EXPERTS_PER_GROUP), logits, NEG_INF)
        m1 = jnp.max(elog, axis=-1, keepdims=True)
        i1 = jnp.min(jnp.where(elog == m1, lane, LANES), axis=-1, keepdims=True)
        elog2 = jnp.where(lane == i1, NEG_INF, elog)
        m2 = jnp.max(elog2, axis=-1, keepdims=True)
        i2 = jnp.min(jnp.where(elog2 == m2, lane, LANES), axis=-1, keepdims=True)
        e2 = jnp.exp(m2 - m1)
        w1 = p_grp / (1.0 + e2)
        w2 = p_grp * e2 / (1.0 + e2)
        oh1 = jnp.where(lane == i1 - N_GROUPS, 1.0, 0.0)
        oh2 = jnp.where(lane == i2 - N_GROUPS, 1.0, 0.0)
        both = oh1 + oh2
        before = _dot(tri_ref[...], both.astype(BF16)) + run
        r1 = jnp.sum(before * oh1, axis=-1, keepdims=True)
        r2 = jnp.sum(before * oh2, axis=-1, keepdims=True)
        cols = ((i1 - N_GROUPS).astype(F32), (i2 - N_GROUPS).astype(F32), w1, w2, r1, r2)
        route = jnp.zeros((pr, LANES), F32)
        for li, col in enumerate(cols):
            route = jnp.where(lane == li, col, route)
        route_ref[:, rows] = route.T[:ROUTE_FIELDS, :]
        return run + jnp.sum(both, axis=0, keepdims=True)

    slabs = [slice(h * pr, (h + 1) * pr) for h in range(parts)]
    mixed = [mix(rows) for rows in slabs]
    run = run_ref[...]
    for rows, x1 in zip(slabs, mixed):
        run = route_rows(rows, x1, run)
    run_ref[...] = run
    cnt_ref[0] = run


def _merge(hg_o, mb_o, gates, x2, wa, wb, wo, g1, b1, wr, br, tm, moe_tile):
    n, d = x2.shape
    wa_n = hg_o.shape[1]
    wb_n = mb_o.shape[1]
    assert moe_tile % tm == 0
    parts = 1
    tri = jnp.asarray(np.tril(np.ones((tm // parts, tm // parts), np.float32), -1), BF16)

    def full(a):
        nd = a.ndim
        return pl.BlockSpec(a.shape, lambda i: (0,) * nd)

    return pl.pallas_call(
        functools.partial(_merge_kernel, blocks_per_moe_tile=moe_tile // tm, parts=parts),
        grid=(n // tm,),
        in_specs=[
            pl.BlockSpec((tm, wa_n), lambda i: (i, 0)),
            pl.BlockSpec((tm, wb_n), lambda i: (i, 0)),
            pl.BlockSpec((tm, d), lambda i: (i, 0)),
            pl.BlockSpec((tm, d), lambda i: (i, 1)),
            pl.BlockSpec((tm, d), lambda i: (i, 0)),
            full(wa), full(wb), full(wo), full(g1), full(b1), full(wr), full(br), full(tri),
        ],
        out_specs=[
            pl.BlockSpec((tm, d), lambda i: (i, 0)),
            pl.BlockSpec((ROUTE_FIELDS, tm), lambda i: (0, i)),
            pl.BlockSpec((1, 1, LANES), lambda i: (i, 0, 0)),
        ],
        out_shape=[
            jax.ShapeDtypeStruct((n, d), F32),
            jax.ShapeDtypeStruct((ROUTE_FIELDS, n), F32),
            jax.ShapeDtypeStruct((n // tm, 1, LANES), F32),
        ],
        scratch_shapes=[pltpu.VMEM((1, LANES), F32)],
        compiler_params=pltpu.CompilerParams(
            dimension_semantics=("arbitrary",), vmem_limit_bytes=VMEM_LIMIT),
        name="merge_ln1_router",
    )(hg_o, mb_o, gates, gates, x2, wa, wb, wo, g1, b1, wr, br, tri)


def _segment_copies(n_small, make_copy, act):
    per_big = SEG_BIG // SEG_PAD
    n_big = n_small // per_big

    def big(c, carry):
        act(make_copy(c * SEG_BIG, SEG_BIG))
        return carry
    lax.fori_loop(0, n_big, big, 0)

    def small(c, carry):
        act(make_copy(n_big * SEG_BIG + c * SEG_PAD, SEG_PAD))
        return carry
    lax.fori_loop(0, n_small - n_big * per_big, small, 0)


def _wait_copies(n_big, n_small, make_copy):
    def big(c, carry):
        make_copy(0, SEG_BIG).wait()
        return carry
    lax.fori_loop(0, n_big, big, 0)

    def small(c, carry):
        make_copy(0, SEG_PAD).wait()
        return carry
    lax.fori_loop(0, n_small, small, 0)


def _dispatch_kernel(lbase_ref, gseg_ref, nseg_ref, tot_ref, gap_ref, p0_ref, p1_ref, x1_ref, xs_hbm,
                     xs_ref, xb_ref, zero_ref, sem, *, cv_rows):
    ti = pl.program_id(0)
    n_tiles = pl.num_programs(0)
    groups, _, d = x1_ref.shape

    def out_copy(t, ex):
        def make(off, size):
            src = pl.multiple_of(lbase_ref[t * N_EXPERTS + ex] + off, SEG_PAD)
            dst = pl.multiple_of(gseg_ref[t * N_EXPERTS + ex] + off, SEG_PAD)
            return pltpu.make_async_copy(xb_ref.at[pl.ds(src, size), :], xs_hbm.at[pl.ds(dst, size), :], sem.at[0])
        return make

    @pl.when(ti == 0)
    def _():
        def zero(c, carry):
            r0 = pl.multiple_of(c * cv_rows, cv_rows)
            xs_ref[pl.ds(r0, cv_rows), :] = jnp.zeros((cv_rows, d), F32)
            return carry
        lax.fori_loop(0, xs_ref.shape[0] // cv_rows, zero, 0)
        zero_ref[...] = jnp.zeros_like(zero_ref)

    def sort_rows(grp, carry):
        t0 = grp * SUBLANES
        for k in range(SUBLANES):
            row = x1_ref[grp, pl.ds(k, 1), :]
            xs_ref[pl.ds(p0_ref[0, 0, t0 + k], 1), :] = row
            xs_ref[pl.ds(p1_ref[0, 0, t0 + k], 1), :] = row
        return carry
    lax.fori_loop(0, groups, sort_rows, 0)

    @pl.when(ti > 0)
    def _():
        _wait_copies(tot_ref[2 * (ti - 1)], tot_ref[2 * (ti - 1) + 1], out_copy(0, 0))

    def convert(c, carry):
        r0 = pl.multiple_of(c * cv_rows, cv_rows)
        xb_ref[pl.ds(r0, cv_rows), :] = xs_ref[pl.ds(r0, cv_rows), :].astype(BF16)
        return carry
    lax.fori_loop(0, xs_ref.shape[0] // cv_rows, convert, 0)

    def send(ex, carry):
        _segment_copies(nseg_ref[ti * N_EXPERTS + ex], out_copy(ti, ex), lambda cp: cp.start())
        return carry
    lax.fori_loop(0, N_EXPERTS, send, 0)

    @pl.when(ti == n_tiles - 1)
    def _():
        _wait_copies(tot_ref[2 * ti], tot_ref[2 * ti + 1], out_copy(0, 0))

        def fill_copy(row):
            return pltpu.make_async_copy(zero_ref, xs_hbm.at[pl.ds(pl.multiple_of(row, SEG_PAD), SEG_PAD), :],
                                         sem.at[1])

        def fill(ex, carry):
            def one(c, carry2):
                fill_copy(gap_ref[2 * ex] + c * SEG_PAD).start()
                return carry2
            lax.fori_loop(0, gap_ref[2 * ex + 1], one, 0)
            return carry
        lax.fori_loop(0, N_EXPERTS, fill, 0)

        def fill_wait(ex, carry):
            def one(c, carry2):
                fill_copy(0).wait()
                return carry2
            lax.fori_loop(0, gap_ref[2 * ex + 1], one, 0)
            return carry
        lax.fori_loop(0, N_EXPERTS, fill_wait, 0)


def _expert_kernel(be_ref, nu_ref, x_ref, wgu_ref, wd_ref, y_ref, wgu_bf, wd_bf, *, cast_rows):
    i = pl.program_id(0)

    @pl.when(i < nu_ref[0])
    def _():
        @pl.when(jnp.logical_or(i == 0, be_ref[i] != be_ref[jnp.maximum(i - 1, 0)]))
        def _():
            for r0 in range(0, wgu_bf.shape[0], cast_rows):
                wgu_bf[r0:r0 + cast_rows, :] = wgu_ref[0, r0:r0 + cast_rows, :].astype(BF16)
            for r0 in range(0, wd_bf.shape[0], cast_rows):
                wd_bf[r0:r0 + cast_rows, :] = wd_ref[0, r0:r0 + cast_rows, :].astype(BF16)

        gu = _dot(x_ref[...], wgu_bf[...])
        gate = gu[:, :EXPERT_HIDDEN]
        up = gu[:, EXPERT_HIDDEN:]
        hdn = (gate * jax.nn.sigmoid(gate) * up).astype(BF16)
        y_ref[...] = _dot(hdn, wd_bf[...])


def _combine_kernel(lbase_ref, gseg_ref, nseg_ref, tot_ref, p0_ref, p1_ref, w0_ref, w1_ref, x1_ref, ys_hbm,
                    g2_ref, b2_ref,
                    o_ref, xs_ref, sem, *, ln_rows):
    ti = pl.program_id(0)
    n_tiles = pl.num_programs(0)
    groups, _, d = x1_ref.shape
    slot = ti % 2

    def in_copy(t, ex, sl):
        def make(off, size):
            src = pl.multiple_of(gseg_ref[t * N_EXPERTS + ex] + off, SEG_PAD)
            dst = pl.multiple_of(lbase_ref[t * N_EXPERTS + ex] + off, SEG_PAD)
            return pltpu.make_async_copy(ys_hbm.at[pl.ds(src, size), :], xs_ref.at[sl, pl.ds(dst, size), :],
                                         sem.at[sl])
        return make

    def fetch(t, sl):
        def one(ex, carry):
            _segment_copies(nseg_ref[t * N_EXPERTS + ex], in_copy(t, ex, sl), lambda cp: cp.start())
            return carry
        lax.fori_loop(0, N_EXPERTS, one, 0)

    @pl.when(ti == 0)
    def _():
        fetch(0, 0)

    @pl.when(ti + 1 < n_tiles)
    def _():
        fetch(ti + 1, 1 - slot)

    _wait_copies(tot_ref[2 * ti], tot_ref[2 * ti + 1], in_copy(0, 0, slot))

    def combine(grp, carry):
        t0 = grp * SUBLANES
        for k in range(SUBLANES):
            t = t0 + k
            o_ref[grp, pl.ds(k, 1), :] = (DN_ALPHA * x1_ref[grp, pl.ds(k, 1), :]
                                          + w0_ref[0, 0, t] * xs_ref[slot, pl.ds(p0_ref[0, 0, t], 1), :]
                                          + w1_ref[0, 0, t] * xs_ref[slot, pl.ds(p1_ref[0, 0, t], 1), :])
        return carry
    lax.fori_loop(0, groups, combine, 0)

    ln_groups = ln_rows // SUBLANES

    def norm(c, carry):
        g0 = pl.multiple_of(c * ln_groups, ln_groups)
        rows = o_ref[pl.ds(g0, ln_groups), :, :].reshape(ln_rows, d)
        o_ref[pl.ds(g0, ln_groups), :, :] = _layer_norm(rows, g2_ref[...], b2_ref[...]).reshape(ln_groups, SUBLANES, d)
        return carry
    lax.fori_loop(0, groups // ln_groups, norm, 0)


def _moe_tile(n):
    return min(MOE_TILE, n)


def _round_up(a, m):
    return (a + m - 1) // m * m


def _moe(x1, route, cnt_run, w_gate_up, w_down, g2, b2):
    n, d = x1.shape
    tile = _moe_tile(n)
    assert n % tile == 0
    n_tiles = n // tile
    i32 = jnp.int32
    cnt = cnt_run.reshape(n_tiles, -1, LANES)[:, -1, :N_EXPERTS].astype(i32)
    seg = _round_up(cnt, SEG_PAD)
    lbase = jnp.cumsum(seg, axis=1) - seg
    used = jnp.sum(seg, axis=0)
    region = _round_up(used, EXP_ROWS)
    e_start = jnp.cumsum(region) - region
    gseg = e_start[None, :] + jnp.cumsum(seg, axis=0) - seg
    nseg = seg // SEG_PAD
    per_big = SEG_BIG // SEG_PAD
    tot = jnp.stack([jnp.sum(nseg // per_big, axis=1), jnp.sum(nseg % per_big, axis=1)], axis=1)
    gap = jnp.stack([e_start + used, (region - used) // SEG_PAD], axis=1)
    rows_max = _round_up(TOP_K * n + n_tiles * N_EXPERTS * (SEG_PAD - 1) + N_EXPERTS * (EXP_ROWS - 1), EXP_ROWS)
    n_blocks = rows_max // EXP_ROWS
    n_used = (jnp.sum(region) // EXP_ROWS).astype(i32).reshape(1)
    blk_row = jnp.arange(n_blocks, dtype=i32)[:, None] * EXP_ROWS
    blk_expert = jnp.minimum(jnp.sum((blk_row >= jnp.cumsum(region)[None, :]).astype(i32), axis=1), N_EXPERTS - 1)
    r_exp = route[0:TOP_K].astype(i32).reshape(TOP_K, n_tiles, tile)
    r_rank = route[2 * TOP_K:3 * TOP_K].astype(i32).reshape(TOP_K, n_tiles, tile)
    seg_start = jnp.sum(jnp.where(r_exp[..., None] == jnp.arange(N_EXPERTS, dtype=i32), lbase[None, :, None, :], 0),
                        axis=-1)
    pos = (seg_start + r_rank).reshape(TOP_K, n_tiles, 1, tile)
    r_wgt = route[TOP_K:2 * TOP_K].reshape(TOP_K, n_tiles, 1, tile)
    local_rows = _round_up(TOP_K * tile + N_EXPERTS * (SEG_PAD - 1), 256)
    flat = lambda a: a.reshape(-1).astype(i32)

    def smem_spec():
        return pl.BlockSpec((1, 1, tile), lambda t, *_: (t, 0, 0), memory_space=pltpu.SMEM)

    x1_groups = x1.reshape(n // SUBLANES, SUBLANES, d)
    row_groups_spec = pl.BlockSpec((tile // SUBLANES, SUBLANES, d), lambda t, *_: (t, 0, 0))

    xs_hbm = pl.pallas_call(
        functools.partial(_dispatch_kernel, cv_rows=256),
        grid_spec=pltpu.PrefetchScalarGridSpec(
            num_scalar_prefetch=5,
            grid=(n_tiles,),
            in_specs=[smem_spec(), smem_spec(), row_groups_spec],
            out_specs=pl.BlockSpec(memory_space=pl.ANY),
            scratch_shapes=[
                pltpu.VMEM((local_rows, d), F32),
                pltpu.VMEM((local_rows, d), BF16),
                pltpu.VMEM((SEG_PAD, d), BF16),
                pltpu.SemaphoreType.DMA((2,)),
            ],
        ),
        out_shape=jax.ShapeDtypeStruct((rows_max, d), BF16),
        compiler_params=pltpu.CompilerParams(dimension_semantics=("arbitrary",), vmem_limit_bytes=VMEM_LIMIT),
        name="moe_dispatch",
    )(flat(lbase), flat(gseg), flat(nseg), flat(tot), flat(gap), pos[0], pos[1], x1_groups)

    def blk(i, be, nu):
        return jnp.minimum(i, nu[0] - 1)

    ys_hbm = pl.pallas_call(
        functools.partial(_expert_kernel, cast_rows=256),
        grid_spec=pltpu.PrefetchScalarGridSpec(
            num_scalar_prefetch=2,
            grid=(n_blocks,),
            in_specs=[
                pl.BlockSpec((EXP_ROWS, d), lambda i, be, nu: (blk(i, be, nu), 0)),
                pl.BlockSpec((1, d, 2 * EXPERT_HIDDEN), lambda i, be, nu: (be[blk(i, be, nu)], 0, 0)),
                pl.BlockSpec((1, EXPERT_HIDDEN, d), lambda i, be, nu: (be[blk(i, be, nu)], 0, 0)),
            ],
            out_specs=pl.BlockSpec((EXP_ROWS, d), lambda i, be, nu: (blk(i, be, nu), 0)),
            scratch_shapes=[pltpu.VMEM((d, 2 * EXPERT_HIDDEN), BF16), pltpu.VMEM((EXPERT_HIDDEN, d), BF16)],
        ),
        out_shape=jax.ShapeDtypeStruct((rows_max, d), F32),
        compiler_params=pltpu.CompilerParams(dimension_semantics=("arbitrary",), vmem_limit_bytes=VMEM_LIMIT),
        name="moe_experts",
    )(blk_expert, n_used, xs_hbm, w_gate_up, w_down)

    return pl.pallas_call(
        functools.partial(_combine_kernel, ln_rows=min(256, tile)),
        grid_spec=pltpu.PrefetchScalarGridSpec(
            num_scalar_prefetch=4,
            grid=(n_tiles,),
            in_specs=[
                smem_spec(), smem_spec(), smem_spec(), smem_spec(),
                row_groups_spec,
                pl.BlockSpec(memory_space=pl.ANY),
                pl.BlockSpec((1, d), lambda t, *_: (0, 0)),
                pl.BlockSpec((1, d), lambda t, *_: (0, 0)),
            ],
            out_specs=row_groups_spec,
            scratch_shapes=[pltpu.VMEM((2, local_rows, d), F32), pltpu.SemaphoreType.DMA((2,))],
        ),
        out_shape=jax.ShapeDtypeStruct(x1_groups.shape, F32),
        compiler_params=pltpu.CompilerParams(dimension_semantics=("arbitrary",), vmem_limit_bytes=VMEM_LIMIT),
        name="moe_combine_ln2",
    )(flat(lbase), flat(gseg), flat(nseg), flat(tot), pos[0], pos[1], r_wgt[0], r_wgt[1], x1_groups, ys_hbm,
      g2, b2).reshape(n, d)


def _block(x, w_in, b_in, lower_bound, hg_norm_g, rel_bias, w_proj_a, w_proj_b, w_out, ln1_g, ln1_b,
           w_group, b_group, w_expert, b_expert, w_gate_up, w_down, ln2_g, ln2_b, *, tm_proj, tm_merge):
    b, s, d = x.shape
    n = b * s
    n_hg = 4 * HG_HEADS * HG_DK
    n_mb = 3 * MB_HEADS * MB_DH
    n_gt = 2 * d
    x2 = x.reshape(n, d)
    hg, mb, gates = _in_proj(x2, w_in.astype(BF16), b_in.reshape(1, -1), n_hg, n_mb, n_gt, tm_proj)
    hg_o = _hgrn(hg.reshape(b, s, n_hg), lower_bound.reshape(1, -1), hg_norm_g.reshape(1, -1))
    mb_o = _moba(mb.reshape(b, s, n_mb), rel_bias)
    w_r = jnp.zeros((d, LANES), F32).at[:, :N_GROUPS].set(w_group).at[:, N_GROUPS:N_GROUPS + N_EXPERTS].set(w_expert)
    b_r = jnp.zeros((1, LANES), F32).at[0, :N_GROUPS].set(b_group).at[0, N_GROUPS:N_GROUPS + N_EXPERTS].set(b_expert)
    tm_merge = min(tm_merge, n)
    x1, route, cnt_run = _merge(hg_o.reshape(n, -1), mb_o.reshape(n, -1), gates, x2,
                                w_proj_a.astype(BF16), w_proj_b.astype(BF16), w_out.astype(BF16),
                                ln1_g.reshape(1, d), ln1_b.reshape(1, d), jnp.stack(_split2(w_r)), b_r,
                                tm_merge, _moe_tile(n))
    out = _moe(x1, route, cnt_run, w_gate_up, w_down, ln2_g.reshape(1, d), ln2_b.reshape(1, d))
    return out.reshape(b, s, d)


def kernel(x, w_in, b_in, lb_logits, hg_norm_g, rel_bias, w_proj_a, w_proj_b, w_out, ln1_g, ln1_b, w_group,
           b_group, w_expert, b_expert, w_gate_up, w_down, ln2_g, ln2_b):
    lower_bounds = jnp.cumsum(jax.nn.softmax(lb_logits.astype(F32), axis=0), axis=0)
    l = 0
    return _block(x, w_in[l], b_in[l], lower_bounds[l], hg_norm_g[l], rel_bias, w_proj_a[l], w_proj_b[l],
                  w_out[l], ln1_g[l], ln1_b[l], w_group[l], b_group[l], w_expert[l], b_expert[l],
                  w_gate_up[l], w_down[l], ln2_g[l], ln2_b[l], tm_proj=512, tm_merge=512)
```

```python
import functools
import math

import numpy as np
import jax
import jax.numpy as jnp
from jax import lax
from jax.experimental import pallas as pl
from jax.experimental.pallas import tpu as pltpu

F32 = jnp.float32
BF16 = jnp.bfloat16

HG_HEADS = 4
HG_DK = 128
HG_CHUNK = 128
HG_GROUP = 8
MB_HEADS = 8
MB_DH = 64
MB_BLOCK = 256
MB_TOPK = 3
PIPE_AHEAD = 2
DEN_ROWS = 16
ROUTE_FIELDS = 8
GATE_ROWS = 16
REL_BUCKETS = 32
REL_MAX_DIST = 128
N_GROUPS = 4
EXPERTS_PER_GROUP = 8
N_EXPERTS = N_GROUPS * EXPERTS_PER_GROUP
TOP_K = 2
EXPERT_HIDDEN = 512
MOE_TILE = 1024
SEG_PAD = 16
SEG_BIG = 64
EXP_ROWS = 512
DEPTH = 1
DN_ALPHA = (2.0 * DEPTH) ** 0.25
NORM_EPS = 1e-5
LANES = 128
SUBLANES = 8
VMEM_LIMIT = 56 * 1024 * 1024
NEG_INF = float("-inf")
LOG2_E = 1.4426950408889634


def _split2(a):
    hi = a.astype(BF16)
    lo = (a - hi.astype(F32)).astype(BF16)
    return hi, lo


def _split3(a):
    hi = a.astype(BF16)
    r = a - hi.astype(F32)
    mid = r.astype(BF16)
    lo = (r - mid.astype(F32)).astype(BF16)
    return hi, mid, lo


def _dot_nt(a, b):
    return lax.dot_general(a, b, (((1,), (1,)), ((), ())), preferred_element_type=F32)


def _dot_tn(a, b):
    return lax.dot_general(a, b, (((0,), (0,)), ((), ())), preferred_element_type=F32)


def _dot(a, b):
    return jnp.dot(a, b, preferred_element_type=F32)


def _in_proj_kernel(x_ref, w_ref, b_ref, hg_ref, mb_ref, gt_ref, *, col_chunk):
    xb = x_ref[...].astype(BF16)
    outs = ((hg_ref, 0), (mb_ref, hg_ref.shape[1]), (gt_ref, hg_ref.shape[1] + mb_ref.shape[1]))
    for o_ref, base in outs:
        for c0 in range(0, o_ref.shape[1], col_chunk):
            acc = _dot(xb, w_ref[:, base + c0:base + c0 + col_chunk])
            o_ref[:, c0:c0 + col_chunk] = acc + b_ref[:, base + c0:base + c0 + col_chunk]


def _in_proj(x2, w_bf, b_in, n_hg, n_mb, n_gt, tm):
    n, d = x2.shape
    cols = w_bf.shape[1]
    return pl.pallas_call(
        functools.partial(_in_proj_kernel, col_chunk=512),
        grid=(n // tm,),
        in_specs=[
            pl.BlockSpec((tm, d), lambda i: (i, 0)),
            pl.BlockSpec((d, cols), lambda i: (0, 0), pipeline_mode=pl.Buffered(1)),
            pl.BlockSpec((1, cols), lambda i: (0, 0)),
        ],
        out_specs=[
            pl.BlockSpec((tm, n_hg), lambda i: (i, 0)),
            pl.BlockSpec((tm, n_mb), lambda i: (i, 0)),
            pl.BlockSpec((tm, n_gt), lambda i: (i, 0)),
        ],
        out_shape=[
            jax.ShapeDtypeStruct((n, n_hg), F32),
            jax.ShapeDtypeStruct((n, n_mb), F32),
            jax.ShapeDtypeStruct((n, n_gt), F32),
        ],
        compiler_params=pltpu.CompilerParams(
            dimension_semantics=("parallel",), vmem_limit_bytes=VMEM_LIMIT),
        name="in_proj",
    )(x2, w_bf, b_in)


def _hgrn_tables():
    c = HG_CHUNK
    levels = [c >> (i + 1) for i in range(int(math.log2(c)))]
    masks = np.zeros((len(levels) + 1, c, c), np.float32)
    for li, m in enumerate(levels):
        for r in range(c):
            c0 = (r // (2 * m)) * (2 * m)
            if r - c0 >= m:
                masks[li, r, c0:c0 + m] = 1.0
    masks[len(levels)] = np.eye(c, dtype=np.float32)
    return np.tril(np.ones((c, c), np.float32)), masks, tuple(levels)


def _level_ref_rows(p, m):
    c, w = p.shape
    if 2 * m > SUBLANES:
        parts = [jnp.broadcast_to(p[c0 + m - 1:c0 + m, :], (2 * m, w)) for c0 in range(0, c, 2 * m)]
        return parts[0] if len(parts) == 1 else jnp.concatenate(parts, axis=0)
    p3 = p.reshape(c // SUBLANES, SUBLANES, w)
    sub = lax.broadcasted_iota(jnp.int32, p3.shape, 1)
    out = None
    for c0 in range(0, SUBLANES, 2 * m):
        b = jnp.broadcast_to(p3[:, c0 + m - 1:c0 + m, :], p3.shape)
        out = b if out is None else jnp.where(sub >= c0, b, out)
    return out.reshape(c, w)


def _hgrn_kernel(q_ref, f_ref, i_ref, g_ref, lb_ref, ng_ref, tril_ref, msk_ref, o_ref,
                 *, n_chunks, levels, group):
    c = HG_CHUNK
    dk = HG_DK
    assert n_chunks % group == 0
    lb = jnp.concatenate([lb_ref[...]] * group, axis=1)
    oml = 1.0 - lb
    ng = ng_ref[...]
    tril = tril_ref[...]
    n_levels = len(levels)

    def load(ref, r0):
        return jnp.concatenate([ref[0, pl.ds(r0 + u * c, c), :] for u in range(group)], axis=1)

    def lanes(a, u):
        return a[:, u * dk:(u + 1) * dk]

    def intra(r0):
        z = load(f_ref, r0)
        qr = load(q_ref, r0)
        lf = jnp.log(lb + oml * jax.nn.sigmoid(z))
        kk = oml * jax.nn.sigmoid(-z)
        qf = qr * jax.nn.sigmoid(qr)
        l_hi, l_mid, l_lo = _split3(lf)
        p = _dot(tril, l_hi) + _dot(tril, l_mid) + _dot(tril, l_lo)
        b_end = p[c - 1:c, :]
        qb = (qf * jnp.exp(p)).astype(BF16)
        kd = (kk * jnp.exp(b_end - p)).astype(BF16)
        dec = jnp.exp(b_end)
        qh = qf.astype(BF16)
        kh = kk.astype(BF16)
        scores = [msk_ref[n_levels] * _dot_nt(lanes(qh, u), lanes(kh, u)) for u in range(group)]
        row = lax.broadcasted_iota(jnp.int32, p.shape, 0)
        for li, m in enumerate(levels):
            ex = jnp.exp2(jnp.abs(p - _level_ref_rows(p, m)) * (-LOG2_E))
            qk = (jnp.where((row & (2 * m - 1)) >= m, qf, kk) * ex).astype(BF16)
            for u in range(group):
                scores[u] = scores[u] + msk_ref[li] * _dot_nt(lanes(qk, u), lanes(qk, u))
        vbs = [i_ref[0, pl.ds(r0 + u * c, c), :].astype(BF16) for u in range(group)]
        o_intra = [_dot(scores[u].astype(BF16), vbs[u]) for u in range(group)]
        return qb, kd, dec, vbs, o_intra

    def chain(r0, staged, st):
        qb, kd, dec, vbs, o_intra = staged
        for u in range(group):
            rows = pl.ds(r0 + u * c, c)
            g = g_ref[0, rows, :]
            o = _dot_nt(lanes(qb, u), st.astype(BF16)) + o_intra[u]
            st = st * lanes(dec, u) + _dot_tn(vbs[u], lanes(kd, u))
            o = o * lax.rsqrt(jnp.mean(o * o, axis=-1, keepdims=True) + NORM_EPS)
            o_ref[0, rows, :] = o * ng * (g * jax.nn.sigmoid(g))
        return st

    n_groups = n_chunks // group
    st = jnp.zeros((HG_DK, HG_DK), F32)
    staged = intra(0)
    for gi in range(n_groups):
        nxt = intra((gi + 1) * group * c) if gi + 1 < n_groups else None
        st = chain(gi * group * c, staged, st)
        staged = nxt


def _hgrn(hg3, lb_row, ng_row):
    b, s, _ = hg3.shape
    tril, masks, levels = _hgrn_tables()
    mst = jnp.asarray(tril, BF16)
    msk = jnp.asarray(masks, F32)
    h = HG_HEADS

    def col(off):
        return pl.BlockSpec((1, s, HG_DK), lambda bi, hi: (bi, 0, off + hi))

    return pl.pallas_call(
        functools.partial(_hgrn_kernel, n_chunks=s // HG_CHUNK, levels=levels,
                          group=math.gcd(s // HG_CHUNK, HG_GROUP)),
        grid=(b, h),
        in_specs=[
            col(0), col(h), col(2 * h), col(3 * h),
            pl.BlockSpec((1, HG_DK), lambda bi, hi: (0, hi)),
            pl.BlockSpec((1, HG_DK), lambda bi, hi: (0, hi)),
            pl.BlockSpec(mst.shape, lambda bi, hi: (0, 0)),
            pl.BlockSpec(msk.shape, lambda bi, hi: (0, 0, 0)),
        ],
        out_specs=pl.BlockSpec((1, s, HG_DK), lambda bi, hi: (bi, 0, hi)),
        out_shape=jax.ShapeDtypeStruct((b, s, h * HG_DK), F32),
        compiler_params=pltpu.CompilerParams(
            dimension_semantics=("parallel", "parallel"), vmem_limit_bytes=VMEM_LIMIT),
        name="hgrn2",
    )(hg3, hg3, hg3, hg3, lb_row, ng_row, mst, msk)


def _t5_bucket_np(dist):
    max_exact = REL_BUCKETS // 2
    d = np.maximum(dist, 1).astype(np.float32)
    log_part = max_exact + (np.log(d / np.float32(max_exact)) / np.float32(math.log(REL_MAX_DIST / max_exact))
                            * np.float32(REL_BUCKETS - max_exact)).astype(np.int32)
    return np.where(dist < max_exact, dist, np.minimum(log_part, REL_BUCKETS - 1))


def _moba_kernel(q_ref, k_ref, v_ref, avg_ref, bkt_ref, rb_ref, o_ref, own_ref, prev_ref, *, n_blocks):
    blk = MB_BLOCK
    scale = MB_DH ** -0.5 * LOG2_E
    avg = avg_ref[...]
    hp = LANES // MB_DH

    @pl.when(pl.program_id(1) == 0)
    def _():
        causal = (lax.broadcasted_iota(jnp.int32, (blk, blk), 0) <= lax.broadcasted_iota(jnp.int32, (blk, blk), 1))
        for hh in range(hp):
            head = pl.program_id(0) * hp + hh
            own_t = jnp.zeros((blk, blk), F32)
            prev_t = jnp.zeros((blk, blk), F32)
            for bk in range(REL_BUCKETS):
                val = rb_ref[bk, head] * LOG2_E
                own_t = jnp.where(bkt_ref[0] == bk, val, own_t)
                prev_t = jnp.where(bkt_ref[1] == bk, val, prev_t)
            own_ref[hh] = jnp.where(causal, own_t, NEG_INF)
            prev_ref[hh] = prev_t

    grow = lax.broadcasted_iota(jnp.int32, (GATE_ROWS, blk), 0)
    vt_all = v_ref[0].T

    def fold(a):
        return a.reshape(blk // SUBLANES, SUBLANES, blk)

    heads = []
    for hh in range(hp):
        ls = slice(hh * MB_DH, (hh + 1) * MB_DH)
        k_h = k_ref[0, :, ls]
        k_hi, k_lo = _split2(k_h)
        k_mean = (_dot(avg, k_hi) + _dot(avg, k_lo))[:GATE_ROWS]
        vt = jnp.concatenate([vt_all[hh * MB_DH:(hh + 1) * MB_DH, :], jnp.ones((DEN_ROWS, vt_all.shape[1]), F32)],
                             axis=0).astype(BF16)
        far_bias = rb_ref[REL_BUCKETS - 1, pl.program_id(0) * hp + hh] * LOG2_E
        heads.append((ls, _split2(k_mean), k_h.astype(BF16), vt, far_bias))

    def logits(hh, i):
        ls, (km_hi, km_lo), kb, _, far_bias = heads[hh]
        qi = q_ref[0, i * blk:(i + 1) * blk, ls] * scale
        st = _dot_nt(kb[:(i + 1) * blk], qi.astype(BF16))
        sel = None
        if i > 0:
            q_hi, q_lo = _split2(qi)
            gate = _dot_nt(km_hi, q_hi) + _dot_nt(km_lo, q_hi) + _dot_nt(km_hi, q_lo)
            gate = jnp.where(grow < i, gate, NEG_INF)
            rank = jnp.zeros((GATE_ROWS, blk), F32)
            if i > MB_TOPK:
                for j2 in range(i):
                    gj = jnp.broadcast_to(gate[j2:j2 + 1, :], (GATE_ROWS, blk))
                    tie = jnp.where(grow > j2, 1.0, 0.0)
                    rank = rank + jnp.where(gj > gate, 1.0, jnp.where(gj == gate, tie, 0.0))
            sel = jnp.where(rank < MB_TOPK, gate, NEG_INF) > NEG_INF
        pieces = []
        for j in range(i + 1):
            sj = st[j * blk:(j + 1) * blk, :]
            if j == i:
                pieces.append(sj + own_ref[hh])
            elif j == i - 1:
                pieces.append(sj + prev_ref[hh] + jnp.where(sel[j:j + 1, :], 0.0, NEG_INF))
            else:
                pieces.append(sj + jnp.where(sel[j:j + 1, :], far_bias, NEG_INF))
        mx8 = fold(pieces[0]).max(axis=0)
        for p in pieces[1:]:
            mx8 = jnp.maximum(mx8, fold(p).max(axis=0))
        return pieces, mx8.max(axis=0, keepdims=True)

    def attend(hh, pieces, mx):
        vt = heads[hh][3]
        weights = jnp.concatenate([jnp.exp2(p - mx).astype(BF16) for p in pieces], axis=0)
        acc = _dot(vt[:, :len(pieces) * blk], weights)
        return acc[:MB_DH] / acc[MB_DH:MB_DH + 1]

    items = [(i, hh) for i in range(n_blocks) for hh in range(hp)]
    staged = [logits(hh, i) for i, hh in items[:PIPE_AHEAD]]
    done = {}
    for n, (i, hh) in enumerate(items):
        if n + PIPE_AHEAD < len(items):
            staged.append(logits(items[n + PIPE_AHEAD][1], items[n + PIPE_AHEAD][0]))
        done[hh] = attend(hh, *staged.pop(0))
        if hh == hp - 1:
            o_ref[0, i * blk:(i + 1) * blk, :] = jnp.concatenate([done[h] for h in range(hp)], axis=0).T


def _moba(mb3, rel_bias):
    b, s, _ = mb3.shape
    assert s % MB_BLOCK == 0
    nb = s // MB_BLOCK
    hp = LANES // MB_DH
    n_hp = MB_HEADS // hp
    t = np.arange(MB_BLOCK)
    d_own = t[:, None] - t[None, :]
    buckets = np.stack([_t5_bucket_np(np.maximum(d_own, 0)), _t5_bucket_np(d_own + MB_BLOCK)]).astype(np.int32)
    assert _t5_bucket_np(np.array([MB_BLOCK + 1]))[0] == REL_BUCKETS - 1
    assert nb <= GATE_ROWS
    bkt = jnp.asarray(buckets.transpose(0, 2, 1))
    avg_np = np.zeros((LANES, s), np.float32)
    for j in range(nb):
        avg_np[j, j * MB_BLOCK:(j + 1) * MB_BLOCK] = 1.0 / MB_BLOCK
    avg = jnp.asarray(avg_np, BF16)

    def col(off):
        return pl.BlockSpec((1, s, LANES), lambda hi, bi: (bi, 0, off + hi))

    return pl.pallas_call(
        functools.partial(_moba_kernel, n_blocks=nb),
        grid=(n_hp, b),
        in_specs=[
            col(0), col(n_hp), col(2 * n_hp),
            pl.BlockSpec(avg.shape, lambda hi, bi: (0, 0)),
            pl.BlockSpec(bkt.shape, lambda hi, bi: (0, 0, 0)),
            pl.BlockSpec(memory_space=pltpu.SMEM),
        ],
        out_specs=pl.BlockSpec((1, s, LANES), lambda hi, bi: (bi, 0, hi)),
        out_shape=jax.ShapeDtypeStruct((b, s, MB_HEADS * MB_DH), F32),
        scratch_shapes=[pltpu.VMEM((hp, MB_BLOCK, MB_BLOCK), F32), pltpu.VMEM((hp, MB_BLOCK, MB_BLOCK), F32)],
        compiler_params=pltpu.CompilerParams(
            dimension_semantics=("parallel", "arbitrary"), vmem_limit_bytes=VMEM_LIMIT),
        name="moba",
    )(mb3, mb3, mb3, avg, bkt, rel_bias.astype(F32))


def _layer_norm(x, g, b):
    mu = jnp.mean(x, axis=-1, keepdims=True)
    xc = x - mu
    var = jnp.mean(xc * xc, axis=-1, keepdims=True)
    return xc * lax.rsqrt(var + NORM_EPS) * g + b


def _merge_kernel(hg_ref, mb_ref, ga_ref, gb_ref, x_ref, wa_ref, wb_ref, wo_ref, g1_ref, b1_ref,
                  wr_ref, br_ref, tri_ref, x1_ref, route_ref, cnt_ref, run_ref, *, blocks_per_moe_tile, parts):
    tm = x_ref.shape[0]
    pr = tm // parts
    lane = lax.broadcasted_iota(jnp.int32, (pr, LANES), 1)

    @pl.when(pl.program_id(0) % blocks_per_moe_tile == 0)
    def _():
        run_ref[...] = jnp.zeros_like(run_ref)

    def mix(rows):
        ya = _dot(hg_ref[rows, :].astype(BF16), wa_ref[...])
        yb = _dot(mb_ref[rows, :].astype(BF16), wb_ref[...])
        mixed_in = jax.nn.sigmoid(ga_ref[rows, :]) * ya + jax.nn.sigmoid(gb_ref[rows, :]) * yb
        mixed = _dot(mixed_in.astype(BF16), wo_ref[...])
        x1 = _layer_norm(DN_ALPHA * x_ref[rows, :] + mixed, g1_ref[...], b1_ref[...])
        x1_ref[rows, :] = x1
        return x1

    def route_rows(rows, x1, run):
        x_hi, x_lo = _split2(x1)
        w_hi, w_lo = wr_ref[0], wr_ref[1]
        logits = _dot(x_hi, w_hi) + _dot(x_hi, w_lo) + _dot(x_lo, w_hi) + br_ref[...]
        glog = jnp.where(lane < N_GROUPS, logits, NEG_INF)
        gmax = jnp.max(glog, axis=-1, keepdims=True)
        grp = jnp.min(jnp.where(glog == gmax, lane, LANES), axis=-1, keepdims=True)
        p_grp = 1.0 / jnp.sum(jnp.exp(glog - gmax), axis=-1, keepdims=True)
        e_lo = N_GROUPS + grp * EXPERTS_PER_GROUP
        elog = jnp.where(jnp.logical_and(lane >= e_lo, lane < e_lo + EXPERTS_PER_GROUP), logits, NEG_INF)
        m1 = jnp.max(elog, axis=-1, keepdims=True)
        i1 = jnp.min(jnp.where(elog == m1, lane, LANES), axis=-1, keepdims=True)
        elog2 = jnp.where(lane == i1, NEG_INF, elog)
        m2 = jnp.max(elog2, axis=-1, keepdims=True)
        i2 = jnp.min(jnp.where(elog2 == m2, lane, LANES), axis=-1, keepdims=True)
        e2 = jnp.exp(m2 - m1)
        w1 = p_grp / (1.0 + e2)
        w2 = p_grp * e2 / (1.0 + e2)
        oh1 = jnp.where(lane == i1 - N_GROUPS, 1.0, 0.0)
        oh2 = jnp.where(lane == i2 - N_GROUPS, 1.0, 0.0)
        both = oh1 + oh2
        before = _dot(tri_ref[...], both.astype(BF16)) + run
        r1 = jnp.sum(before * oh1, axis=-1, keepdims=True)
        r2 = jnp.sum(before * oh2, axis=-1, keepdims=True)
        cols = ((i1 - N_GROUPS).astype(F32), (i2 - N_GROUPS).astype(F32), w1, w2, r1, r2)
        route = jnp.zeros((pr, LANES), F32)
        for li, col in enumerate(cols):
            route = jnp.where(lane == li, col, route)
        route_ref[:, rows] = route.T[:ROUTE_FIELDS, :]
        return run + jnp.sum(both, axis=0, keepdims=True)

    slabs = [slice(h * pr, (h + 1) * pr) for h in range(parts)]
    mixed = [mix(rows) for rows in slabs]
    run = run_ref[...]
    for rows, x1 in zip(slabs, mixed):
        run = route_rows(rows, x1, run)
    run_ref[...] = run
    cnt_ref[0] = run


def _merge(hg_o, mb_o, gates, x2, wa, wb, wo, g1, b1, wr, br, tm, moe_tile):
    n, d = x2.shape
    wa_n = hg_o.shape[1]
    wb_n = mb_o.shape[1]
    assert moe_tile % tm == 0
    parts = 1
    tri = jnp.asarray(np.tril(np.ones((tm // parts, tm // parts), np.float32), -1), BF16)

    def full(a):
        nd = a.ndim
        return pl.BlockSpec(a.shape, lambda i: (0,) * nd)

    return pl.pallas_call(
        functools.partial(_merge_kernel, blocks_per_moe_tile=moe_tile // tm, parts=parts),
        grid=(n // tm,),
        in_specs=[
            pl.BlockSpec((tm, wa_n), lambda i: (i, 0)),
            pl.BlockSpec((tm, wb_n), lambda i: (i, 0)),
            pl.BlockSpec((tm, d), lambda i: (i, 0)),
            pl.BlockSpec((tm, d), lambda i: (i, 1)),
            pl.BlockSpec((tm, d), lambda i: (i, 0)),
            full(wa), full(wb), full(wo), full(g1), full(b1), full(wr), full(br), full(tri),
        ],
        out_specs=[
            pl.BlockSpec((tm, d), lambda i: (i, 0)),
            pl.BlockSpec((ROUTE_FIELDS, tm), lambda i: (0, i)),
            pl.BlockSpec((1, 1, LANES), lambda i: (i, 0, 0)),
        ],
        out_shape=[
            jax.ShapeDtypeStruct((n, d), F32),
            jax.ShapeDtypeStruct((ROUTE_FIELDS, n), F32),
            jax.ShapeDtypeStruct((n // tm, 1, LANES), F32),
        ],
        scratch_shapes=[pltpu.VMEM((1, LANES), F32)],
        compiler_params=pltpu.CompilerParams(
            dimension_semantics=("arbitrary",), vmem_limit_bytes=VMEM_LIMIT),
        name="merge_ln1_router",
    )(hg_o, mb_o, gates, gates, x2, wa, wb, wo, g1, b1, wr, br, tri)


def _segment_copies(n_small, make_copy, act):
    per_big = SEG_BIG // SEG_PAD
    n_big = n_small // per_big

    def big(c, carry):
        act(make_copy(c * SEG_BIG, SEG_BIG))
        return carry
    lax.fori_loop(0, n_big, big, 0)

    def small(c, carry):
        act(make_copy(n_big * SEG_BIG + c * SEG_PAD, SEG_PAD))
        return carry
    lax.fori_loop(0, n_small - n_big * per_big, small, 0)


def _wait_copies(n_big, n_small, make_copy):
    def big(c, carry):
        make_copy(0, SEG_BIG).wait()
        return carry
    lax.fori_loop(0, n_big, big, 0)

    def small(c, carry):
        make_copy(0, SEG_PAD).wait()
        return carry
    lax.fori_loop(0, n_small, small, 0)


def _dispatch_kernel(lbase_ref, gseg_ref, nseg_ref, tot_ref, gap_ref, p0_ref, p1_ref, x1_ref, xs_hbm,
                     xs_ref, xb_ref, zero_ref, sem, *, cv_rows):
    ti = pl.program_id(0)
    n_tiles = pl.num_programs(0)
    groups, _, d = x1_ref.shape

    def out_copy(t, ex):
        def make(off, size):
            src = pl.multiple_of(lbase_ref[t * N_EXPERTS + ex] + off, SEG_PAD)
            dst = pl.multiple_of(gseg_ref[t * N_EXPERTS + ex] + off, SEG_PAD)
            return pltpu.make_async_copy(xb_ref.at[pl.ds(src, size), :], xs_hbm.at[pl.ds(dst, size), :], sem.at[0])
        return make

    @pl.when(ti == 0)
    def _():
        def zero(c, carry):
            r0 = pl.multiple_of(c * cv_rows, cv_rows)
            xs_ref[pl.ds(r0, cv_rows), :] = jnp.zeros((cv_rows, d), F32)
            return carry
        lax.fori_loop(0, xs_ref.shape[0] // cv_rows, zero, 0)
        zero_ref[...] = jnp.zeros_like(zero_ref)

    def sort_rows(grp, carry):
        t0 = grp * SUBLANES
        for k in range(SUBLANES):
            row = x1_ref[grp, pl.ds(k, 1), :]
            xs_ref[pl.ds(p0_ref[0, 0, t0 + k], 1), :] = row
            xs_ref[pl.ds(p1_ref[0, 0, t0 + k], 1), :] = row
        return carry
    lax.fori_loop(0, groups, sort_rows, 0)

    @pl.when(ti > 0)
    def _():
        _wait_copies(tot_ref[2 * (ti - 1)], tot_ref[2 * (ti - 1) + 1], out_copy(0, 0))

    def convert(c, carry):
        r0 = pl.multiple_of(c * cv_rows, cv_rows)
        xb_ref[pl.ds(r0, cv_rows), :] = xs_ref[pl.ds(r0, cv_rows), :].astype(BF16)
        return carry
    lax.fori_loop(0, xs_ref.shape[0] // cv_rows, convert, 0)

    def send(ex, carry):
        _segment_copies(nseg_ref[ti * N_EXPERTS + ex], out_copy(ti, ex), lambda cp: cp.start())
        return carry
    lax.fori_loop(0, N_EXPERTS, send, 0)

    @pl.when(ti == n_tiles - 1)
    def _():
        _wait_copies(tot_ref[2 * ti], tot_ref[2 * ti + 1], out_copy(0, 0))

        def fill_copy(row):
            return pltpu.make_async_copy(zero_ref, xs_hbm.at[pl.ds(pl.multiple_of(row, SEG_PAD), SEG_PAD), :],
                                         sem.at[1])

        def fill(ex, carry):
            def one(c, carry2):
                fill_copy(gap_ref[2 * ex] + c * SEG_PAD).start()
                return carry2
            lax.fori_loop(0, gap_ref[2 * ex + 1], one, 0)
            return carry
        lax.fori_loop(0, N_EXPERTS, fill, 0)

        def fill_wait(ex, carry):
            def one(c, carry2):
                fill_copy(0).wait()
                return carry2
            lax.fori_loop(0, gap_ref[2 * ex + 1], one, 0)
            return carry
        lax.fori_loop(0, N_EXPERTS, fill_wait, 0)


def _expert_kernel(be_ref, nu_ref, x_ref, wgu_ref, wd_ref, y_ref, wgu_bf, wd_bf, *, cast_rows):
    i = pl.program_id(0)

    @pl.when(i < nu_ref[0])
    def _():
        @pl.when(jnp.logical_or(i == 0, be_ref[i] != be_ref[jnp.maximum(i - 1, 0)]))
        def _():
            for r0 in range(0, wgu_bf.shape[0], cast_rows):
                wgu_bf[r0:r0 + cast_rows, :] = wgu_ref[0, r0:r0 + cast_rows, :].astype(BF16)
            for r0 in range(0, wd_bf.shape[0], cast_rows):
                wd_bf[r0:r0 + cast_rows, :] = wd_ref[0, r0:r0 + cast_rows, :].astype(BF16)

        gu = _dot(x_ref[...], wgu_bf[...])
        gate = gu[:, :EXPERT_HIDDEN]
        up = gu[:, EXPERT_HIDDEN:]
        hdn = (gate * jax.nn.sigmoid(gate) * up).astype(BF16)
        y_ref[...] = _dot(hdn, wd_bf[...])


def _combine_kernel(lbase_ref, gseg_ref, nseg_ref, tot_ref, p0_ref, p1_ref, w0_ref, w1_ref, x1_ref, ys_hbm,
                    g2_ref, b2_ref,
                    o_ref, xs_ref, sem, *, ln_rows):
    ti = pl.program_id(0)
    n_tiles = pl.num_programs(0)
    groups, _, d = x1_ref.shape
    slot = ti % 2

    def in_copy(t, ex, sl):
        def make(off, size):
            src = pl.multiple_of(gseg_ref[t * N_EXPERTS + ex] + off, SEG_PAD)
            dst = pl.multiple_of(lbase_ref[t * N_EXPERTS + ex] + off, SEG_PAD)
            return pltpu.make_async_copy(ys_hbm.at[pl.ds(src, size), :], xs_ref.at[sl, pl.ds(dst, size), :],
                                         sem.at[sl])
        return make

    def fetch(t, sl):
        def one(ex, carry):
            _segment_copies(nseg_ref[t * N_EXPERTS + ex], in_copy(t, ex, sl), lambda cp: cp.start())
            return carry
        lax.fori_loop(0, N_EXPERTS, one, 0)

    @pl.when(ti == 0)
    def _():
        fetch(0, 0)

    @pl.when(ti + 1 < n_tiles)
    def _():
        fetch(ti + 1, 1 - slot)

    _wait_copies(tot_ref[2 * ti], tot_ref[2 * ti + 1], in_copy(0, 0, slot))

    def combine(grp, carry):
        t0 = grp * SUBLANES
        for k in range(SUBLANES):
            t = t0 + k
            o_ref[grp, pl.ds(k, 1), :] = (DN_ALPHA * x1_ref[grp, pl.ds(k, 1), :]
                                          + w0_ref[0, 0, t] * xs_ref[slot, pl.ds(p0_ref[0, 0, t], 1), :]
                                          + w1_ref[0, 0, t] * xs_ref[slot, pl.ds(p1_ref[0, 0, t], 1), :])
        return carry
    lax.fori_loop(0, groups, combine, 0)

    ln_groups = ln_rows // SUBLANES

    def norm(c, carry):
        g0 = pl.multiple_of(c * ln_groups, ln_groups)
        rows = o_ref[pl.ds(g0, ln_groups), :, :].reshape(ln_rows, d)
        o_ref[pl.ds(g0, ln_groups), :, :] = _layer_norm(rows, g2_ref[...], b2_ref[...]).reshape(ln_groups, SUBLANES, d)
        return carry
    lax.fori_loop(0, groups // ln_groups, norm, 0)


def _moe_tile(n):
    return min(MOE_TILE, n)


def _round_up(a, m):
    return (a + m - 1) // m * m


def _moe(x1, route, cnt_run, w_gate_up, w_down, g2, b2):
    n, d = x1.shape
    tile = _moe_tile(n)
    assert n % tile == 0
    n_tiles = n // tile
    i32 = jnp.int32
    cnt = cnt_run.reshape(n_tiles, -1, LANES)[:, -1, :N_EXPERTS].astype(i32)
    seg = _round_up(cnt, SEG_PAD)
    lbase = jnp.cumsum(seg, axis=1) - seg
    used = jnp.sum(seg, axis=0)
    region = _round_up(used, EXP_ROWS)
    e_start = jnp.cumsum(region) - region
    gseg = e_start[None, :] + jnp.cumsum(seg, axis=0) - seg
    nseg = seg // SEG_PAD
    per_big = SEG_BIG // SEG_PAD
    tot = jnp.stack([jnp.sum(nseg // per_big, axis=1), jnp.sum(nseg % per_big, axis=1)], axis=1)
    gap = jnp.stack([e_start + used, (region - used) // SEG_PAD], axis=1)
    rows_max = _round_up(TOP_K * n + n_tiles * N_EXPERTS * (SEG_PAD - 1) + N_EXPERTS * (EXP_ROWS - 1), EXP_ROWS)
    n_blocks = rows_max // EXP_ROWS
    n_used = (jnp.sum(region) // EXP_ROWS).astype(i32).reshape(1)
    blk_row = jnp.arange(n_blocks, dtype=i32)[:, None] * EXP_ROWS
    blk_expert = jnp.minimum(jnp.sum((blk_row >= jnp.cumsum(region)[None, :]).astype(i32), axis=1), N_EXPERTS - 1)
    r_exp = route[0:TOP_K].astype(i32).reshape(TOP_K, n_tiles, tile)
    r_rank = route[2 * TOP_K:3 * TOP_K].astype(i32).reshape(TOP_K, n_tiles, tile)
    seg_start = jnp.sum(jnp.where(r_exp[..., None] == jnp.arange(N_EXPERTS, dtype=i32), lbase[None, :, None, :], 0),
                        axis=-1)
    pos = (seg_start + r_rank).reshape(TOP_K, n_tiles, 1, tile)
    r_wgt = route[TOP_K:2 * TOP_K].reshape(TOP_K, n_tiles, 1, tile)
    local_rows = _round_up(TOP_K * tile + N_EXPERTS * (SEG_PAD - 1), 256)
    flat = lambda a: a.reshape(-1).astype(i32)

    def smem_spec():
        return pl.BlockSpec((1, 1, tile), lambda t, *_: (t, 0, 0), memory_space=pltpu.SMEM)

    x1_groups = x1.reshape(n // SUBLANES, SUBLANES, d)
    row_groups_spec = pl.BlockSpec((tile // SUBLANES, SUBLANES, d), lambda t, *_: (t, 0, 0))

    xs_hbm = pl.pallas_call(
        functools.partial(_dispatch_kernel, cv_rows=256),
        grid_spec=pltpu.PrefetchScalarGridSpec(
            num_scalar_prefetch=5,
            grid=(n_tiles,),
            in_specs=[smem_spec(), smem_spec(), row_groups_spec],
            out_specs=pl.BlockSpec(memory_space=pl.ANY),
            scratch_shapes=[
                pltpu.VMEM((local_rows, d), F32),
                pltpu.VMEM((local_rows, d), BF16),
                pltpu.VMEM((SEG_PAD, d), BF16),
                pltpu.SemaphoreType.DMA((2,)),
            ],
        ),
        out_shape=jax.ShapeDtypeStruct((rows_max, d), BF16),
        compiler_params=pltpu.CompilerParams(dimension_semantics=("arbitrary",), vmem_limit_bytes=VMEM_LIMIT),
        name="moe_dispatch",
    )(flat(lbase), flat(gseg), flat(nseg), flat(tot), flat(gap), pos[0], pos[1], x1_groups)

    def blk(i, be, nu):
        return jnp.minimum(i, nu[0] - 1)

    ys_hbm = pl.pallas_call(
        functools.partial(_expert_kernel, cast_rows=256),
        grid_spec=pltpu.PrefetchScalarGridSpec(
            num_scalar_prefetch=2,
            grid=(n_blocks,),
            in_specs=[
                pl.BlockSpec((EXP_ROWS, d), lambda i, be, nu: (blk(i, be, nu), 0)),
                pl.BlockSpec((1, d, 2 * EXPERT_HIDDEN), lambda i, be, nu: (be[blk(i, be, nu)], 0, 0)),
                pl.BlockSpec((1, EXPERT_HIDDEN, d), lambda i, be, nu: (be[blk(i, be, nu)], 0, 0)),
            ],
            out_specs=pl.BlockSpec((EXP_ROWS, d), lambda i, be, nu: (blk(i, be, nu), 0)),
            scratch_shapes=[pltpu.VMEM((d, 2 * EXPERT_HIDDEN), BF16), pltpu.VMEM((EXPERT_HIDDEN, d), BF16)],
        ),
        out_shape=jax.ShapeDtypeStruct((rows_max, d), F32),
        compiler_params=pltpu.CompilerParams(dimension_semantics=("arbitrary",), vmem_limit_bytes=VMEM_LIMIT),
        name="moe_experts",
    )(blk_expert, n_used, xs_hbm, w_gate_up, w_down)

    return pl.pallas_call(
        functools.partial(_combine_kernel, ln_rows=min(256, tile)),
        grid_spec=pltpu.PrefetchScalarGridSpec(
            num_scalar_prefetch=4,
            grid=(n_tiles,),
            in_specs=[
                smem_spec(), smem_spec(), smem_spec(), smem_spec(),
                row_groups_spec,
                pl.BlockSpec(memory_space=pl.ANY),
                pl.BlockSpec((1, d), lambda t, *_: (0, 0)),
                pl.BlockSpec((1, d), lambda t, *_: (0, 0)),
            ],
            out_specs=row_groups_spec,
            scratch_shapes=[pltpu.VMEM((2, local_rows, d), F32), pltpu.SemaphoreType.DMA((2,))],
        ),
        out_shape=jax.ShapeDtypeStruct(x1_groups.shape, F32),
        compiler_params=pltpu.CompilerParams(dimension_semantics=("arbitrary",), vmem_limit_bytes=VMEM_LIMIT),
        name="moe_combine_ln2",
    )(flat(lbase), flat(gseg), flat(nseg), flat(tot), pos[0], pos[1], r_wgt[0], r_wgt[1], x1_groups, ys_hbm,
      g2, b2).reshape(n, d)


def _block(x, w_in, b_in, lower_bound, hg_norm_g, rel_bias, w_proj_a, w_proj_b, w_out, ln1_g, ln1_b,
           w_group, b_group, w_expert, b_expert, w_gate_up, w_down, ln2_g, ln2_b, *, tm_proj, tm_merge):
    b, s, d = x.shape
    n = b * s
    n_hg = 4 * HG_HEADS * HG_DK
    n_mb = 3 * MB_HEADS * MB_DH
    n_gt = 2 * d
    x2 = x.reshape(n, d)
    hg, mb, gates = _in_proj(x2, w_in.astype(BF16), b_in.reshape(1, -1), n_hg, n_mb, n_gt, tm_proj)
    hg_o = _hgrn(hg.reshape(b, s, n_hg), lower_bound.reshape(1, -1), hg_norm_g.reshape(1, -1))
    mb_o = _moba(mb.reshape(b, s, n_mb), rel_bias)
    w_r = jnp.zeros((d, LANES), F32).at[:, :N_GROUPS].set(w_group).at[:, N_GROUPS:N_GROUPS + N_EXPERTS].set(w_expert)
    b_r = jnp.zeros((1, LANES), F32).at[0, :N_GROUPS].set(b_group).at[0, N_GROUPS:N_GROUPS + N_EXPERTS].set(b_expert)
    tm_merge = min(tm_merge, n)
    x1, route, cnt_run = _merge(hg_o.reshape(n, -1), mb_o.reshape(n, -1), gates, x2,
                                w_proj_a.astype(BF16), w_proj_b.astype(BF16), w_out.astype(BF16),
                                ln1_g.reshape(1, d), ln1_b.reshape(1, d), jnp.stack(_split2(w_r)), b_r,
                                tm_merge, _moe_tile(n))
    out = _moe(x1, route, cnt_run, w_gate_up, w_down, ln2_g.reshape(1, d), ln2_b.reshape(1, d))
    return out.reshape(b, s, d)


def kernel(x, w_in, b_in, lb_logits, hg_norm_g, rel_bias, w_proj_a, w_proj_b, w_out, ln1_g, ln1_b, w_group,
           b_group, w_expert, b_expert, w_gate_up, w_down, ln2_g, ln2_b):
    lower_bounds = jnp.cumsum(jax.nn.softmax(lb_logits.astype(F32), axis=0), axis=0)
    l = 0
    return _block(x, w_in[l], b_in[l], lower_bounds[l], hg_norm_g[l], rel_bias, w_proj_a[l], w_proj_b[l],
                  w_out[l], ln1_g[l], ln1_b[l], w_group[l], b_group[l], w_expert[l], b_expert[l],
                  w_gate_up[l], w_down[l], ln2_g[l], ln2_b[l], tm_proj=512, tm_merge=512)
```

```python
import functools
import math

import numpy as np
import jax
import jax.numpy as jnp
from jax import lax
from jax.experimental import pallas as pl
from jax.experimental.pallas import tpu as pltpu

F32 = jnp.float32
BF16 = jnp.bfloat16

HG_HEADS = 4
HG_DK = 128
HG_CHUNK = 128
HG_GROUP = 8
MB_HEADS = 8
MB_DH = 64
MB_BLOCK = 256
MB_TOPK = 3
PIPE_AHEAD = 2
DEN_ROWS = 16
MERGE_SLAB = 512
ROUTE_FIELDS = 8
GATE_ROWS = 16
REL_BUCKETS = 32
REL_MAX_DIST = 128
N_GROUPS = 4
EXPERTS_PER_GROUP = 8
N_EXPERTS = N_GROUPS * EXPERTS_PER_GROUP
TOP_K = 2
EXPERT_HIDDEN = 512
MOE_TILE = 1024
SEG_PAD = 16
SEG_BIG = 64
EXP_ROWS = 512
DEPTH = 1
DN_ALPHA = (2.0 * DEPTH) ** 0.25
NORM_EPS = 1e-5
LANES = 128
SUBLANES = 8
VMEM_LIMIT = 56 * 1024 * 1024
NEG_INF = float("-inf")
LOG2_E = 1.4426950408889634


def _split2(a):
    hi = a.astype(BF16)
    lo = (a - hi.astype(F32)).astype(BF16)
    return hi, lo


def _split3(a):
    hi = a.astype(BF16)
    r = a - hi.astype(F32)
    mid = r.astype(BF16)
    lo = (r - mid.astype(F32)).astype(BF16)
    return hi, mid, lo


def _dot_nt(a, b):
    return lax.dot_general(a, b, (((1,), (1,)), ((), ())), preferred_element_type=F32)


def _dot_tn(a, b):
    return lax.dot_general(a, b, (((0,), (0,)), ((), ())), preferred_element_type=F32)


def _dot(a, b):
    return jnp.dot(a, b, preferred_element_type=F32)


def _in_proj_kernel(x_ref, w_ref, b_ref, hg_ref, mb_ref, gt_ref, *, col_chunk):
    xb = x_ref[...].astype(BF16)
    outs = ((hg_ref, 0), (mb_ref, hg_ref.shape[1]), (gt_ref, hg_ref.shape[1] + mb_ref.shape[1]))
    for o_ref, base in outs:
        for c0 in range(0, o_ref.shape[1], col_chunk):
            acc = _dot(xb, w_ref[:, base + c0:base + c0 + col_chunk])
            o_ref[:, c0:c0 + col_chunk] = acc + b_ref[:, base + c0:base + c0 + col_chunk]


def _in_proj(x2, w_bf, b_in, n_hg, n_mb, n_gt, tm):
    n, d = x2.shape
    cols = w_bf.shape[1]
    return pl.pallas_call(
        functools.partial(_in_proj_kernel, col_chunk=512),
        grid=(n // tm,),
        in_specs=[
            pl.BlockSpec((tm, d), lambda i: (i, 0)),
            pl.BlockSpec((d, cols), lambda i: (0, 0), pipeline_mode=pl.Buffered(1)),
            pl.BlockSpec((1, cols), lambda i: (0, 0)),
        ],
        out_specs=[
            pl.BlockSpec((tm, n_hg), lambda i: (i, 0)),
            pl.BlockSpec((tm, n_mb), lambda i: (i, 0)),
            pl.BlockSpec((tm, n_gt), lambda i: (i, 0)),
        ],
        out_shape=[
            jax.ShapeDtypeStruct((n, n_hg), F32),
            jax.ShapeDtypeStruct((n, n_mb), F32),
            jax.ShapeDtypeStruct((n, n_gt), F32),
        ],
        compiler_params=pltpu.CompilerParams(
            dimension_semantics=("parallel",), vmem_limit_bytes=VMEM_LIMIT),
        name="in_proj",
    )(x2, w_bf, b_in)


def _hgrn_tables():
    c = HG_CHUNK
    levels = [c >> (i + 1) for i in range(int(math.log2(c)))]
    masks = np.zeros((len(levels) + 1, c, c), np.float32)
    for li, m in enumerate(levels):
        for r in range(c):
            c0 = (r // (2 * m)) * (2 * m)
            if r - c0 >= m:
                masks[li, r, c0:c0 + m] = 1.0
    masks[len(levels)] = np.eye(c, dtype=np.float32)
    return np.tril(np.ones((c, c), np.float32)), masks, tuple(levels)


def _level_ref_rows(p, m):
    c, w = p.shape
    if 2 * m > SUBLANES:
        parts = [jnp.broadcast_to(p[c0 + m - 1:c0 + m, :], (2 * m, w)) for c0 in range(0, c, 2 * m)]
        return parts[0] if len(parts) == 1 else jnp.concatenate(parts, axis=0)
    p3 = p.reshape(c // SUBLANES, SUBLANES, w)
    sub = lax.broadcasted_iota(jnp.int32, p3.shape, 1)
    out = None
    for c0 in range(0, SUBLANES, 2 * m):
        b = jnp.broadcast_to(p3[:, c0 + m - 1:c0 + m, :], p3.shape)
        out = b if out is None else jnp.where(sub >= c0, b, out)
    return out.reshape(c, w)


def _hgrn_kernel(q_ref, f_ref, i_ref, g_ref, lb_ref, ng_ref, tril_ref, msk_ref, o_ref,
                 *, n_chunks, levels, group):
    c = HG_CHUNK
    dk = HG_DK
    assert n_chunks % group == 0
    lb = jnp.concatenate([lb_ref[...]] * group, axis=1)
    oml = 1.0 - lb
    ng = ng_ref[...]
    tril = tril_ref[...]
    n_levels = len(levels)

    def load(ref, r0):
        return jnp.concatenate([ref[0, pl.ds(r0 + u * c, c), :] for u in range(group)], axis=1)

    def lanes(a, u):
        return a[:, u * dk:(u + 1) * dk]

    def intra(r0):
        z = load(f_ref, r0)
        qr = load(q_ref, r0)
        lf = jnp.log(lb + oml * jax.nn.sigmoid(z))
        kk = oml * jax.nn.sigmoid(-z)
        qf = qr * jax.nn.sigmoid(qr)
        l_hi, l_mid, l_lo = _split3(lf)
        p = _dot(tril, l_hi) + _dot(tril, l_mid) + _dot(tril, l_lo)
        b_end = p[c - 1:c, :]
        qb = (qf * jnp.exp(p)).astype(BF16)
        kd = (kk * jnp.exp(b_end - p)).astype(BF16)
        dec = jnp.exp(b_end)
        qh = qf.astype(BF16)
        kh = kk.astype(BF16)
        scores = [msk_ref[n_levels] * _dot_nt(lanes(qh, u), lanes(kh, u)) for u in range(group)]
        row = lax.broadcasted_iota(jnp.int32, p.shape, 0)
        for li, m in enumerate(levels):
            ex = jnp.exp2(jnp.abs(p - _level_ref_rows(p, m)) * (-LOG2_E))
            qk = (jnp.where((row & (2 * m - 1)) >= m, qf, kk) * ex).astype(BF16)
            for u in range(group):
                scores[u] = scores[u] + msk_ref[li] * _dot_nt(lanes(qk, u), lanes(qk, u))
        vbs = [i_ref[0, pl.ds(r0 + u * c, c), :].astype(BF16) for u in range(group)]
        o_intra = [_dot(scores[u].astype(BF16), vbs[u]) for u in range(group)]
        return qb, kd, dec, vbs, o_intra

    def chain(r0, staged, st):
        qb, kd, dec, vbs, o_intra = staged
        for u in range(group):
            rows = pl.ds(r0 + u * c, c)
            g = g_ref[0, rows, :]
            o = _dot_nt(lanes(qb, u), st.astype(BF16)) + o_intra[u]
            st = st * lanes(dec, u) + _dot_tn(vbs[u], lanes(kd, u))
            o = o * lax.rsqrt(jnp.mean(o * o, axis=-1, keepdims=True) + NORM_EPS)
            o_ref[0, rows, :] = o * ng * (g * jax.nn.sigmoid(g))
        return st

    n_groups = n_chunks // group
    st = jnp.zeros((HG_DK, HG_DK), F32)
    staged = intra(0)
    for gi in range(n_groups):
        nxt = intra((gi + 1) * group * c) if gi + 1 < n_groups else None
        st = chain(gi * group * c, staged, st)
        staged = nxt


def _hgrn(hg3, lb_row, ng_row):
    b, s, _ = hg3.shape
    tril, masks, levels = _hgrn_tables()
    mst = jnp.asarray(tril, BF16)
    msk = jnp.asarray(masks, F32)
    h = HG_HEADS

    def col(off):
        return pl.BlockSpec((1, s, HG_DK), lambda bi, hi: (bi, 0, off + hi))

    return pl.pallas_call(
        functools.partial(_hgrn_kernel, n_chunks=s // HG_CHUNK, levels=levels,
                          group=math.gcd(s // HG_CHUNK, HG_GROUP)),
        grid=(b, h),
        in_specs=[
            col(0), col(h), col(2 * h), col(3 * h),
            pl.BlockSpec((1, HG_DK), lambda bi, hi: (0, hi)),
            pl.BlockSpec((1, HG_DK), lambda bi, hi: (0, hi)),
            pl.BlockSpec(mst.shape, lambda bi, hi: (0, 0)),
            pl.BlockSpec(msk.shape, lambda bi, hi: (0, 0, 0)),
        ],
        out_specs=pl.BlockSpec((1, s, HG_DK), lambda bi, hi: (bi, 0, hi)),
        out_shape=jax.ShapeDtypeStruct((b, s, h * HG_DK), F32),
        compiler_params=pltpu.CompilerParams(
            dimension_semantics=("parallel", "parallel"), vmem_limit_bytes=VMEM_LIMIT),
        name="hgrn2",
    )(hg3, hg3, hg3, hg3, lb_row, ng_row, mst, msk)


def _t5_bucket_np(dist):
    max_exact = REL_BUCKETS // 2
    d = np.maximum(dist, 1).astype(np.float32)
    log_part = max_exact + (np.log(d / np.float32(max_exact)) / np.float32(math.log(REL_MAX_DIST / max_exact))
                            * np.float32(REL_BUCKETS - max_exact)).astype(np.int32)
    return np.where(dist < max_exact, dist, np.minimum(log_part, REL_BUCKETS - 1))


def _moba_kernel(q_ref, k_ref, v_ref, avg_ref, bkt_ref, rb_ref, o_ref, own_ref, prev_ref, *, n_blocks):
    blk = MB_BLOCK
    scale = MB_DH ** -0.5 * LOG2_E
    avg = avg_ref[...]
    hp = LANES // MB_DH

    @pl.when(pl.program_id(1) == 0)
    def _():
        causal = (lax.broadcasted_iota(jnp.int32, (blk, blk), 0) <= lax.broadcasted_iota(jnp.int32, (blk, blk), 1))
        for hh in range(hp):
            head = pl.program_id(0) * hp + hh
            own_t = jnp.zeros((blk, blk), F32)
            prev_t = jnp.zeros((blk, blk), F32)
            for bk in range(REL_BUCKETS):
                val = rb_ref[bk, head] * LOG2_E
                own_t = jnp.where(bkt_ref[0] == bk, val, own_t)
                prev_t = jnp.where(bkt_ref[1] == bk, val, prev_t)
            own_ref[hh] = jnp.where(causal, own_t, NEG_INF)
            prev_ref[hh] = prev_t

    grow = lax.broadcasted_iota(jnp.int32, (GATE_ROWS, blk), 0)
    vt_all = v_ref[0].T

    def fold(a):
        return a.reshape(blk // SUBLANES, SUBLANES, blk)

    heads = []
    for hh in range(hp):
        ls = slice(hh * MB_DH, (hh + 1) * MB_DH)
        k_h = k_ref[0, :, ls]
        k_hi, k_lo = _split2(k_h)
        k_mean = (_dot(avg, k_hi) + _dot(avg, k_lo))[:GATE_ROWS]
        vt = jnp.concatenate([vt_all[hh * MB_DH:(hh + 1) * MB_DH, :], jnp.ones((DEN_ROWS, vt_all.shape[1]), F32)],
                             axis=0).astype(BF16)
        far_bias = rb_ref[REL_BUCKETS - 1, pl.program_id(0) * hp + hh] * LOG2_E
        heads.append((ls, _split2(k_mean), k_h.astype(BF16), vt, far_bias))

    def logits(hh, i):
        ls, (km_hi, km_lo), kb, _, far_bias = heads[hh]
        qi = q_ref[0, i * blk:(i + 1) * blk, ls] * scale
        st = _dot_nt(kb[:(i + 1) * blk], qi.astype(BF16))
        sel = None
        if i > 0:
            q_hi, q_lo = _split2(qi)
            gate = _dot_nt(km_hi, q_hi) + _dot_nt(km_lo, q_hi) + _dot_nt(km_hi, q_lo)
            gate = jnp.where(grow < i, gate, NEG_INF)
            rank = jnp.zeros((GATE_ROWS, blk), F32)
            if i > MB_TOPK:
                for j2 in range(i):
                    gj = jnp.broadcast_to(gate[j2:j2 + 1, :], (GATE_ROWS, blk))
                    tie = jnp.where(grow > j2, 1.0, 0.0)
                    rank = rank + jnp.where(gj > gate, 1.0, jnp.where(gj == gate, tie, 0.0))
            sel = jnp.where(rank < MB_TOPK, gate, NEG_INF) > NEG_INF
        pieces = []
        for j in range(i + 1):
            sj = st[j * blk:(j + 1) * blk, :]
            if j == i:
                pieces.append(sj + own_ref[hh])
            elif j == i - 1:
                pieces.append(sj + prev_ref[hh] + jnp.where(sel[j:j + 1, :], 0.0, NEG_INF))
            else:
                pieces.append(sj + jnp.where(sel[j:j + 1, :], far_bias, NEG_INF))
        mx8 = fold(pieces[0]).max(axis=0)
        for p in pieces[1:]:
            mx8 = jnp.maximum(mx8, fold(p).max(axis=0))
        return pieces, mx8.max(axis=0, keepdims=True)

    def attend(hh, pieces, mx):
        vt = heads[hh][3]
        weights = jnp.concatenate([jnp.exp2(p - mx).astype(BF16) for p in pieces], axis=0)
        acc = _dot(vt[:, :len(pieces) * blk], weights)
        return acc[:MB_DH] / acc[MB_DH:MB_DH + 1]

    items = [(i, hh) for i in range(n_blocks) for hh in range(hp)]
    staged = [logits(hh, i) for i, hh in items[:PIPE_AHEAD]]
    done = {}
    for n, (i, hh) in enumerate(items):
        if n + PIPE_AHEAD < len(items):
            staged.append(logits(items[n + PIPE_AHEAD][1], items[n + PIPE_AHEAD][0]))
        done[hh] = attend(hh, *staged.pop(0))
        if hh == hp - 1:
            o_ref[0, i * blk:(i + 1) * blk, :] = jnp.concatenate([done[h] for h in range(hp)], axis=0).T


def _moba(mb3, rel_bias):
    b, s, _ = mb3.shape
    assert s % MB_BLOCK == 0
    nb = s // MB_BLOCK
    hp = LANES // MB_DH
    n_hp = MB_HEADS // hp
    t = np.arange(MB_BLOCK)
    d_own = t[:, None] - t[None, :]
    buckets = np.stack([_t5_bucket_np(np.maximum(d_own, 0)), _t5_bucket_np(d_own + MB_BLOCK)]).astype(np.int32)
    assert _t5_bucket_np(np.array([MB_BLOCK + 1]))[0] == REL_BUCKETS - 1
    assert nb <= GATE_ROWS
    bkt = jnp.asarray(buckets.transpose(0, 2, 1))
    avg_np = np.zeros((LANES, s), np.float32)
    for j in range(nb):
        avg_np[j, j * MB_BLOCK:(j + 1) * MB_BLOCK] = 1.0 / MB_BLOCK
    avg = jnp.asarray(avg_np, BF16)

    def col(off):
        return pl.BlockSpec((1, s, LANES), lambda hi, bi: (bi, 0, off + hi))

    return pl.pallas_call(
        functools.partial(_moba_kernel, n_blocks=nb),
        grid=(n_hp, b),
        in_specs=[
            col(0), col(n_hp), col(2 * n_hp),
            pl.BlockSpec(avg.shape, lambda hi, bi: (0, 0)),
            pl.BlockSpec(bkt.shape, lambda hi, bi: (0, 0, 0)),
            pl.BlockSpec(memory_space=pltpu.SMEM),
        ],
        out_specs=pl.BlockSpec((1, s, LANES), lambda hi, bi: (bi, 0, hi)),
        out_shape=jax.ShapeDtypeStruct((b, s, MB_HEADS * MB_DH), F32),
        scratch_shapes=[pltpu.VMEM((hp, MB_BLOCK, MB_BLOCK), F32), pltpu.VMEM((hp, MB_BLOCK, MB_BLOCK), F32)],
        compiler_params=pltpu.CompilerParams(
            dimension_semantics=("parallel", "arbitrary"), vmem_limit_bytes=VMEM_LIMIT),
        name="moba",
    )(mb3, mb3, mb3, avg, bkt, rel_bias.astype(F32))


def _layer_norm(x, g, b):
    mu = jnp.mean(x, axis=-1, keepdims=True)
    xc = x - mu
    var = jnp.mean(xc * xc, axis=-1, keepdims=True)
    return xc * lax.rsqrt(var + NORM_EPS) * g + b


def _merge_kernel(hg_ref, mb_ref, ga_ref, gb_ref, x_ref, wa_ref, wb_ref, wo_ref, g1_ref, b1_ref,
                  wr_ref, br_ref, tri_ref, x1_ref, route_ref, cnt_ref, run_ref, *, blocks_per_moe_tile, parts):
    tm = x_ref.shape[0]
    pr = tm // parts
    lane = lax.broadcasted_iota(jnp.int32, (pr, LANES), 1)

    @pl.when(pl.program_id(0) % blocks_per_moe_tile == 0)
    def _():
        run_ref[...] = jnp.zeros_like(run_ref)

    def mix(rows):
        ya = _dot(hg_ref[rows, :].astype(BF16), wa_ref[...])
        yb = _dot(mb_ref[rows, :].astype(BF16), wb_ref[...])
        mixed_in = jax.nn.sigmoid(ga_ref[rows, :]) * ya + jax.nn.sigmoid(gb_ref[rows, :]) * yb
        mixed = _dot(mixed_in.astype(BF16), wo_ref[...])
        x1 = _layer_norm(DN_ALPHA * x_ref[rows, :] + mixed, g1_ref[...], b1_ref[...])
        x1_ref[rows, :] = x1
        return x1

    def route_rows(rows, x1, run):
        x_hi, x_lo = _split2(x1)
        w_hi, w_lo = wr_ref[0], wr_ref[1]
        hr = pr // 2
        logits = jnp.concatenate(
            [_dot(x_hi[r:r + hr], w_hi) + _dot(x_hi[r:r + hr], w_lo) + _dot(x_lo[r:r + hr], w_hi) for r in (0, hr)],
            axis=0) + br_ref[...]
        glog = jnp.where(lane < N_GROUPS, logits, NEG_INF)
        gmax = jnp.max(glog, axis=-1, keepdims=True)
        grp = jnp.min(jnp.where(glog == gmax, lane, LANES), axis=-1, keepdims=True)
        p_grp = 1.0 / jnp.sum(jnp.exp(glog - gmax), axis=-1, keepdims=True)
        e_lo = N_GROUPS + grp * EXPERTS_PER_GROUP
        elog = jnp.where(jnp.logical_and(lane >= e_lo, lane < e_lo + EXPERTS_PER_GROUP), logits, NEG_INF)
        m1 = jnp.max(elog, axis=-1, keepdims=True)
        i1 = jnp.min(jnp.where(elog == m1, lane, LANES), axis=-1, keepdims=True)
        elog2 = jnp.where(lane == i1, NEG_INF, elog)
        m2 = jnp.max(elog2, axis=-1, keepdims=True)
        i2 = jnp.min(jnp.where(elog2 == m2, lane, LANES), axis=-1, keepdims=True)
        e2 = jnp.exp(m2 - m1)
        w1 = p_grp / (1.0 + e2)
        w2 = p_grp * e2 / (1.0 + e2)
        oh1 = jnp.where(lane == i1 - N_GROUPS, 1.0, 0.0)
        oh2 = jnp.where(lane == i2 - N_GROUPS, 1.0, 0.0)
        both = oh1 + oh2
        before = _dot(tri_ref[...], both.astype(BF16)) + run
        r1 = jnp.sum(before * oh1, axis=-1, keepdims=True)
        r2 = jnp.sum(before * oh2, axis=-1, keepdims=True)
        cols = ((i1 - N_GROUPS).astype(F32), (i2 - N_GROUPS).astype(F32), w1, w2, r1, r2)
        route = jnp.zeros((pr, LANES), F32)
        for li, col in enumerate(cols):
            route = jnp.where(lane == li, col, route)
        route_ref[:, rows] = route.T[:ROUTE_FIELDS, :]
        return run + jnp.sum(both, axis=0, keepdims=True)

    slabs = [slice(h * pr, (h + 1) * pr) for h in range(parts)]
    mixed = [mix(rows) for rows in slabs]
    run = run_ref[...]
    for rows, x1 in zip(slabs, mixed):
        run = route_rows(rows, x1, run)
    run_ref[...] = run
    cnt_ref[0] = run


def _merge(hg_o, mb_o, gates, x2, wa, wb, wo, g1, b1, wr, br, tm, moe_tile):
    n, d = x2.shape
    wa_n = hg_o.shape[1]
    wb_n = mb_o.shape[1]
    assert moe_tile % tm == 0
    parts = max(1, tm // MERGE_SLAB)
    tri = jnp.asarray(np.tril(np.ones((tm // parts, tm // parts), np.float32), -1), BF16)

    def full(a):
        nd = a.ndim
        return pl.BlockSpec(a.shape, lambda i: (0,) * nd, pipeline_mode=pl.Buffered(1))

    return pl.pallas_call(
        functools.partial(_merge_kernel, blocks_per_moe_tile=moe_tile // tm, parts=parts),
        grid=(n // tm,),
        in_specs=[
            pl.BlockSpec((tm, wa_n), lambda i: (i, 0)),
            pl.BlockSpec((tm, wb_n), lambda i: (i, 0)),
            pl.BlockSpec((tm, d), lambda i: (i, 0)),
            pl.BlockSpec((tm, d), lambda i: (i, 1)),
            pl.BlockSpec((tm, d), lambda i: (i, 0)),
            full(wa), full(wb), full(wo), full(g1), full(b1), full(wr), full(br), full(tri),
        ],
        out_specs=[
            pl.BlockSpec((tm, d), lambda i: (i, 0)),
            pl.BlockSpec((ROUTE_FIELDS, tm), lambda i: (0, i)),
            pl.BlockSpec((1, 1, LANES), lambda i: (i, 0, 0)),
        ],
        out_shape=[
            jax.ShapeDtypeStruct((n, d), F32),
            jax.ShapeDtypeStruct((ROUTE_FIELDS, n), F32),
            jax.ShapeDtypeStruct((n // tm, 1, LANES), F32),
        ],
        scratch_shapes=[pltpu.VMEM((1, LANES), F32)],
        compiler_params=pltpu.CompilerParams(
            dimension_semantics=("arbitrary",), vmem_limit_bytes=VMEM_LIMIT),
        name="merge_ln1_router",
    )(hg_o, mb_o, gates, gates, x2, wa, wb, wo, g1, b1, wr, br, tri)


def _segment_copies(n_small, make_copy, act):
    per_big = SEG_BIG // SEG_PAD
    n_big = n_small // per_big

    def big(c, carry):
        act(make_copy(c * SEG_BIG, SEG_BIG))
        return carry
    lax.fori_loop(0, n_big, big, 0)

    def small(c, carry):
        act(make_copy(n_big * SEG_BIG + c * SEG_PAD, SEG_PAD))
        return carry
    lax.fori_loop(0, n_small - n_big * per_big, small, 0)


def _wait_copies(n_big, n_small, make_copy):
    def big(c, carry):
        make_copy(0, SEG_BIG).wait()
        return carry
    lax.fori_loop(0, n_big, big, 0)

    def small(c, carry):
        make_copy(0, SEG_PAD).wait()
        return carry
    lax.fori_loop(0, n_small, small, 0)


def _dispatch_kernel(lbase_ref, gseg_ref, nseg_ref, tot_ref, gap_ref, p0_ref, p1_ref, x1_ref, xs_hbm,
                     xs_ref, xb_ref, zero_ref, sem, *, cv_rows):
    ti = pl.program_id(0)
    n_tiles = pl.num_programs(0)
    groups, _, d = x1_ref.shape

    def out_copy(t, ex):
        def make(off, size):
            src = pl.multiple_of(lbase_ref[t * N_EXPERTS + ex] + off, SEG_PAD)
            dst = pl.multiple_of(gseg_ref[t * N_EXPERTS + ex] + off, SEG_PAD)
            return pltpu.make_async_copy(xb_ref.at[pl.ds(src, size), :], xs_hbm.at[pl.ds(dst, size), :], sem.at[0])
        return make

    @pl.when(ti == 0)
    def _():
        def zero(c, carry):
            r0 = pl.multiple_of(c * cv_rows, cv_rows)
            xs_ref[pl.ds(r0, cv_rows), :] = jnp.zeros((cv_rows, d), F32)
            return carry
        lax.fori_loop(0, xs_ref.shape[0] // cv_rows, zero, 0)
        zero_ref[...] = jnp.zeros_like(zero_ref)

    def sort_rows(grp, carry):
        t0 = grp * SUBLANES
        for k in range(SUBLANES):
            row = x1_ref[grp, pl.ds(k, 1), :]
            xs_ref[pl.ds(p0_ref[0, 0, t0 + k], 1), :] = row
            xs_ref[pl.ds(p1_ref[0, 0, t0 + k], 1), :] = row
        return carry
    lax.fori_loop(0, groups, sort_rows, 0)

    @pl.when(ti > 0)
    def _():
        _wait_copies(tot_ref[2 * (ti - 1)], tot_ref[2 * (ti - 1) + 1], out_copy(0, 0))

    def convert(c, carry):
        r0 = pl.multiple_of(c * cv_rows, cv_rows)
        xb_ref[pl.ds(r0, cv_rows), :] = xs_ref[pl.ds(r0, cv_rows), :].astype(BF16)
        return carry
    lax.fori_loop(0, xs_ref.shape[0] // cv_rows, convert, 0)

    def send(ex, carry):
        _segment_copies(nseg_ref[ti * N_EXPERTS + ex], out_copy(ti, ex), lambda cp: cp.start())
        return carry
    lax.fori_loop(0, N_EXPERTS, send, 0)

    @pl.when(ti == n_tiles - 1)
    def _():
        _wait_copies(tot_ref[2 * ti], tot_ref[2 * ti + 1], out_copy(0, 0))

        def fill_copy(row):
            return pltpu.make_async_copy(zero_ref, xs_hbm.at[pl.ds(pl.multiple_of(row, SEG_PAD), SEG_PAD), :],
                                         sem.at[1])

        def fill(ex, carry):
            def one(c, carry2):
                fill_copy(gap_ref[2 * ex] + c * SEG_PAD).start()
                return carry2
            lax.fori_loop(0, gap_ref[2 * ex + 1], one, 0)
            return carry
        lax.fori_loop(0, N_EXPERTS, fill, 0)

        def fill_wait(ex, carry):
            def one(c, carry2):
                fill_copy(0).wait()
                return carry2
            lax.fori_loop(0, gap_ref[2 * ex + 1], one, 0)
            return carry
        lax.fori_loop(0, N_EXPERTS, fill_wait, 0)


def _expert_kernel(be_ref, nu_ref, x_ref, wgu_ref, wd_ref, y_ref, wgu_bf, wd_bf, *, cast_rows):
    i = pl.program_id(0)

    @pl.when(i < nu_ref[0])
    def _():
        @pl.when(jnp.logical_or(i == 0, be_ref[i] != be_ref[jnp.maximum(i - 1, 0)]))
        def _():
            for r0 in range(0, wgu_bf.shape[0], cast_rows):
                wgu_bf[r0:r0 + cast_rows, :] = wgu_ref[0, r0:r0 + cast_rows, :].astype(BF16)
            for r0 in range(0, wd_bf.shape[0], cast_rows):
                wd_bf[r0:r0 + cast_rows, :] = wd_ref[0, r0:r0 + cast_rows, :].astype(BF16)

        gu = _dot(x_ref[...], wgu_bf[...])
        gate = gu[:, :EXPERT_HIDDEN]
        up = gu[:, EXPERT_HIDDEN:]
        hdn = (gate * jax.nn.sigmoid(gate) * up).astype(BF16)
        y_ref[...] = _dot(hdn, wd_bf[...])


def _combine_kernel(lbase_ref, gseg_ref, nseg_ref, tot_ref, p0_ref, p1_ref, w0_ref, w1_ref, x1_ref, ys_hbm,
                    g2_ref, b2_ref,
                    o_ref, xs_ref, sem, *, ln_rows):
    ti = pl.program_id(0)
    n_tiles = pl.num_programs(0)
    groups, _, d = x1_ref.shape
    slot = ti % 2

    def in_copy(t, ex, sl):
        def make(off, size):
            src = pl.multiple_of(gseg_ref[t * N_EXPERTS + ex] + off, SEG_PAD)
            dst = pl.multiple_of(lbase_ref[t * N_EXPERTS + ex] + off, SEG_PAD)
            return pltpu.make_async_copy(ys_hbm.at[pl.ds(src, size), :], xs_ref.at[sl, pl.ds(dst, size), :],
                                         sem.at[sl])
        return make

    def fetch(t, sl):
        def one(ex, carry):
            _segment_copies(nseg_ref[t * N_EXPERTS + ex], in_copy(t, ex, sl), lambda cp: cp.start())
            return carry
        lax.fori_loop(0, N_EXPERTS, one, 0)

    @pl.when(ti == 0)
    def _():
        fetch(0, 0)

    @pl.when(ti + 1 < n_tiles)
    def _():
        fetch(ti + 1, 1 - slot)

    _wait_copies(tot_ref[2 * ti], tot_ref[2 * ti + 1], in_copy(0, 0, slot))

    def combine(grp, carry):
        t0 = grp * SUBLANES
        for k in range(SUBLANES):
            t = t0 + k
            o_ref[grp, pl.ds(k, 1), :] = (DN_ALPHA * x1_ref[grp, pl.ds(k, 1), :]
                                          + w0_ref[0, 0, t] * xs_ref[slot, pl.ds(p0_ref[0, 0, t], 1), :]
                                          + w1_ref[0, 0, t] * xs_ref[slot, pl.ds(p1_ref[0, 0, t], 1), :])
        return carry
    lax.fori_loop(0, groups, combine, 0)

    ln_groups = ln_rows // SUBLANES

    def norm(c, carry):
        g0 = pl.multiple_of(c * ln_groups, ln_groups)
        rows = o_ref[pl.ds(g0, ln_groups), :, :].reshape(ln_rows, d)
        o_ref[pl.ds(g0, ln_groups), :, :] = _layer_norm(rows, g2_ref[...], b2_ref[...]).reshape(ln_groups, SUBLANES, d)
        return carry
    lax.fori_loop(0, groups // ln_groups, norm, 0)


def _moe_tile(n):
    return min(MOE_TILE, n)


def _round_up(a, m):
    return (a + m - 1) // m * m


def _moe(x1, route, cnt_run, w_gate_up, w_down, g2, b2):
    n, d = x1.shape
    tile = _moe_tile(n)
    assert n % tile == 0
    n_tiles = n // tile
    i32 = jnp.int32
    cnt = cnt_run.reshape(n_tiles, -1, LANES)[:, -1, :N_EXPERTS].astype(i32)
    seg = _round_up(cnt, SEG_PAD)
    lbase = jnp.cumsum(seg, axis=1) - seg
    used = jnp.sum(seg, axis=0)
    region = _round_up(used, EXP_ROWS)
    e_start = jnp.cumsum(region) - region
    gseg = e_start[None, :] + jnp.cumsum(seg, axis=0) - seg
    nseg = seg // SEG_PAD
    per_big = SEG_BIG // SEG_PAD
    tot = jnp.stack([jnp.sum(nseg // per_big, axis=1), jnp.sum(nseg % per_big, axis=1)], axis=1)
    gap = jnp.stack([e_start + used, (region - used) // SEG_PAD], axis=1)
    rows_max = _round_up(TOP_K * n + n_tiles * N_EXPERTS * (SEG_PAD - 1) + N_EXPERTS * (EXP_ROWS - 1), EXP_ROWS)
    n_blocks = rows_max // EXP_ROWS
    n_used = (jnp.sum(region) // EXP_ROWS).astype(i32).reshape(1)
    blk_row = jnp.arange(n_blocks, dtype=i32)[:, None] * EXP_ROWS
    blk_expert = jnp.minimum(jnp.sum((blk_row >= jnp.cumsum(region)[None, :]).astype(i32), axis=1), N_EXPERTS - 1)
    r_exp = route[0:TOP_K].astype(i32).reshape(TOP_K, n_tiles, tile)
    r_rank = route[2 * TOP_K:3 * TOP_K].astype(i32).reshape(TOP_K, n_tiles, tile)
    seg_start = jnp.sum(jnp.where(r_exp[..., None] == jnp.arange(N_EXPERTS, dtype=i32), lbase[None, :, None, :], 0),
                        axis=-1)
    pos = (seg_start + r_rank).reshape(TOP_K, n_tiles, 1, tile)
    r_wgt = route[TOP_K:2 * TOP_K].reshape(TOP_K, n_tiles, 1, tile)
    local_rows = _round_up(TOP_K * tile + N_EXPERTS * (SEG_PAD - 1), 256)
    flat = lambda a: a.reshape(-1).astype(i32)

    def smem_spec():
        return pl.BlockSpec((1, 1, tile), lambda t, *_: (t, 0, 0), memory_space=pltpu.SMEM)

    x1_groups = x1.reshape(n // SUBLANES, SUBLANES, d)
    row_groups_spec = pl.BlockSpec((tile // SUBLANES, SUBLANES, d), lambda t, *_: (t, 0, 0))

    xs_hbm = pl.pallas_call(
        functools.partial(_dispatch_kernel, cv_rows=256),
        grid_spec=pltpu.PrefetchScalarGridSpec(
            num_scalar_prefetch=5,
            grid=(n_tiles,),
            in_specs=[smem_spec(), smem_spec(), row_groups_spec],
            out_specs=pl.BlockSpec(memory_space=pl.ANY),
            scratch_shapes=[
                pltpu.VMEM((local_rows, d), F32),
                pltpu.VMEM((local_rows, d), BF16),
                pltpu.VMEM((SEG_PAD, d), BF16),
                pltpu.SemaphoreType.DMA((2,)),
            ],
        ),
        out_shape=jax.ShapeDtypeStruct((rows_max, d), BF16),
        compiler_params=pltpu.CompilerParams(dimension_semantics=("arbitrary",), vmem_limit_bytes=VMEM_LIMIT),
        name="moe_dispatch",
    )(flat(lbase), flat(gseg), flat(nseg), flat(tot), flat(gap), pos[0], pos[1], x1_groups)

    def blk(i, be, nu):
        return jnp.minimum(i, nu[0] - 1)

    ys_hbm = pl.pallas_call(
        functools.partial(_expert_kernel, cast_rows=256),
        grid_spec=pltpu.PrefetchScalarGridSpec(
            num_scalar_prefetch=2,
            grid=(n_blocks,),
            in_specs=[
                pl.BlockSpec((EXP_ROWS, d), lambda i, be, nu: (blk(i, be, nu), 0)),
                pl.BlockSpec((1, d, 2 * EXPERT_HIDDEN), lambda i, be, nu: (be[blk(i, be, nu)], 0, 0)),
                pl.BlockSpec((1, EXPERT_HIDDEN, d), lambda i, be, nu: (be[blk(i, be, nu)], 0, 0)),
            ],
            out_specs=pl.BlockSpec((EXP_ROWS, d), lambda i, be, nu: (blk(i, be, nu), 0)),
            scratch_shapes=[pltpu.VMEM((d, 2 * EXPERT_HIDDEN), BF16), pltpu.VMEM((EXPERT_HIDDEN, d), BF16)],
        ),
        out_shape=jax.ShapeDtypeStruct((rows_max, d), F32),
        compiler_params=pltpu.CompilerParams(dimension_semantics=("arbitrary",), vmem_limit_bytes=VMEM_LIMIT),
        name="moe_experts",
    )(blk_expert, n_used, xs_hbm, w_gate_up, w_down)

    return pl.pallas_call(
        functools.partial(_combine_kernel, ln_rows=min(256, tile)),
        grid_spec=pltpu.PrefetchScalarGridSpec(
            num_scalar_prefetch=4,
            grid=(n_tiles,),
            in_specs=[
                smem_spec(), smem_spec(), smem_spec(), smem_spec(),
                row_groups_spec,
                pl.BlockSpec(memory_space=pl.ANY),
                pl.BlockSpec((1, d), lambda t, *_: (0, 0)),
                pl.BlockSpec((1, d), lambda t, *_: (0, 0)),
            ],
            out_specs=row_groups_spec,
            scratch_shapes=[pltpu.VMEM((2, local_rows, d), F32), pltpu.SemaphoreType.DMA((2,))],
        ),
        out_shape=jax.ShapeDtypeStruct(x1_groups.shape, F32),
        compiler_params=pltpu.CompilerParams(dimension_semantics=("arbitrary",), vmem_limit_bytes=VMEM_LIMIT),
        name="moe_combine_ln2",
    )(flat(lbase), flat(gseg), flat(nseg), flat(tot), pos[0], pos[1], r_wgt[0], r_wgt[1], x1_groups, ys_hbm,
      g2, b2).reshape(n, d)


def _block(x, w_in, b_in, lower_bound, hg_norm_g, rel_bias, w_proj_a, w_proj_b, w_out, ln1_g, ln1_b,
           w_group, b_group, w_expert, b_expert, w_gate_up, w_down, ln2_g, ln2_b, *, tm_proj, tm_merge):
    b, s, d = x.shape
    n = b * s
    n_hg = 4 * HG_HEADS * HG_DK
    n_mb = 3 * MB_HEADS * MB_DH
    n_gt = 2 * d
    x2 = x.reshape(n, d)
    hg, mb, gates = _in_proj(x2, w_in.astype(BF16), b_in.reshape(1, -1), n_hg, n_mb, n_gt, tm_proj)
    hg_o = _hgrn(hg.reshape(b, s, n_hg), lower_bound.reshape(1, -1), hg_norm_g.reshape(1, -1))
    mb_o = _moba(mb.reshape(b, s, n_mb), rel_bias)
    w_r = jnp.zeros((d, LANES), F32).at[:, :N_GROUPS].set(w_group).at[:, N_GROUPS:N_GROUPS + N_EXPERTS].set(w_expert)
    b_r = jnp.zeros((1, LANES), F32).at[0, :N_GROUPS].set(b_group).at[0, N_GROUPS:N_GROUPS + N_EXPERTS].set(b_expert)
    tm_merge = min(tm_merge, n)
    x1, route, cnt_run = _merge(hg_o.reshape(n, -1), mb_o.reshape(n, -1), gates, x2,
                                w_proj_a.astype(BF16), w_proj_b.astype(BF16), w_out.astype(BF16),
                                ln1_g.reshape(1, d), ln1_b.reshape(1, d), jnp.stack(_split2(w_r)), b_r,
                                tm_merge, _moe_tile(n))
    out = _moe(x1, route, cnt_run, w_gate_up, w_down, ln2_g.reshape(1, d), ln2_b.reshape(1, d))
    return out.reshape(b, s, d)


def kernel(x, w_in, b_in, lb_logits, hg_norm_g, rel_bias, w_proj_a, w_proj_b, w_out, ln1_g, ln1_b, w_group,
           b_group, w_expert, b_expert, w_gate_up, w_down, ln2_g, ln2_b):
    lower_bounds = jnp.cumsum(jax.nn.softmax(lb_logits.astype(F32), axis=0), axis=0)
    l = 0
    return _block(x, w_in[l], b_in[l], lower_bounds[l], hg_norm_g[l], rel_bias, w_proj_a[l], w_proj_b[l],
                  w_out[l], ln1_g[l], ln1_b[l], w_group[l], b_group[l], w_expert[l], b_expert[l],
                  w_gate_up[l], w_down[l], ln2_g[l], ln2_b[l], tm_proj=512, tm_merge=1024)
```

```python
import functools
import math

import numpy as np
import jax
import jax.numpy as jnp
from jax import lax
from jax.experimental import pallas as pl
from jax.experimental.pallas import tpu as pltpu

F32 = jnp.float32
BF16 = jnp.bfloat16

HG_HEADS = 4
HG_DK = 128
HG_CHUNK = 128
HG_GROUP = 8
MB_HEADS = 8
MB_DH = 64
MB_BLOCK = 256
MB_TOPK = 3
PIPE_AHEAD = 2
DEN_ROWS = 16
GATE_ROWS = 16
MERGE_SLAB = 512
ROUTE_FIELDS = 8
REL_BUCKETS = 32
REL_MAX_DIST = 128
N_GROUPS = 4
EXPERTS_PER_GROUP = 8
N_EXPERTS = N_GROUPS * EXPERTS_PER_GROUP
TOP_K = 2
EXPERT_HIDDEN = 512
MOE_TILE = 1024
SEG_PAD = 16
SEG_BIG = 64
EXP_ROWS = 512
DEPTH = 1
DN_ALPHA = (2.0 * DEPTH) ** 0.25
NORM_EPS = 1e-5
LANES = 128
SUBLANES = 8
VMEM_LIMIT = 56 * 1024 * 1024
NEG_INF = float("-inf")
LOG2_E = 1.4426950408889634


def _split2(a):
    hi = a.astype(BF16)
    lo = (a - hi.astype(F32)).astype(BF16)
    return hi, lo


def _split3(a):
    hi = a.astype(BF16)
    r = a - hi.astype(F32)
    mid = r.astype(BF16)
    lo = (r - mid.astype(F32)).astype(BF16)
    return hi, mid, lo


def _dot_nt(a, b):
    return lax.dot_general(a, b, (((1,), (1,)), ((), ())), preferred_element_type=F32)


def _dot_tn(a, b):
    return lax.dot_general(a, b, (((0,), (0,)), ((), ())), preferred_element_type=F32)


def _dot(a, b):
    return jnp.dot(a, b, preferred_element_type=F32)


def _in_proj_kernel(x_ref, w_ref, b_ref, hg_ref, mb_ref, gt_ref, *, col_chunk):
    xb = x_ref[...].astype(BF16)
    outs = ((hg_ref, 0), (mb_ref, hg_ref.shape[1]), (gt_ref, hg_ref.shape[1] + mb_ref.shape[1]))
    for o_ref, base in outs:
        for c0 in range(0, o_ref.shape[1], col_chunk):
            acc = _dot(xb, w_ref[:, base + c0:base + c0 + col_chunk])
            o_ref[:, c0:c0 + col_chunk] = acc + b_ref[:, base + c0:base + c0 + col_chunk]


def _in_proj(x2, w_bf, b_in, n_hg, n_mb, n_gt, tm):
    n, d = x2.shape
    cols = w_bf.shape[1]
    return pl.pallas_call(
        functools.partial(_in_proj_kernel, col_chunk=512),
        grid=(n // tm,),
        in_specs=[
            pl.BlockSpec((tm, d), lambda i: (i, 0)),
            pl.BlockSpec((d, cols), lambda i: (0, 0), pipeline_mode=pl.Buffered(1)),
            pl.BlockSpec((1, cols), lambda i: (0, 0)),
        ],
        out_specs=[
            pl.BlockSpec((tm, n_hg), lambda i: (i, 0)),
            pl.BlockSpec((tm, n_mb), lambda i: (i, 0)),
            pl.BlockSpec((tm, n_gt), lambda i: (i, 0)),
        ],
        out_shape=[
            jax.ShapeDtypeStruct((n, n_hg), F32),
            jax.ShapeDtypeStruct((n, n_mb), F32),
            jax.ShapeDtypeStruct((n, n_gt), F32),
        ],
        compiler_params=pltpu.CompilerParams(
            dimension_semantics=("parallel",), vmem_limit_bytes=VMEM_LIMIT),
        name="in_proj",
    )(x2, w_bf, b_in)


def _hgrn_tables():
    c = HG_CHUNK
    levels = [c >> (i + 1) for i in range(int(math.log2(c)))]
    masks = np.zeros((len(levels) + 1, c, c), np.float32)
    for li, m in enumerate(levels):
        for r in range(c):
            c0 = (r // (2 * m)) * (2 * m)
            if r - c0 >= m:
                masks[li, r, c0:c0 + m] = 1.0
    masks[len(levels)] = np.eye(c, dtype=np.float32)
    return np.tril(np.ones((c, c), np.float32)), masks, tuple(levels)


def _level_ref_rows(p, m):
    c, w = p.shape
    if 2 * m > SUBLANES:
        parts = [jnp.broadcast_to(p[c0 + m - 1:c0 + m, :], (2 * m, w)) for c0 in range(0, c, 2 * m)]
        return parts[0] if len(parts) == 1 else jnp.concatenate(parts, axis=0)
    p3 = p.reshape(c // SUBLANES, SUBLANES, w)
    sub = lax.broadcasted_iota(jnp.int32, p3.shape, 1)
    out = None
    for c0 in range(0, SUBLANES, 2 * m):
        b = jnp.broadcast_to(p3[:, c0 + m - 1:c0 + m, :], p3.shape)
        out = b if out is None else jnp.where(sub >= c0, b, out)
    return out.reshape(c, w)


def _hgrn_kernel(q_ref, f_ref, i_ref, g_ref, lb_ref, ng_ref, tril_ref, msk_ref, o_ref,
                 *, n_chunks, levels, group):
    c = HG_CHUNK
    dk = HG_DK
    assert n_chunks % group == 0
    lb = jnp.concatenate([lb_ref[...]] * group, axis=1)
    oml = 1.0 - lb
    ng = ng_ref[...]
    tril = tril_ref[...]
    n_levels = len(levels)

    def load(ref, r0):
        return jnp.concatenate([ref[0, pl.ds(r0 + u * c, c), :] for u in range(group)], axis=1)

    def lanes(a, u):
        return a[:, u * dk:(u + 1) * dk]

    def intra(r0):
        z = load(f_ref, r0)
        qr = load(q_ref, r0)
        lf = jnp.log(lb + oml * jax.nn.sigmoid(z))
        kk = oml * jax.nn.sigmoid(-z)
        qf = qr * jax.nn.sigmoid(qr)
        l_hi, l_mid, l_lo = _split3(lf)
        p = _dot(tril, l_hi) + _dot(tril, l_mid) + _dot(tril, l_lo)
        b_end = p[c - 1:c, :]
        qb = (qf * jnp.exp(p)).astype(BF16)
        kd = (kk * jnp.exp(b_end - p)).astype(BF16)
        dec = jnp.exp(b_end)
        qh = qf.astype(BF16)
        kh = kk.astype(BF16)
        scores = [msk_ref[n_levels] * _dot_nt(lanes(qh, u), lanes(kh, u)) for u in range(group)]
        row = lax.broadcasted_iota(jnp.int32, p.shape, 0)
        for li, m in enumerate(levels):
            ex = jnp.exp2(jnp.abs(p - _level_ref_rows(p, m)) * (-LOG2_E))
            qk = (jnp.where((row & (2 * m - 1)) >= m, qf, kk) * ex).astype(BF16)
            for u in range(group):
                scores[u] = scores[u] + msk_ref[li] * _dot_nt(lanes(qk, u), lanes(qk, u))
        vbs = [i_ref[0, pl.ds(r0 + u * c, c), :].astype(BF16) for u in range(group)]
        o_intra = [_dot(scores[u].astype(BF16), vbs[u]) for u in range(group)]
        return qb, kd, dec, vbs, o_intra

    def chain(r0, staged, st):
        qb, kd, dec, vbs, o_intra = staged
        for u in range(group):
            rows = pl.ds(r0 + u * c, c)
            g = g_ref[0, rows, :]
            o = _dot_nt(lanes(qb, u), st.astype(BF16)) + o_intra[u]
            st = st * lanes(dec, u) + _dot_tn(vbs[u], lanes(kd, u))
            o = o * lax.rsqrt(jnp.mean(o * o, axis=-1, keepdims=True) + NORM_EPS)
            o_ref[0, rows, :] = o * ng * (g * jax.nn.sigmoid(g))
        return st

    n_groups = n_chunks // group
    st = jnp.zeros((HG_DK, HG_DK), F32)
    staged = intra(0)
    for gi in range(n_groups):
        nxt = intra((gi + 1) * group * c) if gi + 1 < n_groups else None
        st = chain(gi * group * c, staged, st)
        staged = nxt


def _hgrn(hg3, lb_row, ng_row):
    b, s, _ = hg3.shape
    tril, masks, levels = _hgrn_tables()
    mst = jnp.asarray(tril, BF16)
    msk = jnp.asarray(masks, F32)
    h = HG_HEADS

    def col(off):
        return pl.BlockSpec((1, s, HG_DK), lambda bi, hi: (bi, 0, off + hi))

    return pl.pallas_call(
        functools.partial(_hgrn_kernel, n_chunks=s // HG_CHUNK, levels=levels,
                          group=math.gcd(s // HG_CHUNK, HG_GROUP)),
        grid=(b, h),
        in_specs=[
            col(0), col(h), col(2 * h), col(3 * h),
            pl.BlockSpec((1, HG_DK), lambda bi, hi: (0, hi)),
            pl.BlockSpec((1, HG_DK), lambda bi, hi: (0, hi)),
            pl.BlockSpec(mst.shape, lambda bi, hi: (0, 0)),
            pl.BlockSpec(msk.shape, lambda bi, hi: (0, 0, 0)),
        ],
        out_specs=pl.BlockSpec((1, s, HG_DK), lambda bi, hi: (bi, 0, hi)),
        out_shape=jax.ShapeDtypeStruct((b, s, h * HG_DK), F32),
        compiler_params=pltpu.CompilerParams(
            dimension_semantics=("parallel", "parallel"), vmem_limit_bytes=VMEM_LIMIT),
        name="hgrn2",
    )(hg3, hg3, hg3, hg3, lb_row, ng_row, mst, msk)


def _t5_bucket_np(dist):
    max_exact = REL_BUCKETS // 2
    d = np.maximum(dist, 1).astype(np.float32)
    log_part = max_exact + (np.log(d / np.float32(max_exact)) / np.float32(math.log(REL_MAX_DIST / max_exact))
                            * np.float32(REL_BUCKETS - max_exact)).astype(np.int32)
    return np.where(dist < max_exact, dist, np.minimum(log_part, REL_BUCKETS - 1))


def _moba_kernel(q_ref, k_ref, v_ref, avg_ref, bkt_ref, rb_ref, o_ref, own_ref, prev_ref, *, n_blocks):
    blk = MB_BLOCK
    scale = MB_DH ** -0.5 * LOG2_E
    avg = avg_ref[...]
    hp = LANES // MB_DH

    @pl.when(pl.program_id(1) == 0)
    def _():
        causal = (lax.broadcasted_iota(jnp.int32, (blk, blk), 0) <= lax.broadcasted_iota(jnp.int32, (blk, blk), 1))
        for hh in range(hp):
            head = pl.program_id(0) * hp + hh
            own_t = jnp.zeros((blk, blk), F32)
            prev_t = jnp.zeros((blk, blk), F32)
            for bk in range(REL_BUCKETS):
                val = rb_ref[bk, head] * LOG2_E
                own_t = jnp.where(bkt_ref[0] == bk, val, own_t)
                prev_t = jnp.where(bkt_ref[1] == bk, val, prev_t)
            own_ref[hh] = jnp.where(causal, own_t, NEG_INF)
            prev_ref[hh] = prev_t

    grow = lax.broadcasted_iota(jnp.int32, (GATE_ROWS, blk), 0)
    vt_all = v_ref[0].T

    def fold(a):
        return a.reshape(blk // SUBLANES, SUBLANES, blk)

    heads = []
    for hh in range(hp):
        ls = slice(hh * MB_DH, (hh + 1) * MB_DH)
        k_h = k_ref[0, :, ls]
        k_hi, k_lo = _split2(k_h)
        k_mean = (_dot(avg, k_hi) + _dot(avg, k_lo))[:GATE_ROWS]
        vt = jnp.concatenate([vt_all[hh * MB_DH:(hh + 1) * MB_DH, :], jnp.ones((DEN_ROWS, vt_all.shape[1]), F32)],
                             axis=0).astype(BF16)
        far_bias = rb_ref[REL_BUCKETS - 1, pl.program_id(0) * hp + hh] * LOG2_E
        heads.append((ls, _split2(k_mean), k_h.astype(BF16), vt, far_bias))

    def logits(hh, i):
        ls, (km_hi, km_lo), kb, _, far_bias = heads[hh]
        qi = q_ref[0, i * blk:(i + 1) * blk, ls] * scale
        st = _dot_nt(kb[:(i + 1) * blk], qi.astype(BF16))
        sel = None
        if i > 0:
            q_hi, q_lo = _split2(qi)
            gate = _dot_nt(km_hi, q_hi) + _dot_nt(km_lo, q_hi) + _dot_nt(km_hi, q_lo)
            gate = jnp.where(grow < i, gate, NEG_INF)
            rank = jnp.zeros((GATE_ROWS, blk), F32)
            if i > MB_TOPK:
                for j2 in range(i):
                    gj = jnp.broadcast_to(gate[j2:j2 + 1, :], (GATE_ROWS, blk))
                    tie = jnp.where(grow > j2, 1.0, 0.0)
                    rank = rank + jnp.where(gj > gate, 1.0, jnp.where(gj == gate, tie, 0.0))
            sel = jnp.where(rank < MB_TOPK, gate, NEG_INF) > NEG_INF
        pieces = []
        for j in range(i + 1):
            sj = st[j * blk:(j + 1) * blk, :]
            if j == i:
                pieces.append(sj + own_ref[hh])
            elif j == i - 1:
                pieces.append(sj + prev_ref[hh] + jnp.where(sel[j:j + 1, :], 0.0, NEG_INF))
            else:
                pieces.append(sj + jnp.where(sel[j:j + 1, :], far_bias, NEG_INF))
        mx8 = fold(pieces[0]).max(axis=0)
        for p in pieces[1:]:
            mx8 = jnp.maximum(mx8, fold(p).max(axis=0))
        return pieces, mx8.max(axis=0, keepdims=True)

    def attend(hh, pieces, mx):
        vt = heads[hh][3]
        weights = jnp.concatenate([jnp.exp2(p - mx).astype(BF16) for p in pieces], axis=0)
        acc = _dot(vt[:, :len(pieces) * blk], weights)
        return acc[:MB_DH] / acc[MB_DH:MB_DH + 1]

    items = [(i, hh) for i in range(n_blocks) for hh in range(hp)]
    staged = [logits(hh, i) for i, hh in items[:PIPE_AHEAD]]
    done = {}
    for n, (i, hh) in enumerate(items):
        if n + PIPE_AHEAD < len(items):
            staged.append(logits(items[n + PIPE_AHEAD][1], items[n + PIPE_AHEAD][0]))
        done[hh] = attend(hh, *staged.pop(0))
        if hh == hp - 1:
            o_ref[0, i * blk:(i + 1) * blk, :] = jnp.concatenate([done[h] for h in range(hp)], axis=0).T


def _moba(mb3, rel_bias):
    b, s, _ = mb3.shape
    assert s % MB_BLOCK == 0
    nb = s // MB_BLOCK
    hp = LANES // MB_DH
    n_hp = MB_HEADS // hp
    t = np.arange(MB_BLOCK)
    d_own = t[:, None] - t[None, :]
    buckets = np.stack([_t5_bucket_np(np.maximum(d_own, 0)), _t5_bucket_np(d_own + MB_BLOCK)]).astype(np.int32)
    assert _t5_bucket_np(np.array([MB_BLOCK + 1]))[0] == REL_BUCKETS - 1
    assert nb <= GATE_ROWS
    bkt = jnp.asarray(buckets.transpose(0, 2, 1))
    avg_np = np.zeros((LANES, s), np.float32)
    for j in range(nb):
        avg_np[j, j * MB_BLOCK:(j + 1) * MB_BLOCK] = 1.0 / MB_BLOCK
    avg = jnp.asarray(avg_np, BF16)

    def col(off):
        return pl.BlockSpec((1, s, LANES), lambda hi, bi: (bi, 0, off + hi))

    return pl.pallas_call(
        functools.partial(_moba_kernel, n_blocks=nb),
        grid=(n_hp, b),
        in_specs=[
            col(0), col(n_hp), col(2 * n_hp),
            pl.BlockSpec(avg.shape, lambda hi, bi: (0, 0)),
            pl.BlockSpec(bkt.shape, lambda hi, bi: (0, 0, 0)),
            pl.BlockSpec(memory_space=pltpu.SMEM),
        ],
        out_specs=pl.BlockSpec((1, s, LANES), lambda hi, bi: (bi, 0, hi)),
        out_shape=jax.ShapeDtypeStruct((b, s, MB_HEADS * MB_DH), F32),
        scratch_shapes=[pltpu.VMEM((hp, MB_BLOCK, MB_BLOCK), F32), pltpu.VMEM((hp, MB_BLOCK, MB_BLOCK), F32)],
        compiler_params=pltpu.CompilerParams(
            dimension_semantics=("parallel", "arbitrary"), vmem_limit_bytes=VMEM_LIMIT),
        name="moba",
    )(mb3, mb3, mb3, avg, bkt, rel_bias.astype(F32))


def _layer_norm(x, g, b):
    mu = jnp.mean(x, axis=-1, keepdims=True)
    xc = x - mu
    var = jnp.mean(xc * xc, axis=-1, keepdims=True)
    return xc * lax.rsqrt(var + NORM_EPS) * g + b


def _merge_kernel(hg_ref, mb_ref, ga_ref, gb_ref, x_ref, wa_ref, wb_ref, wo_ref, g1_ref, b1_ref,
                  wr_ref, br_ref, tri_ref, x1_ref, route_ref, cnt_ref, run_ref, *, blocks_per_moe_tile, parts):
    tm = x_ref.shape[0]
    pr = tm // parts
    lane = lax.broadcasted_iota(jnp.int32, (pr, LANES), 1)

    @pl.when(pl.program_id(0) % blocks_per_moe_tile == 0)
    def _():
        run_ref[...] = jnp.zeros_like(run_ref)

    def mix(rows):
        ya = _dot(hg_ref[rows, :].astype(BF16), wa_ref[...])
        yb = _dot(mb_ref[rows, :].astype(BF16), wb_ref[...])
        mixed_in = jax.nn.sigmoid(ga_ref[rows, :]) * ya + jax.nn.sigmoid(gb_ref[rows, :]) * yb
        mixed = _dot(mixed_in.astype(BF16), wo_ref[...])
        x1 = _layer_norm(DN_ALPHA * x_ref[rows, :] + mixed, g1_ref[...], b1_ref[...])
        x1_ref[rows, :] = x1
        return x1

    def route_rows(rows, x1, run):
        x_hi, x_lo = _split2(x1)
        w_hi, w_lo = wr_ref[0], wr_ref[1]
        hr = pr // 2
        logits = jnp.concatenate(
            [_dot(x_hi[r:r + hr], w_hi) + _dot(x_hi[r:r + hr], w_lo) + _dot(x_lo[r:r + hr], w_hi) for r in (0, hr)],
            axis=0) + br_ref[...]
        glog = jnp.where(lane < N_GROUPS, logits, NEG_INF)
        gmax = jnp.max(glog, axis=-1, keepdims=True)
        grp = jnp.min(jnp.where(glog == gmax, lane, LANES), axis=-1, keepdims=True)
        p_grp = 1.0 / jnp.sum(jnp.exp(glog - gmax), axis=-1, keepdims=True)
        e_lo = N_GROUPS + grp * EXPERTS_PER_GROUP
        elog = jnp.where(jnp.logical_and(lane >= e_lo, lane < e_lo + EXPERTS_PER_GROUP), logits, NEG_INF)
        m1 = jnp.max(elog, axis=-1, keepdims=True)
        i1 = jnp.min(jnp.where(elog == m1, lane, LANES), axis=-1, keepdims=True)
        elog2 = jnp.where(lane == i1, NEG_INF, elog)
        m2 = jnp.max(elog2, axis=-1, keepdims=True)
        i2 = jnp.min(jnp.where(elog2 == m2, lane, LANES), axis=-1, keepdims=True)
        e2 = jnp.exp(m2 - m1)
        w1 = p_grp / (1.0 + e2)
        w2 = p_grp * e2 / (1.0 + e2)
        oh1 = jnp.where(lane == i1 - N_GROUPS, 1.0, 0.0)
        oh2 = jnp.where(lane == i2 - N_GROUPS, 1.0, 0.0)
        both = oh1 + oh2
        before = _dot(tri_ref[...], both.astype(BF16)) + run
        r1 = jnp.sum(before * oh1, axis=-1, keepdims=True)
        r2 = jnp.sum(before * oh2, axis=-1, keepdims=True)
        cols = ((i1 - N_GROUPS).astype(F32), (i2 - N_GROUPS).astype(F32), w1, w2, r1, r2)
        route = jnp.zeros((pr, LANES), F32)
        for li, col in enumerate(cols):
            route = jnp.where(lane == li, col, route)
        route_ref[:, rows] = route.T[:ROUTE_FIELDS, :]
        return run + jnp.sum(both, axis=0, keepdims=True)

    slabs = [slice(h * pr, (h + 1) * pr) for h in range(parts)]
    mixed = [mix(rows) for rows in slabs]
    run = run_ref[...]
    for rows, x1 in zip(slabs, mixed):
        run = route_rows(rows, x1, run)
    run_ref[...] = run
    cnt_ref[0] = run


def _merge(hg_o, mb_o, gates, x2, wa, wb, wo, g1, b1, wr, br, tm, moe_tile):
    n, d = x2.shape
    wa_n = hg_o.shape[1]
    wb_n = mb_o.shape[1]
    assert moe_tile % tm == 0
    parts = max(1, tm // MERGE_SLAB)
    tri = jnp.asarray(np.tril(np.ones((tm // parts, tm // parts), np.float32), -1), BF16)

    def full(a):
        nd = a.ndim
        return pl.BlockSpec(a.shape, lambda i: (0,) * nd, pipeline_mode=pl.Buffered(1))

    return pl.pallas_call(
        functools.partial(_merge_kernel, blocks_per_moe_tile=moe_tile // tm, parts=parts),
        grid=(n // tm,),
        in_specs=[
            pl.BlockSpec((tm, wa_n), lambda i: (i, 0)),
            pl.BlockSpec((tm, wb_n), lambda i: (i, 0)),
            pl.BlockSpec((tm, d), lambda i: (i, 0)),
            pl.BlockSpec((tm, d), lambda i: (i, 1)),
            pl.BlockSpec((tm, d), lambda i: (i, 0)),
            full(wa), full(wb), full(wo), full(g1), full(b1), full(wr), full(br), full(tri),
        ],
        out_specs=[
            pl.BlockSpec((tm, d), lambda i: (i, 0)),
            pl.BlockSpec((ROUTE_FIELDS, tm), lambda i: (0, i)),
            pl.BlockSpec((1, 1, LANES), lambda i: (i, 0, 0)),
        ],
        out_shape=[
            jax.ShapeDtypeStruct((n, d), F32),
            jax.ShapeDtypeStruct((ROUTE_FIELDS, n), F32),
            jax.ShapeDtypeStruct((n // tm, 1, LANES), F32),
        ],
        scratch_shapes=[pltpu.VMEM((1, LANES), F32)],
        compiler_params=pltpu.CompilerParams(
            dimension_semantics=("arbitrary",), vmem_limit_bytes=VMEM_LIMIT),
        name="merge_ln1_router",
    )(hg_o, mb_o, gates, gates, x2, wa, wb, wo, g1, b1, wr, br, tri)


def _segment_copies(n_small, make_copy, act):
    per_big = SEG_BIG // SEG_PAD
    n_big = n_small // per_big

    def big(c, carry):
        act(make_copy(c * SEG_BIG, SEG_BIG))
        return carry
    lax.fori_loop(0, n_big, big, 0)

    def small(c, carry):
        act(make_copy(n_big * SEG_BIG + c * SEG_PAD, SEG_PAD))
        return carry
    lax.fori_loop(0, n_small - n_big * per_big, small, 0)


def _wait_copies(n_big, n_small, make_copy):
    def big(c, carry):
        make_copy(0, SEG_BIG).wait()
        return carry
    lax.fori_loop(0, n_big, big, 0)

    def small(c, carry):
        make_copy(0, SEG_PAD).wait()
        return carry
    lax.fori_loop(0, n_small, small, 0)


def _dispatch_kernel(lbase_ref, gseg_ref, nseg_ref, tot_ref, gap_ref, p0_ref, p1_ref, x1_ref, xs_hbm,
                     xs_ref, xb_ref, zero_ref, sem, *, cv_rows):
    ti = pl.program_id(0)
    n_tiles = pl.num_programs(0)
    groups, _, d = x1_ref.shape

    def out_copy(t, ex):
        def make(off, size):
            src = pl.multiple_of(lbase_ref[t * N_EXPERTS + ex] + off, SEG_PAD)
            dst = pl.multiple_of(gseg_ref[t * N_EXPERTS + ex] + off, SEG_PAD)
            return pltpu.make_async_copy(xb_ref.at[pl.ds(src, size), :], xs_hbm.at[pl.ds(dst, size), :], sem.at[0])
        return make

    @pl.when(ti == 0)
    def _():
        def zero(c, carry):
            r0 = pl.multiple_of(c * cv_rows, cv_rows)
            xs_ref[pl.ds(r0, cv_rows), :] = jnp.zeros((cv_rows, d), F32)
            return carry
        lax.fori_loop(0, xs_ref.shape[0] // cv_rows, zero, 0)
        zero_ref[...] = jnp.zeros_like(zero_ref)

    def sort_rows(grp, carry):
        t0 = grp * SUBLANES
        for k in range(SUBLANES):
            row = x1_ref[grp, pl.ds(k, 1), :]
            xs_ref[pl.ds(p0_ref[0, 0, t0 + k], 1), :] = row
            xs_ref[pl.ds(p1_ref[0, 0, t0 + k], 1), :] = row
        return carry
    lax.fori_loop(0, groups, sort_rows, 0)

    @pl.when(ti > 0)
    def _():
        _wait_copies(tot_ref[2 * (ti - 1)], tot_ref[2 * (ti - 1) + 1], out_copy(0, 0))

    def convert(c, carry):
        r0 = pl.multiple_of(c * cv_rows, cv_rows)
        xb_ref[pl.ds(r0, cv_rows), :] = xs_ref[pl.ds(r0, cv_rows), :].astype(BF16)
        return carry
    lax.fori_loop(0, xs_ref.shape[0] // cv_rows, convert, 0)

    def send(ex, carry):
        _segment_copies(nseg_ref[ti * N_EXPERTS + ex], out_copy(ti, ex), lambda cp: cp.start())
        return carry
    lax.fori_loop(0, N_EXPERTS, send, 0)

    @pl.when(ti == n_tiles - 1)
    def _():
        _wait_copies(tot_ref[2 * ti], tot_ref[2 * ti + 1], out_copy(0, 0))

        def fill_copy(row):
            return pltpu.make_async_copy(zero_ref, xs_hbm.at[pl.ds(pl.multiple_of(row, SEG_PAD), SEG_PAD), :],
                                         sem.at[1])

        def fill(ex, carry):
            def one(c, carry2):
                fill_copy(gap_ref[2 * ex] + c * SEG_PAD).start()
                return carry2
            lax.fori_loop(0, gap_ref[2 * ex + 1], one, 0)
            return carry
        lax.fori_loop(0, N_EXPERTS, fill, 0)

        def fill_wait(ex, carry):
            def one(c, carry2):
                fill_copy(0).wait()
                return carry2
            lax.fori_loop(0, gap_ref[2 * ex + 1], one, 0)
            return carry
        lax.fori_loop(0, N_EXPERTS, fill_wait, 0)


def _expert_kernel(be_ref, nu_ref, first_ref, slot_ref, nxt_ref, x_ref, wgu_hbm, wd_hbm, y_ref,
                   wgu_f32, wd_f32, wgu_bf, wd_bf, sem, *, cast_rows):
    i = pl.program_id(0)

    def fetch(expert, slot):
        return (pltpu.make_async_copy(wgu_hbm.at[expert], wgu_f32.at[slot], sem.at[0, slot]),
                pltpu.make_async_copy(wd_hbm.at[expert], wd_f32.at[slot], sem.at[1, slot]))

    @pl.when(i < nu_ref[0])
    def _():
        slot = slot_ref[i]

        @pl.when(i == 0)
        def _():
            for cp in fetch(be_ref[0], 0):
                cp.start()

        @pl.when(first_ref[i] == 1)
        def _():
            for cp in fetch(be_ref[i], slot):
                cp.wait()
            for r0 in range(0, wgu_bf.shape[0], cast_rows):
                wgu_bf[r0:r0 + cast_rows, :] = wgu_f32[slot, r0:r0 + cast_rows, :].astype(BF16)
            for r0 in range(0, wd_bf.shape[0], cast_rows):
                wd_bf[r0:r0 + cast_rows, :] = wd_f32[slot, r0:r0 + cast_rows, :].astype(BF16)

            @pl.when(nxt_ref[i] >= 0)
            def _():
                for cp in fetch(nxt_ref[i], 1 - slot):
                    cp.start()

        gu = _dot(x_ref[...], wgu_bf[...])
        gate = gu[:, :EXPERT_HIDDEN]
        up = gu[:, EXPERT_HIDDEN:]
        hdn = (gate * jax.nn.sigmoid(gate) * up).astype(BF16)
        y_ref[...] = _dot(hdn, wd_bf[...])


def _combine_kernel(lbase_ref, gseg_ref, nseg_ref, tot_ref, p0_ref, p1_ref, w0_ref, w1_ref, x1_ref, ys_hbm,
                    g2_ref, b2_ref,
                    o_ref, xs_ref, sem, *, ln_rows):
    ti = pl.program_id(0)
    n_tiles = pl.num_programs(0)
    groups, _, d = x1_ref.shape
    slot = ti % 2

    def in_copy(t, ex, sl):
        def make(off, size):
            src = pl.multiple_of(gseg_ref[t * N_EXPERTS + ex] + off, SEG_PAD)
            dst = pl.multiple_of(lbase_ref[t * N_EXPERTS + ex] + off, SEG_PAD)
            return pltpu.make_async_copy(ys_hbm.at[pl.ds(src, size), :], xs_ref.at[sl, pl.ds(dst, size), :],
                                         sem.at[sl])
        return make

    def fetch(t, sl):
        def one(ex, carry):
            _segment_copies(nseg_ref[t * N_EXPERTS + ex], in_copy(t, ex, sl), lambda cp: cp.start())
            return carry
        lax.fori_loop(0, N_EXPERTS, one, 0)

    @pl.when(ti == 0)
    def _():
        fetch(0, 0)

    @pl.when(ti + 1 < n_tiles)
    def _():
        fetch(ti + 1, 1 - slot)

    _wait_copies(tot_ref[2 * ti], tot_ref[2 * ti + 1], in_copy(0, 0, slot))

    def combine(grp, carry):
        t0 = grp * SUBLANES
        for k in range(SUBLANES):
            t = t0 + k
            o_ref[grp, pl.ds(k, 1), :] = (DN_ALPHA * x1_ref[grp, pl.ds(k, 1), :]
                                          + w0_ref[0, 0, t] * xs_ref[slot, pl.ds(p0_ref[0, 0, t], 1), :]
                                          + w1_ref[0, 0, t] * xs_ref[slot, pl.ds(p1_ref[0, 0, t], 1), :])
        return carry
    lax.fori_loop(0, groups, combine, 0)

    ln_groups = ln_rows // SUBLANES

    def norm(c, carry):
        g0 = pl.multiple_of(c * ln_groups, ln_groups)
        rows = o_ref[pl.ds(g0, ln_groups), :, :].reshape(ln_rows, d)
        o_ref[pl.ds(g0, ln_groups), :, :] = _layer_norm(rows, g2_ref[...], b2_ref[...]).reshape(ln_groups, SUBLANES, d)
        return carry
    lax.fori_loop(0, groups // ln_groups, norm, 0)


def _moe_tile(n):
    return min(MOE_TILE, n)


def _round_up(a, m):
    return (a + m - 1) // m * m


def _moe(x1, route, cnt_run, w_gate_up, w_down, g2, b2):
    n, d = x1.shape
    tile = _moe_tile(n)
    assert n % tile == 0
    n_tiles = n // tile
    i32 = jnp.int32
    cnt = cnt_run.reshape(n_tiles, -1, LANES)[:, -1, :N_EXPERTS].astype(i32)
    seg = _round_up(cnt, SEG_PAD)
    lbase = jnp.cumsum(seg, axis=1) - seg
    used = jnp.sum(seg, axis=0)
    region = _round_up(used, EXP_ROWS)
    e_start = jnp.cumsum(region) - region
    gseg = e_start[None, :] + jnp.cumsum(seg, axis=0) - seg
    nseg = seg // SEG_PAD
    per_big = SEG_BIG // SEG_PAD
    tot = jnp.stack([jnp.sum(nseg // per_big, axis=1), jnp.sum(nseg % per_big, axis=1)], axis=1)
    gap = jnp.stack([e_start + used, (region - used) // SEG_PAD], axis=1)
    rows_max = _round_up(TOP_K * n + n_tiles * N_EXPERTS * (SEG_PAD - 1) + N_EXPERTS * (EXP_ROWS - 1), EXP_ROWS)
    n_blocks = rows_max // EXP_ROWS
    n_used = (jnp.sum(region) // EXP_ROWS).astype(i32).reshape(1)
    blk_row = jnp.arange(n_blocks, dtype=i32)[:, None] * EXP_ROWS
    blk_expert = jnp.minimum(jnp.sum((blk_row >= jnp.cumsum(region)[None, :]).astype(i32), axis=1), N_EXPERTS - 1)
    r_exp = route[0:TOP_K].astype(i32).reshape(TOP_K, n_tiles, tile)
    r_rank = route[2 * TOP_K:3 * TOP_K].astype(i32).reshape(TOP_K, n_tiles, tile)
    seg_start = jnp.sum(jnp.where(r_exp[..., None] == jnp.arange(N_EXPERTS, dtype=i32), lbase[None, :, None, :], 0),
                        axis=-1)
    pos = (seg_start + r_rank).reshape(TOP_K, n_tiles, 1, tile)
    r_wgt = route[TOP_K:2 * TOP_K].reshape(TOP_K, n_tiles, 1, tile)
    local_rows = _round_up(TOP_K * tile + N_EXPERTS * (SEG_PAD - 1), 256)
    flat = lambda a: a.reshape(-1).astype(i32)

    def smem_spec():
        return pl.BlockSpec((1, 1, tile), lambda t, *_: (t, 0, 0), memory_space=pltpu.SMEM)

    x1_groups = x1.reshape(n // SUBLANES, SUBLANES, d)
    row_groups_spec = pl.BlockSpec((tile // SUBLANES, SUBLANES, d), lambda t, *_: (t, 0, 0))

    xs_hbm = pl.pallas_call(
        functools.partial(_dispatch_kernel, cv_rows=256),
        grid_spec=pltpu.PrefetchScalarGridSpec(
            num_scalar_prefetch=5,
            grid=(n_tiles,),
            in_specs=[smem_spec(), smem_spec(), row_groups_spec],
            out_specs=pl.BlockSpec(memory_space=pl.ANY),
            scratch_shapes=[
                pltpu.VMEM((local_rows, d), F32),
                pltpu.VMEM((local_rows, d), BF16),
                pltpu.VMEM((SEG_PAD, d), BF16),
                pltpu.SemaphoreType.DMA((2,)),
            ],
        ),
        out_shape=jax.ShapeDtypeStruct((rows_max, d), BF16),
        compiler_params=pltpu.CompilerParams(dimension_semantics=("arbitrary",), vmem_limit_bytes=VMEM_LIMIT),
        name="moe_dispatch",
    )(flat(lbase), flat(gseg), flat(nseg), flat(tot), flat(gap), pos[0], pos[1], x1_groups)

    def blk(i, be, nu):
        return jnp.minimum(i, nu[0] - 1)

    blk_idx = jnp.arange(n_blocks, dtype=i32)
    first = ((blk_idx == 0) | (blk_expert != jnp.roll(blk_expert, 1))).astype(i32)
    w_slot = (jnp.cumsum(first) - 1) % 2
    run_end = (jnp.cumsum(region) // EXP_ROWS)[blk_expert]
    nxt_expert = jnp.where(run_end < n_used[0], blk_expert[jnp.minimum(run_end, n_blocks - 1)], -1)

    ys_hbm = pl.pallas_call(
        functools.partial(_expert_kernel, cast_rows=256),
        grid_spec=pltpu.PrefetchScalarGridSpec(
            num_scalar_prefetch=5,
            grid=(n_blocks,),
            in_specs=[
                pl.BlockSpec((EXP_ROWS, d), lambda i, be, nu, *_: (blk(i, be, nu), 0)),
                pl.BlockSpec(memory_space=pl.ANY),
                pl.BlockSpec(memory_space=pl.ANY),
            ],
            out_specs=pl.BlockSpec((EXP_ROWS, d), lambda i, be, nu, *_: (blk(i, be, nu), 0)),
            scratch_shapes=[
                pltpu.VMEM((2, d, 2 * EXPERT_HIDDEN), F32), pltpu.VMEM((2, EXPERT_HIDDEN, d), F32),
                pltpu.VMEM((d, 2 * EXPERT_HIDDEN), BF16), pltpu.VMEM((EXPERT_HIDDEN, d), BF16),
                pltpu.SemaphoreType.DMA((2, 2)),
            ],
        ),
        out_shape=jax.ShapeDtypeStruct((rows_max, d), F32),
        compiler_params=pltpu.CompilerParams(dimension_semantics=("arbitrary",), vmem_limit_bytes=VMEM_LIMIT),
        name="moe_experts",
    )(blk_expert, n_used, first, w_slot.astype(i32), nxt_expert.astype(i32), xs_hbm, w_gate_up, w_down)

    return pl.pallas_call(
        functools.partial(_combine_kernel, ln_rows=min(256, tile)),
        grid_spec=pltpu.PrefetchScalarGridSpec(
            num_scalar_prefetch=4,
            grid=(n_tiles,),
            in_specs=[
                smem_spec(), smem_spec(), smem_spec(), smem_spec(),
                row_groups_spec,
                pl.BlockSpec(memory_space=pl.ANY),
                pl.BlockSpec((1, d), lambda t, *_: (0, 0)),
                pl.BlockSpec((1, d), lambda t, *_: (0, 0)),
            ],
            out_specs=row_groups_spec,
            scratch_shapes=[pltpu.VMEM((2, local_rows, d), F32), pltpu.SemaphoreType.DMA((2,))],
        ),
        out_shape=jax.ShapeDtypeStruct(x1_groups.shape, F32),
        compiler_params=pltpu.CompilerParams(dimension_semantics=("arbitrary",), vmem_limit_bytes=VMEM_LIMIT),
        name="moe_combine_ln2",
    )(flat(lbase), flat(gseg), flat(nseg), flat(tot), pos[0], pos[1], r_wgt[0], r_wgt[1], x1_groups, ys_hbm,
      g2, b2).reshape(n, d)


def _block(x, w_in, b_in, lower_bound, hg_norm_g, rel_bias, w_proj_a, w_proj_b, w_out, ln1_g, ln1_b,
           w_group, b_group, w_expert, b_expert, w_gate_up, w_down, ln2_g, ln2_b, *, tm_proj, tm_merge):
    b, s, d = x.shape
    n = b * s
    n_hg = 4 * HG_HEADS * HG_DK
    n_mb = 3 * MB_HEADS * MB_DH
    n_gt = 2 * d
    x2 = x.reshape(n, d)
    hg, mb, gates = _in_proj(x2, w_in.astype(BF16), b_in.reshape(1, -1), n_hg, n_mb, n_gt, tm_proj)
    hg_o = _hgrn(hg.reshape(b, s, n_hg), lower_bound.reshape(1, -1), hg_norm_g.reshape(1, -1))
    mb_o = _moba(mb.reshape(b, s, n_mb), rel_bias)
    w_r = jnp.zeros((d, LANES), F32).at[:, :N_GROUPS].set(w_group).at[:, N_GROUPS:N_GROUPS + N_EXPERTS].set(w_expert)
    b_r = jnp.zeros((1, LANES), F32).at[0, :N_GROUPS].set(b_group).at[0, N_GROUPS:N_GROUPS + N_EXPERTS].set(b_expert)
    tm_merge = min(tm_merge, n)
    x1, route, cnt_run = _merge(hg_o.reshape(n, -1), mb_o.reshape(n, -1), gates, x2,
                                w_proj_a.astype(BF16), w_proj_b.astype(BF16), w_out.astype(BF16),
                                ln1_g.reshape(1, d), ln1_b.reshape(1, d), jnp.stack(_split2(w_r)), b_r,
                                tm_merge, _moe_tile(n))
    out = _moe(x1, route, cnt_run, w_gate_up, w_down, ln2_g.reshape(1, d), ln2_b.reshape(1, d))
    return out.reshape(b, s, d)


def kernel(x, w_in, b_in, lb_logits, hg_norm_g, rel_bias, w_proj_a, w_proj_b, w_out, ln1_g, ln1_b, w_group,
           b_group, w_expert, b_expert, w_gate_up, w_down, ln2_g, ln2_b):
    lower_bounds = jnp.cumsum(jax.nn.softmax(lb_logits.astype(F32), axis=0), axis=0)
    l = 0
    return _block(x, w_in[l], b_in[l], lower_bounds[l], hg_norm_g[l], rel_bias, w_proj_a[l], w_proj_b[l],
                  w_out[l], ln1_g[l], ln1_b[l], w_group[l], b_group[l], w_expert[l], b_expert[l],
                  w_gate_up[l], w_down[l], ln2_g[l], ln2_b[l], tm_proj=512, tm_merge=1024)
```

```python
import functools
import math

import numpy as np
import jax
import jax.numpy as jnp
from jax import lax
from jax.experimental import pallas as pl
from jax.experimental.pallas import tpu as pltpu

F32 = jnp.float32
BF16 = jnp.bfloat16

HG_HEADS = 4
HG_DK = 128
HG_CHUNK = 128
HG_GROUP = 8
MB_HEADS = 8
MB_DH = 64
MB_BLOCK = 256
MB_TOPK = 3
PIPE_AHEAD = 2
DEN_ROWS = 16
GATE_ROWS = 16
MERGE_SLAB = 512
ROUTE_FIELDS = 8
REL_BUCKETS = 32
REL_MAX_DIST = 128
N_GROUPS = 4
EXPERTS_PER_GROUP = 8
N_EXPERTS = N_GROUPS * EXPERTS_PER_GROUP
TOP_K = 2
EXPERT_HIDDEN = 512
MOE_TILE = 1024
SEG_PAD = 16
SEG_BIG = 64
EXP_ROWS = 512
DEPTH = 1
DN_ALPHA = (2.0 * DEPTH) ** 0.25
NORM_EPS = 1e-5
LANES = 128
SUBLANES = 8
VMEM_LIMIT = 56 * 1024 * 1024
NEG_INF = float("-inf")
LOG2_E = 1.4426950408889634


def _split2(a):
    hi = a.astype(BF16)
    lo = (a - hi.astype(F32)).astype(BF16)
    return hi, lo


def _split3(a):
    hi = a.astype(BF16)
    r = a - hi.astype(F32)
    mid = r.astype(BF16)
    lo = (r - mid.astype(F32)).astype(BF16)
    return hi, mid, lo


def _dot_nt(a, b):
    return lax.dot_general(a, b, (((1,), (1,)), ((), ())), preferred_element_type=F32)


def _dot_tn(a, b):
    return lax.dot_general(a, b, (((0,), (0,)), ((), ())), preferred_element_type=F32)


def _dot(a, b):
    return jnp.dot(a, b, preferred_element_type=F32)


def _in_proj_kernel(x_ref, w_ref, b_ref, hg_ref, mb_ref, gt_ref, *, col_chunk):
    xb = x_ref[...].astype(BF16)
    outs = ((hg_ref, 0), (mb_ref, hg_ref.shape[1]), (gt_ref, hg_ref.shape[1] + mb_ref.shape[1]))
    for o_ref, base in outs:
        for c0 in range(0, o_ref.shape[1], col_chunk):
            acc = _dot(xb, w_ref[:, base + c0:base + c0 + col_chunk])
            o_ref[:, c0:c0 + col_chunk] = acc + b_ref[:, base + c0:base + c0 + col_chunk]


def _in_proj(x2, w_bf, b_in, n_hg, n_mb, n_gt, tm):
    n, d = x2.shape
    cols = w_bf.shape[1]
    return pl.pallas_call(
        functools.partial(_in_proj_kernel, col_chunk=512),
        grid=(n // tm,),
        in_specs=[
            pl.BlockSpec((tm, d), lambda i: (i, 0)),
            pl.BlockSpec((d, cols), lambda i: (0, 0), pipeline_mode=pl.Buffered(1)),
            pl.BlockSpec((1, cols), lambda i: (0, 0)),
        ],
        out_specs=[
            pl.BlockSpec((tm, n_hg), lambda i: (i, 0)),
            pl.BlockSpec((tm, n_mb), lambda i: (i, 0)),
            pl.BlockSpec((tm, n_gt), lambda i: (i, 0)),
        ],
        out_shape=[
            jax.ShapeDtypeStruct((n, n_hg), F32),
            jax.ShapeDtypeStruct((n, n_mb), F32),
            jax.ShapeDtypeStruct((n, n_gt), F32),
        ],
        compiler_params=pltpu.CompilerParams(
            dimension_semantics=("parallel",), vmem_limit_bytes=VMEM_LIMIT),
        name="in_proj",
    )(x2, w_bf, b_in)


def _hgrn_tables():
    c = HG_CHUNK
    levels = [c >> (i + 1) for i in range(int(math.log2(c)))]
    masks = np.zeros((len(levels) + 1, c, c), np.float32)
    for li, m in enumerate(levels):
        for r in range(c):
            c0 = (r // (2 * m)) * (2 * m)
            if r - c0 >= m:
                masks[li, r, c0:c0 + m] = 1.0
    masks[len(levels)] = np.eye(c, dtype=np.float32)
    return np.tril(np.ones((c, c), np.float32)), masks, tuple(levels)


def _level_ref_rows(p, m):
    c, w = p.shape
    if 2 * m > SUBLANES:
        parts = [jnp.broadcast_to(p[c0 + m - 1:c0 + m, :], (2 * m, w)) for c0 in range(0, c, 2 * m)]
        return parts[0] if len(parts) == 1 else jnp.concatenate(parts, axis=0)
    p3 = p.reshape(c // SUBLANES, SUBLANES, w)
    sub = lax.broadcasted_iota(jnp.int32, p3.shape, 1)
    out = None
    for c0 in range(0, SUBLANES, 2 * m):
        b = jnp.broadcast_to(p3[:, c0 + m - 1:c0 + m, :], p3.shape)
        out = b if out is None else jnp.where(sub >= c0, b, out)
    return out.reshape(c, w)


def _hgrn_kernel(q_ref, f_ref, i_ref, g_ref, lb_ref, ng_ref, tril_ref, msk_ref, o_ref,
                 *, n_chunks, levels, group):
    c = HG_CHUNK
    dk = HG_DK
    assert n_chunks % group == 0
    lb = jnp.concatenate([lb_ref[...]] * group, axis=1)
    oml = 1.0 - lb
    ng = ng_ref[...]
    tril = tril_ref[...]
    n_levels = len(levels)

    def load(ref, r0):
        return jnp.concatenate([ref[0, pl.ds(r0 + u * c, c), :] for u in range(group)], axis=1)

    def lanes(a, u):
        return a[:, u * dk:(u + 1) * dk]

    def intra(r0):
        z = load(f_ref, r0)
        qr = load(q_ref, r0)
        lf = jnp.log(lb + oml * jax.nn.sigmoid(z))
        kk = oml * jax.nn.sigmoid(-z)
        qf = qr * jax.nn.sigmoid(qr)
        l_hi, l_mid, l_lo = _split3(lf)
        p = _dot(tril, l_hi) + _dot(tril, l_mid) + _dot(tril, l_lo)
        b_end = p[c - 1:c, :]
        qb = (qf * jnp.exp(p)).astype(BF16)
        kd = (kk * jnp.exp(b_end - p)).astype(BF16)
        dec = jnp.exp(b_end)
        qh = qf.astype(BF16)
        kh = kk.astype(BF16)
        scores = [msk_ref[n_levels] * _dot_nt(lanes(qh, u), lanes(kh, u)) for u in range(group)]
        row = lax.broadcasted_iota(jnp.int32, p.shape, 0)
        for li, m in enumerate(levels):
            ex = jnp.exp2(jnp.abs(p - _level_ref_rows(p, m)) * (-LOG2_E))
            qk = (jnp.where((row & (2 * m - 1)) >= m, qf, kk) * ex).astype(BF16)
            for u in range(group):
                scores[u] = scores[u] + msk_ref[li] * _dot_nt(lanes(qk, u), lanes(qk, u))
        vbs = [i_ref[0, pl.ds(r0 + u * c, c), :].astype(BF16) for u in range(group)]
        o_intra = [_dot(scores[u].astype(BF16), vbs[u]) for u in range(group)]
        return qb, kd, dec, vbs, o_intra

    def chain(r0, staged, st):
        qb, kd, dec, vbs, o_intra = staged
        for u in range(group):
            rows = pl.ds(r0 + u * c, c)
            g = g_ref[0, rows, :]
            o = _dot_nt(lanes(qb, u), st.astype(BF16)) + o_intra[u]
            st = st * lanes(dec, u) + _dot_tn(vbs[u], lanes(kd, u))
            o = o * lax.rsqrt(jnp.mean(o * o, axis=-1, keepdims=True) + NORM_EPS)
            o_ref[0, rows, :] = o * ng * (g * jax.nn.sigmoid(g))
        return st

    n_groups = n_chunks // group
    st = jnp.zeros((HG_DK, HG_DK), F32)
    staged = intra(0)
    for gi in range(n_groups):
        nxt = intra((gi + 1) * group * c) if gi + 1 < n_groups else None
        st = chain(gi * group * c, staged, st)
        staged = nxt


def _hgrn(hg3, lb_row, ng_row):
    b, s, _ = hg3.shape
    tril, masks, levels = _hgrn_tables()
    mst = jnp.asarray(tril, BF16)
    msk = jnp.asarray(masks, F32)
    h = HG_HEADS

    def col(off):
        return pl.BlockSpec((1, s, HG_DK), lambda bi, hi: (bi, 0, off + hi))

    return pl.pallas_call(
        functools.partial(_hgrn_kernel, n_chunks=s // HG_CHUNK, levels=levels,
                          group=math.gcd(s // HG_CHUNK, HG_GROUP)),
        grid=(b, h),
        in_specs=[
            col(0), col(h), col(2 * h), col(3 * h),
            pl.BlockSpec((1, HG_DK), lambda bi, hi: (0, hi)),
            pl.BlockSpec((1, HG_DK), lambda bi, hi: (0, hi)),
            pl.BlockSpec(mst.shape, lambda bi, hi: (0, 0)),
            pl.BlockSpec(msk.shape, lambda bi, hi: (0, 0, 0)),
        ],
        out_specs=pl.BlockSpec((1, s, HG_DK), lambda bi, hi: (bi, 0, hi)),
        out_shape=jax.ShapeDtypeStruct((b, s, h * HG_DK), F32),
        compiler_params=pltpu.CompilerParams(
            dimension_semantics=("parallel", "parallel"), vmem_limit_bytes=VMEM_LIMIT),
        name="hgrn2",
    )(hg3, hg3, hg3, hg3, lb_row, ng_row, mst, msk)


def _t5_bucket_np(dist):
    max_exact = REL_BUCKETS // 2
    d = np.maximum(dist, 1).astype(np.float32)
    log_part = max_exact + (np.log(d / np.float32(max_exact)) / np.float32(math.log(REL_MAX_DIST / max_exact))
                            * np.float32(REL_BUCKETS - max_exact)).astype(np.int32)
    return np.where(dist < max_exact, dist, np.minimum(log_part, REL_BUCKETS - 1))


def _moba_kernel(q_ref, k_ref, v_ref, avg_ref, bkt_ref, rb_ref, o_ref, own_ref, prev_ref, *, n_blocks):
    blk = MB_BLOCK
    scale = MB_DH ** -0.5 * LOG2_E
    avg = avg_ref[...]
    hp = LANES // MB_DH

    @pl.when(pl.program_id(1) == 0)
    def _():
        causal = (lax.broadcasted_iota(jnp.int32, (blk, blk), 0) <= lax.broadcasted_iota(jnp.int32, (blk, blk), 1))
        for hh in range(hp):
            head = pl.program_id(0) * hp + hh
            own_t = jnp.zeros((blk, blk), F32)
            prev_t = jnp.zeros((blk, blk), F32)
            for bk in range(REL_BUCKETS):
                val = rb_ref[bk, head] * LOG2_E
                own_t = jnp.where(bkt_ref[0] == bk, val, own_t)
                prev_t = jnp.where(bkt_ref[1] == bk, val, prev_t)
            own_ref[hh] = jnp.where(causal, own_t, NEG_INF)
            prev_ref[hh] = prev_t

    grow = lax.broadcasted_iota(jnp.int32, (GATE_ROWS, blk), 0)
    vt_all = v_ref[0].T

    def fold(a):
        return a.reshape(blk // SUBLANES, SUBLANES, blk)

    heads = []
    for hh in range(hp):
        ls = slice(hh * MB_DH, (hh + 1) * MB_DH)
        k_h = k_ref[0, :, ls]
        k_hi, k_lo = _split2(k_h)
        k_mean = (_dot(avg, k_hi) + _dot(avg, k_lo))[:GATE_ROWS]
        vt = jnp.concatenate([vt_all[hh * MB_DH:(hh + 1) * MB_DH, :], jnp.ones((DEN_ROWS, vt_all.shape[1]), F32)],
                             axis=0).astype(BF16)
        far_bias = rb_ref[REL_BUCKETS - 1, pl.program_id(0) * hp + hh] * LOG2_E
        heads.append((ls, _split2(k_mean), k_h.astype(BF16), vt, far_bias))

    def logits(hh, i):
        ls, (km_hi, km_lo), kb, _, far_bias = heads[hh]
        qi = q_ref[0, i * blk:(i + 1) * blk, ls] * scale
        st = _dot_nt(kb[:(i + 1) * blk], qi.astype(BF16))
        sel = None
        if i > 0:
            q_hi, q_lo = _split2(qi)
            gate = _dot_nt(km_hi, q_hi) + _dot_nt(km_lo, q_hi) + _dot_nt(km_hi, q_lo)
            gate = jnp.where(grow < i, gate, NEG_INF)
            rank = jnp.zeros((GATE_ROWS, blk), F32)
            if i > MB_TOPK:
                for j2 in range(i):
                    gj = jnp.broadcast_to(gate[j2:j2 + 1, :], (GATE_ROWS, blk))
                    tie = jnp.where(grow > j2, 1.0, 0.0)
                    rank = rank + jnp.where(gj > gate, 1.0, jnp.where(gj == gate, tie, 0.0))
            sel = jnp.where(rank < MB_TOPK, gate, NEG_INF) > NEG_INF
        pieces = []
        for j in range(i + 1):
            sj = st[j * blk:(j + 1) * blk, :]
            if j == i:
                pieces.append(sj + own_ref[hh])
            elif j == i - 1:
                pieces.append(sj + prev_ref[hh] + jnp.where(sel[j:j + 1, :], 0.0, NEG_INF))
            else:
                pieces.append(sj + jnp.where(sel[j:j + 1, :], far_bias, NEG_INF))
        mx8 = fold(pieces[0]).max(axis=0)
        for p in pieces[1:]:
            mx8 = jnp.maximum(mx8, fold(p).max(axis=0))
        return pieces, mx8.max(axis=0, keepdims=True)

    def attend(hh, pieces, mx):
        vt = heads[hh][3]
        weights = jnp.concatenate([jnp.exp2(p - mx).astype(BF16) for p in pieces], axis=0)
        acc = _dot(vt[:, :len(pieces) * blk], weights)
        return acc[:MB_DH] / acc[MB_DH:MB_DH + 1]

    items = [(i, hh) for i in range(n_blocks) for hh in range(hp)]
    staged = [logits(hh, i) for i, hh in items[:PIPE_AHEAD]]
    done = {}
    for n, (i, hh) in enumerate(items):
        if n + PIPE_AHEAD < len(items):
            staged.append(logits(items[n + PIPE_AHEAD][1], items[n + PIPE_AHEAD][0]))
        done[hh] = attend(hh, *staged.pop(0))
        if hh == hp - 1:
            o_ref[0, i * blk:(i + 1) * blk, :] = jnp.concatenate([done[h] for h in range(hp)], axis=0).T


def _moba(mb3, rel_bias):
    b, s, _ = mb3.shape
    assert s % MB_BLOCK == 0
    nb = s // MB_BLOCK
    hp = LANES // MB_DH
    n_hp = MB_HEADS // hp
    t = np.arange(MB_BLOCK)
    d_own = t[:, None] - t[None, :]
    buckets = np.stack([_t5_bucket_np(np.maximum(d_own, 0)), _t5_bucket_np(d_own + MB_BLOCK)]).astype(np.int32)
    assert _t5_bucket_np(np.array([MB_BLOCK + 1]))[0] == REL_BUCKETS - 1
    assert nb <= GATE_ROWS
    bkt = jnp.asarray(buckets.transpose(0, 2, 1))
    avg_np = np.zeros((LANES, s), np.float32)
    for j in range(nb):
        avg_np[j, j * MB_BLOCK:(j + 1) * MB_BLOCK] = 1.0 / MB_BLOCK
    avg = jnp.asarray(avg_np, BF16)

    def col(off):
        return pl.BlockSpec((1, s, LANES), lambda hi, bi: (bi, 0, off + hi))

    return pl.pallas_call(
        functools.partial(_moba_kernel, n_blocks=nb),
        grid=(n_hp, b),
        in_specs=[
            col(0), col(n_hp), col(2 * n_hp),
            pl.BlockSpec(avg.shape, lambda hi, bi: (0, 0)),
            pl.BlockSpec(bkt.shape, lambda hi, bi: (0, 0, 0)),
            pl.BlockSpec(memory_space=pltpu.SMEM),
        ],
        out_specs=pl.BlockSpec((1, s, LANES), lambda hi, bi: (bi, 0, hi)),
        out_shape=jax.ShapeDtypeStruct((b, s, MB_HEADS * MB_DH), F32),
        scratch_shapes=[pltpu.VMEM((hp, MB_BLOCK, MB_BLOCK), F32), pltpu.VMEM((hp, MB_BLOCK, MB_BLOCK), F32)],
        compiler_params=pltpu.CompilerParams(
            dimension_semantics=("parallel", "arbitrary"), vmem_limit_bytes=VMEM_LIMIT),
        name="moba",
    )(mb3, mb3, mb3, avg, bkt, rel_bias.astype(F32))


def _layer_norm(x, g, b):
    mu = jnp.mean(x, axis=-1, keepdims=True)
    xc = x - mu
    var = jnp.mean(xc * xc, axis=-1, keepdims=True)
    return xc * lax.rsqrt(var + NORM_EPS) * g + b


def _merge_kernel(hg_ref, mb_ref, ga_ref, gb_ref, x_ref, wa_ref, wb_ref, wo_ref, g1_ref, b1_ref,
                  wr_ref, br_ref, tri_ref, x1_ref, route_ref, cnt_ref, run_ref, *, blocks_per_moe_tile, parts):
    tm = x_ref.shape[0]
    pr = tm // parts
    lane = lax.broadcasted_iota(jnp.int32, (pr, LANES), 1)

    @pl.when(pl.program_id(0) % blocks_per_moe_tile == 0)
    def _():
        run_ref[...] = jnp.zeros_like(run_ref)

    def mix(rows):
        ya = _dot(hg_ref[rows, :].astype(BF16), wa_ref[...])
        yb = _dot(mb_ref[rows, :].astype(BF16), wb_ref[...])
        mixed_in = jax.nn.sigmoid(ga_ref[rows, :]) * ya + jax.nn.sigmoid(gb_ref[rows, :]) * yb
        mixed = _dot(mixed_in.astype(BF16), wo_ref[...])
        x1 = _layer_norm(DN_ALPHA * x_ref[rows, :] + mixed, g1_ref[...], b1_ref[...])
        x1_ref[rows, :] = x1
        return x1

    def route_rows(rows, x1, run):
        x_hi, x_lo = _split2(x1)
        w_hi, w_lo = wr_ref[0], wr_ref[1]
        hr = pr // 2
        logits = jnp.concatenate(
            [_dot(x_hi[r:r + hr], w_hi) + _dot(x_hi[r:r + hr], w_lo) + _dot(x_lo[r:r + hr], w_hi) for r in (0, hr)],
            axis=0) + br_ref[...]
        glog = jnp.where(lane < N_GROUPS, logits, NEG_INF)
        gmax = jnp.max(glog, axis=-1, keepdims=True)
        grp = jnp.min(jnp.where(glog == gmax, lane, LANES), axis=-1, keepdims=True)
        p_grp = 1.0 / jnp.sum(jnp.exp(glog - gmax), axis=-1, keepdims=True)
        e_lo = N_GROUPS + grp * EXPERTS_PER_GROUP
        elog = jnp.where(jnp.logical_and(lane >= e_lo, lane < e_lo + EXPERTS_PER_GROUP), logits, NEG_INF)
        m1 = jnp.max(elog, axis=-1, keepdims=True)
        i1 = jnp.min(jnp.where(elog == m1, lane, LANES), axis=-1, keepdims=True)
        elog2 = jnp.where(lane == i1, NEG_INF, elog)
        m2 = jnp.max(elog2, axis=-1, keepdims=True)
        i2 = jnp.min(jnp.where(elog2 == m2, lane, LANES), axis=-1, keepdims=True)
        e2 = jnp.exp(m2 - m1)
        w1 = p_grp / (1.0 + e2)
        w2 = p_grp * e2 / (1.0 + e2)
        oh1 = jnp.where(lane == i1 - N_GROUPS, 1.0, 0.0)
        oh2 = jnp.where(lane == i2 - N_GROUPS, 1.0, 0.0)
        both = oh1 + oh2
        before = _dot(tri_ref[...], both.astype(BF16)) + run
        r1 = jnp.sum(before * oh1, axis=-1, keepdims=True)
        r2 = jnp.sum(before * oh2, axis=-1, keepdims=True)
        cols = ((i1 - N_GROUPS).astype(F32), (i2 - N_GROUPS).astype(F32), w1, w2, r1, r2)
        route = jnp.zeros((pr, LANES), F32)
        for li, col in enumerate(cols):
            route = jnp.where(lane == li, col, route)
        route_ref[:, rows] = route.T[:ROUTE_FIELDS, :]
        return run + jnp.sum(both, axis=0, keepdims=True)

    slabs = [slice(h * pr, (h + 1) * pr) for h in range(parts)]
    mixed = [mix(rows) for rows in slabs]
    run = run_ref[...]
    for rows, x1 in zip(slabs, mixed):
        run = route_rows(rows, x1, run)
    run_ref[...] = run
    cnt_ref[0] = run


def _merge(hg_o, mb_o, gates, x2, wa, wb, wo, g1, b1, wr, br, tm, moe_tile):
    n, d = x2.shape
    wa_n = hg_o.shape[1]
    wb_n = mb_o.shape[1]
    assert moe_tile % tm == 0
    parts = max(1, tm // MERGE_SLAB)
    tri = jnp.asarray(np.tril(np.ones((tm // parts, tm // parts), np.float32), -1), BF16)

    def full(a):
        nd = a.ndim
        return pl.BlockSpec(a.shape, lambda i: (0,) * nd, pipeline_mode=pl.Buffered(1))

    return pl.pallas_call(
        functools.partial(_merge_kernel, blocks_per_moe_tile=moe_tile // tm, parts=parts),
        grid=(n // tm,),
        in_specs=[
            pl.BlockSpec((tm, wa_n), lambda i: (i, 0)),
            pl.BlockSpec((tm, wb_n), lambda i: (i, 0)),
            pl.BlockSpec((tm, d), lambda i: (i, 0)),
            pl.BlockSpec((tm, d), lambda i: (i, 1)),
            pl.BlockSpec((tm, d), lambda i: (i, 0)),
            full(wa), full(wb), full(wo), full(g1), full(b1), full(wr), full(br), full(tri),
        ],
        out_specs=[
            pl.BlockSpec((tm, d), lambda i: (i, 0)),
            pl.BlockSpec((ROUTE_FIELDS, tm), lambda i: (0, i)),
            pl.BlockSpec((1, 1, LANES), lambda i: (i, 0, 0)),
        ],
        out_shape=[
            jax.ShapeDtypeStruct((n, d), F32),
            jax.ShapeDtypeStruct((ROUTE_FIELDS, n), F32),
            jax.ShapeDtypeStruct((n // tm, 1, LANES), F32),
        ],
        scratch_shapes=[pltpu.VMEM((1, LANES), F32)],
        compiler_params=pltpu.CompilerParams(
            dimension_semantics=("arbitrary",), vmem_limit_bytes=VMEM_LIMIT),
        name="merge_ln1_router",
    )(hg_o, mb_o, gates, gates, x2, wa, wb, wo, g1, b1, wr, br, tri)


def _tile_copies(t, loc_ref, glob_ref, tot_ref, max_big, max_small, make_copy, act):
    base = t * (max_big + max_small)

    def big(q, carry):
        act(make_copy(loc_ref[base + q], glob_ref[base + q], SEG_BIG))
        return carry
    lax.fori_loop(0, tot_ref[2 * t], big, 0)

    def small(q, carry):
        act(make_copy(loc_ref[base + max_big + q], glob_ref[base + max_big + q], SEG_PAD))
        return carry
    lax.fori_loop(0, tot_ref[2 * t + 1], small, 0)


def _dispatch_kernel(loc_ref, glob_ref, tot_ref, gap_ref, p0_ref, p1_ref, x1_ref, xs_hbm,
                     xs_ref, xb_ref, zero_ref, sem, *, cv_rows, max_big, max_small):
    ti = pl.program_id(0)
    n_tiles = pl.num_programs(0)
    groups, _, d = x1_ref.shape

    def out_copy(loc, glob, size):
        src = pl.multiple_of(loc, SEG_PAD)
        dst = pl.multiple_of(glob, SEG_PAD)
        return pltpu.make_async_copy(xb_ref.at[pl.ds(src, size), :], xs_hbm.at[pl.ds(dst, size), :], sem.at[0])

    def tile_copies(t, act):
        _tile_copies(t, loc_ref, glob_ref, tot_ref, max_big, max_small, out_copy, act)

    @pl.when(ti == 0)
    def _():
        def zero(c, carry):
            r0 = pl.multiple_of(c * cv_rows, cv_rows)
            xs_ref[pl.ds(r0, cv_rows), :] = jnp.zeros((cv_rows, d), F32)
            return carry
        lax.fori_loop(0, xs_ref.shape[0] // cv_rows, zero, 0)
        zero_ref[...] = jnp.zeros_like(zero_ref)

    def sort_rows(grp, carry):
        t0 = grp * SUBLANES
        for k in range(SUBLANES):
            row = x1_ref[grp, pl.ds(k, 1), :]
            xs_ref[pl.ds(p0_ref[0, 0, t0 + k], 1), :] = row
            xs_ref[pl.ds(p1_ref[0, 0, t0 + k], 1), :] = row
        return carry
    lax.fori_loop(0, groups, sort_rows, 0)

    @pl.when(ti > 0)
    def _():
        tile_copies(ti - 1, lambda cp: cp.wait())

    def convert(c, carry):
        r0 = pl.multiple_of(c * cv_rows, cv_rows)
        xb_ref[pl.ds(r0, cv_rows), :] = xs_ref[pl.ds(r0, cv_rows), :].astype(BF16)
        return carry
    lax.fori_loop(0, xs_ref.shape[0] // cv_rows, convert, 0)

    tile_copies(ti, lambda cp: cp.start())

    @pl.when(ti == n_tiles - 1)
    def _():
        tile_copies(ti, lambda cp: cp.wait())

        def fill_copy(row):
            return pltpu.make_async_copy(zero_ref, xs_hbm.at[pl.ds(pl.multiple_of(row, SEG_PAD), SEG_PAD), :],
                                         sem.at[1])

        def fill(ex, carry):
            def one(c, carry2):
                fill_copy(gap_ref[2 * ex] + c * SEG_PAD).start()
                return carry2
            lax.fori_loop(0, gap_ref[2 * ex + 1], one, 0)
            return carry
        lax.fori_loop(0, N_EXPERTS, fill, 0)

        def fill_wait(ex, carry):
            def one(c, carry2):
                fill_copy(0).wait()
                return carry2
            lax.fori_loop(0, gap_ref[2 * ex + 1], one, 0)
            return carry
        lax.fori_loop(0, N_EXPERTS, fill_wait, 0)


def _expert_kernel(be_ref, nu_ref, first_ref, slot_ref, nxt_ref, x_ref, wgu_hbm, wd_hbm, y_ref,
                   wgu_f32, wd_f32, wgu_bf, wd_bf, sem, *, cast_rows):
    i = pl.program_id(0)

    def fetch(expert, slot):
        return (pltpu.make_async_copy(wgu_hbm.at[expert], wgu_f32.at[slot], sem.at[0, slot]),
                pltpu.make_async_copy(wd_hbm.at[expert], wd_f32.at[slot], sem.at[1, slot]))

    @pl.when(i < nu_ref[0])
    def _():
        slot = slot_ref[i]

        @pl.when(i == 0)
        def _():
            for cp in fetch(be_ref[0], 0):
                cp.start()

        @pl.when(first_ref[i] == 1)
        def _():
            for cp in fetch(be_ref[i], slot):
                cp.wait()
            for r0 in range(0, wgu_bf.shape[0], cast_rows):
                wgu_bf[r0:r0 + cast_rows, :] = wgu_f32[slot, r0:r0 + cast_rows, :].astype(BF16)
            for r0 in range(0, wd_bf.shape[0], cast_rows):
                wd_bf[r0:r0 + cast_rows, :] = wd_f32[slot, r0:r0 + cast_rows, :].astype(BF16)

            @pl.when(nxt_ref[i] >= 0)
            def _():
                for cp in fetch(nxt_ref[i], 1 - slot):
                    cp.start()

        gu = _dot(x_ref[...], wgu_bf[...])
        gate = gu[:, :EXPERT_HIDDEN]
        up = gu[:, EXPERT_HIDDEN:]
        hdn = (gate * jax.nn.sigmoid(gate) * up).astype(BF16)
        y_ref[...] = _dot(hdn, wd_bf[...])


def _combine_kernel(loc_ref, glob_ref, tot_ref, p0_ref, p1_ref, w0_ref, w1_ref, x1_ref, ys_hbm, g2_ref, b2_ref,
                    o_ref, xs_ref, sem, *, ln_rows, max_big, max_small):
    ti = pl.program_id(0)
    n_tiles = pl.num_programs(0)
    groups, _, d = x1_ref.shape
    slot = ti % 2

    def tile_copies(t, sl, act):
        def in_copy(loc, glob, size):
            src = pl.multiple_of(glob, SEG_PAD)
            dst = pl.multiple_of(loc, SEG_PAD)
            return pltpu.make_async_copy(ys_hbm.at[pl.ds(src, size), :], xs_ref.at[sl, pl.ds(dst, size), :],
                                         sem.at[sl])
        _tile_copies(t, loc_ref, glob_ref, tot_ref, max_big, max_small, in_copy, act)

    @pl.when(ti == 0)
    def _():
        tile_copies(0, 0, lambda cp: cp.start())

    @pl.when(ti + 1 < n_tiles)
    def _():
        tile_copies(ti + 1, 1 - slot, lambda cp: cp.start())

    tile_copies(ti, slot, lambda cp: cp.wait())

    def combine(grp, carry):
        t0 = grp * SUBLANES
        for k in range(SUBLANES):
            t = t0 + k
            o_ref[grp, pl.ds(k, 1), :] = (DN_ALPHA * x1_ref[grp, pl.ds(k, 1), :]
                                          + w0_ref[0, 0, t] * xs_ref[slot, pl.ds(p0_ref[0, 0, t], 1), :]
                                          + w1_ref[0, 0, t] * xs_ref[slot, pl.ds(p1_ref[0, 0, t], 1), :])
        return carry
    lax.fori_loop(0, groups, combine, 0)

    ln_groups = ln_rows // SUBLANES

    def norm(c, carry):
        g0 = pl.multiple_of(c * ln_groups, ln_groups)
        rows = o_ref[pl.ds(g0, ln_groups), :, :].reshape(ln_rows, d)
        o_ref[pl.ds(g0, ln_groups), :, :] = _layer_norm(rows, g2_ref[...], b2_ref[...]).reshape(ln_groups, SUBLANES, d)
        return carry
    lax.fori_loop(0, groups // ln_groups, norm, 0)


def _moe_tile(n):
    return min(MOE_TILE, n)


def _round_up(a, m):
    return (a + m - 1) // m * m


def _moe(x1, route, cnt_run, w_gate_up, w_down, g2, b2):
    n, d = x1.shape
    tile = _moe_tile(n)
    assert n % tile == 0
    n_tiles = n // tile
    i32 = jnp.int32
    cnt = cnt_run.reshape(n_tiles, -1, LANES)[:, -1, :N_EXPERTS].astype(i32)
    seg = _round_up(cnt, SEG_PAD)
    lbase = jnp.cumsum(seg, axis=1) - seg
    used = jnp.sum(seg, axis=0)
    region = _round_up(used, EXP_ROWS)
    e_start = jnp.cumsum(region) - region
    gseg = e_start[None, :] + jnp.cumsum(seg, axis=0) - seg
    n_bigs = seg // SEG_BIG
    n_smalls = (seg - n_bigs * SEG_BIG) // SEG_PAD
    max_big = TOP_K * tile // SEG_BIG
    max_small = N_EXPERTS * (SEG_BIG // SEG_PAD - 1)

    def piece_table(counts, first_off, step, width):
        incl = jnp.cumsum(counts, axis=1)
        excl = incl - counts
        q = jnp.arange(width, dtype=i32)[None, :, None]
        owner = (q >= excl[:, None, :]) & (q < incl[:, None, :])
        pick = lambda a: jnp.sum(jnp.where(owner, a[:, None, :], 0), axis=-1)
        off = pick(first_off) + (q[:, :, 0] - pick(excl)) * step
        return pick(lbase) + off, pick(gseg) + off

    big_loc, big_glob = piece_table(n_bigs, jnp.zeros_like(seg), SEG_BIG, max_big)
    small_loc, small_glob = piece_table(n_smalls, n_bigs * SEG_BIG, SEG_PAD, max_small)
    piece_loc = jnp.concatenate([big_loc, small_loc], axis=1)
    piece_glob = jnp.concatenate([big_glob, small_glob], axis=1)
    tot = jnp.stack([jnp.sum(n_bigs, axis=1), jnp.sum(n_smalls, axis=1)], axis=1)
    gap = jnp.stack([e_start + used, (region - used) // SEG_PAD], axis=1)
    rows_max = _round_up(TOP_K * n + n_tiles * N_EXPERTS * (SEG_PAD - 1) + N_EXPERTS * (EXP_ROWS - 1), EXP_ROWS)
    n_blocks = rows_max // EXP_ROWS
    n_used = (jnp.sum(region) // EXP_ROWS).astype(i32).reshape(1)
    blk_row = jnp.arange(n_blocks, dtype=i32)[:, None] * EXP_ROWS
    blk_expert = jnp.minimum(jnp.sum((blk_row >= jnp.cumsum(region)[None, :]).astype(i32), axis=1), N_EXPERTS - 1)
    r_exp = route[0:TOP_K].astype(i32).reshape(TOP_K, n_tiles, tile)
    r_rank = route[2 * TOP_K:3 * TOP_K].astype(i32).reshape(TOP_K, n_tiles, tile)
    seg_start = jnp.sum(jnp.where(r_exp[..., None] == jnp.arange(N_EXPERTS, dtype=i32), lbase[None, :, None, :], 0),
                        axis=-1)
    pos = (seg_start + r_rank).reshape(TOP_K, n_tiles, 1, tile)
    r_wgt = route[TOP_K:2 * TOP_K].reshape(TOP_K, n_tiles, 1, tile)
    local_rows = _round_up(TOP_K * tile + N_EXPERTS * (SEG_PAD - 1), 256)
    flat = lambda a: a.reshape(-1).astype(i32)

    def smem_spec():
        return pl.BlockSpec((1, 1, tile), lambda t, *_: (t, 0, 0), memory_space=pltpu.SMEM)

    x1_groups = x1.reshape(n // SUBLANES, SUBLANES, d)
    row_groups_spec = pl.BlockSpec((tile // SUBLANES, SUBLANES, d), lambda t, *_: (t, 0, 0))

    xs_hbm = pl.pallas_call(
        functools.partial(_dispatch_kernel, cv_rows=256, max_big=max_big, max_small=max_small),
        grid_spec=pltpu.PrefetchScalarGridSpec(
            num_scalar_prefetch=4,
            grid=(n_tiles,),
            in_specs=[smem_spec(), smem_spec(), row_groups_spec],
            out_specs=pl.BlockSpec(memory_space=pl.ANY),
            scratch_shapes=[
                pltpu.VMEM((local_rows, d), F32),
                pltpu.VMEM((local_rows, d), BF16),
                pltpu.VMEM((SEG_PAD, d), BF16),
                pltpu.SemaphoreType.DMA((2,)),
            ],
        ),
        out_shape=jax.ShapeDtypeStruct((rows_max, d), BF16),
        compiler_params=pltpu.CompilerParams(dimension_semantics=("arbitrary",), vmem_limit_bytes=VMEM_LIMIT),
        name="moe_dispatch",
    )(flat(piece_loc), flat(piece_glob), flat(tot), flat(gap), pos[0], pos[1], x1_groups)

    def blk(i, be, nu):
        return jnp.minimum(i, nu[0] - 1)

    blk_idx = jnp.arange(n_blocks, dtype=i32)
    first = ((blk_idx == 0) | (blk_expert != jnp.roll(blk_expert, 1))).astype(i32)
    w_slot = (jnp.cumsum(first) - 1) % 2
    e_idx = jnp.arange(N_EXPERTS, dtype=i32)
    later_used = (region > 0)[None, :] & (e_idx[None, :] > e_idx[:, None])
    nxt_of = jnp.min(jnp.where(later_used, e_idx[None, :], N_EXPERTS), axis=1)
    nxt_of = jnp.where(nxt_of < N_EXPERTS, nxt_of, -1)
    nxt_expert = jnp.sum(jnp.where(blk_expert[:, None] == e_idx[None, :], nxt_of[None, :], 0), axis=1)

    ys_hbm = pl.pallas_call(
        functools.partial(_expert_kernel, cast_rows=256),
        grid_spec=pltpu.PrefetchScalarGridSpec(
            num_scalar_prefetch=5,
            grid=(n_blocks,),
            in_specs=[
                pl.BlockSpec((EXP_ROWS, d), lambda i, be, nu, *_: (blk(i, be, nu), 0)),
                pl.BlockSpec(memory_space=pl.ANY),
                pl.BlockSpec(memory_space=pl.ANY),
            ],
            out_specs=pl.BlockSpec((EXP_ROWS, d), lambda i, be, nu, *_: (blk(i, be, nu), 0)),
            scratch_shapes=[
                pltpu.VMEM((2, d, 2 * EXPERT_HIDDEN), F32), pltpu.VMEM((2, EXPERT_HIDDEN, d), F32),
                pltpu.VMEM((d, 2 * EXPERT_HIDDEN), BF16), pltpu.VMEM((EXPERT_HIDDEN, d), BF16),
                pltpu.SemaphoreType.DMA((2, 2)),
            ],
        ),
        out_shape=jax.ShapeDtypeStruct((rows_max, d), F32),
        compiler_params=pltpu.CompilerParams(dimension_semantics=("arbitrary",), vmem_limit_bytes=VMEM_LIMIT),
        name="moe_experts",
    )(blk_expert, n_used, first, w_slot.astype(i32), nxt_expert.astype(i32), xs_hbm, w_gate_up, w_down)

    return pl.pallas_call(
        functools.partial(_combine_kernel, ln_rows=min(256, tile), max_big=max_big, max_small=max_small),
        grid_spec=pltpu.PrefetchScalarGridSpec(
            num_scalar_prefetch=3,
            grid=(n_tiles,),
            in_specs=[
                smem_spec(), smem_spec(), smem_spec(), smem_spec(),
                row_groups_spec,
                pl.BlockSpec(memory_space=pl.ANY),
                pl.BlockSpec((1, d), lambda t, *_: (0, 0)),
                pl.BlockSpec((1, d), lambda t, *_: (0, 0)),
            ],
            out_specs=row_groups_spec,
            scratch_shapes=[pltpu.VMEM((2, local_rows, d), F32), pltpu.SemaphoreType.DMA((2,))],
        ),
        out_shape=jax.ShapeDtypeStruct(x1_groups.shape, F32),
        compiler_params=pltpu.CompilerParams(dimension_semantics=("arbitrary",), vmem_limit_bytes=VMEM_LIMIT),
        name="moe_combine_ln2",
    )(flat(piece_loc), flat(piece_glob), flat(tot), pos[0], pos[1], r_wgt[0], r_wgt[1], x1_groups, ys_hbm,
      g2, b2).reshape(n, d)


def _block(x, w_in, b_in, lower_bound, hg_norm_g, rel_bias, w_proj_a, w_proj_b, w_out, ln1_g, ln1_b,
           w_group, b_group, w_expert, b_expert, w_gate_up, w_down, ln2_g, ln2_b, *, tm_proj, tm_merge):
    b, s, d = x.shape
    n = b * s
    n_hg = 4 * HG_HEADS * HG_DK
    n_mb = 3 * MB_HEADS * MB_DH
    n_gt = 2 * d
    x2 = x.reshape(n, d)
    hg, mb, gates = _in_proj(x2, w_in.astype(BF16), b_in.reshape(1, -1), n_hg, n_mb, n_gt, tm_proj)
    hg_o = _hgrn(hg.reshape(b, s, n_hg), lower_bound.reshape(1, -1), hg_norm_g.reshape(1, -1))
    mb_o = _moba(mb.reshape(b, s, n_mb), rel_bias)
    w_r = jnp.zeros((d, LANES), F32).at[:, :N_GROUPS].set(w_group).at[:, N_GROUPS:N_GROUPS + N_EXPERTS].set(w_expert)
    b_r = jnp.zeros((1, LANES), F32).at[0, :N_GROUPS].set(b_group).at[0, N_GROUPS:N_GROUPS + N_EXPERTS].set(b_expert)
    tm_merge = min(tm_merge, n)
    x1, route, cnt_run = _merge(hg_o.reshape(n, -1), mb_o.reshape(n, -1), gates, x2,
                                w_proj_a.astype(BF16), w_proj_b.astype(BF16), w_out.astype(BF16),
                                ln1_g.reshape(1, d), ln1_b.reshape(1, d), jnp.stack(_split2(w_r)), b_r,
                                tm_merge, _moe_tile(n))
    out = _moe(x1, route, cnt_run, w_gate_up, w_down, ln2_g.reshape(1, d), ln2_b.reshape(1, d))
    return out.reshape(b, s, d)


def kernel(x, w_in, b_in, lb_logits, hg_norm_g, rel_bias, w_proj_a, w_proj_b, w_out, ln1_g, ln1_b, w_group,
           b_group, w_expert, b_expert, w_gate_up, w_down, ln2_g, ln2_b):
    lower_bounds = jnp.cumsum(jax.nn.softmax(lb_logits.astype(F32), axis=0), axis=0)
    l = 0
    return _block(x, w_in[l], b_in[l], lower_bounds[l], hg_norm_g[l], rel_bias, w_proj_a[l], w_proj_b[l],
                  w_out[l], ln1_g[l], ln1_b[l], w_group[l], b_group[l], w_expert[l], b_expert[l],
                  w_gate_up[l], w_down[l], ln2_g[l], ln2_b[l], tm_proj=512, tm_merge=1024)
```

```python
import functools
import math

import numpy as np
import jax
import jax.numpy as jnp
from jax import lax
from jax.experimental import pallas as pl
from jax.experimental.pallas import tpu as pltpu

F32 = jnp.float32
BF16 = jnp.bfloat16

HG_HEADS = 4
HG_DK = 128
HG_CHUNK = 128
HG_GROUP = 8
MB_HEADS = 8
MB_DH = 64
MB_BLOCK = 256
MB_TOPK = 3
PIPE_AHEAD = 2
DEN_ROWS = 16
GATE_ROWS = 16
MERGE_SLAB = 512
ROUTE_FIELDS = 8
REL_BUCKETS = 32
REL_MAX_DIST = 128
N_GROUPS = 4
EXPERTS_PER_GROUP = 8
N_EXPERTS = N_GROUPS * EXPERTS_PER_GROUP
TOP_K = 2
EXPERT_HIDDEN = 512
MOE_TILE = 1024
SEG_PAD = 16
SEG_BIG = 64
EXP_ROWS = 512
DEPTH = 1
DN_ALPHA = (2.0 * DEPTH) ** 0.25
NORM_EPS = 1e-5
LANES = 128
SUBLANES = 8
VMEM_LIMIT = 56 * 1024 * 1024
NEG_INF = float("-inf")
LOG2_E = 1.4426950408889634


def _split2(a):
    hi = a.astype(BF16)
    lo = (a - hi.astype(F32)).astype(BF16)
    return hi, lo


def _split3(a):
    hi = a.astype(BF16)
    r = a - hi.astype(F32)
    mid = r.astype(BF16)
    lo = (r - mid.astype(F32)).astype(BF16)
    return hi, mid, lo


def _dot_nt(a, b):
    return lax.dot_general(a, b, (((1,), (1,)), ((), ())), preferred_element_type=F32)


def _dot_tn(a, b):
    return lax.dot_general(a, b, (((0,), (0,)), ((), ())), preferred_element_type=F32)


def _dot(a, b):
    return jnp.dot(a, b, preferred_element_type=F32)


def _in_proj_kernel(x_ref, w_ref, b_ref, hg_ref, mb_ref, gt_ref, *, col_chunk):
    xb = x_ref[...].astype(BF16)
    outs = ((hg_ref, 0), (mb_ref, hg_ref.shape[1]), (gt_ref, hg_ref.shape[1] + mb_ref.shape[1]))
    for o_ref, base in outs:
        for c0 in range(0, o_ref.shape[1], col_chunk):
            acc = _dot(xb, w_ref[:, base + c0:base + c0 + col_chunk])
            o_ref[:, c0:c0 + col_chunk] = acc + b_ref[:, base + c0:base + c0 + col_chunk]


def _in_proj(x2, w_bf, b_in, n_hg, n_mb, n_gt, tm):
    n, d = x2.shape
    cols = w_bf.shape[1]
    return pl.pallas_call(
        functools.partial(_in_proj_kernel, col_chunk=512),
        grid=(n // tm,),
        in_specs=[
            pl.BlockSpec((tm, d), lambda i: (i, 0)),
            pl.BlockSpec((d, cols), lambda i: (0, 0), pipeline_mode=pl.Buffered(1)),
            pl.BlockSpec((1, cols), lambda i: (0, 0)),
        ],
        out_specs=[
            pl.BlockSpec((tm, n_hg), lambda i: (i, 0)),
            pl.BlockSpec((tm, n_mb), lambda i: (i, 0)),
            pl.BlockSpec((tm, n_gt), lambda i: (i, 0)),
        ],
        out_shape=[
            jax.ShapeDtypeStruct((n, n_hg), F32),
            jax.ShapeDtypeStruct((n, n_mb), F32),
            jax.ShapeDtypeStruct((n, n_gt), F32),
        ],
        compiler_params=pltpu.CompilerParams(
            dimension_semantics=("parallel",), vmem_limit_bytes=VMEM_LIMIT),
        name="in_proj",
    )(x2, w_bf, b_in)


def _hgrn_tables():
    c = HG_CHUNK
    levels = [c >> (i + 1) for i in range(int(math.log2(c)))]
    masks = np.zeros((len(levels) + 1, c, c), np.float32)
    for li, m in enumerate(levels):
        for r in range(c):
            c0 = (r // (2 * m)) * (2 * m)
            if r - c0 >= m:
                masks[li, r, c0:c0 + m] = 1.0
    masks[len(levels)] = np.eye(c, dtype=np.float32)
    return np.tril(np.ones((c, c), np.float32)), masks, tuple(levels)


def _level_ref_rows(p, m):
    c, w = p.shape
    if 2 * m > SUBLANES:
        parts = [jnp.broadcast_to(p[c0 + m - 1:c0 + m, :], (2 * m, w)) for c0 in range(0, c, 2 * m)]
        return parts[0] if len(parts) == 1 else jnp.concatenate(parts, axis=0)
    p3 = p.reshape(c // SUBLANES, SUBLANES, w)
    sub = lax.broadcasted_iota(jnp.int32, p3.shape, 1)
    out = None
    for c0 in range(0, SUBLANES, 2 * m):
        b = jnp.broadcast_to(p3[:, c0 + m - 1:c0 + m, :], p3.shape)
        out = b if out is None else jnp.where(sub >= c0, b, out)
    return out.reshape(c, w)


def _hgrn_kernel(q_ref, f_ref, i_ref, g_ref, lb_ref, ng_ref, tril_ref, msk_ref, o_ref,
                 *, n_chunks, levels, group):
    c = HG_CHUNK
    dk = HG_DK
    assert n_chunks % group == 0
    lb = jnp.concatenate([lb_ref[...]] * group, axis=1)
    oml = 1.0 - lb
    ng = ng_ref[...]
    tril = tril_ref[...]
    n_levels = len(levels)

    def load(ref, r0):
        return jnp.concatenate([ref[0, pl.ds(r0 + u * c, c), :] for u in range(group)], axis=1)

    def lanes(a, u):
        return a[:, u * dk:(u + 1) * dk]

    def intra(r0):
        z = load(f_ref, r0)
        qr = load(q_ref, r0)
        lf = jnp.log(lb + oml * jax.nn.sigmoid(z))
        kk = oml * jax.nn.sigmoid(-z)
        qf = qr * jax.nn.sigmoid(qr)
        l_hi, l_mid, l_lo = _split3(lf)
        p = _dot(tril, l_hi) + _dot(tril, l_mid) + _dot(tril, l_lo)
        b_end = p[c - 1:c, :]
        qb = (qf * jnp.exp(p)).astype(BF16)
        kd = (kk * jnp.exp(b_end - p)).astype(BF16)
        dec = jnp.exp(b_end)
        qh = qf.astype(BF16)
        kh = kk.astype(BF16)
        scores = [msk_ref[n_levels] * _dot_nt(lanes(qh, u), lanes(kh, u)) for u in range(group)]
        row = lax.broadcasted_iota(jnp.int32, p.shape, 0)
        for li, m in enumerate(levels):
            ex = jnp.exp2(jnp.abs(p - _level_ref_rows(p, m)) * (-LOG2_E))
            qk = (jnp.where((row & (2 * m - 1)) >= m, qf, kk) * ex).astype(BF16)
            for u in range(group):
                scores[u] = scores[u] + msk_ref[li] * _dot_nt(lanes(qk, u), lanes(qk, u))
        vbs = [i_ref[0, pl.ds(r0 + u * c, c), :].astype(BF16) for u in range(group)]
        o_intra = [_dot(scores[u].astype(BF16), vbs[u]) for u in range(group)]
        return qb, kd, dec, vbs, o_intra

    def chain(r0, staged, st):
        qb, kd, dec, vbs, o_intra = staged
        for u in range(group):
            rows = pl.ds(r0 + u * c, c)
            g = g_ref[0, rows, :]
            o = _dot_nt(lanes(qb, u), st.astype(BF16)) + o_intra[u]
            st = st * lanes(dec, u) + _dot_tn(vbs[u], lanes(kd, u))
            o = o * lax.rsqrt(jnp.mean(o * o, axis=-1, keepdims=True) + NORM_EPS)
            o_ref[0, rows, :] = o * ng * (g * jax.nn.sigmoid(g))
        return st

    n_groups = n_chunks // group
    st = jnp.zeros((HG_DK, HG_DK), F32)
    staged = intra(0)
    for gi in range(n_groups):
        nxt = intra((gi + 1) * group * c) if gi + 1 < n_groups else None
        st = chain(gi * group * c, staged, st)
        staged = nxt


def _hgrn(hg3, lb_row, ng_row):
    b, s, _ = hg3.shape
    tril, masks, levels = _hgrn_tables()
    mst = jnp.asarray(tril, BF16)
    msk = jnp.asarray(masks, F32)
    h = HG_HEADS

    def col(off):
        return pl.BlockSpec((1, s, HG_DK), lambda bi, hi: (bi, 0, off + hi))

    return pl.pallas_call(
        functools.partial(_hgrn_kernel, n_chunks=s // HG_CHUNK, levels=levels,
                          group=math.gcd(s // HG_CHUNK, HG_GROUP)),
        grid=(b, h),
        in_specs=[
            col(0), col(h), col(2 * h), col(3 * h),
            pl.BlockSpec((1, HG_DK), lambda bi, hi: (0, hi)),
            pl.BlockSpec((1, HG_DK), lambda bi, hi: (0, hi)),
            pl.BlockSpec(mst.shape, lambda bi, hi: (0, 0)),
            pl.BlockSpec(msk.shape, lambda bi, hi: (0, 0, 0)),
        ],
        out_specs=pl.BlockSpec((1, s, HG_DK), lambda bi, hi: (bi, 0, hi)),
        out_shape=jax.ShapeDtypeStruct((b, s, h * HG_DK), F32),
        compiler_params=pltpu.CompilerParams(
            dimension_semantics=("parallel", "parallel"), vmem_limit_bytes=VMEM_LIMIT),
        name="hgrn2",
    )(hg3, hg3, hg3, hg3, lb_row, ng_row, mst, msk)


def _t5_bucket_np(dist):
    max_exact = REL_BUCKETS // 2
    d = np.maximum(dist, 1).astype(np.float32)
    log_part = max_exact + (np.log(d / np.float32(max_exact)) / np.float32(math.log(REL_MAX_DIST / max_exact))
                            * np.float32(REL_BUCKETS - max_exact)).astype(np.int32)
    return np.where(dist < max_exact, dist, np.minimum(log_part, REL_BUCKETS - 1))


def _moba_kernel(q_ref, k_ref, v_ref, avg_ref, bkt_ref, rb_ref, o_ref, own_ref, prev_ref, *, n_blocks):
    blk = MB_BLOCK
    scale = MB_DH ** -0.5 * LOG2_E
    avg = avg_ref[...]
    hp = LANES // MB_DH

    @pl.when(pl.program_id(1) == 0)
    def _():
        causal = (lax.broadcasted_iota(jnp.int32, (blk, blk), 0) <= lax.broadcasted_iota(jnp.int32, (blk, blk), 1))
        for hh in range(hp):
            head = pl.program_id(0) * hp + hh
            own_t = jnp.zeros((blk, blk), F32)
            prev_t = jnp.zeros((blk, blk), F32)
            for bk in range(REL_BUCKETS):
                val = rb_ref[bk, head] * LOG2_E
                own_t = jnp.where(bkt_ref[0] == bk, val, own_t)
                prev_t = jnp.where(bkt_ref[1] == bk, val, prev_t)
            own_ref[hh] = jnp.where(causal, own_t, NEG_INF)
            prev_ref[hh] = prev_t

    grow = lax.broadcasted_iota(jnp.int32, (GATE_ROWS, blk), 0)
    vt_all = v_ref[0].T

    def fold(a):
        return a.reshape(blk // SUBLANES, SUBLANES, blk)

    heads = []
    for hh in range(hp):
        ls = slice(hh * MB_DH, (hh + 1) * MB_DH)
        k_h = k_ref[0, :, ls]
        k_hi, k_lo = _split2(k_h)
        k_mean = (_dot(avg, k_hi) + _dot(avg, k_lo))[:GATE_ROWS]
        vt = jnp.concatenate([vt_all[hh * MB_DH:(hh + 1) * MB_DH, :], jnp.ones((DEN_ROWS, vt_all.shape[1]), F32)],
                             axis=0).astype(BF16)
        far_bias = rb_ref[REL_BUCKETS - 1, pl.program_id(0) * hp + hh] * LOG2_E
        heads.append((ls, _split2(k_mean), k_h.astype(BF16), vt, far_bias))

    def logits(hh, i):
        ls, (km_hi, km_lo), kb, _, far_bias = heads[hh]
        qi = q_ref[0, i * blk:(i + 1) * blk, ls] * scale
        st = _dot_nt(kb[:(i + 1) * blk], qi.astype(BF16))
        sel = None
        if i > 0:
            q_hi, q_lo = _split2(qi)
            gate = _dot_nt(km_hi, q_hi) + _dot_nt(km_lo, q_hi) + _dot_nt(km_hi, q_lo)
            gate = jnp.where(grow < i, gate, NEG_INF)
            rank = jnp.zeros((GATE_ROWS, blk), F32)
            if i > MB_TOPK:
                for j2 in range(i):
                    gj = jnp.broadcast_to(gate[j2:j2 + 1, :], (GATE_ROWS, blk))
                    tie = jnp.where(grow > j2, 1.0, 0.0)
                    rank = rank + jnp.where(gj > gate, 1.0, jnp.where(gj == gate, tie, 0.0))
            sel = jnp.where(rank < MB_TOPK, gate, NEG_INF) > NEG_INF
        pieces = []
        for j in range(i + 1):
            sj = st[j * blk:(j + 1) * blk, :]
            if j == i:
                pieces.append(sj + own_ref[hh])
            elif j == i - 1:
                pieces.append(sj + prev_ref[hh] + jnp.where(sel[j:j + 1, :], 0.0, NEG_INF))
            else:
                pieces.append(sj + jnp.where(sel[j:j + 1, :], far_bias, NEG_INF))
        mx8 = fold(pieces[0]).max(axis=0)
        for p in pieces[1:]:
            mx8 = jnp.maximum(mx8, fold(p).max(axis=0))
        return pieces, mx8.max(axis=0, keepdims=True)

    def attend(hh, pieces, mx):
        vt = heads[hh][3]
        weights = jnp.concatenate([jnp.exp2(p - mx).astype(BF16) for p in pieces], axis=0)
        acc = _dot(vt[:, :len(pieces) * blk], weights)
        return acc[:MB_DH] / acc[MB_DH:MB_DH + 1]

    items = [(i, hh) for i in range(n_blocks) for hh in range(hp)]
    staged = [logits(hh, i) for i, hh in items[:PIPE_AHEAD]]
    done = {}
    for n, (i, hh) in enumerate(items):
        if n + PIPE_AHEAD < len(items):
            staged.append(logits(items[n + PIPE_AHEAD][1], items[n + PIPE_AHEAD][0]))
        done[hh] = attend(hh, *staged.pop(0))
        if hh == hp - 1:
            o_ref[0, i * blk:(i + 1) * blk, :] = jnp.concatenate([done[h] for h in range(hp)], axis=0).T


def _moba(mb3, rel_bias):
    b, s, _ = mb3.shape
    assert s % MB_BLOCK == 0
    nb = s // MB_BLOCK
    hp = LANES // MB_DH
    n_hp = MB_HEADS // hp
    t = np.arange(MB_BLOCK)
    d_own = t[:, None] - t[None, :]
    buckets = np.stack([_t5_bucket_np(np.maximum(d_own, 0)), _t5_bucket_np(d_own + MB_BLOCK)]).astype(np.int32)
    assert _t5_bucket_np(np.array([MB_BLOCK + 1]))[0] == REL_BUCKETS - 1
    assert nb <= GATE_ROWS
    bkt = jnp.asarray(buckets.transpose(0, 2, 1))
    avg_np = np.zeros((LANES, s), np.float32)
    for j in range(nb):
        avg_np[j, j * MB_BLOCK:(j + 1) * MB_BLOCK] = 1.0 / MB_BLOCK
    avg = jnp.asarray(avg_np, BF16)

    def col(off):
        return pl.BlockSpec((1, s, LANES), lambda hi, bi: (bi, 0, off + hi))

    return pl.pallas_call(
        functools.partial(_moba_kernel, n_blocks=nb),
        grid=(n_hp, b),
        in_specs=[
            col(0), col(n_hp), col(2 * n_hp),
            pl.BlockSpec(avg.shape, lambda hi, bi: (0, 0)),
            pl.BlockSpec(bkt.shape, lambda hi, bi: (0, 0, 0)),
            pl.BlockSpec(memory_space=pltpu.SMEM),
        ],
        out_specs=pl.BlockSpec((1, s, LANES), lambda hi, bi: (bi, 0, hi)),
        out_shape=jax.ShapeDtypeStruct((b, s, MB_HEADS * MB_DH), F32),
        scratch_shapes=[pltpu.VMEM((hp, MB_BLOCK, MB_BLOCK), F32), pltpu.VMEM((hp, MB_BLOCK, MB_BLOCK), F32)],
        compiler_params=pltpu.CompilerParams(
            dimension_semantics=("parallel", "arbitrary"), vmem_limit_bytes=VMEM_LIMIT),
        name="moba",
    )(mb3, mb3, mb3, avg, bkt, rel_bias.astype(F32))


def _layer_norm(x, g, b):
    mu = jnp.mean(x, axis=-1, keepdims=True)
    xc = x - mu
    var = jnp.mean(xc * xc, axis=-1, keepdims=True)
    return xc * lax.rsqrt(var + NORM_EPS) * g + b


def _merge_kernel(hg_ref, mb_ref, ga_ref, gb_ref, x_ref, wa_ref, wb_ref, wo_ref, g1_ref, b1_ref,
                  wr_ref, br_ref, tri_ref, x1_ref, route_ref, cnt_ref, run_ref, *, blocks_per_moe_tile, parts):
    tm = x_ref.shape[0]
    pr = tm // parts
    lane = lax.broadcasted_iota(jnp.int32, (pr, LANES), 1)

    @pl.when(pl.program_id(0) % blocks_per_moe_tile == 0)
    def _():
        run_ref[...] = jnp.zeros_like(run_ref)

    def mix(rows):
        ya = _dot(hg_ref[rows, :].astype(BF16), wa_ref[...])
        yb = _dot(mb_ref[rows, :].astype(BF16), wb_ref[...])
        mixed_in = jax.nn.sigmoid(ga_ref[rows, :]) * ya + jax.nn.sigmoid(gb_ref[rows, :]) * yb
        mixed = _dot(mixed_in.astype(BF16), wo_ref[...])
        x1 = _layer_norm(DN_ALPHA * x_ref[rows, :] + mixed, g1_ref[...], b1_ref[...])
        x1_ref[rows, :] = x1
        return x1

    def route_rows(rows, x1, run):
        x_hi, x_lo = _split2(x1)
        w_hi, w_lo = wr_ref[0], wr_ref[1]
        hr = pr // 2
        logits = jnp.concatenate(
            [_dot(x_hi[r:r + hr], w_hi) + _dot(x_hi[r:r + hr], w_lo) + _dot(x_lo[r:r + hr], w_hi) for r in (0, hr)],
            axis=0) + br_ref[...]
        glog = jnp.where(lane < N_GROUPS, logits, NEG_INF)
        gmax = jnp.max(glog, axis=-1, keepdims=True)
        grp = jnp.min(jnp.where(glog == gmax, lane, LANES), axis=-1, keepdims=True)
        p_grp = 1.0 / jnp.sum(jnp.exp(glog - gmax), axis=-1, keepdims=True)
        e_lo = N_GROUPS + grp * EXPERTS_PER_GROUP
        elog = jnp.where(jnp.logical_and(lane >= e_lo, lane < e_lo + EXPERTS_PER_GROUP), logits, NEG_INF)
        m1 = jnp.max(elog, axis=-1, keepdims=True)
        i1 = jnp.min(jnp.where(elog == m1, lane, LANES), axis=-1, keepdims=True)
        elog2 = jnp.where(lane == i1, NEG_INF, elog)
        m2 = jnp.max(elog2, axis=-1, keepdims=True)
        i2 = jnp.min(jnp.where(elog2 == m2, lane, LANES), axis=-1, keepdims=True)
        e2 = jnp.exp(m2 - m1)
        w1 = p_grp / (1.0 + e2)
        w2 = p_grp * e2 / (1.0 + e2)
        oh1 = jnp.where(lane == i1 - N_GROUPS, 1.0, 0.0)
        oh2 = jnp.where(lane == i2 - N_GROUPS, 1.0, 0.0)
        both = oh1 + oh2
        before = _dot(tri_ref[...], both.astype(BF16)) + run
        r1 = jnp.sum(before * oh1, axis=-1, keepdims=True)
        r2 = jnp.sum(before * oh2, axis=-1, keepdims=True)
        cols = ((i1 - N_GROUPS).astype(F32), (i2 - N_GROUPS).astype(F32), w1, w2, r1, r2)
        route = jnp.zeros((pr, LANES), F32)
        for li, col in enumerate(cols):
            route = jnp.where(lane == li, col, route)
        route_ref[:, rows] = route.T[:ROUTE_FIELDS, :]
        return run + jnp.sum(both, axis=0, keepdims=True)

    slabs = [slice(h * pr, (h + 1) * pr) for h in range(parts)]
    mixed = [mix(rows) for rows in slabs]
    run = run_ref[...]
    for rows, x1 in zip(slabs, mixed):
        run = route_rows(rows, x1, run)
    run_ref[...] = run
    cnt_ref[0] = run


def _merge(hg_o, mb_o, gates, x2, wa, wb, wo, g1, b1, wr, br, tm, moe_tile):
    n, d = x2.shape
    wa_n = hg_o.shape[1]
    wb_n = mb_o.shape[1]
    assert moe_tile % tm == 0
    parts = max(1, tm // MERGE_SLAB)
    tri = jnp.asarray(np.tril(np.ones((tm // parts, tm // parts), np.float32), -1), BF16)

    def full(a):
        nd = a.ndim
        return pl.BlockSpec(a.shape, lambda i: (0,) * nd, pipeline_mode=pl.Buffered(1))

    return pl.pallas_call(
        functools.partial(_merge_kernel, blocks_per_moe_tile=moe_tile // tm, parts=parts),
        grid=(n // tm,),
        in_specs=[
            pl.BlockSpec((tm, wa_n), lambda i: (i, 0)),
            pl.BlockSpec((tm, wb_n), lambda i: (i, 0)),
            pl.BlockSpec((tm, d), lambda i: (i, 0)),
            pl.BlockSpec((tm, d), lambda i: (i, 1)),
            pl.BlockSpec((tm, d), lambda i: (i, 0)),
            full(wa), full(wb), full(wo), full(g1), full(b1), full(wr), full(br), full(tri),
        ],
        out_specs=[
            pl.BlockSpec((tm, d), lambda i: (i, 0)),
            pl.BlockSpec((ROUTE_FIELDS, tm), lambda i: (0, i)),
            pl.BlockSpec((1, 1, LANES), lambda i: (i, 0, 0)),
        ],
        out_shape=[
            jax.ShapeDtypeStruct((n, d), F32),
            jax.ShapeDtypeStruct((ROUTE_FIELDS, n), F32),
            jax.ShapeDtypeStruct((n // tm, 1, LANES), F32),
        ],
        scratch_shapes=[pltpu.VMEM((1, LANES), F32)],
        compiler_params=pltpu.CompilerParams(
            dimension_semantics=("arbitrary",), vmem_limit_bytes=VMEM_LIMIT),
        name="merge_ln1_router",
    )(hg_o, mb_o, gates, gates, x2, wa, wb, wo, g1, b1, wr, br, tri)


def _tile_copies(t, loc_ref, glob_ref, tot_ref, max_big, max_small, make_copy, act):
    base = t * (max_big + max_small)

    def big(q, carry):
        act(make_copy(loc_ref[base + q], glob_ref[base + q], SEG_BIG))
        return carry
    lax.fori_loop(0, tot_ref[2 * t], big, 0)

    def small(q, carry):
        act(make_copy(loc_ref[base + max_big + q], glob_ref[base + max_big + q], SEG_PAD))
        return carry
    lax.fori_loop(0, tot_ref[2 * t + 1], small, 0)


def _dispatch_kernel(loc_ref, glob_ref, tot_ref, gap_ref, p0_ref, p1_ref, x1_ref, xs_hbm,
                     xs_ref, xb_ref, zero_ref, sem, *, cv_rows, max_big, max_small):
    ti = pl.program_id(0)
    n_tiles = pl.num_programs(0)
    groups, _, d = x1_ref.shape

    def out_copy(loc, glob, size):
        src = pl.multiple_of(loc, SEG_PAD)
        dst = pl.multiple_of(glob, SEG_PAD)
        return pltpu.make_async_copy(xb_ref.at[pl.ds(src, size), :], xs_hbm.at[pl.ds(dst, size), :], sem.at[0])

    def tile_copies(t, act):
        _tile_copies(t, loc_ref, glob_ref, tot_ref, max_big, max_small, out_copy, act)

    @pl.when(ti == 0)
    def _():
        def zero(c, carry):
            r0 = pl.multiple_of(c * cv_rows, cv_rows)
            xs_ref[pl.ds(r0, cv_rows), :] = jnp.zeros((cv_rows, d), F32)
            return carry
        lax.fori_loop(0, xs_ref.shape[0] // cv_rows, zero, 0)
        zero_ref[...] = jnp.zeros_like(zero_ref)

    def sort_rows(grp, carry):
        t0 = grp * SUBLANES
        for k in range(SUBLANES):
            row = x1_ref[grp, pl.ds(k, 1), :]
            xs_ref[pl.ds(p0_ref[0, 0, t0 + k], 1), :] = row
            xs_ref[pl.ds(p1_ref[0, 0, t0 + k], 1), :] = row
        return carry
    lax.fori_loop(0, groups, sort_rows, 0)

    @pl.when(ti > 0)
    def _():
        tile_copies(ti - 1, lambda cp: cp.wait())

    def convert(c, carry):
        r0 = pl.multiple_of(c * cv_rows, cv_rows)
        xb_ref[pl.ds(r0, cv_rows), :] = xs_ref[pl.ds(r0, cv_rows), :].astype(BF16)
        return carry
    lax.fori_loop(0, xs_ref.shape[0] // cv_rows, convert, 0)

    tile_copies(ti, lambda cp: cp.start())

    @pl.when(ti == n_tiles - 1)
    def _():
        tile_copies(ti, lambda cp: cp.wait())

        def fill_copy(row):
            return pltpu.make_async_copy(zero_ref, xs_hbm.at[pl.ds(pl.multiple_of(row, SEG_PAD), SEG_PAD), :],
                                         sem.at[1])

        def fill(ex, carry):
            def one(c, carry2):
                fill_copy(gap_ref[2 * ex] + c * SEG_PAD).start()
                return carry2
            lax.fori_loop(0, gap_ref[2 * ex + 1], one, 0)
            return carry
        lax.fori_loop(0, N_EXPERTS, fill, 0)

        def fill_wait(ex, carry):
            def one(c, carry2):
                fill_copy(0).wait()
                return carry2
            lax.fori_loop(0, gap_ref[2 * ex + 1], one, 0)
            return carry
        lax.fori_loop(0, N_EXPERTS, fill_wait, 0)


def _expert_kernel(be_ref, nu_ref, first_ref, slot_ref, nxt_ref, x_ref, wgu_hbm, wd_hbm, y_ref,
                   wgu_f32, wd_f32, wgu_bf, wd_bf, sem, *, cast_rows):
    i = pl.program_id(0)

    def fetch(expert, slot):
        return (pltpu.make_async_copy(wgu_hbm.at[expert], wgu_f32.at[slot], sem.at[0, slot]),
                pltpu.make_async_copy(wd_hbm.at[expert], wd_f32.at[slot], sem.at[1, slot]))

    @pl.when(i < nu_ref[0])
    def _():
        slot = slot_ref[i]

        @pl.when(i == 0)
        def _():
            for cp in fetch(be_ref[0], 0):
                cp.start()

        @pl.when(first_ref[i] == 1)
        def _():
            for cp in fetch(be_ref[i], slot):
                cp.wait()
            for r0 in range(0, wgu_bf.shape[0], cast_rows):
                wgu_bf[r0:r0 + cast_rows, :] = wgu_f32[slot, r0:r0 + cast_rows, :].astype(BF16)
            for r0 in range(0, wd_bf.shape[0], cast_rows):
                wd_bf[r0:r0 + cast_rows, :] = wd_f32[slot, r0:r0 + cast_rows, :].astype(BF16)

            @pl.when(nxt_ref[i] >= 0)
            def _():
                for cp in fetch(nxt_ref[i], 1 - slot):
                    cp.start()

        gu = _dot(x_ref[...], wgu_bf[...])
        gate = gu[:, :EXPERT_HIDDEN]
        up = gu[:, EXPERT_HIDDEN:]
        hdn = (gate * jax.nn.sigmoid(gate) * up).astype(BF16)
        y_ref[...] = _dot(hdn, wd_bf[...])


def _combine_kernel(loc_ref, glob_ref, tot_ref, p0_ref, p1_ref, w0_ref, w1_ref, x1_ref, ys_hbm, g2_ref, b2_ref,
                    o_ref, xs_ref, sem, *, ln_rows, max_big, max_small):
    ti = pl.program_id(0)
    n_tiles = pl.num_programs(0)
    groups, _, d = x1_ref.shape
    slot = ti % 2

    def tile_copies(t, sl, act):
        def in_copy(loc, glob, size):
            src = pl.multiple_of(glob, SEG_PAD)
            dst = pl.multiple_of(loc, SEG_PAD)
            return pltpu.make_async_copy(ys_hbm.at[pl.ds(src, size), :], xs_ref.at[sl, pl.ds(dst, size), :],
                                         sem.at[sl])
        _tile_copies(t, loc_ref, glob_ref, tot_ref, max_big, max_small, in_copy, act)

    @pl.when(ti == 0)
    def _():
        tile_copies(0, 0, lambda cp: cp.start())

    @pl.when(ti + 1 < n_tiles)
    def _():
        tile_copies(ti + 1, 1 - slot, lambda cp: cp.start())

    tile_copies(ti, slot, lambda cp: cp.wait())

    def combine(grp, carry):
        t0 = grp * SUBLANES
        for k in range(SUBLANES):
            t = t0 + k
            o_ref[grp, pl.ds(k, 1), :] = (DN_ALPHA * x1_ref[grp, pl.ds(k, 1), :]
                                          + w0_ref[0, 0, t] * xs_ref[slot, pl.ds(p0_ref[0, 0, t], 1), :]
                                          + w1_ref[0, 0, t] * xs_ref[slot, pl.ds(p1_ref[0, 0, t], 1), :])
        return carry
    lax.fori_loop(0, groups, combine, 0, unroll=4)

    ln_groups = ln_rows // SUBLANES

    def norm(c, carry):
        g0 = pl.multiple_of(c * ln_groups, ln_groups)
        rows = o_ref[pl.ds(g0, ln_groups), :, :].reshape(ln_rows, d)
        o_ref[pl.ds(g0, ln_groups), :, :] = _layer_norm(rows, g2_ref[...], b2_ref[...]).reshape(ln_groups, SUBLANES, d)
        return carry
    lax.fori_loop(0, groups // ln_groups, norm, 0)


def _moe_tile(n):
    return min(MOE_TILE, n)


def _round_up(a, m):
    return (a + m - 1) // m * m


def _moe(x1, route, cnt_run, w_gate_up, w_down, g2, b2):
    n, d = x1.shape
    tile = _moe_tile(n)
    assert n % tile == 0
    n_tiles = n // tile
    i32 = jnp.int32
    cnt = cnt_run.reshape(n_tiles, -1, LANES)[:, -1, :N_EXPERTS].astype(i32)
    seg = _round_up(cnt, SEG_PAD)
    lbase = jnp.cumsum(seg, axis=1) - seg
    used = jnp.sum(seg, axis=0)
    region = _round_up(used, EXP_ROWS)
    e_start = jnp.cumsum(region) - region
    gseg = e_start[None, :] + jnp.cumsum(seg, axis=0) - seg
    n_bigs = seg // SEG_BIG
    n_smalls = (seg - n_bigs * SEG_BIG) // SEG_PAD
    max_big = TOP_K * tile // SEG_BIG
    max_small = N_EXPERTS * (SEG_BIG // SEG_PAD - 1)

    def piece_table(counts, first_off, step, width):
        incl = jnp.cumsum(counts, axis=1)
        excl = incl - counts
        q = jnp.arange(width, dtype=i32)[None, :, None]
        owner = (q >= excl[:, None, :]) & (q < incl[:, None, :])
        pick = lambda a: jnp.sum(jnp.where(owner, a[:, None, :], 0), axis=-1)
        off = pick(first_off) + (q[:, :, 0] - pick(excl)) * step
        return pick(lbase) + off, pick(gseg) + off

    big_loc, big_glob = piece_table(n_bigs, jnp.zeros_like(seg), SEG_BIG, max_big)
    small_loc, small_glob = piece_table(n_smalls, n_bigs * SEG_BIG, SEG_PAD, max_small)
    piece_loc = jnp.concatenate([big_loc, small_loc], axis=1)
    piece_glob = jnp.concatenate([big_glob, small_glob], axis=1)
    tot = jnp.stack([jnp.sum(n_bigs, axis=1), jnp.sum(n_smalls, axis=1)], axis=1)
    gap = jnp.stack([e_start + used, (region - used) // SEG_PAD], axis=1)
    rows_max = _round_up(TOP_K * n + n_tiles * N_EXPERTS * (SEG_PAD - 1) + N_EXPERTS * (EXP_ROWS - 1), EXP_ROWS)
    n_blocks = rows_max // EXP_ROWS
    n_used = (jnp.sum(region) // EXP_ROWS).astype(i32).reshape(1)
    blk_row = jnp.arange(n_blocks, dtype=i32)[:, None] * EXP_ROWS
    blk_expert = jnp.minimum(jnp.sum((blk_row >= jnp.cumsum(region)[None, :]).astype(i32), axis=1), N_EXPERTS - 1)
    r_exp = route[0:TOP_K].astype(i32).reshape(TOP_K, n_tiles, tile)
    r_rank = route[2 * TOP_K:3 * TOP_K].astype(i32).reshape(TOP_K, n_tiles, tile)
    seg_start = jnp.sum(jnp.where(r_exp[..., None] == jnp.arange(N_EXPERTS, dtype=i32), lbase[None, :, None, :], 0),
                        axis=-1)
    pos = (seg_start + r_rank).reshape(TOP_K, n_tiles, 1, tile)
    r_wgt = route[TOP_K:2 * TOP_K].reshape(TOP_K, n_tiles, 1, tile)
    local_rows = _round_up(TOP_K * tile + N_EXPERTS * (SEG_PAD - 1), 256)
    flat = lambda a: a.reshape(-1).astype(i32)

    def smem_spec():
        return pl.BlockSpec((1, 1, tile), lambda t, *_: (t, 0, 0), memory_space=pltpu.SMEM)

    x1_groups = x1.reshape(n // SUBLANES, SUBLANES, d)
    row_groups_spec = pl.BlockSpec((tile // SUBLANES, SUBLANES, d), lambda t, *_: (t, 0, 0))

    xs_hbm = pl.pallas_call(
        functools.partial(_dispatch_kernel, cv_rows=256, max_big=max_big, max_small=max_small),
        grid_spec=pltpu.PrefetchScalarGridSpec(
            num_scalar_prefetch=4,
            grid=(n_tiles,),
            in_specs=[smem_spec(), smem_spec(), row_groups_spec],
            out_specs=pl.BlockSpec(memory_space=pl.ANY),
            scratch_shapes=[
                pltpu.VMEM((local_rows, d), F32),
                pltpu.VMEM((local_rows, d), BF16),
                pltpu.VMEM((SEG_PAD, d), BF16),
                pltpu.SemaphoreType.DMA((2,)),
            ],
        ),
        out_shape=jax.ShapeDtypeStruct((rows_max, d), BF16),
        compiler_params=pltpu.CompilerParams(dimension_semantics=("arbitrary",), vmem_limit_bytes=VMEM_LIMIT),
        name="moe_dispatch",
    )(flat(piece_loc), flat(piece_glob), flat(tot), flat(gap), pos[0], pos[1], x1_groups)

    def blk(i, be, nu):
        return jnp.minimum(i, nu[0] - 1)

    blk_idx = jnp.arange(n_blocks, dtype=i32)
    first = ((blk_idx == 0) | (blk_expert != jnp.roll(blk_expert, 1))).astype(i32)
    w_slot = (jnp.cumsum(first) - 1) % 2
    e_idx = jnp.arange(N_EXPERTS, dtype=i32)
    later_used = (region > 0)[None, :] & (e_idx[None, :] > e_idx[:, None])
    nxt_of = jnp.min(jnp.where(later_used, e_idx[None, :], N_EXPERTS), axis=1)
    nxt_of = jnp.where(nxt_of < N_EXPERTS, nxt_of, -1)
    nxt_expert = jnp.sum(jnp.where(blk_expert[:, None] == e_idx[None, :], nxt_of[None, :], 0), axis=1)

    ys_hbm = pl.pallas_call(
        functools.partial(_expert_kernel, cast_rows=256),
        grid_spec=pltpu.PrefetchScalarGridSpec(
            num_scalar_prefetch=5,
            grid=(n_blocks,),
            in_specs=[
                pl.BlockSpec((EXP_ROWS, d), lambda i, be, nu, *_: (blk(i, be, nu), 0)),
                pl.BlockSpec(memory_space=pl.ANY),
                pl.BlockSpec(memory_space=pl.ANY),
            ],
            out_specs=pl.BlockSpec((EXP_ROWS, d), lambda i, be, nu, *_: (blk(i, be, nu), 0)),
            scratch_shapes=[
                pltpu.VMEM((2, d, 2 * EXPERT_HIDDEN), F32), pltpu.VMEM((2, EXPERT_HIDDEN, d), F32),
                pltpu.VMEM((d, 2 * EXPERT_HIDDEN), BF16), pltpu.VMEM((EXPERT_HIDDEN, d), BF16),
                pltpu.SemaphoreType.DMA((2, 2)),
            ],
        ),
        out_shape=jax.ShapeDtypeStruct((rows_max, d), F32),
        compiler_params=pltpu.CompilerParams(dimension_semantics=("arbitrary",), vmem_limit_bytes=VMEM_LIMIT),
        name="moe_experts",
    )(blk_expert, n_used, first, w_slot.astype(i32), nxt_expert.astype(i32), xs_hbm, w_gate_up, w_down)

    return pl.pallas_call(
        functools.partial(_combine_kernel, ln_rows=min(256, tile), max_big=max_big, max_small=max_small),
        grid_spec=pltpu.PrefetchScalarGridSpec(
            num_scalar_prefetch=3,
            grid=(n_tiles,),
            in_specs=[
                smem_spec(), smem_spec(), smem_spec(), smem_spec(),
                row_groups_spec,
                pl.BlockSpec(memory_space=pl.ANY),
                pl.BlockSpec((1, d), lambda t, *_: (0, 0)),
                pl.BlockSpec((1, d), lambda t, *_: (0, 0)),
            ],
            out_specs=row_groups_spec,
            scratch_shapes=[pltpu.VMEM((2, local_rows, d), F32), pltpu.SemaphoreType.DMA((2,))],
        ),
        out_shape=jax.ShapeDtypeStruct(x1_groups.shape, F32),
        compiler_params=pltpu.CompilerParams(dimension_semantics=("arbitrary",), vmem_limit_bytes=VMEM_LIMIT),
        name="moe_combine_ln2",
    )(flat(piece_loc), flat(piece_glob), flat(tot), pos[0], pos[1], r_wgt[0], r_wgt[1], x1_groups, ys_hbm,
      g2, b2).reshape(n, d)


def _block(x, w_in, b_in, lower_bound, hg_norm_g, rel_bias, w_proj_a, w_proj_b, w_out, ln1_g, ln1_b,
           w_group, b_group, w_expert, b_expert, w_gate_up, w_down, ln2_g, ln2_b, *, tm_proj, tm_merge):
    b, s, d = x.shape
    n = b * s
    n_hg = 4 * HG_HEADS * HG_DK
    n_mb = 3 * MB_HEADS * MB_DH
    n_gt = 2 * d
    x2 = x.reshape(n, d)
    hg, mb, gates = _in_proj(x2, w_in.astype(BF16), b_in.reshape(1, -1), n_hg, n_mb, n_gt, tm_proj)
    hg_o = _hgrn(hg.reshape(b, s, n_hg), lower_bound.reshape(1, -1), hg_norm_g.reshape(1, -1))
    mb_o = _moba(mb.reshape(b, s, n_mb), rel_bias)
    w_r = jnp.zeros((d, LANES), F32).at[:, :N_GROUPS].set(w_group).at[:, N_GROUPS:N_GROUPS + N_EXPERTS].set(w_expert)
    b_r = jnp.zeros((1, LANES), F32).at[0, :N_GROUPS].set(b_group).at[0, N_GROUPS:N_GROUPS + N_EXPERTS].set(b_expert)
    tm_merge = min(tm_merge, n)
    x1, route, cnt_run = _merge(hg_o.reshape(n, -1), mb_o.reshape(n, -1), gates, x2,
                                w_proj_a.astype(BF16), w_proj_b.astype(BF16), w_out.astype(BF16),
                                ln1_g.reshape(1, d), ln1_b.reshape(1, d), jnp.stack(_split2(w_r)), b_r,
                                tm_merge, _moe_tile(n))
    out = _moe(x1, route, cnt_run, w_gate_up, w_down, ln2_g.reshape(1, d), ln2_b.reshape(1, d))
    return out.reshape(b, s, d)


def kernel(x, w_in, b_in, lb_logits, hg_norm_g, rel_bias, w_proj_a, w_proj_b, w_out, ln1_g, ln1_b, w_group,
           b_group, w_expert, b_expert, w_gate_up, w_down, ln2_g, ln2_b):
    lower_bounds = jnp.cumsum(jax.nn.softmax(lb_logits.astype(F32), axis=0), axis=0)
    l = 0
    return _block(x, w_in[l], b_in[l], lower_bounds[l], hg_norm_g[l], rel_bias, w_proj_a[l], w_proj_b[l],
                  w_out[l], ln1_g[l], ln1_b[l], w_group[l], b_group[l], w_expert[l], b_expert[l],
                  w_gate_up[l], w_down[l], ln2_g[l], ln2_b[l], tm_proj=512, tm_merge=1024)
```

```python
import functools
import math

import numpy as np
import jax
import jax.numpy as jnp
from jax import lax
from jax.experimental import pallas as pl
from jax.experimental.pallas import tpu as pltpu

F32 = jnp.float32
BF16 = jnp.bfloat16

HG_HEADS = 4
HG_DK = 128
HG_CHUNK = 128
HG_GROUP = 8
MB_HEADS = 8
MB_DH = 64
MB_BLOCK = 256
MB_TOPK = 3
PIPE_AHEAD = 2
DEN_ROWS = 16
GATE_ROWS = 16
MERGE_SLAB = 512
ROUTE_FIELDS = 8
REL_BUCKETS = 32
REL_MAX_DIST = 128
N_GROUPS = 4
EXPERTS_PER_GROUP = 8
N_EXPERTS = N_GROUPS * EXPERTS_PER_GROUP
TOP_K = 2
EXPERT_HIDDEN = 512
MOE_TILE = 1024
SEG_PAD = 16
SEG_BIG = 64
EXP_ROWS = 512
DEPTH = 1
DN_ALPHA = (2.0 * DEPTH) ** 0.25
NORM_EPS = 1e-5
LANES = 128
SUBLANES = 8
VMEM_LIMIT = 56 * 1024 * 1024
NEG_INF = float("-inf")
LOG2_E = 1.4426950408889634


def _split2(a):
    hi = a.astype(BF16)
    lo = (a - hi.astype(F32)).astype(BF16)
    return hi, lo


def _split3(a):
    hi = a.astype(BF16)
    r = a - hi.astype(F32)
    mid = r.astype(BF16)
    lo = (r - mid.astype(F32)).astype(BF16)
    return hi, mid, lo


def _dot_nt(a, b):
    return lax.dot_general(a, b, (((1,), (1,)), ((), ())), preferred_element_type=F32)


def _dot_tn(a, b):
    return lax.dot_general(a, b, (((0,), (0,)), ((), ())), preferred_element_type=F32)


def _dot(a, b):
    return jnp.dot(a, b, preferred_element_type=F32)


def _in_proj_kernel(x_ref, w_ref, b_ref, hg_ref, mb_ref, gt_ref, *, col_chunk):
    xb = x_ref[...].astype(BF16)
    outs = ((hg_ref, 0), (mb_ref, hg_ref.shape[1]), (gt_ref, hg_ref.shape[1] + mb_ref.shape[1]))
    for o_ref, base in outs:
        for c0 in range(0, o_ref.shape[1], col_chunk):
            acc = _dot(xb, w_ref[:, base + c0:base + c0 + col_chunk])
            o_ref[:, c0:c0 + col_chunk] = acc + b_ref[:, base + c0:base + c0 + col_chunk]


def _in_proj(x2, w_bf, b_in, n_hg, n_mb, n_gt, tm):
    n, d = x2.shape
    cols = w_bf.shape[1]
    return pl.pallas_call(
        functools.partial(_in_proj_kernel, col_chunk=512),
        grid=(n // tm,),
        in_specs=[
            pl.BlockSpec((tm, d), lambda i: (i, 0)),
            pl.BlockSpec((d, cols), lambda i: (0, 0), pipeline_mode=pl.Buffered(1)),
            pl.BlockSpec((1, cols), lambda i: (0, 0)),
        ],
        out_specs=[
            pl.BlockSpec((tm, n_hg), lambda i: (i, 0)),
            pl.BlockSpec((tm, n_mb), lambda i: (i, 0)),
            pl.BlockSpec((tm, n_gt), lambda i: (i, 0)),
        ],
        out_shape=[
            jax.ShapeDtypeStruct((n, n_hg), F32),
            jax.ShapeDtypeStruct((n, n_mb), F32),
            jax.ShapeDtypeStruct((n, n_gt), F32),
        ],
        compiler_params=pltpu.CompilerParams(
            dimension_semantics=("parallel",), vmem_limit_bytes=VMEM_LIMIT),
        name="in_proj",
    )(x2, w_bf, b_in)


def _hgrn_tables():
    c = HG_CHUNK
    levels = [c >> (i + 1) for i in range(int(math.log2(c)))]
    masks = np.zeros((len(levels) + 1, c, c), np.float32)
    for li, m in enumerate(levels):
        for r in range(c):
            c0 = (r // (2 * m)) * (2 * m)
            if r - c0 >= m:
                masks[li, r, c0:c0 + m] = 1.0
    masks[len(levels)] = np.eye(c, dtype=np.float32)
    return np.tril(np.ones((c, c), np.float32)), masks, tuple(levels)


def _level_ref_rows(p, m):
    c, w = p.shape
    if 2 * m > SUBLANES:
        parts = [jnp.broadcast_to(p[c0 + m - 1:c0 + m, :], (2 * m, w)) for c0 in range(0, c, 2 * m)]
        return parts[0] if len(parts) == 1 else jnp.concatenate(parts, axis=0)
    p3 = p.reshape(c // SUBLANES, SUBLANES, w)
    sub = lax.broadcasted_iota(jnp.int32, p3.shape, 1)
    out = None
    for c0 in range(0, SUBLANES, 2 * m):
        b = jnp.broadcast_to(p3[:, c0 + m - 1:c0 + m, :], p3.shape)
        out = b if out is None else jnp.where(sub >= c0, b, out)
    return out.reshape(c, w)


def _hgrn_kernel(q_ref, f_ref, i_ref, g_ref, lb_ref, ng_ref, tril_ref, msk_ref, o_ref,
                 *, n_chunks, levels, group):
    c = HG_CHUNK
    dk = HG_DK
    assert n_chunks % group == 0
    lb = jnp.concatenate([lb_ref[...]] * group, axis=1)
    oml = 1.0 - lb
    ng = ng_ref[...]
    tril = tril_ref[...]
    n_levels = len(levels)

    def load(ref, r0):
        return jnp.concatenate([ref[0, pl.ds(r0 + u * c, c), :] for u in range(group)], axis=1)

    def lanes(a, u):
        return a[:, u * dk:(u + 1) * dk]

    def intra(r0):
        z = load(f_ref, r0)
        qr = load(q_ref, r0)
        lf = jnp.log(lb + oml * jax.nn.sigmoid(z))
        kk = oml * jax.nn.sigmoid(-z)
        qf = qr * jax.nn.sigmoid(qr)
        l_hi, l_mid, l_lo = _split3(lf)
        p = (_dot(tril, l_hi) + _dot(tril, l_mid) + _dot(tril, l_lo)) * LOG2_E
        b_end = p[c - 1:c, :]
        qb = (qf * jnp.exp2(p)).astype(BF16)
        kd = (kk * jnp.exp2(b_end - p)).astype(BF16)
        dec = jnp.exp2(b_end)
        qh = qf.astype(BF16)
        kh = kk.astype(BF16)
        scores = [msk_ref[n_levels] * _dot_nt(lanes(qh, u), lanes(kh, u)) for u in range(group)]
        row = lax.broadcasted_iota(jnp.int32, p.shape, 0)
        for li, m in enumerate(levels):
            ref_rows = _level_ref_rows(p, m)
            if m >= SUBLANES:
                expo = jnp.concatenate([(p[a:a + m] - ref_rows[a:a + m]) if (a // m) % 2 else
                                        (ref_rows[a:a + m] - p[a:a + m]) for a in range(0, c, m)], axis=0)
            else:
                expo = -jnp.abs(p - ref_rows)
            ex = jnp.exp2(expo)
            qk = (jnp.where((row & (2 * m - 1)) >= m, qf, kk) * ex).astype(BF16)
            for u in range(group):
                scores[u] = scores[u] + msk_ref[li] * _dot_nt(lanes(qk, u), lanes(qk, u))
        vbs = [i_ref[0, pl.ds(r0 + u * c, c), :].astype(BF16) for u in range(group)]
        o_intra = [_dot(scores[u].astype(BF16), vbs[u]) for u in range(group)]
        return qb, kd, dec, vbs, o_intra

    def chain(r0, staged, st):
        qb, kd, dec, vbs, o_intra = staged
        for u in range(group):
            rows = pl.ds(r0 + u * c, c)
            g = g_ref[0, rows, :]
            o = _dot_nt(lanes(qb, u), st.astype(BF16)) + o_intra[u]
            st = st * lanes(dec, u) + _dot_tn(vbs[u], lanes(kd, u))
            o = o * lax.rsqrt(jnp.mean(o * o, axis=-1, keepdims=True) + NORM_EPS)
            o_ref[0, rows, :] = o * ng * (g * jax.nn.sigmoid(g))
        return st

    n_groups = n_chunks // group
    st = jnp.zeros((HG_DK, HG_DK), F32)
    staged = intra(0)
    for gi in range(n_groups):
        nxt = intra((gi + 1) * group * c) if gi + 1 < n_groups else None
        st = chain(gi * group * c, staged, st)
        staged = nxt


def _hgrn(hg3, lb_row, ng_row):
    b, s, _ = hg3.shape
    tril, masks, levels = _hgrn_tables()
    mst = jnp.asarray(tril, BF16)
    msk = jnp.asarray(masks, F32)
    h = HG_HEADS

    def col(off):
        return pl.BlockSpec((1, s, HG_DK), lambda bi, hi: (bi, 0, off + hi))

    return pl.pallas_call(
        functools.partial(_hgrn_kernel, n_chunks=s // HG_CHUNK, levels=levels,
                          group=math.gcd(s // HG_CHUNK, HG_GROUP)),
        grid=(b, h),
        in_specs=[
            col(0), col(h), col(2 * h), col(3 * h),
            pl.BlockSpec((1, HG_DK), lambda bi, hi: (0, hi)),
            pl.BlockSpec((1, HG_DK), lambda bi, hi: (0, hi)),
            pl.BlockSpec(mst.shape, lambda bi, hi: (0, 0)),
            pl.BlockSpec(msk.shape, lambda bi, hi: (0, 0, 0)),
        ],
        out_specs=pl.BlockSpec((1, s, HG_DK), lambda bi, hi: (bi, 0, hi)),
        out_shape=jax.ShapeDtypeStruct((b, s, h * HG_DK), F32),
        compiler_params=pltpu.CompilerParams(
            dimension_semantics=("parallel", "parallel"), vmem_limit_bytes=VMEM_LIMIT),
        name="hgrn2",
    )(hg3, hg3, hg3, hg3, lb_row, ng_row, mst, msk)


def _t5_bucket_np(dist):
    max_exact = REL_BUCKETS // 2
    d = np.maximum(dist, 1).astype(np.float32)
    log_part = max_exact + (np.log(d / np.float32(max_exact)) / np.float32(math.log(REL_MAX_DIST / max_exact))
                            * np.float32(REL_BUCKETS - max_exact)).astype(np.int32)
    return np.where(dist < max_exact, dist, np.minimum(log_part, REL_BUCKETS - 1))


def _moba_kernel(q_ref, k_ref, v_ref, avg_ref, bkt_ref, rb_ref, o_ref, own_ref, prev_ref, *, n_blocks):
    blk = MB_BLOCK
    scale = MB_DH ** -0.5 * LOG2_E
    avg = avg_ref[...]
    hp = LANES // MB_DH

    @pl.when(pl.program_id(1) == 0)
    def _():
        causal = (lax.broadcasted_iota(jnp.int32, (blk, blk), 0) <= lax.broadcasted_iota(jnp.int32, (blk, blk), 1))
        for hh in range(hp):
            head = pl.program_id(0) * hp + hh
            own_t = jnp.zeros((blk, blk), F32)
            prev_t = jnp.zeros((blk, blk), F32)
            for bk in range(REL_BUCKETS):
                val = rb_ref[bk, head] * LOG2_E
                own_t = jnp.where(bkt_ref[0] == bk, val, own_t)
                prev_t = jnp.where(bkt_ref[1] == bk, val, prev_t)
            own_ref[hh] = jnp.where(causal, own_t, NEG_INF)
            prev_ref[hh] = prev_t

    grow = lax.broadcasted_iota(jnp.int32, (GATE_ROWS, blk), 0)
    vt_all = v_ref[0].T

    def fold(a):
        return a.reshape(blk // SUBLANES, SUBLANES, blk)

    heads = []
    for hh in range(hp):
        ls = slice(hh * MB_DH, (hh + 1) * MB_DH)
        k_h = k_ref[0, :, ls]
        k_hi, k_lo = _split2(k_h)
        k_mean = (_dot(avg, k_hi) + _dot(avg, k_lo))[:GATE_ROWS]
        vt = jnp.concatenate([vt_all[hh * MB_DH:(hh + 1) * MB_DH, :], jnp.ones((DEN_ROWS, vt_all.shape[1]), F32)],
                             axis=0).astype(BF16)
        far_bias = rb_ref[REL_BUCKETS - 1, pl.program_id(0) * hp + hh] * LOG2_E
        heads.append((ls, _split2(k_mean), k_h.astype(BF16), vt, far_bias))

    def logits(hh, i):
        ls, (km_hi, km_lo), kb, _, far_bias = heads[hh]
        qi = q_ref[0, i * blk:(i + 1) * blk, ls] * scale
        st = _dot_nt(kb[:(i + 1) * blk], qi.astype(BF16))
        sel = None
        if i > 0:
            q_hi, q_lo = _split2(qi)
            gate = _dot_nt(km_hi, q_hi) + _dot_nt(km_lo, q_hi) + _dot_nt(km_hi, q_lo)
            gate = jnp.where(grow < i, gate, NEG_INF)
            rank = jnp.zeros((GATE_ROWS, blk), F32)
            if i > MB_TOPK:
                for j2 in range(i):
                    gj = jnp.broadcast_to(gate[j2:j2 + 1, :], (GATE_ROWS, blk))
                    tie = jnp.where(grow > j2, 1.0, 0.0)
                    rank = rank + jnp.where(gj > gate, 1.0, jnp.where(gj == gate, tie, 0.0))
            sel = jnp.where(rank < MB_TOPK, gate, NEG_INF) > NEG_INF
        pieces = []
        for j in range(i + 1):
            sj = st[j * blk:(j + 1) * blk, :]
            if j == i:
                pieces.append(sj + own_ref[hh])
            elif j == i - 1:
                pieces.append(sj + prev_ref[hh] + jnp.where(sel[j:j + 1, :], 0.0, NEG_INF))
            else:
                pieces.append(sj + jnp.where(sel[j:j + 1, :], far_bias, NEG_INF))
        mx8 = fold(pieces[0]).max(axis=0)
        for p in pieces[1:]:
            mx8 = jnp.maximum(mx8, fold(p).max(axis=0))
        return pieces, mx8.max(axis=0, keepdims=True)

    def attend(hh, pieces, mx):
        vt = heads[hh][3]
        weights = jnp.concatenate([jnp.exp2(p - mx).astype(BF16) for p in pieces], axis=0)
        acc = _dot(vt[:, :len(pieces) * blk], weights)
        return acc[:MB_DH] / acc[MB_DH:MB_DH + 1]

    items = [(i, hh) for i in range(n_blocks) for hh in range(hp)]
    staged = [logits(hh, i) for i, hh in items[:PIPE_AHEAD]]
    done = {}
    for n, (i, hh) in enumerate(items):
        if n + PIPE_AHEAD < len(items):
            staged.append(logits(items[n + PIPE_AHEAD][1], items[n + PIPE_AHEAD][0]))
        done[hh] = attend(hh, *staged.pop(0))
        if hh == hp - 1:
            o_ref[0, i * blk:(i + 1) * blk, :] = jnp.concatenate([done[h] for h in range(hp)], axis=0).T


def _moba(mb3, rel_bias):
    b, s, _ = mb3.shape
    assert s % MB_BLOCK == 0
    nb = s // MB_BLOCK
    hp = LANES // MB_DH
    n_hp = MB_HEADS // hp
    t = np.arange(MB_BLOCK)
    d_own = t[:, None] - t[None, :]
    buckets = np.stack([_t5_bucket_np(np.maximum(d_own, 0)), _t5_bucket_np(d_own + MB_BLOCK)]).astype(np.int32)
    assert _t5_bucket_np(np.array([MB_BLOCK + 1]))[0] == REL_BUCKETS - 1
    assert nb <= GATE_ROWS
    bkt = jnp.asarray(buckets.transpose(0, 2, 1))
    avg_np = np.zeros((LANES, s), np.float32)
    for j in range(nb):
        avg_np[j, j * MB_BLOCK:(j + 1) * MB_BLOCK] = 1.0 / MB_BLOCK
    avg = jnp.asarray(avg_np, BF16)

    def col(off):
        return pl.BlockSpec((1, s, LANES), lambda hi, bi: (bi, 0, off + hi))

    return pl.pallas_call(
        functools.partial(_moba_kernel, n_blocks=nb),
        grid=(n_hp, b),
        in_specs=[
            col(0), col(n_hp), col(2 * n_hp),
            pl.BlockSpec(avg.shape, lambda hi, bi: (0, 0)),
            pl.BlockSpec(bkt.shape, lambda hi, bi: (0, 0, 0)),
            pl.BlockSpec(memory_space=pltpu.SMEM),
        ],
        out_specs=pl.BlockSpec((1, s, LANES), lambda hi, bi: (bi, 0, hi)),
        out_shape=jax.ShapeDtypeStruct((b, s, MB_HEADS * MB_DH), F32),
        scratch_shapes=[pltpu.VMEM((hp, MB_BLOCK, MB_BLOCK), F32), pltpu.VMEM((hp, MB_BLOCK, MB_BLOCK), F32)],
        compiler_params=pltpu.CompilerParams(
            dimension_semantics=("parallel", "arbitrary"), vmem_limit_bytes=VMEM_LIMIT),
        name="moba",
    )(mb3, mb3, mb3, avg, bkt, rel_bias.astype(F32))


def _layer_norm(x, g, b):
    mu = jnp.mean(x, axis=-1, keepdims=True)
    xc = x - mu
    var = jnp.mean(xc * xc, axis=-1, keepdims=True)
    return xc * lax.rsqrt(var + NORM_EPS) * g + b


def _merge_kernel(hg_ref, mb_ref, ga_ref, gb_ref, x_ref, wa_ref, wb_ref, wo_ref, g1_ref, b1_ref,
                  wr_ref, br_ref, tri_ref, x1_ref, route_ref, cnt_ref, run_ref, *, blocks_per_moe_tile, parts):
    tm = x_ref.shape[0]
    pr = tm // parts
    lane = lax.broadcasted_iota(jnp.int32, (pr, LANES), 1)

    @pl.when(pl.program_id(0) % blocks_per_moe_tile == 0)
    def _():
        run_ref[...] = jnp.zeros_like(run_ref)

    def mix(rows):
        ya = _dot(hg_ref[rows, :].astype(BF16), wa_ref[...])
        yb = _dot(mb_ref[rows, :].astype(BF16), wb_ref[...])
        mixed_in = jax.nn.sigmoid(ga_ref[rows, :]) * ya + jax.nn.sigmoid(gb_ref[rows, :]) * yb
        mixed = _dot(mixed_in.astype(BF16), wo_ref[...])
        x1 = _layer_norm(DN_ALPHA * x_ref[rows, :] + mixed, g1_ref[...], b1_ref[...])
        x1_ref[rows, :] = x1
        return x1

    def route_rows(rows, x1, run):
        x_hi, x_lo = _split2(x1)
        w_hi, w_lo = wr_ref[0], wr_ref[1]
        hr = pr // 2
        logits = jnp.concatenate(
            [_dot(x_hi[r:r + hr], w_hi) + _dot(x_hi[r:r + hr], w_lo) + _dot(x_lo[r:r + hr], w_hi) for r in (0, hr)],
            axis=0) + br_ref[...]
        glog = jnp.where(lane < N_GROUPS, logits, NEG_INF)
        gmax = jnp.max(glog, axis=-1, keepdims=True)
        grp = jnp.min(jnp.where(glog == gmax, lane, LANES), axis=-1, keepdims=True)
        p_grp = 1.0 / jnp.sum(jnp.exp(glog - gmax), axis=-1, keepdims=True)
        e_lo = N_GROUPS + grp * EXPERTS_PER_GROUP
        elog = jnp.where(jnp.logical_and(lane >= e_lo, lane < e_lo + EXPERTS_PER_GROUP), logits, NEG_INF)
        m1 = jnp.max(elog, axis=-1, keepdims=True)
        i1 = jnp.min(jnp.where(elog == m1, lane, LANES), axis=-1, keepdims=True)
        elog2 = jnp.where(lane == i1, NEG_INF, elog)
        m2 = jnp.max(elog2, axis=-1, keepdims=True)
        i2 = jnp.min(jnp.where(elog2 == m2, lane, LANES), axis=-1, keepdims=True)
        e2 = jnp.exp(m2 - m1)
        w1 = p_grp / (1.0 + e2)
        w2 = p_grp * e2 / (1.0 + e2)
        oh1 = jnp.where(lane == i1 - N_GROUPS, 1.0, 0.0)
        oh2 = jnp.where(lane == i2 - N_GROUPS, 1.0, 0.0)
        both = oh1 + oh2
        before = _dot(tri_ref[...], both.astype(BF16)) + run
        r1 = jnp.sum(before * oh1, axis=-1, keepdims=True)
        r2 = jnp.sum(before * oh2, axis=-1, keepdims=True)
        cols = ((i1 - N_GROUPS).astype(F32), (i2 - N_GROUPS).astype(F32), w1, w2, r1, r2)
        route = jnp.zeros((pr, LANES), F32)
        for li, col in enumerate(cols):
            route = jnp.where(lane == li, col, route)
        route_ref[:, rows] = route.T[:ROUTE_FIELDS, :]
        return run + jnp.sum(both, axis=0, keepdims=True)

    slabs = [slice(h * pr, (h + 1) * pr) for h in range(parts)]
    mixed = [mix(rows) for rows in slabs]
    run = run_ref[...]
    for rows, x1 in zip(slabs, mixed):
        run = route_rows(rows, x1, run)
    run_ref[...] = run
    cnt_ref[0] = run


def _merge(hg_o, mb_o, gates, x2, wa, wb, wo, g1, b1, wr, br, tm, moe_tile):
    n, d = x2.shape
    wa_n = hg_o.shape[1]
    wb_n = mb_o.shape[1]
    assert moe_tile % tm == 0
    parts = max(1, tm // MERGE_SLAB)
    tri = jnp.asarray(np.tril(np.ones((tm // parts, tm // parts), np.float32), -1), BF16)

    def full(a):
        nd = a.ndim
        return pl.BlockSpec(a.shape, lambda i: (0,) * nd, pipeline_mode=pl.Buffered(1))

    return pl.pallas_call(
        functools.partial(_merge_kernel, blocks_per_moe_tile=moe_tile // tm, parts=parts),
        grid=(n // tm,),
        in_specs=[
            pl.BlockSpec((tm, wa_n), lambda i: (i, 0)),
            pl.BlockSpec((tm, wb_n), lambda i: (i, 0)),
            pl.BlockSpec((tm, d), lambda i: (i, 0)),
            pl.BlockSpec((tm, d), lambda i: (i, 1)),
            pl.BlockSpec((tm, d), lambda i: (i, 0)),
            full(wa), full(wb), full(wo), full(g1), full(b1), full(wr), full(br), full(tri),
        ],
        out_specs=[
            pl.BlockSpec((tm, d), lambda i: (i, 0)),
            pl.BlockSpec((ROUTE_FIELDS, tm), lambda i: (0, i)),
            pl.BlockSpec((1, 1, LANES), lambda i: (i, 0, 0)),
        ],
        out_shape=[
            jax.ShapeDtypeStruct((n, d), F32),
            jax.ShapeDtypeStruct((ROUTE_FIELDS, n), F32),
            jax.ShapeDtypeStruct((n // tm, 1, LANES), F32),
        ],
        scratch_shapes=[pltpu.VMEM((1, LANES), F32)],
        compiler_params=pltpu.CompilerParams(
            dimension_semantics=("arbitrary",), vmem_limit_bytes=VMEM_LIMIT),
        name="merge_ln1_router",
    )(hg_o, mb_o, gates, gates, x2, wa, wb, wo, g1, b1, wr, br, tri)


def _tile_copies(t, loc_ref, glob_ref, tot_ref, max_big, max_small, make_copy, act):
    base = t * (max_big + max_small)

    def big(q, carry):
        act(make_copy(loc_ref[base + q], glob_ref[base + q], SEG_BIG))
        return carry
    lax.fori_loop(0, tot_ref[2 * t], big, 0)

    def small(q, carry):
        act(make_copy(loc_ref[base + max_big + q], glob_ref[base + max_big + q], SEG_PAD))
        return carry
    lax.fori_loop(0, tot_ref[2 * t + 1], small, 0)


def _dispatch_kernel(loc_ref, glob_ref, tot_ref, gap_ref, p0_ref, p1_ref, x1_ref, xs_hbm,
                     xs_ref, xb_ref, zero_ref, sem, *, cv_rows, max_big, max_small):
    ti = pl.program_id(0)
    n_tiles = pl.num_programs(0)
    groups, _, d = x1_ref.shape

    def out_copy(loc, glob, size):
        src = pl.multiple_of(loc, SEG_PAD)
        dst = pl.multiple_of(glob, SEG_PAD)
        return pltpu.make_async_copy(xb_ref.at[pl.ds(src, size), :], xs_hbm.at[pl.ds(dst, size), :], sem.at[0])

    def tile_copies(t, act):
        _tile_copies(t, loc_ref, glob_ref, tot_ref, max_big, max_small, out_copy, act)

    @pl.when(ti == 0)
    def _():
        def zero(c, carry):
            r0 = pl.multiple_of(c * cv_rows, cv_rows)
            xs_ref[pl.ds(r0, cv_rows), :] = jnp.zeros((cv_rows, d), F32)
            return carry
        lax.fori_loop(0, xs_ref.shape[0] // cv_rows, zero, 0)
        zero_ref[...] = jnp.zeros_like(zero_ref)

    def sort_rows(grp, carry):
        t0 = grp * SUBLANES
        for k in range(SUBLANES):
            row = x1_ref[grp, pl.ds(k, 1), :]
            xs_ref[pl.ds(p0_ref[0, 0, t0 + k], 1), :] = row
            xs_ref[pl.ds(p1_ref[0, 0, t0 + k], 1), :] = row
        return carry
    lax.fori_loop(0, groups, sort_rows, 0)

    @pl.when(ti > 0)
    def _():
        tile_copies(ti - 1, lambda cp: cp.wait())

    def convert(c, carry):
        r0 = pl.multiple_of(c * cv_rows, cv_rows)
        xb_ref[pl.ds(r0, cv_rows), :] = xs_ref[pl.ds(r0, cv_rows), :].astype(BF16)
        return carry
    lax.fori_loop(0, xs_ref.shape[0] // cv_rows, convert, 0)

    tile_copies(ti, lambda cp: cp.start())

    @pl.when(ti == n_tiles - 1)
    def _():
        tile_copies(ti, lambda cp: cp.wait())

        def fill_copy(row):
            return pltpu.make_async_copy(zero_ref, xs_hbm.at[pl.ds(pl.multiple_of(row, SEG_PAD), SEG_PAD), :],
                                         sem.at[1])

        def fill(ex, carry):
            def one(c, carry2):
                fill_copy(gap_ref[2 * ex] + c * SEG_PAD).start()
                return carry2
            lax.fori_loop(0, gap_ref[2 * ex + 1], one, 0)
            return carry
        lax.fori_loop(0, N_EXPERTS, fill, 0)

        def fill_wait(ex, carry):
            def one(c, carry2):
                fill_copy(0).wait()
                return carry2
            lax.fori_loop(0, gap_ref[2 * ex + 1], one, 0)
            return carry
        lax.fori_loop(0, N_EXPERTS, fill_wait, 0)


def _expert_kernel(be_ref, nu_ref, first_ref, slot_ref, nxt_ref, x_ref, wgu_hbm, wd_hbm, y_ref,
                   wgu_f32, wd_f32, wgu_bf, wd_bf, sem, *, cast_rows):
    i = pl.program_id(0)

    def fetch(expert, slot):
        return (pltpu.make_async_copy(wgu_hbm.at[expert], wgu_f32.at[slot], sem.at[0, slot]),
                pltpu.make_async_copy(wd_hbm.at[expert], wd_f32.at[slot], sem.at[1, slot]))

    @pl.when(i < nu_ref[0])
    def _():
        slot = slot_ref[i]

        @pl.when(i == 0)
        def _():
            for cp in fetch(be_ref[0], 0):
                cp.start()

        @pl.when(first_ref[i] == 1)
        def _():
            for cp in fetch(be_ref[i], slot):
                cp.wait()
            for r0 in range(0, wgu_bf.shape[0], cast_rows):
                wgu_bf[r0:r0 + cast_rows, :] = wgu_f32[slot, r0:r0 + cast_rows, :].astype(BF16)
            for r0 in range(0, wd_bf.shape[0], cast_rows):
                wd_bf[r0:r0 + cast_rows, :] = wd_f32[slot, r0:r0 + cast_rows, :].astype(BF16)

            @pl.when(nxt_ref[i] >= 0)
            def _():
                for cp in fetch(nxt_ref[i], 1 - slot):
                    cp.start()

        gu = _dot(x_ref[...], wgu_bf[...])
        gate = gu[:, :EXPERT_HIDDEN]
        up = gu[:, EXPERT_HIDDEN:]
        hdn = (gate * jax.nn.sigmoid(gate) * up).astype(BF16)
        y_ref[...] = _dot(hdn, wd_bf[...])


def _combine_kernel(loc_ref, glob_ref, tot_ref, p0_ref, p1_ref, w0_ref, w1_ref, x1_ref, ys_hbm, g2_ref, b2_ref,
                    o_ref, xs_ref, sem, *, ln_rows, max_big, max_small):
    ti = pl.program_id(0)
    n_tiles = pl.num_programs(0)
    groups, _, d = x1_ref.shape
    slot = ti % 2

    def tile_copies(t, sl, act):
        def in_copy(loc, glob, size):
            src = pl.multiple_of(glob, SEG_PAD)
            dst = pl.multiple_of(loc, SEG_PAD)
            return pltpu.make_async_copy(ys_hbm.at[pl.ds(src, size), :], xs_ref.at[sl, pl.ds(dst, size), :],
                                         sem.at[sl])
        _tile_copies(t, loc_ref, glob_ref, tot_ref, max_big, max_small, in_copy, act)

    @pl.when(ti == 0)
    def _():
        tile_copies(0, 0, lambda cp: cp.start())

    @pl.when(ti + 1 < n_tiles)
    def _():
        tile_copies(ti + 1, 1 - slot, lambda cp: cp.start())

    tile_copies(ti, slot, lambda cp: cp.wait())

    def combine(grp, carry):
        t0 = grp * SUBLANES
        for k in range(SUBLANES):
            t = t0 + k
            o_ref[grp, pl.ds(k, 1), :] = (DN_ALPHA * x1_ref[grp, pl.ds(k, 1), :]
                                          + w0_ref[0, 0, t] * xs_ref[slot, pl.ds(p0_ref[0, 0, t], 1), :]
                                          + w1_ref[0, 0, t] * xs_ref[slot, pl.ds(p1_ref[0, 0, t], 1), :])
        return carry
    lax.fori_loop(0, groups, combine, 0, unroll=8)

    ln_groups = ln_rows // SUBLANES

    def norm(c, carry):
        g0 = pl.multiple_of(c * ln_groups, ln_groups)
        rows = o_ref[pl.ds(g0, ln_groups), :, :].reshape(ln_rows, d)
        o_ref[pl.ds(g0, ln_groups), :, :] = _layer_norm(rows, g2_ref[...], b2_ref[...]).reshape(ln_groups, SUBLANES, d)
        return carry
    lax.fori_loop(0, groups // ln_groups, norm, 0)


def _moe_tile(n):
    return min(MOE_TILE, n)


def _round_up(a, m):
    return (a + m - 1) // m * m


def _moe(x1, route, cnt_run, w_gate_up, w_down, g2, b2):
    n, d = x1.shape
    tile = _moe_tile(n)
    assert n % tile == 0
    n_tiles = n // tile
    i32 = jnp.int32
    cnt = cnt_run.reshape(n_tiles, -1, LANES)[:, -1, :N_EXPERTS].astype(i32)
    seg = _round_up(cnt, SEG_PAD)
    lbase = jnp.cumsum(seg, axis=1) - seg
    used = jnp.sum(seg, axis=0)
    region = _round_up(used, EXP_ROWS)
    e_start = jnp.cumsum(region) - region
    gseg = e_start[None, :] + jnp.cumsum(seg, axis=0) - seg
    n_bigs = seg // SEG_BIG
    n_smalls = (seg - n_bigs * SEG_BIG) // SEG_PAD
    max_big = TOP_K * tile // SEG_BIG
    max_small = N_EXPERTS * (SEG_BIG // SEG_PAD - 1)

    def piece_table(counts, first_off, step, width):
        incl = jnp.cumsum(counts, axis=1)
        excl = incl - counts
        q = jnp.arange(width, dtype=i32)[None, :, None]
        owner = (q >= excl[:, None, :]) & (q < incl[:, None, :])
        pick = lambda a: jnp.sum(jnp.where(owner, a[:, None, :], 0), axis=-1)
        off = pick(first_off) + (q[:, :, 0] - pick(excl)) * step
        return pick(lbase) + off, pick(gseg) + off

    big_loc, big_glob = piece_table(n_bigs, jnp.zeros_like(seg), SEG_BIG, max_big)
    small_loc, small_glob = piece_table(n_smalls, n_bigs * SEG_BIG, SEG_PAD, max_small)
    piece_loc = jnp.concatenate([big_loc, small_loc], axis=1)
    piece_glob = jnp.concatenate([big_glob, small_glob], axis=1)
    tot = jnp.stack([jnp.sum(n_bigs, axis=1), jnp.sum(n_smalls, axis=1)], axis=1)
    gap = jnp.stack([e_start + used, (region - used) // SEG_PAD], axis=1)
    rows_max = _round_up(TOP_K * n + n_tiles * N_EXPERTS * (SEG_PAD - 1) + N_EXPERTS * (EXP_ROWS - 1), EXP_ROWS)
    n_blocks = rows_max // EXP_ROWS
    n_used = (jnp.sum(region) // EXP_ROWS).astype(i32).reshape(1)
    blk_row = jnp.arange(n_blocks, dtype=i32)[:, None] * EXP_ROWS
    blk_expert = jnp.minimum(jnp.sum((blk_row >= jnp.cumsum(region)[None, :]).astype(i32), axis=1), N_EXPERTS - 1)
    r_exp = route[0:TOP_K].astype(i32).reshape(TOP_K, n_tiles, tile)
    r_rank = route[2 * TOP_K:3 * TOP_K].astype(i32).reshape(TOP_K, n_tiles, tile)
    seg_start = jnp.sum(jnp.where(r_exp[..., None] == jnp.arange(N_EXPERTS, dtype=i32), lbase[None, :, None, :], 0),
                        axis=-1)
    pos = (seg_start + r_rank).reshape(TOP_K, n_tiles, 1, tile)
    r_wgt = route[TOP_K:2 * TOP_K].reshape(TOP_K, n_tiles, 1, tile)
    local_rows = _round_up(TOP_K * tile + N_EXPERTS * (SEG_PAD - 1), 256)
    flat = lambda a: a.reshape(-1).astype(i32)

    def smem_spec():
        return pl.BlockSpec((1, 1, tile), lambda t, *_: (t, 0, 0), memory_space=pltpu.SMEM)

    x1_groups = x1.reshape(n // SUBLANES, SUBLANES, d)
    row_groups_spec = pl.BlockSpec((tile // SUBLANES, SUBLANES, d), lambda t, *_: (t, 0, 0))

    xs_hbm = pl.pallas_call(
        functools.partial(_dispatch_kernel, cv_rows=256, max_big=max_big, max_small=max_small),
        grid_spec=pltpu.PrefetchScalarGridSpec(
            num_scalar_prefetch=4,
            grid=(n_tiles,),
            in_specs=[smem_spec(), smem_spec(), row_groups_spec],
            out_specs=pl.BlockSpec(memory_space=pl.ANY),
            scratch_shapes=[
                pltpu.VMEM((local_rows, d), F32),
                pltpu.VMEM((local_rows, d), BF16),
                pltpu.VMEM((SEG_PAD, d), BF16),
                pltpu.SemaphoreType.DMA((2,)),
            ],
        ),
        out_shape=jax.ShapeDtypeStruct((rows_max, d), BF16),
        compiler_params=pltpu.CompilerParams(dimension_semantics=("arbitrary",), vmem_limit_bytes=VMEM_LIMIT),
        name="moe_dispatch",
    )(flat(piece_loc), flat(piece_glob), flat(tot), flat(gap), pos[0], pos[1], x1_groups)

    def blk(i, be, nu):
        return jnp.minimum(i, nu[0] - 1)

    blk_idx = jnp.arange(n_blocks, dtype=i32)
    first = ((blk_idx == 0) | (blk_expert != jnp.roll(blk_expert, 1))).astype(i32)
    w_slot = (jnp.cumsum(first) - 1) % 2
    e_idx = jnp.arange(N_EXPERTS, dtype=i32)
    later_used = (region > 0)[None, :] & (e_idx[None, :] > e_idx[:, None])
    nxt_of = jnp.min(jnp.where(later_used, e_idx[None, :], N_EXPERTS), axis=1)
    nxt_of = jnp.where(nxt_of < N_EXPERTS, nxt_of, -1)
    nxt_expert = jnp.sum(jnp.where(blk_expert[:, None] == e_idx[None, :], nxt_of[None, :], 0), axis=1)

    ys_hbm = pl.pallas_call(
        functools.partial(_expert_kernel, cast_rows=256),
        grid_spec=pltpu.PrefetchScalarGridSpec(
            num_scalar_prefetch=5,
            grid=(n_blocks,),
            in_specs=[
                pl.BlockSpec((EXP_ROWS, d), lambda i, be, nu, *_: (blk(i, be, nu), 0)),
                pl.BlockSpec(memory_space=pl.ANY),
                pl.BlockSpec(memory_space=pl.ANY),
            ],
            out_specs=pl.BlockSpec((EXP_ROWS, d), lambda i, be, nu, *_: (blk(i, be, nu), 0)),
            scratch_shapes=[
                pltpu.VMEM((2, d, 2 * EXPERT_HIDDEN), F32), pltpu.VMEM((2, EXPERT_HIDDEN, d), F32),
                pltpu.VMEM((d, 2 * EXPERT_HIDDEN), BF16), pltpu.VMEM((EXPERT_HIDDEN, d), BF16),
                pltpu.SemaphoreType.DMA((2, 2)),
            ],
        ),
        out_shape=jax.ShapeDtypeStruct((rows_max, d), F32),
        compiler_params=pltpu.CompilerParams(dimension_semantics=("arbitrary",), vmem_limit_bytes=VMEM_LIMIT),
        name="moe_experts",
    )(blk_expert, n_used, first, w_slot.astype(i32), nxt_expert.astype(i32), xs_hbm, w_gate_up, w_down)

    return pl.pallas_call(
        functools.partial(_combine_kernel, ln_rows=min(256, tile), max_big=max_big, max_small=max_small),
        grid_spec=pltpu.PrefetchScalarGridSpec(
            num_scalar_prefetch=3,
            grid=(n_tiles,),
            in_specs=[
                smem_spec(), smem_spec(), smem_spec(), smem_spec(),
                row_groups_spec,
                pl.BlockSpec(memory_space=pl.ANY),
                pl.BlockSpec((1, d), lambda t, *_: (0, 0)),
                pl.BlockSpec((1, d), lambda t, *_: (0, 0)),
            ],
            out_specs=row_groups_spec,
            scratch_shapes=[pltpu.VMEM((2, local_rows, d), F32), pltpu.SemaphoreType.DMA((2,))],
        ),
        out_shape=jax.ShapeDtypeStruct(x1_groups.shape, F32),
        compiler_params=pltpu.CompilerParams(dimension_semantics=("arbitrary",), vmem_limit_bytes=VMEM_LIMIT),
        name="moe_combine_ln2",
    )(flat(piece_loc), flat(piece_glob), flat(tot), pos[0], pos[1], r_wgt[0], r_wgt[1], x1_groups, ys_hbm,
      g2, b2).reshape(n, d)


def _block(x, w_in, b_in, lower_bound, hg_norm_g, rel_bias, w_proj_a, w_proj_b, w_out, ln1_g, ln1_b,
           w_group, b_group, w_expert, b_expert, w_gate_up, w_down, ln2_g, ln2_b, *, tm_proj, tm_merge):
    b, s, d = x.shape
    n = b * s
    n_hg = 4 * HG_HEADS * HG_DK
    n_mb = 3 * MB_HEADS * MB_DH
    n_gt = 2 * d
    x2 = x.reshape(n, d)
    hg, mb, gates = _in_proj(x2, w_in.astype(BF16), b_in.reshape(1, -1), n_hg, n_mb, n_gt, tm_proj)
    hg_o = _hgrn(hg.reshape(b, s, n_hg), lower_bound.reshape(1, -1), hg_norm_g.reshape(1, -1))
    mb_o = _moba(mb.reshape(b, s, n_mb), rel_bias)
    w_r = jnp.zeros((d, LANES), F32).at[:, :N_GROUPS].set(w_group).at[:, N_GROUPS:N_GROUPS + N_EXPERTS].set(w_expert)
    b_r = jnp.zeros((1, LANES), F32).at[0, :N_GROUPS].set(b_group).at[0, N_GROUPS:N_GROUPS + N_EXPERTS].set(b_expert)
    tm_merge = min(tm_merge, n)
    x1, route, cnt_run = _merge(hg_o.reshape(n, -1), mb_o.reshape(n, -1), gates, x2,
                                w_proj_a.astype(BF16), w_proj_b.astype(BF16), w_out.astype(BF16),
                                ln1_g.reshape(1, d), ln1_b.reshape(1, d), jnp.stack(_split2(w_r)), b_r,
                                tm_merge, _moe_tile(n))
    out = _moe(x1, route, cnt_run, w_gate_up, w_down, ln2_g.reshape(1, d), ln2_b.reshape(1, d))
    return out.reshape(b, s, d)


def kernel(x, w_in, b_in, lb_logits, hg_norm_g, rel_bias, w_proj_a, w_proj_b, w_out, ln1_g, ln1_b, w_group,
           b_group, w_expert, b_expert, w_gate_up, w_down, ln2_g, ln2_b):
    lower_bounds = jnp.cumsum(jax.nn.softmax(lb_logits.astype(F32), axis=0), axis=0)
    l = 0
    return _block(x, w_in[l], b_in[l], lower_bounds[l], hg_norm_g[l], rel_bias, w_proj_a[l], w_proj_b[l],
                  w_out[l], ln1_g[l], ln1_b[l], w_group[l], b_group[l], w_expert[l], b_expert[l],
                  w_gate_up[l], w_down[l], ln2_g[l], ln2_b[l], tm_proj=512, tm_merge=1024)
```

```python
import functools
import math

import numpy as np
import jax
import jax.numpy as jnp
from jax import lax
from jax.experimental import pallas as pl
from jax.experimental.pallas import tpu as pltpu

F32 = jnp.float32
BF16 = jnp.bfloat16

HG_HEADS = 4
HG_DK = 128
HG_CHUNK = 128
HG_GROUP = 8
MB_HEADS = 8
MB_DH = 64
MB_BLOCK = 256
MB_TOPK = 3
PIPE_AHEAD = 2
DEN_ROWS = 16
GATE_ROWS = 16
MERGE_SLAB = 512
ROUTE_FIELDS = 8
REL_BUCKETS = 32
REL_MAX_DIST = 128
N_GROUPS = 4
EXPERTS_PER_GROUP = 8
N_EXPERTS = N_GROUPS * EXPERTS_PER_GROUP
TOP_K = 2
EXPERT_HIDDEN = 512
MOE_TILE = 1024
SEG_PAD = 16
SEG_BIG = 64
EXP_ROWS = 512
DEPTH = 1
DN_ALPHA = (2.0 * DEPTH) ** 0.25
NORM_EPS = 1e-5
LANES = 128
SUBLANES = 8
VMEM_LIMIT = 56 * 1024 * 1024
NEG_INF = float("-inf")
LOG2_E = 1.4426950408889634


def _split2(a):
    hi = a.astype(BF16)
    lo = (a - hi.astype(F32)).astype(BF16)
    return hi, lo


def _split3(a):
    hi = a.astype(BF16)
    r = a - hi.astype(F32)
    mid = r.astype(BF16)
    lo = (r - mid.astype(F32)).astype(BF16)
    return hi, mid, lo


def _dot_nt(a, b):
    return lax.dot_general(a, b, (((1,), (1,)), ((), ())), preferred_element_type=F32)


def _dot_tn(a, b):
    return lax.dot_general(a, b, (((0,), (0,)), ((), ())), preferred_element_type=F32)


def _dot(a, b):
    return jnp.dot(a, b, preferred_element_type=F32)


def _in_proj_kernel(x_ref, w_ref, b_ref, hg_ref, mb_ref, gt_ref, *, col_chunk):
    xb = x_ref[...].astype(BF16)
    outs = ((hg_ref, 0), (mb_ref, hg_ref.shape[1]), (gt_ref, hg_ref.shape[1] + mb_ref.shape[1]))
    for o_ref, base in outs:
        for c0 in range(0, o_ref.shape[1], col_chunk):
            acc = _dot(xb, w_ref[:, base + c0:base + c0 + col_chunk])
            o_ref[:, c0:c0 + col_chunk] = acc + b_ref[:, base + c0:base + c0 + col_chunk]


def _in_proj(x2, w_bf, b_in, n_hg, n_mb, n_gt, tm):
    n, d = x2.shape
    cols = w_bf.shape[1]
    return pl.pallas_call(
        functools.partial(_in_proj_kernel, col_chunk=512),
        grid=(n // tm,),
        in_specs=[
            pl.BlockSpec((tm, d), lambda i: (i, 0)),
            pl.BlockSpec((d, cols), lambda i: (0, 0), pipeline_mode=pl.Buffered(1)),
            pl.BlockSpec((1, cols), lambda i: (0, 0)),
        ],
        out_specs=[
            pl.BlockSpec((tm, n_hg), lambda i: (i, 0)),
            pl.BlockSpec((tm, n_mb), lambda i: (i, 0)),
            pl.BlockSpec((tm, n_gt), lambda i: (i, 0)),
        ],
        out_shape=[
            jax.ShapeDtypeStruct((n, n_hg), F32),
            jax.ShapeDtypeStruct((n, n_mb), F32),
            jax.ShapeDtypeStruct((n, n_gt), F32),
        ],
        compiler_params=pltpu.CompilerParams(
            dimension_semantics=("parallel",), vmem_limit_bytes=VMEM_LIMIT),
        name="in_proj",
    )(x2, w_bf, b_in)


def _hgrn_tables():
    c = HG_CHUNK
    levels = [c >> (i + 1) for i in range(int(math.log2(c)))]
    masks = np.zeros((len(levels) + 1, c, c), np.float32)
    for li, m in enumerate(levels):
        for r in range(c):
            c0 = (r // (2 * m)) * (2 * m)
            if r - c0 >= m:
                masks[li, r, c0:c0 + m] = 1.0
    masks[len(levels)] = np.eye(c, dtype=np.float32)
    return np.tril(np.ones((c, c), np.float32)), masks, tuple(levels)


def _level_ref_rows(p, m):
    c, w = p.shape
    if 2 * m > SUBLANES:
        parts = [jnp.broadcast_to(p[c0 + m - 1:c0 + m, :], (2 * m, w)) for c0 in range(0, c, 2 * m)]
        return parts[0] if len(parts) == 1 else jnp.concatenate(parts, axis=0)
    p3 = p.reshape(c // SUBLANES, SUBLANES, w)
    sub = lax.broadcasted_iota(jnp.int32, p3.shape, 1)
    out = None
    for c0 in range(0, SUBLANES, 2 * m):
        b = jnp.broadcast_to(p3[:, c0 + m - 1:c0 + m, :], p3.shape)
        out = b if out is None else jnp.where(sub >= c0, b, out)
    return out.reshape(c, w)


def _hgrn_kernel(x_ref, lb_ref, ng_ref, tril_ref, msk_ref, o_ref, *, n_chunks, levels, group):
    c = HG_CHUNK
    dk = HG_DK
    assert n_chunks % group == 0
    lb = jnp.concatenate([lb_ref[...]] * group, axis=1)
    oml = 1.0 - lb
    ng = ng_ref[...]
    tril = tril_ref[...]
    n_levels = len(levels)

    def load(field, r0):
        cols = slice(field * dk, (field + 1) * dk)
        return jnp.concatenate([x_ref[0, pl.ds(r0 + u * c, c), cols] for u in range(group)], axis=1)

    def lanes(a, u):
        return a[:, u * dk:(u + 1) * dk]

    def intra(r0):
        z = load(1, r0)
        qr = load(0, r0)
        lf = jnp.log(lb + oml * jax.nn.sigmoid(z))
        kk = oml * jax.nn.sigmoid(-z)
        qf = qr * jax.nn.sigmoid(qr)
        l_hi, l_mid, l_lo = _split3(lf)
        p = (_dot(tril, l_hi) + _dot(tril, l_mid) + _dot(tril, l_lo)) * LOG2_E
        b_end = p[c - 1:c, :]
        qb = (qf * jnp.exp2(p)).astype(BF16)
        kd = (kk * jnp.exp2(b_end - p)).astype(BF16)
        dec = jnp.exp2(b_end)
        qh = qf.astype(BF16)
        kh = kk.astype(BF16)
        scores = [msk_ref[n_levels] * _dot_nt(lanes(qh, u), lanes(kh, u)) for u in range(group)]
        row = lax.broadcasted_iota(jnp.int32, p.shape, 0)
        for li, m in enumerate(levels):
            ref_rows = _level_ref_rows(p, m)
            if m >= SUBLANES:
                expo = jnp.concatenate([(p[a:a + m] - ref_rows[a:a + m]) if (a // m) % 2 else
                                        (ref_rows[a:a + m] - p[a:a + m]) for a in range(0, c, m)], axis=0)
            else:
                expo = -jnp.abs(p - ref_rows)
            ex = jnp.exp2(expo)
            qk = (jnp.where((row & (2 * m - 1)) >= m, qf, kk) * ex).astype(BF16)
            for u in range(group):
                scores[u] = scores[u] + msk_ref[li] * _dot_nt(lanes(qk, u), lanes(qk, u))
        vbs = [x_ref[0, pl.ds(r0 + u * c, c), 2 * dk:3 * dk].astype(BF16) for u in range(group)]
        o_intra = [_dot(scores[u].astype(BF16), vbs[u]) for u in range(group)]
        return qb, kd, dec, vbs, o_intra

    def chain(r0, staged, st):
        qb, kd, dec, vbs, o_intra = staged
        for u in range(group):
            rows = pl.ds(r0 + u * c, c)
            g = x_ref[0, rows, 3 * dk:4 * dk]
            o = _dot_nt(lanes(qb, u), st.astype(BF16)) + o_intra[u]
            st = st * lanes(dec, u) + _dot_tn(vbs[u], lanes(kd, u))
            o = o * lax.rsqrt(jnp.mean(o * o, axis=-1, keepdims=True) + NORM_EPS)
            o_ref[0, rows, :] = o * ng * (g * jax.nn.sigmoid(g))
        return st

    n_groups = n_chunks // group
    st = jnp.zeros((HG_DK, HG_DK), F32)
    staged = intra(0)
    for gi in range(n_groups):
        nxt = intra((gi + 1) * group * c) if gi + 1 < n_groups else None
        st = chain(gi * group * c, staged, st)
        staged = nxt


def _hgrn(hg3, lb_row, ng_row):
    b, s, _ = hg3.shape
    tril, masks, levels = _hgrn_tables()
    mst = jnp.asarray(tril, BF16)
    msk = jnp.asarray(masks, F32)
    h = HG_HEADS

    return pl.pallas_call(
        functools.partial(_hgrn_kernel, n_chunks=s // HG_CHUNK, levels=levels,
                          group=math.gcd(s // HG_CHUNK, HG_GROUP)),
        grid=(b, h),
        in_specs=[
            pl.BlockSpec((1, s, 4 * HG_DK), lambda bi, hi: (bi, 0, hi)),
            pl.BlockSpec((1, HG_DK), lambda bi, hi: (0, hi)),
            pl.BlockSpec((1, HG_DK), lambda bi, hi: (0, hi)),
            pl.BlockSpec(mst.shape, lambda bi, hi: (0, 0)),
            pl.BlockSpec(msk.shape, lambda bi, hi: (0, 0, 0)),
        ],
        out_specs=pl.BlockSpec((1, s, HG_DK), lambda bi, hi: (bi, 0, hi)),
        out_shape=jax.ShapeDtypeStruct((b, s, h * HG_DK), F32),
        compiler_params=pltpu.CompilerParams(
            dimension_semantics=("parallel", "parallel"), vmem_limit_bytes=VMEM_LIMIT),
        name="hgrn2",
    )(hg3, lb_row, ng_row, mst, msk)


def _t5_bucket_np(dist):
    max_exact = REL_BUCKETS // 2
    d = np.maximum(dist, 1).astype(np.float32)
    log_part = max_exact + (np.log(d / np.float32(max_exact)) / np.float32(math.log(REL_MAX_DIST / max_exact))
                            * np.float32(REL_BUCKETS - max_exact)).astype(np.int32)
    return np.where(dist < max_exact, dist, np.minimum(log_part, REL_BUCKETS - 1))


def _moba_kernel(q_ref, k_ref, v_ref, avg_ref, bkt_ref, rb_ref, o_ref, own_ref, prev_ref, *, n_blocks):
    blk = MB_BLOCK
    scale = MB_DH ** -0.5 * LOG2_E
    avg = avg_ref[...]
    hp = LANES // MB_DH

    @pl.when(pl.program_id(1) == 0)
    def _():
        causal = (lax.broadcasted_iota(jnp.int32, (blk, blk), 0) <= lax.broadcasted_iota(jnp.int32, (blk, blk), 1))
        for hh in range(hp):
            head = pl.program_id(0) * hp + hh
            own_t = jnp.zeros((blk, blk), F32)
            prev_t = jnp.zeros((blk, blk), F32)
            for bk in range(REL_BUCKETS):
                val = rb_ref[bk, head] * LOG2_E
                own_t = jnp.where(bkt_ref[0] == bk, val, own_t)
                prev_t = jnp.where(bkt_ref[1] == bk, val, prev_t)
            own_ref[hh] = jnp.where(causal, own_t, NEG_INF)
            prev_ref[hh] = prev_t

    grow = lax.broadcasted_iota(jnp.int32, (GATE_ROWS, blk), 0)
    vt_all = v_ref[0].T

    def fold(a):
        return a.reshape(blk // SUBLANES, SUBLANES, blk)

    heads = []
    for hh in range(hp):
        ls = slice(hh * MB_DH, (hh + 1) * MB_DH)
        k_h = k_ref[0, :, ls]
        k_hi, k_lo = _split2(k_h)
        k_mean = (_dot(avg, k_hi) + _dot(avg, k_lo))[:GATE_ROWS]
        vt = jnp.concatenate([vt_all[hh * MB_DH:(hh + 1) * MB_DH, :], jnp.ones((DEN_ROWS, vt_all.shape[1]), F32)],
                             axis=0).astype(BF16)
        far_bias = rb_ref[REL_BUCKETS - 1, pl.program_id(0) * hp + hh] * LOG2_E
        heads.append((ls, _split2(k_mean), k_h.astype(BF16), vt, far_bias))

    def logits(hh, i):
        ls, (km_hi, km_lo), kb, _, far_bias = heads[hh]
        qi = q_ref[0, i * blk:(i + 1) * blk, ls] * scale
        st = _dot_nt(kb[:(i + 1) * blk], qi.astype(BF16))
        sel = None
        if i > 0:
            q_hi, q_lo = _split2(qi)
            gate = _dot_nt(km_hi, q_hi) + _dot_nt(km_lo, q_hi) + _dot_nt(km_hi, q_lo)
            gate = jnp.where(grow < i, gate, NEG_INF)
            rank = jnp.zeros((GATE_ROWS, blk), F32)
            if i > MB_TOPK:
                for j2 in range(i):
                    gj = jnp.broadcast_to(gate[j2:j2 + 1, :], (GATE_ROWS, blk))
                    tie = jnp.where(grow > j2, 1.0, 0.0)
                    rank = rank + jnp.where(gj > gate, 1.0, jnp.where(gj == gate, tie, 0.0))
            sel = jnp.where(rank < MB_TOPK, gate, NEG_INF) > NEG_INF
        pieces = []
        for j in range(i + 1):
            sj = st[j * blk:(j + 1) * blk, :]
            if j == i:
                pieces.append(sj + own_ref[hh])
            elif j == i - 1:
                pieces.append(sj + prev_ref[hh] + jnp.where(sel[j:j + 1, :], 0.0, NEG_INF))
            else:
                pieces.append(sj + jnp.where(sel[j:j + 1, :], far_bias, NEG_INF))
        mx8 = fold(pieces[0]).max(axis=0)
        for p in pieces[1:]:
            mx8 = jnp.maximum(mx8, fold(p).max(axis=0))
        return pieces, mx8.max(axis=0, keepdims=True)

    def attend(hh, pieces, mx):
        vt = heads[hh][3]
        weights = jnp.concatenate([jnp.exp2(p - mx).astype(BF16) for p in pieces], axis=0)
        acc = _dot(vt[:, :len(pieces) * blk], weights)
        return acc[:MB_DH] / acc[MB_DH:MB_DH + 1]

    items = [(i, hh) for i in range(n_blocks) for hh in range(hp)]
    staged = [logits(hh, i) for i, hh in items[:PIPE_AHEAD]]
    done = {}
    for n, (i, hh) in enumerate(items):
        if n + PIPE_AHEAD < len(items):
            staged.append(logits(items[n + PIPE_AHEAD][1], items[n + PIPE_AHEAD][0]))
        done[hh] = attend(hh, *staged.pop(0))
        if hh == hp - 1:
            o_ref[0, i * blk:(i + 1) * blk, :] = jnp.concatenate([done[h] for h in range(hp)], axis=0).T


def _moba(mb3, rel_bias):
    b, s, _ = mb3.shape
    assert s % MB_BLOCK == 0
    nb = s // MB_BLOCK
    hp = LANES // MB_DH
    n_hp = MB_HEADS // hp
    t = np.arange(MB_BLOCK)
    d_own = t[:, None] - t[None, :]
    buckets = np.stack([_t5_bucket_np(np.maximum(d_own, 0)), _t5_bucket_np(d_own + MB_BLOCK)]).astype(np.int32)
    assert _t5_bucket_np(np.array([MB_BLOCK + 1]))[0] == REL_BUCKETS - 1
    assert nb <= GATE_ROWS
    bkt = jnp.asarray(buckets.transpose(0, 2, 1))
    avg_np = np.zeros((LANES, s), np.float32)
    for j in range(nb):
        avg_np[j, j * MB_BLOCK:(j + 1) * MB_BLOCK] = 1.0 / MB_BLOCK
    avg = jnp.asarray(avg_np, BF16)

    def col(off):
        return pl.BlockSpec((1, s, LANES), lambda hi, bi: (bi, 0, off + hi))

    return pl.pallas_call(
        functools.partial(_moba_kernel, n_blocks=nb),
        grid=(n_hp, b),
        in_specs=[
            col(0), col(n_hp), col(2 * n_hp),
            pl.BlockSpec(avg.shape, lambda hi, bi: (0, 0)),
            pl.BlockSpec(bkt.shape, lambda hi, bi: (0, 0, 0)),
            pl.BlockSpec(memory_space=pltpu.SMEM),
        ],
        out_specs=pl.BlockSpec((1, s, LANES), lambda hi, bi: (bi, 0, hi)),
        out_shape=jax.ShapeDtypeStruct((b, s, MB_HEADS * MB_DH), F32),
        scratch_shapes=[pltpu.VMEM((hp, MB_BLOCK, MB_BLOCK), F32), pltpu.VMEM((hp, MB_BLOCK, MB_BLOCK), F32)],
        compiler_params=pltpu.CompilerParams(
            dimension_semantics=("parallel", "arbitrary"), vmem_limit_bytes=VMEM_LIMIT),
        name="moba",
    )(mb3, mb3, mb3, avg, bkt, rel_bias.astype(F32))


def _layer_norm(x, g, b):
    mu = jnp.mean(x, axis=-1, keepdims=True)
    xc = x - mu
    var = jnp.mean(xc * xc, axis=-1, keepdims=True)
    return xc * lax.rsqrt(var + NORM_EPS) * g + b


def _merge_kernel(hg_ref, mb_ref, ga_ref, gb_ref, x_ref, wa_ref, wb_ref, wo_ref, g1_ref, b1_ref,
                  wr_ref, br_ref, tri_ref, x1_ref, route_ref, cnt_ref, run_ref, *, blocks_per_moe_tile, parts):
    tm = x_ref.shape[0]
    pr = tm // parts
    lane = lax.broadcasted_iota(jnp.int32, (pr, LANES), 1)

    @pl.when(pl.program_id(0) % blocks_per_moe_tile == 0)
    def _():
        run_ref[...] = jnp.zeros_like(run_ref)

    def mix(rows):
        ya = _dot(hg_ref[rows, :].astype(BF16), wa_ref[...])
        yb = _dot(mb_ref[rows, :].astype(BF16), wb_ref[...])
        mixed_in = jax.nn.sigmoid(ga_ref[rows, :]) * ya + jax.nn.sigmoid(gb_ref[rows, :]) * yb
        mixed = _dot(mixed_in.astype(BF16), wo_ref[...])
        x1 = _layer_norm(DN_ALPHA * x_ref[rows, :] + mixed, g1_ref[...], b1_ref[...])
        x1_ref[rows, :] = x1
        return x1

    def route_rows(rows, x1, run):
        x_hi, x_lo = _split2(x1)
        w_hi, w_lo = wr_ref[0], wr_ref[1]
        hr = pr // 2
        logits = jnp.concatenate(
            [_dot(x_hi[r:r + hr], w_hi) + _dot(x_hi[r:r + hr], w_lo) + _dot(x_lo[r:r + hr], w_hi) for r in (0, hr)],
            axis=0) + br_ref[...]
        glog = jnp.where(lane < N_GROUPS, logits, NEG_INF)
        gmax = jnp.max(glog, axis=-1, keepdims=True)
        grp = jnp.min(jnp.where(glog == gmax, lane, LANES), axis=-1, keepdims=True)
        p_grp = 1.0 / jnp.sum(jnp.exp(glog - gmax), axis=-1, keepdims=True)
        e_lo = N_GROUPS + grp * EXPERTS_PER_GROUP
        elog = jnp.where(jnp.logical_and(lane >= e_lo, lane < e_lo + EXPERTS_PER_GROUP), logits, NEG_INF)
        m1 = jnp.max(elog, axis=-1, keepdims=True)
        i1 = jnp.min(jnp.where(elog == m1, lane, LANES), axis=-1, keepdims=True)
        elog2 = jnp.where(lane == i1, NEG_INF, elog)
        m2 = jnp.max(elog2, axis=-1, keepdims=True)
        i2 = jnp.min(jnp.where(elog2 == m2, lane, LANES), axis=-1, keepdims=True)
        e2 = jnp.exp(m2 - m1)
        w1 = p_grp / (1.0 + e2)
        w2 = p_grp * e2 / (1.0 + e2)
        oh1 = jnp.where(lane == i1 - N_GROUPS, 1.0, 0.0)
        oh2 = jnp.where(lane == i2 - N_GROUPS, 1.0, 0.0)
        both = oh1 + oh2
        before = _dot(tri_ref[...], both.astype(BF16)) + run
        r1 = jnp.sum(before * oh1, axis=-1, keepdims=True)
        r2 = jnp.sum(before * oh2, axis=-1, keepdims=True)
        cols = ((i1 - N_GROUPS).astype(F32), (i2 - N_GROUPS).astype(F32), w1, w2, r1, r2)
        route = jnp.zeros((pr, LANES), F32)
        for li, col in enumerate(cols):
            route = jnp.where(lane == li, col, route)
        route_ref[:, rows] = route.T[:ROUTE_FIELDS, :]
        return run + jnp.sum(both, axis=0, keepdims=True)

    slabs = [slice(h * pr, (h + 1) * pr) for h in range(parts)]
    mixed = [mix(rows) for rows in slabs]
    run = run_ref[...]
    for rows, x1 in zip(slabs, mixed):
        run = route_rows(rows, x1, run)
    run_ref[...] = run
    cnt_ref[0] = run


def _merge(hg_o, mb_o, gates, x2, wa, wb, wo, g1, b1, wr, br, tm, moe_tile):
    n, d = x2.shape
    wa_n = hg_o.shape[1]
    wb_n = mb_o.shape[1]
    assert moe_tile % tm == 0
    parts = max(1, tm // MERGE_SLAB)
    tri = jnp.asarray(np.tril(np.ones((tm // parts, tm // parts), np.float32), -1), BF16)

    def full(a):
        nd = a.ndim
        return pl.BlockSpec(a.shape, lambda i: (0,) * nd, pipeline_mode=pl.Buffered(1))

    return pl.pallas_call(
        functools.partial(_merge_kernel, blocks_per_moe_tile=moe_tile // tm, parts=parts),
        grid=(n // tm,),
        in_specs=[
            pl.BlockSpec((tm, wa_n), lambda i: (i, 0)),
            pl.BlockSpec((tm, wb_n), lambda i: (i, 0)),
            pl.BlockSpec((tm, d), lambda i: (i, 0)),
            pl.BlockSpec((tm, d), lambda i: (i, 1)),
            pl.BlockSpec((tm, d), lambda i: (i, 0)),
            full(wa), full(wb), full(wo), full(g1), full(b1), full(wr), full(br), full(tri),
        ],
        out_specs=[
            pl.BlockSpec((tm, d), lambda i: (i, 0)),
            pl.BlockSpec((ROUTE_FIELDS, tm), lambda i: (0, i)),
            pl.BlockSpec((1, 1, LANES), lambda i: (i, 0, 0)),
        ],
        out_shape=[
            jax.ShapeDtypeStruct((n, d), F32),
            jax.ShapeDtypeStruct((ROUTE_FIELDS, n), F32),
            jax.ShapeDtypeStruct((n // tm, 1, LANES), F32),
        ],
        scratch_shapes=[pltpu.VMEM((1, LANES), F32)],
        compiler_params=pltpu.CompilerParams(
            dimension_semantics=("arbitrary",), vmem_limit_bytes=VMEM_LIMIT),
        name="merge_ln1_router",
    )(hg_o, mb_o, gates, gates, x2, wa, wb, wo, g1, b1, wr, br, tri)


def _tile_copies(t, loc_ref, glob_ref, tot_ref, max_big, max_small, make_copy, act):
    base = t * (max_big + max_small)

    def big(q, carry):
        act(make_copy(loc_ref[base + q], glob_ref[base + q], SEG_BIG))
        return carry
    lax.fori_loop(0, tot_ref[2 * t], big, 0)

    def small(q, carry):
        act(make_copy(loc_ref[base + max_big + q], glob_ref[base + max_big + q], SEG_PAD))
        return carry
    lax.fori_loop(0, tot_ref[2 * t + 1], small, 0)


def _dispatch_kernel(loc_ref, glob_ref, tot_ref, gap_ref, p0_ref, p1_ref, x1_ref, xs_hbm,
                     xs_ref, xb_ref, zero_ref, sem, *, cv_rows, max_big, max_small):
    ti = pl.program_id(0)
    n_tiles = pl.num_programs(0)
    groups, _, d = x1_ref.shape

    def out_copy(loc, glob, size):
        src = pl.multiple_of(loc, SEG_PAD)
        dst = pl.multiple_of(glob, SEG_PAD)
        return pltpu.make_async_copy(xb_ref.at[pl.ds(src, size), :], xs_hbm.at[pl.ds(dst, size), :], sem.at[0])

    def tile_copies(t, act):
        _tile_copies(t, loc_ref, glob_ref, tot_ref, max_big, max_small, out_copy, act)

    @pl.when(ti == 0)
    def _():
        def zero(c, carry):
            r0 = pl.multiple_of(c * cv_rows, cv_rows)
            xs_ref[pl.ds(r0, cv_rows), :] = jnp.zeros((cv_rows, d), F32)
            return carry
        lax.fori_loop(0, xs_ref.shape[0] // cv_rows, zero, 0)
        zero_ref[...] = jnp.zeros_like(zero_ref)

    def sort_rows(grp, carry):
        t0 = grp * SUBLANES
        for k in range(SUBLANES):
            row = x1_ref[grp, pl.ds(k, 1), :]
            xs_ref[pl.ds(p0_ref[0, 0, t0 + k], 1), :] = row
            xs_ref[pl.ds(p1_ref[0, 0, t0 + k], 1), :] = row
        return carry
    lax.fori_loop(0, groups, sort_rows, 0)

    @pl.when(ti > 0)
    def _():
        tile_copies(ti - 1, lambda cp: cp.wait())

    def convert(c, carry):
        r0 = pl.multiple_of(c * cv_rows, cv_rows)
        xb_ref[pl.ds(r0, cv_rows), :] = xs_ref[pl.ds(r0, cv_rows), :].astype(BF16)
        return carry
    lax.fori_loop(0, xs_ref.shape[0] // cv_rows, convert, 0)

    tile_copies(ti, lambda cp: cp.start())

    @pl.when(ti == n_tiles - 1)
    def _():
        tile_copies(ti, lambda cp: cp.wait())

        def fill_copy(row):
            return pltpu.make_async_copy(zero_ref, xs_hbm.at[pl.ds(pl.multiple_of(row, SEG_PAD), SEG_PAD), :],
                                         sem.at[1])

        def fill(ex, carry):
            def one(c, carry2):
                fill_copy(gap_ref[2 * ex] + c * SEG_PAD).start()
                return carry2
            lax.fori_loop(0, gap_ref[2 * ex + 1], one, 0)
            return carry
        lax.fori_loop(0, N_EXPERTS, fill, 0)

        def fill_wait(ex, carry):
            def one(c, carry2):
                fill_copy(0).wait()
                return carry2
            lax.fori_loop(0, gap_ref[2 * ex + 1], one, 0)
            return carry
        lax.fori_loop(0, N_EXPERTS, fill_wait, 0)


def _expert_kernel(be_ref, nu_ref, first_ref, slot_ref, nxt_ref, x_ref, wgu_hbm, wd_hbm, y_ref,
                   wgu_f32, wd_f32, wgu_bf, wd_bf, sem, *, cast_rows):
    i = pl.program_id(0)

    def fetch(expert, slot):
        return (pltpu.make_async_copy(wgu_hbm.at[expert], wgu_f32.at[slot], sem.at[0, slot]),
                pltpu.make_async_copy(wd_hbm.at[expert], wd_f32.at[slot], sem.at[1, slot]))

    @pl.when(i < nu_ref[0])
    def _():
        slot = slot_ref[i]

        @pl.when(i == 0)
        def _():
            for cp in fetch(be_ref[0], 0):
                cp.start()

        @pl.when(first_ref[i] == 1)
        def _():
            for cp in fetch(be_ref[i], slot):
                cp.wait()
            for r0 in range(0, wgu_bf.shape[0], cast_rows):
                wgu_bf[r0:r0 + cast_rows, :] = wgu_f32[slot, r0:r0 + cast_rows, :].astype(BF16)
            for r0 in range(0, wd_bf.shape[0], cast_rows):
                wd_bf[r0:r0 + cast_rows, :] = wd_f32[slot, r0:r0 + cast_rows, :].astype(BF16)

            @pl.when(nxt_ref[i] >= 0)
            def _():
                for cp in fetch(nxt_ref[i], 1 - slot):
                    cp.start()

        gu = _dot(x_ref[...], wgu_bf[...])
        gate = gu[:, :EXPERT_HIDDEN]
        up = gu[:, EXPERT_HIDDEN:]
        hdn = (gate * jax.nn.sigmoid(gate) * up).astype(BF16)
        y_ref[...] = _dot(hdn, wd_bf[...])


def _combine_kernel(loc_ref, glob_ref, tot_ref, p0_ref, p1_ref, w0_ref, w1_ref, x1_ref, ys_hbm, g2_ref, b2_ref,
                    o_ref, xs_ref, sem, *, ln_rows, max_big, max_small):
    ti = pl.program_id(0)
    n_tiles = pl.num_programs(0)
    groups, _, d = x1_ref.shape
    slot = ti % 2

    def tile_copies(t, sl, act):
        def in_copy(loc, glob, size):
            src = pl.multiple_of(glob, SEG_PAD)
            dst = pl.multiple_of(loc, SEG_PAD)
            return pltpu.make_async_copy(ys_hbm.at[pl.ds(src, size), :], xs_ref.at[sl, pl.ds(dst, size), :],
                                         sem.at[sl])
        _tile_copies(t, loc_ref, glob_ref, tot_ref, max_big, max_small, in_copy, act)

    @pl.when(ti == 0)
    def _():
        tile_copies(0, 0, lambda cp: cp.start())

    @pl.when(ti + 1 < n_tiles)
    def _():
        tile_copies(ti + 1, 1 - slot, lambda cp: cp.start())

    tile_copies(ti, slot, lambda cp: cp.wait())

    def combine(grp, carry):
        t0 = grp * SUBLANES
        for k in range(SUBLANES):
            t = t0 + k
            o_ref[grp, pl.ds(k, 1), :] = (DN_ALPHA * x1_ref[grp, pl.ds(k, 1), :]
                                          + w0_ref[0, 0, t] * xs_ref[slot, pl.ds(p0_ref[0, 0, t], 1), :]
                                          + w1_ref[0, 0, t] * xs_ref[slot, pl.ds(p1_ref[0, 0, t], 1), :])
        return carry
    lax.fori_loop(0, groups, combine, 0, unroll=8)

    ln_groups = ln_rows // SUBLANES

    def norm(c, carry):
        g0 = pl.multiple_of(c * ln_groups, ln_groups)
        rows = o_ref[pl.ds(g0, ln_groups), :, :].reshape(ln_rows, d)
        o_ref[pl.ds(g0, ln_groups), :, :] = _layer_norm(rows, g2_ref[...], b2_ref[...]).reshape(ln_groups, SUBLANES, d)
        return carry
    lax.fori_loop(0, groups // ln_groups, norm, 0)


def _moe_tile(n):
    return min(MOE_TILE, n)


def _round_up(a, m):
    return (a + m - 1) // m * m


def _moe(x1, route, cnt_run, w_gate_up, w_down, g2, b2):
    n, d = x1.shape
    tile = _moe_tile(n)
    assert n % tile == 0
    n_tiles = n // tile
    i32 = jnp.int32
    cnt = cnt_run.reshape(n_tiles, -1, LANES)[:, -1, :N_EXPERTS].astype(i32)
    seg = _round_up(cnt, SEG_PAD)
    lbase = jnp.cumsum(seg, axis=1) - seg
    used = jnp.sum(seg, axis=0)
    region = _round_up(used, EXP_ROWS)
    e_start = jnp.cumsum(region) - region
    gseg = e_start[None, :] + jnp.cumsum(seg, axis=0) - seg
    n_bigs = seg // SEG_BIG
    n_smalls = (seg - n_bigs * SEG_BIG) // SEG_PAD
    max_big = TOP_K * tile // SEG_BIG
    max_small = N_EXPERTS * (SEG_BIG // SEG_PAD - 1)

    def piece_table(counts, first_off, step, width):
        incl = jnp.cumsum(counts, axis=1)
        excl = incl - counts
        q = jnp.arange(width, dtype=i32)[None, :, None]
        owner = (q >= excl[:, None, :]) & (q < incl[:, None, :])
        pick = lambda a: jnp.sum(jnp.where(owner, a[:, None, :], 0), axis=-1)
        off = pick(first_off) + (q[:, :, 0] - pick(excl)) * step
        return pick(lbase) + off, pick(gseg) + off

    big_loc, big_glob = piece_table(n_bigs, jnp.zeros_like(seg), SEG_BIG, max_big)
    small_loc, small_glob = piece_table(n_smalls, n_bigs * SEG_BIG, SEG_PAD, max_small)
    piece_loc = jnp.concatenate([big_loc, small_loc], axis=1)
    piece_glob = jnp.concatenate([big_glob, small_glob], axis=1)
    tot = jnp.stack([jnp.sum(n_bigs, axis=1), jnp.sum(n_smalls, axis=1)], axis=1)
    gap = jnp.stack([e_start + used, (region - used) // SEG_PAD], axis=1)
    rows_max = _round_up(TOP_K * n + n_tiles * N_EXPERTS * (SEG_PAD - 1) + N_EXPERTS * (EXP_ROWS - 1), EXP_ROWS)
    n_blocks = rows_max // EXP_ROWS
    n_used = (jnp.sum(region) // EXP_ROWS).astype(i32).reshape(1)
    blk_row = jnp.arange(n_blocks, dtype=i32)[:, None] * EXP_ROWS
    blk_expert = jnp.minimum(jnp.sum((blk_row >= jnp.cumsum(region)[None, :]).astype(i32), axis=1), N_EXPERTS - 1)
    r_exp = route[0:TOP_K].astype(i32).reshape(TOP_K, n_tiles, tile)
    r_rank = route[2 * TOP_K:3 * TOP_K].astype(i32).reshape(TOP_K, n_tiles, tile)
    seg_start = jnp.sum(jnp.where(r_exp[..., None] == jnp.arange(N_EXPERTS, dtype=i32), lbase[None, :, None, :], 0),
                        axis=-1)
    pos = (seg_start + r_rank).reshape(TOP_K, n_tiles, 1, tile)
    r_wgt = route[TOP_K:2 * TOP_K].reshape(TOP_K, n_tiles, 1, tile)
    local_rows = _round_up(TOP_K * tile + N_EXPERTS * (SEG_PAD - 1), 256)
    flat = lambda a: a.reshape(-1).astype(i32)

    def smem_spec():
        return pl.BlockSpec((1, 1, tile), lambda t, *_: (t, 0, 0), memory_space=pltpu.SMEM)

    x1_groups = x1.reshape(n // SUBLANES, SUBLANES, d)
    row_groups_spec = pl.BlockSpec((tile // SUBLANES, SUBLANES, d), lambda t, *_: (t, 0, 0))

    xs_hbm = pl.pallas_call(
        functools.partial(_dispatch_kernel, cv_rows=256, max_big=max_big, max_small=max_small),
        grid_spec=pltpu.PrefetchScalarGridSpec(
            num_scalar_prefetch=4,
            grid=(n_tiles,),
            in_specs=[smem_spec(), smem_spec(), row_groups_spec],
            out_specs=pl.BlockSpec(memory_space=pl.ANY),
            scratch_shapes=[
                pltpu.VMEM((local_rows, d), F32),
                pltpu.VMEM((local_rows, d), BF16),
                pltpu.VMEM((SEG_PAD, d), BF16),
                pltpu.SemaphoreType.DMA((2,)),
            ],
        ),
        out_shape=jax.ShapeDtypeStruct((rows_max, d), BF16),
        compiler_params=pltpu.CompilerParams(dimension_semantics=("arbitrary",), vmem_limit_bytes=VMEM_LIMIT),
        name="moe_dispatch",
    )(flat(piece_loc), flat(piece_glob), flat(tot), flat(gap), pos[0], pos[1], x1_groups)

    def blk(i, be, nu):
        return jnp.minimum(i, nu[0] - 1)

    blk_idx = jnp.arange(n_blocks, dtype=i32)
    first = ((blk_idx == 0) | (blk_expert != jnp.roll(blk_expert, 1))).astype(i32)
    w_slot = (jnp.cumsum(first) - 1) % 2
    e_idx = jnp.arange(N_EXPERTS, dtype=i32)
    later_used = (region > 0)[None, :] & (e_idx[None, :] > e_idx[:, None])
    nxt_of = jnp.min(jnp.where(later_used, e_idx[None, :], N_EXPERTS), axis=1)
    nxt_of = jnp.where(nxt_of < N_EXPERTS, nxt_of, -1)
    nxt_expert = jnp.sum(jnp.where(blk_expert[:, None] == e_idx[None, :], nxt_of[None, :], 0), axis=1)

    ys_hbm = pl.pallas_call(
        functools.partial(_expert_kernel, cast_rows=256),
        grid_spec=pltpu.PrefetchScalarGridSpec(
            num_scalar_prefetch=5,
            grid=(n_blocks,),
            in_specs=[
                pl.BlockSpec((EXP_ROWS, d), lambda i, be, nu, *_: (blk(i, be, nu), 0)),
                pl.BlockSpec(memory_space=pl.ANY),
                pl.BlockSpec(memory_space=pl.ANY),
            ],
            out_specs=pl.BlockSpec((EXP_ROWS, d), lambda i, be, nu, *_: (blk(i, be, nu), 0)),
            scratch_shapes=[
                pltpu.VMEM((2, d, 2 * EXPERT_HIDDEN), F32), pltpu.VMEM((2, EXPERT_HIDDEN, d), F32),
                pltpu.VMEM((d, 2 * EXPERT_HIDDEN), BF16), pltpu.VMEM((EXPERT_HIDDEN, d), BF16),
                pltpu.SemaphoreType.DMA((2, 2)),
            ],
        ),
        out_shape=jax.ShapeDtypeStruct((rows_max, d), F32),
        compiler_params=pltpu.CompilerParams(dimension_semantics=("arbitrary",), vmem_limit_bytes=VMEM_LIMIT),
        name="moe_experts",
    )(blk_expert, n_used, first, w_slot.astype(i32), nxt_expert.astype(i32), xs_hbm, w_gate_up, w_down)

    return pl.pallas_call(
        functools.partial(_combine_kernel, ln_rows=min(256, tile), max_big=max_big, max_small=max_small),
        grid_spec=pltpu.PrefetchScalarGridSpec(
            num_scalar_prefetch=3,
            grid=(n_tiles,),
            in_specs=[
                smem_spec(), smem_spec(), smem_spec(), smem_spec(),
                row_groups_spec,
                pl.BlockSpec(memory_space=pl.ANY),
                pl.BlockSpec((1, d), lambda t, *_: (0, 0)),
                pl.BlockSpec((1, d), lambda t, *_: (0, 0)),
            ],
            out_specs=row_groups_spec,
            scratch_shapes=[pltpu.VMEM((2, local_rows, d), F32), pltpu.SemaphoreType.DMA((2,))],
        ),
        out_shape=jax.ShapeDtypeStruct(x1_groups.shape, F32),
        compiler_params=pltpu.CompilerParams(dimension_semantics=("arbitrary",), vmem_limit_bytes=VMEM_LIMIT),
        name="moe_combine_ln2",
    )(flat(piece_loc), flat(piece_glob), flat(tot), pos[0], pos[1], r_wgt[0], r_wgt[1], x1_groups, ys_hbm,
      g2, b2).reshape(n, d)


def _block(x, w_in, b_in, lower_bound, hg_norm_g, rel_bias, w_proj_a, w_proj_b, w_out, ln1_g, ln1_b,
           w_group, b_group, w_expert, b_expert, w_gate_up, w_down, ln2_g, ln2_b, *, tm_proj, tm_merge):
    b, s, d = x.shape
    n = b * s
    n_hg = 4 * HG_HEADS * HG_DK
    n_mb = 3 * MB_HEADS * MB_DH
    n_gt = 2 * d
    x2 = x.reshape(n, d)
    def head_major(a):
        lead = a.shape[:-1]
        hg_cols = a[..., :n_hg].reshape(*lead, 4, HG_HEADS, HG_DK)
        return jnp.concatenate([jnp.swapaxes(hg_cols, -3, -2).reshape(*lead, n_hg), a[..., n_hg:]], axis=-1)

    hg, mb, gates = _in_proj(x2, head_major(w_in.astype(BF16)), head_major(b_in).reshape(1, -1),
                             n_hg, n_mb, n_gt, tm_proj)
    hg_o = _hgrn(hg.reshape(b, s, n_hg), lower_bound.reshape(1, -1), hg_norm_g.reshape(1, -1))
    mb_o = _moba(mb.reshape(b, s, n_mb), rel_bias)
    w_r = jnp.zeros((d, LANES), F32).at[:, :N_GROUPS].set(w_group).at[:, N_GROUPS:N_GROUPS + N_EXPERTS].set(w_expert)
    b_r = jnp.zeros((1, LANES), F32).at[0, :N_GROUPS].set(b_group).at[0, N_GROUPS:N_GROUPS + N_EXPERTS].set(b_expert)
    tm_merge = min(tm_merge, n)
    x1, route, cnt_run = _merge(hg_o.reshape(n, -1), mb_o.reshape(n, -1), gates, x2,
                                w_proj_a.astype(BF16), w_proj_b.astype(BF16), w_out.astype(BF16),
                                ln1_g.reshape(1, d), ln1_b.reshape(1, d), jnp.stack(_split2(w_r)), b_r,
                                tm_merge, _moe_tile(n))
    out = _moe(x1, route, cnt_run, w_gate_up, w_down, ln2_g.reshape(1, d), ln2_b.reshape(1, d))
    return out.reshape(b, s, d)


def kernel(x, w_in, b_in, lb_logits, hg_norm_g, rel_bias, w_proj_a, w_proj_b, w_out, ln1_g, ln1_b, w_group,
           b_group, w_expert, b_expert, w_gate_up, w_down, ln2_g, ln2_b):
    lower_bounds = jnp.cumsum(jax.nn.softmax(lb_logits.astype(F32), axis=0), axis=0)
    l = 0
    return _block(x, w_in[l], b_in[l], lower_bounds[l], hg_norm_g[l], rel_bias, w_proj_a[l], w_proj_b[l],
                  w_out[l], ln1_g[l], ln1_b[l], w_group[l], b_group[l], w_expert[l], b_expert[l],
                  w_gate_up[l], w_down[l], ln2_g[l], ln2_b[l], tm_proj=512, tm_merge=1024)
```

```python
import functools
import math

import numpy as np
import jax
import jax.numpy as jnp
from jax import lax
from jax.experimental import pallas as pl
from jax.experimental.pallas import tpu as pltpu

F32 = jnp.float32
BF16 = jnp.bfloat16

HG_HEADS = 4
HG_DK = 128
HG_CHUNK = 128
HG_GROUP = 8
MB_HEADS = 8
MB_DH = 64
MB_BLOCK = 256
MB_TOPK = 3
PIPE_AHEAD = 2
DEN_ROWS = 16
GATE_ROWS = 16
MERGE_SLAB = 512
ROUTE_FIELDS = 8
REL_BUCKETS = 32
REL_MAX_DIST = 128
N_GROUPS = 4
EXPERTS_PER_GROUP = 8
N_EXPERTS = N_GROUPS * EXPERTS_PER_GROUP
TOP_K = 2
EXPERT_HIDDEN = 512
MOE_TILE = 1024
SEG_PAD = 16
SEG_BIG = 64
EXP_ROWS = 512
DEPTH = 1
DN_ALPHA = (2.0 * DEPTH) ** 0.25
NORM_EPS = 1e-5
LANES = 128
SUBLANES = 8
VMEM_LIMIT = 56 * 1024 * 1024
NEG_INF = float("-inf")
LOG2_E = 1.4426950408889634


def _split2(a):
    hi = a.astype(BF16)
    lo = (a - hi.astype(F32)).astype(BF16)
    return hi, lo


def _split3(a):
    hi = a.astype(BF16)
    r = a - hi.astype(F32)
    mid = r.astype(BF16)
    lo = (r - mid.astype(F32)).astype(BF16)
    return hi, mid, lo


def _dot_nt(a, b):
    return lax.dot_general(a, b, (((1,), (1,)), ((), ())), preferred_element_type=F32)


def _dot_tn(a, b):
    return lax.dot_general(a, b, (((0,), (0,)), ((), ())), preferred_element_type=F32)


def _dot(a, b):
    return jnp.dot(a, b, preferred_element_type=F32)


def _in_proj_kernel(x_ref, w_ref, b_ref, hg_ref, mb_ref, gt_ref, *, col_chunk):
    xb = x_ref[...].astype(BF16)
    outs = ((hg_ref, 0), (mb_ref, hg_ref.shape[1]), (gt_ref, hg_ref.shape[1] + mb_ref.shape[1]))
    for o_ref, base in outs:
        for c0 in range(0, o_ref.shape[1], col_chunk):
            acc = _dot(xb, w_ref[:, base + c0:base + c0 + col_chunk])
            o_ref[:, c0:c0 + col_chunk] = acc + b_ref[:, base + c0:base + c0 + col_chunk]


def _in_proj(x2, w_bf, b_in, n_hg, n_mb, n_gt, tm):
    n, d = x2.shape
    cols = w_bf.shape[1]
    return pl.pallas_call(
        functools.partial(_in_proj_kernel, col_chunk=512),
        grid=(n // tm,),
        in_specs=[
            pl.BlockSpec((tm, d), lambda i: (i, 0)),
            pl.BlockSpec((d, cols), lambda i: (0, 0), pipeline_mode=pl.Buffered(1)),
            pl.BlockSpec((1, cols), lambda i: (0, 0)),
        ],
        out_specs=[
            pl.BlockSpec((tm, n_hg), lambda i: (i, 0)),
            pl.BlockSpec((tm, n_mb), lambda i: (i, 0)),
            pl.BlockSpec((tm, n_gt), lambda i: (i, 0)),
        ],
        out_shape=[
            jax.ShapeDtypeStruct((n, n_hg), F32),
            jax.ShapeDtypeStruct((n, n_mb), F32),
            jax.ShapeDtypeStruct((n, n_gt), F32),
        ],
        compiler_params=pltpu.CompilerParams(
            dimension_semantics=("parallel",), vmem_limit_bytes=VMEM_LIMIT),
        name="in_proj",
    )(x2, w_bf, b_in)


def _hgrn_tables():
    c = HG_CHUNK
    levels = [c >> (i + 1) for i in range(int(math.log2(c)))]
    masks = np.zeros((len(levels) + 1, c, c), np.float32)
    for li, m in enumerate(levels):
        for r in range(c):
            c0 = (r // (2 * m)) * (2 * m)
            if r - c0 >= m:
                masks[li, r, c0:c0 + m] = 1.0
    masks[len(levels)] = np.eye(c, dtype=np.float32)
    return np.tril(np.ones((c, c), np.float32)), masks, tuple(levels)


def _level_ref_rows(p, m):
    c, w = p.shape
    if 2 * m > SUBLANES:
        parts = [jnp.broadcast_to(p[c0 + m - 1:c0 + m, :], (2 * m, w)) for c0 in range(0, c, 2 * m)]
        return parts[0] if len(parts) == 1 else jnp.concatenate(parts, axis=0)
    p3 = p.reshape(c // SUBLANES, SUBLANES, w)
    sub = lax.broadcasted_iota(jnp.int32, p3.shape, 1)
    out = None
    for c0 in range(0, SUBLANES, 2 * m):
        b = jnp.broadcast_to(p3[:, c0 + m - 1:c0 + m, :], p3.shape)
        out = b if out is None else jnp.where(sub >= c0, b, out)
    return out.reshape(c, w)


def _hgrn_kernel(q_ref, f_ref, i_ref, g_ref, lb_ref, ng_ref, tril_ref, msk_ref, o_ref,
                 *, n_chunks, levels, group):
    c = HG_CHUNK
    dk = HG_DK
    assert n_chunks % group == 0
    lb = jnp.concatenate([lb_ref[...]] * group, axis=1)
    oml = 1.0 - lb
    ng = ng_ref[...]
    tril = tril_ref[...]
    n_levels = len(levels)

    def load(ref, r0):
        return jnp.concatenate([ref[0, pl.ds(r0 + u * c, c), :] for u in range(group)], axis=1)

    def lanes(a, u):
        return a[:, u * dk:(u + 1) * dk]

    def intra(r0):
        z = load(f_ref, r0)
        qr = load(q_ref, r0)
        lf = jnp.log(lb + oml * jax.nn.sigmoid(z))
        kk = oml * jax.nn.sigmoid(-z)
        qf = qr * jax.nn.sigmoid(qr)
        l_hi, l_mid, l_lo = _split3(lf)
        p = (_dot(tril, l_hi) + _dot(tril, l_mid) + _dot(tril, l_lo)) * LOG2_E
        b_end = p[c - 1:c, :]
        qb = (qf * jnp.exp2(p)).astype(BF16)
        kd = (kk * jnp.exp2(b_end - p)).astype(BF16)
        dec = jnp.exp2(b_end)
        qh = qf.astype(BF16)
        kh = kk.astype(BF16)
        scores = [msk_ref[n_levels] * _dot_nt(lanes(qh, u), lanes(kh, u)) for u in range(group)]
        row = lax.broadcasted_iota(jnp.int32, p.shape, 0)
        for li, m in enumerate(levels):
            ref_rows = _level_ref_rows(p, m)
            if m >= SUBLANES:
                expo = jnp.concatenate([(p[a:a + m] - ref_rows[a:a + m]) if (a // m) % 2 else
                                        (ref_rows[a:a + m] - p[a:a + m]) for a in range(0, c, m)], axis=0)
            else:
                expo = -jnp.abs(p - ref_rows)
            ex = jnp.exp2(expo)
            qk = (jnp.where((row & (2 * m - 1)) >= m, qf, kk) * ex).astype(BF16)
            for u in range(group):
                scores[u] = scores[u] + msk_ref[li] * _dot_nt(lanes(qk, u), lanes(qk, u))
        vbs = [i_ref[0, pl.ds(r0 + u * c, c), :].astype(BF16) for u in range(group)]
        o_intra = [_dot(scores[u].astype(BF16), vbs[u]) for u in range(group)]
        return qb, kd, dec, vbs, o_intra

    def chain(r0, staged, st):
        qb, kd, dec, vbs, o_intra = staged
        for u in range(group):
            rows = pl.ds(r0 + u * c, c)
            g = g_ref[0, rows, :]
            o = _dot_nt(lanes(qb, u), st.astype(BF16)) + o_intra[u]
            st = st * lanes(dec, u) + _dot_tn(vbs[u], lanes(kd, u))
            o = o * lax.rsqrt(jnp.mean(o * o, axis=-1, keepdims=True) + NORM_EPS)
            o_ref[0, rows, :] = o * ng * (g * jax.nn.sigmoid(g))
        return st

    n_groups = n_chunks // group
    st = jnp.zeros((HG_DK, HG_DK), F32)
    staged = intra(0)
    for gi in range(n_groups):
        nxt = intra((gi + 1) * group * c) if gi + 1 < n_groups else None
        st = chain(gi * group * c, staged, st)
        staged = nxt


def _hgrn(hg3, lb_row, ng_row):
    b, s, _ = hg3.shape
    tril, masks, levels = _hgrn_tables()
    mst = jnp.asarray(tril, BF16)
    msk = jnp.asarray(masks, F32)
    h = HG_HEADS

    def col(off):
        return pl.BlockSpec((1, s, HG_DK), lambda bi, hi: (bi, 0, off + hi))

    return pl.pallas_call(
        functools.partial(_hgrn_kernel, n_chunks=s // HG_CHUNK, levels=levels,
                          group=math.gcd(s // HG_CHUNK, HG_GROUP)),
        grid=(b, h),
        in_specs=[
            col(0), col(h), col(2 * h), col(3 * h),
            pl.BlockSpec((1, HG_DK), lambda bi, hi: (0, hi)),
            pl.BlockSpec((1, HG_DK), lambda bi, hi: (0, hi)),
            pl.BlockSpec(mst.shape, lambda bi, hi: (0, 0)),
            pl.BlockSpec(msk.shape, lambda bi, hi: (0, 0, 0)),
        ],
        out_specs=pl.BlockSpec((1, s, HG_DK), lambda bi, hi: (bi, 0, hi)),
        out_shape=jax.ShapeDtypeStruct((b, s, h * HG_DK), F32),
        compiler_params=pltpu.CompilerParams(
            dimension_semantics=("parallel", "parallel"), vmem_limit_bytes=VMEM_LIMIT),
        name="hgrn2",
    )(hg3, hg3, hg3, hg3, lb_row, ng_row, mst, msk)


def _t5_bucket_np(dist):
    max_exact = REL_BUCKETS // 2
    d = np.maximum(dist, 1).astype(np.float32)
    log_part = max_exact + (np.log(d / np.float32(max_exact)) / np.float32(math.log(REL_MAX_DIST / max_exact))
                            * np.float32(REL_BUCKETS - max_exact)).astype(np.int32)
    return np.where(dist < max_exact, dist, np.minimum(log_part, REL_BUCKETS - 1))


def _moba_kernel(q_ref, k_ref, v_ref, avg_ref, bkt_ref, rb_ref, o_ref, own_ref, prev_ref, *, n_blocks):
    blk = MB_BLOCK
    scale = MB_DH ** -0.5 * LOG2_E
    avg = avg_ref[...]
    hp = LANES // MB_DH

    @pl.when(pl.program_id(1) == 0)
    def _():
        causal = (lax.broadcasted_iota(jnp.int32, (blk, blk), 0) <= lax.broadcasted_iota(jnp.int32, (blk, blk), 1))
        for hh in range(hp):
            head = pl.program_id(0) * hp + hh
            own_t = jnp.zeros((blk, blk), F32)
            prev_t = jnp.zeros((blk, blk), F32)
            for bk in range(REL_BUCKETS):
                val = rb_ref[bk, head] * LOG2_E
                own_t = jnp.where(bkt_ref[0] == bk, val, own_t)
                prev_t = jnp.where(bkt_ref[1] == bk, val, prev_t)
            own_ref[hh] = jnp.where(causal, own_t, NEG_INF)
            prev_ref[hh] = prev_t

    grow = lax.broadcasted_iota(jnp.int32, (GATE_ROWS, blk), 0)
    vt_all = v_ref[0].T

    def fold(a):
        return a.reshape(blk // SUBLANES, SUBLANES, blk)

    heads = []
    for hh in range(hp):
        ls = slice(hh * MB_DH, (hh + 1) * MB_DH)
        k_h = k_ref[0, :, ls]
        k_hi, k_lo = _split2(k_h)
        k_mean = (_dot(avg, k_hi) + _dot(avg, k_lo))[:GATE_ROWS]
        vt = jnp.concatenate([vt_all[hh * MB_DH:(hh + 1) * MB_DH, :], jnp.ones((DEN_ROWS, vt_all.shape[1]), F32)],
                             axis=0).astype(BF16)
        far_bias = rb_ref[REL_BUCKETS - 1, pl.program_id(0) * hp + hh] * LOG2_E
        heads.append((ls, _split2(k_mean), k_h.astype(BF16), vt, far_bias))

    def logits(hh, i):
        ls, (km_hi, km_lo), kb, _, far_bias = heads[hh]
        qi = q_ref[0, i * blk:(i + 1) * blk, ls] * scale
        st = _dot_nt(kb[:(i + 1) * blk], qi.astype(BF16))
        sel = None
        if i > 0:
            q_hi, q_lo = _split2(qi)
            gate = _dot_nt(km_hi, q_hi) + _dot_nt(km_lo, q_hi) + _dot_nt(km_hi, q_lo)
            gate = jnp.where(grow < i, gate, NEG_INF)
            rank = jnp.zeros((GATE_ROWS, blk), F32)
            if i > MB_TOPK:
                for j2 in range(i):
                    gj = jnp.broadcast_to(gate[j2:j2 + 1, :], (GATE_ROWS, blk))
                    tie = jnp.where(grow > j2, 1.0, 0.0)
                    rank = rank + jnp.where(gj > gate, 1.0, jnp.where(gj == gate, tie, 0.0))
            sel = jnp.where(rank < MB_TOPK, gate, NEG_INF) > NEG_INF
        pieces = []
        for j in range(i + 1):
            sj = st[j * blk:(j + 1) * blk, :]
            if j == i:
                pieces.append(sj + own_ref[hh])
            elif j == i - 1:
                pieces.append(sj + prev_ref[hh] + jnp.where(sel[j:j + 1, :], 0.0, NEG_INF))
            else:
                pieces.append(sj + jnp.where(sel[j:j + 1, :], far_bias, NEG_INF))
        mx8 = fold(pieces[0]).max(axis=0)
        for p in pieces[1:]:
            mx8 = jnp.maximum(mx8, fold(p).max(axis=0))
        return pieces, mx8.max(axis=0, keepdims=True)

    def attend(hh, pieces, mx):
        vt = heads[hh][3]
        weights = jnp.concatenate([jnp.exp2(p - mx).astype(BF16) for p in pieces], axis=0)
        acc = _dot(vt[:, :len(pieces) * blk], weights)
        return acc[:MB_DH] / acc[MB_DH:MB_DH + 1]

    items = [(i, hh) for i in range(n_blocks) for hh in range(hp)]
    staged = [logits(hh, i) for i, hh in items[:PIPE_AHEAD]]
    done = {}
    for n, (i, hh) in enumerate(items):
        if n + PIPE_AHEAD < len(items):
            staged.append(logits(items[n + PIPE_AHEAD][1], items[n + PIPE_AHEAD][0]))
        done[hh] = attend(hh, *staged.pop(0))
        if hh == hp - 1:
            o_ref[0, i * blk:(i + 1) * blk, :] = jnp.concatenate([done[h] for h in range(hp)], axis=0).T


def _moba(mb3, rel_bias):
    b, s, _ = mb3.shape
    assert s % MB_BLOCK == 0
    nb = s // MB_BLOCK
    hp = LANES // MB_DH
    n_hp = MB_HEADS // hp
    t = np.arange(MB_BLOCK)
    d_own = t[:, None] - t[None, :]
    buckets = np.stack([_t5_bucket_np(np.maximum(d_own, 0)), _t5_bucket_np(d_own + MB_BLOCK)]).astype(np.int32)
    assert _t5_bucket_np(np.array([MB_BLOCK + 1]))[0] == REL_BUCKETS - 1
    assert nb <= GATE_ROWS
    bkt = jnp.asarray(buckets.transpose(0, 2, 1))
    avg_np = np.zeros((LANES, s), np.float32)
    for j in range(nb):
        avg_np[j, j * MB_BLOCK:(j + 1) * MB_BLOCK] = 1.0 / MB_BLOCK
    avg = jnp.asarray(avg_np, BF16)

    def col(off):
        return pl.BlockSpec((1, s, LANES), lambda hi, bi: (bi, 0, off + hi))

    return pl.pallas_call(
        functools.partial(_moba_kernel, n_blocks=nb),
        grid=(n_hp, b),
        in_specs=[
            col(0), col(n_hp), col(2 * n_hp),
            pl.BlockSpec(avg.shape, lambda hi, bi: (0, 0)),
            pl.BlockSpec(bkt.shape, lambda hi, bi: (0, 0, 0)),
            pl.BlockSpec(memory_space=pltpu.SMEM),
        ],
        out_specs=pl.BlockSpec((1, s, LANES), lambda hi, bi: (bi, 0, hi)),
        out_shape=jax.ShapeDtypeStruct((b, s, MB_HEADS * MB_DH), F32),
        scratch_shapes=[pltpu.VMEM((hp, MB_BLOCK, MB_BLOCK), F32), pltpu.VMEM((hp, MB_BLOCK, MB_BLOCK), F32)],
        compiler_params=pltpu.CompilerParams(
            dimension_semantics=("parallel", "arbitrary"), vmem_limit_bytes=VMEM_LIMIT),
        name="moba",
    )(mb3, mb3, mb3, avg, bkt, rel_bias.astype(F32))


def _layer_norm(x, g, b):
    mu = jnp.mean(x, axis=-1, keepdims=True)
    xc = x - mu
    var = jnp.mean(xc * xc, axis=-1, keepdims=True)
    return xc * lax.rsqrt(var + NORM_EPS) * g + b


def _merge_kernel(hg_ref, mb_ref, ga_ref, gb_ref, x_ref, wa_ref, wb_ref, wo_ref, g1_ref, b1_ref,
                  wr_ref, br_ref, tri_ref, x1_ref, route_ref, cnt_ref, run_ref, *, blocks_per_moe_tile, parts):
    tm = x_ref.shape[0]
    pr = tm // parts
    lane = lax.broadcasted_iota(jnp.int32, (pr, LANES), 1)

    @pl.when(pl.program_id(0) % blocks_per_moe_tile == 0)
    def _():
        run_ref[...] = jnp.zeros_like(run_ref)

    def mix(rows):
        ya = _dot(hg_ref[rows, :].astype(BF16), wa_ref[...])
        yb = _dot(mb_ref[rows, :].astype(BF16), wb_ref[...])
        mixed_in = jax.nn.sigmoid(ga_ref[rows, :]) * ya + jax.nn.sigmoid(gb_ref[rows, :]) * yb
        mixed = _dot(mixed_in.astype(BF16), wo_ref[...])
        x1 = _layer_norm(DN_ALPHA * x_ref[rows, :] + mixed, g1_ref[...], b1_ref[...])
        x1_ref[rows, :] = x1
        return x1

    def route_rows(rows, x1, run):
        x_hi, x_lo = _split2(x1)
        w_hi, w_lo = wr_ref[0], wr_ref[1]
        hr = pr // 2
        logits = jnp.concatenate(
            [_dot(x_hi[r:r + hr], w_hi) + _dot(x_hi[r:r + hr], w_lo) + _dot(x_lo[r:r + hr], w_hi) for r in (0, hr)],
            axis=0) + br_ref[...]
        glog = jnp.where(lane < N_GROUPS, logits, NEG_INF)
        gmax = jnp.max(glog, axis=-1, keepdims=True)
        grp = jnp.min(jnp.where(glog == gmax, lane, LANES), axis=-1, keepdims=True)
        p_grp = 1.0 / jnp.sum(jnp.exp(glog - gmax), axis=-1, keepdims=True)
        e_lo = N_GROUPS + grp * EXPERTS_PER_GROUP
        elog = jnp.where(jnp.logical_and(lane >= e_lo, lane < e_lo + EXPERTS_PER_GROUP), logits, NEG_INF)
        m1 = jnp.max(elog, axis=-1, keepdims=True)
        i1 = jnp.min(jnp.where(elog == m1, lane, LANES), axis=-1, keepdims=True)
        elog2 = jnp.where(lane == i1, NEG_INF, elog)
        m2 = jnp.max(elog2, axis=-1, keepdims=True)
        i2 = jnp.min(jnp.where(elog2 == m2, lane, LANES), axis=-1, keepdims=True)
        e2 = jnp.exp(m2 - m1)
        w1 = p_grp / (1.0 + e2)
        w2 = p_grp * e2 / (1.0 + e2)
        oh1 = jnp.where(lane == i1 - N_GROUPS, 1.0, 0.0)
        oh2 = jnp.where(lane == i2 - N_GROUPS, 1.0, 0.0)
        both = oh1 + oh2
        before = _dot(tri_ref[...], both.astype(BF16)) + run
        r1 = jnp.sum(before * oh1, axis=-1, keepdims=True)
        r2 = jnp.sum(before * oh2, axis=-1, keepdims=True)
        cols = ((i1 - N_GROUPS).astype(F32), (i2 - N_GROUPS).astype(F32), w1, w2, r1, r2)
        route = jnp.zeros((pr, LANES), F32)
        for li, col in enumerate(cols):
            route = jnp.where(lane == li, col, route)
        route_ref[:, rows] = route.T[:ROUTE_FIELDS, :]
        return run + jnp.sum(both, axis=0, keepdims=True)

    slabs = [slice(h * pr, (h + 1) * pr) for h in range(parts)]
    mixed = [mix(rows) for rows in slabs]
    run = run_ref[...]
    for rows, x1 in zip(slabs, mixed):
        run = route_rows(rows, x1, run)
    run_ref[...] = run
    cnt_ref[0] = run


def _merge(hg_o, mb_o, gates, x2, wa, wb, wo, g1, b1, wr, br, tm, moe_tile):
    n, d = x2.shape
    wa_n = hg_o.shape[1]
    wb_n = mb_o.shape[1]
    assert moe_tile % tm == 0
    parts = max(1, tm // MERGE_SLAB)
    tri = jnp.asarray(np.tril(np.ones((tm // parts, tm // parts), np.float32), -1), BF16)

    def full(a):
        nd = a.ndim
        return pl.BlockSpec(a.shape, lambda i: (0,) * nd, pipeline_mode=pl.Buffered(1))

    return pl.pallas_call(
        functools.partial(_merge_kernel, blocks_per_moe_tile=moe_tile // tm, parts=parts),
        grid=(n // tm,),
        in_specs=[
            pl.BlockSpec((tm, wa_n), lambda i: (i, 0)),
            pl.BlockSpec((tm, wb_n), lambda i: (i, 0)),
            pl.BlockSpec((tm, d), lambda i: (i, 0)),
            pl.BlockSpec((tm, d), lambda i: (i, 1)),
            pl.BlockSpec((tm, d), lambda i: (i, 0)),
            full(wa), full(wb), full(wo), full(g1), full(b1), full(wr), full(br), full(tri),
        ],
        out_specs=[
            pl.BlockSpec((tm, d), lambda i: (i, 0)),
            pl.BlockSpec((ROUTE_FIELDS, tm), lambda i: (0, i)),
            pl.BlockSpec((1, 1, LANES), lambda i: (i, 0, 0)),
        ],
        out_shape=[
            jax.ShapeDtypeStruct((n, d), F32),
            jax.ShapeDtypeStruct((ROUTE_FIELDS, n), F32),
            jax.ShapeDtypeStruct((n // tm, 1, LANES), F32),
        ],
        scratch_shapes=[pltpu.VMEM((1, LANES), F32)],
        compiler_params=pltpu.CompilerParams(
            dimension_semantics=("arbitrary",), vmem_limit_bytes=VMEM_LIMIT),
        name="merge_ln1_router",
    )(hg_o, mb_o, gates, gates, x2, wa, wb, wo, g1, b1, wr, br, tri)


def _tile_copies(t, loc_ref, glob_ref, tot_ref, max_big, max_small, make_copy, act):
    base = t * (max_big + max_small)

    def big(q, carry):
        act(make_copy(loc_ref[base + q], glob_ref[base + q], SEG_BIG))
        return carry
    lax.fori_loop(0, tot_ref[2 * t], big, 0)

    def small(q, carry):
        act(make_copy(loc_ref[base + max_big + q], glob_ref[base + max_big + q], SEG_PAD))
        return carry
    lax.fori_loop(0, tot_ref[2 * t + 1], small, 0)


def _dispatch_kernel(loc_ref, glob_ref, tot_ref, gap_ref, p0_ref, p1_ref, x1_ref, xs_hbm,
                     xs_ref, zero_ref, sem, *, cv_rows, max_big, max_small):
    ti = pl.program_id(0)
    n_tiles = pl.num_programs(0)
    groups, _, d = x1_ref.shape
    slot = ti % 2

    def tile_copies(t, sl, act):
        def out_copy(loc, glob, size):
            src = pl.multiple_of(loc, SEG_PAD)
            dst = pl.multiple_of(glob, SEG_PAD)
            return pltpu.make_async_copy(xs_ref.at[sl, pl.ds(src, size), :], xs_hbm.at[pl.ds(dst, size), :],
                                         sem.at[sl])
        _tile_copies(t, loc_ref, glob_ref, tot_ref, max_big, max_small, out_copy, act)

    @pl.when(ti == 0)
    def _():
        def zero(c, carry):
            r0 = pl.multiple_of(c * cv_rows, cv_rows)
            for sl in range(2):
                xs_ref[sl, pl.ds(r0, cv_rows), :] = jnp.zeros((cv_rows, d), F32)
            return carry
        lax.fori_loop(0, xs_ref.shape[1] // cv_rows, zero, 0)
        zero_ref[...] = jnp.zeros_like(zero_ref)

    @pl.when(ti >= 2)
    def _():
        tile_copies(ti - 2, slot, lambda cp: cp.wait())

    for sl in range(2):
        @pl.when(slot == sl)
        def _(sl=sl):
            def sort_rows(grp, carry):
                t0 = grp * SUBLANES
                for k in range(SUBLANES):
                    row = x1_ref[grp, pl.ds(k, 1), :]
                    xs_ref[sl, pl.ds(p0_ref[0, 0, t0 + k], 1), :] = row
                    xs_ref[sl, pl.ds(p1_ref[0, 0, t0 + k], 1), :] = row
                return carry
            lax.fori_loop(0, groups, sort_rows, 0)

    tile_copies(ti, slot, lambda cp: cp.start())

    @pl.when(ti == n_tiles - 1)
    def _():
        @pl.when(ti >= 1)
        def _():
            tile_copies(ti - 1, 1 - slot, lambda cp: cp.wait())
        tile_copies(ti, slot, lambda cp: cp.wait())

        def fill_copy(row):
            return pltpu.make_async_copy(zero_ref, xs_hbm.at[pl.ds(pl.multiple_of(row, SEG_PAD), SEG_PAD), :],
                                         sem.at[2])

        def fill(ex, carry):
            def one(c, carry2):
                fill_copy(gap_ref[2 * ex] + c * SEG_PAD).start()
                return carry2
            lax.fori_loop(0, gap_ref[2 * ex + 1], one, 0)
            return carry
        lax.fori_loop(0, N_EXPERTS, fill, 0)

        def fill_wait(ex, carry):
            def one(c, carry2):
                fill_copy(0).wait()
                return carry2
            lax.fori_loop(0, gap_ref[2 * ex + 1], one, 0)
            return carry
        lax.fori_loop(0, N_EXPERTS, fill_wait, 0)


def _expert_kernel(be_ref, nu_ref, first_ref, slot_ref, nxt_ref, x_ref, wgu_hbm, wd_hbm, y_ref,
                   wgu_f32, wd_f32, wgu_bf, wd_bf, sem, *, cast_rows):
    i = pl.program_id(0)

    def fetch(expert, slot):
        return (pltpu.make_async_copy(wgu_hbm.at[expert], wgu_f32.at[slot], sem.at[0, slot]),
                pltpu.make_async_copy(wd_hbm.at[expert], wd_f32.at[slot], sem.at[1, slot]))

    @pl.when(i < nu_ref[0])
    def _():
        slot = slot_ref[i]

        @pl.when(i == 0)
        def _():
            for cp in fetch(be_ref[0], 0):
                cp.start()

        @pl.when(first_ref[i] == 1)
        def _():
            for cp in fetch(be_ref[i], slot):
                cp.wait()
            for r0 in range(0, wgu_bf.shape[0], cast_rows):
                wgu_bf[r0:r0 + cast_rows, :] = wgu_f32[slot, r0:r0 + cast_rows, :].astype(BF16)
            for r0 in range(0, wd_bf.shape[0], cast_rows):
                wd_bf[r0:r0 + cast_rows, :] = wd_f32[slot, r0:r0 + cast_rows, :].astype(BF16)

            @pl.when(nxt_ref[i] >= 0)
            def _():
                for cp in fetch(nxt_ref[i], 1 - slot):
                    cp.start()

        gu = _dot(x_ref[...].astype(BF16), wgu_bf[...])
        gate = gu[:, :EXPERT_HIDDEN]
        up = gu[:, EXPERT_HIDDEN:]
        hdn = (gate * jax.nn.sigmoid(gate) * up).astype(BF16)
        y_ref[...] = _dot(hdn, wd_bf[...])


def _combine_kernel(loc_ref, glob_ref, tot_ref, p0_ref, p1_ref, w0_ref, w1_ref, x1_ref, ys_hbm, g2_ref, b2_ref,
                    o_ref, xs_ref, sem, *, ln_rows, max_big, max_small):
    ti = pl.program_id(0)
    n_tiles = pl.num_programs(0)
    groups, _, d = x1_ref.shape
    slot = ti % 2

    def tile_copies(t, sl, act):
        def in_copy(loc, glob, size):
            src = pl.multiple_of(glob, SEG_PAD)
            dst = pl.multiple_of(loc, SEG_PAD)
            return pltpu.make_async_copy(ys_hbm.at[pl.ds(src, size), :], xs_ref.at[sl, pl.ds(dst, size), :],
                                         sem.at[sl])
        _tile_copies(t, loc_ref, glob_ref, tot_ref, max_big, max_small, in_copy, act)

    @pl.when(ti == 0)
    def _():
        tile_copies(0, 0, lambda cp: cp.start())

    @pl.when(ti + 1 < n_tiles)
    def _():
        tile_copies(ti + 1, 1 - slot, lambda cp: cp.start())

    tile_copies(ti, slot, lambda cp: cp.wait())

    def combine(grp, carry):
        t0 = grp * SUBLANES
        for k in range(SUBLANES):
            t = t0 + k
            o_ref[grp, pl.ds(k, 1), :] = (DN_ALPHA * x1_ref[grp, pl.ds(k, 1), :]
                                          + w0_ref[0, 0, t] * xs_ref[slot, pl.ds(p0_ref[0, 0, t], 1), :]
                                          + w1_ref[0, 0, t] * xs_ref[slot, pl.ds(p1_ref[0, 0, t], 1), :])
        return carry
    lax.fori_loop(0, groups, combine, 0, unroll=8)

    ln_groups = ln_rows // SUBLANES

    def norm(c, carry):
        g0 = pl.multiple_of(c * ln_groups, ln_groups)
        rows = o_ref[pl.ds(g0, ln_groups), :, :].reshape(ln_rows, d)
        o_ref[pl.ds(g0, ln_groups), :, :] = _layer_norm(rows, g2_ref[...], b2_ref[...]).reshape(ln_groups, SUBLANES, d)
        return carry
    lax.fori_loop(0, groups // ln_groups, norm, 0)


def _moe_tile(n):
    return min(MOE_TILE, n)


def _round_up(a, m):
    return (a + m - 1) // m * m


def _moe(x1, route, cnt_run, w_gate_up, w_down, g2, b2):
    n, d = x1.shape
    tile = _moe_tile(n)
    assert n % tile == 0
    n_tiles = n // tile
    i32 = jnp.int32
    cnt = cnt_run.reshape(n_tiles, -1, LANES)[:, -1, :N_EXPERTS].astype(i32)
    seg = _round_up(cnt, SEG_PAD)
    lbase = jnp.cumsum(seg, axis=1) - seg
    used = jnp.sum(seg, axis=0)
    region = _round_up(used, EXP_ROWS)
    e_start = jnp.cumsum(region) - region
    gseg = e_start[None, :] + jnp.cumsum(seg, axis=0) - seg
    n_bigs = seg // SEG_BIG
    n_smalls = (seg - n_bigs * SEG_BIG) // SEG_PAD
    max_big = TOP_K * tile // SEG_BIG
    max_small = N_EXPERTS * (SEG_BIG // SEG_PAD - 1)

    def piece_table(counts, first_off, step, width):
        incl = jnp.cumsum(counts, axis=1)
        excl = incl - counts
        q = jnp.arange(width, dtype=i32)[None, :, None]
        owner = (q >= excl[:, None, :]) & (q < incl[:, None, :])
        pick = lambda a: jnp.sum(jnp.where(owner, a[:, None, :], 0), axis=-1)
        off = pick(first_off) + (q[:, :, 0] - pick(excl)) * step
        return pick(lbase) + off, pick(gseg) + off

    big_loc, big_glob = piece_table(n_bigs, jnp.zeros_like(seg), SEG_BIG, max_big)
    small_loc, small_glob = piece_table(n_smalls, n_bigs * SEG_BIG, SEG_PAD, max_small)
    piece_loc = jnp.concatenate([big_loc, small_loc], axis=1)
    piece_glob = jnp.concatenate([big_glob, small_glob], axis=1)
    tot = jnp.stack([jnp.sum(n_bigs, axis=1), jnp.sum(n_smalls, axis=1)], axis=1)
    gap = jnp.stack([e_start + used, (region - used) // SEG_PAD], axis=1)
    rows_max = _round_up(TOP_K * n + n_tiles * N_EXPERTS * (SEG_PAD - 1) + N_EXPERTS * (EXP_ROWS - 1), EXP_ROWS)
    n_blocks = rows_max // EXP_ROWS
    n_used = (jnp.sum(region) // EXP_ROWS).astype(i32).reshape(1)
    blk_row = jnp.arange(n_blocks, dtype=i32)[:, None] * EXP_ROWS
    blk_expert = jnp.minimum(jnp.sum((blk_row >= jnp.cumsum(region)[None, :]).astype(i32), axis=1), N_EXPERTS - 1)
    r_exp = route[0:TOP_K].astype(i32).reshape(TOP_K, n_tiles, tile)
    r_rank = route[2 * TOP_K:3 * TOP_K].astype(i32).reshape(TOP_K, n_tiles, tile)
    seg_start = jnp.sum(jnp.where(r_exp[..., None] == jnp.arange(N_EXPERTS, dtype=i32), lbase[None, :, None, :], 0),
                        axis=-1)
    pos = (seg_start + r_rank).reshape(TOP_K, n_tiles, 1, tile)
    r_wgt = route[TOP_K:2 * TOP_K].reshape(TOP_K, n_tiles, 1, tile)
    local_rows = _round_up(TOP_K * tile + N_EXPERTS * (SEG_PAD - 1), 256)
    flat = lambda a: a.reshape(-1).astype(i32)

    def smem_spec():
        return pl.BlockSpec((1, 1, tile), lambda t, *_: (t, 0, 0), memory_space=pltpu.SMEM)

    x1_groups = x1.reshape(n // SUBLANES, SUBLANES, d)
    row_groups_spec = pl.BlockSpec((tile // SUBLANES, SUBLANES, d), lambda t, *_: (t, 0, 0))

    xs_hbm = pl.pallas_call(
        functools.partial(_dispatch_kernel, cv_rows=256, max_big=max_big, max_small=max_small),
        grid_spec=pltpu.PrefetchScalarGridSpec(
            num_scalar_prefetch=4,
            grid=(n_tiles,),
            in_specs=[smem_spec(), smem_spec(), row_groups_spec],
            out_specs=pl.BlockSpec(memory_space=pl.ANY),
            scratch_shapes=[
                pltpu.VMEM((2, local_rows, d), F32),
                pltpu.VMEM((SEG_PAD, d), F32),
                pltpu.SemaphoreType.DMA((3,)),
            ],
        ),
        out_shape=jax.ShapeDtypeStruct((rows_max, d), F32),
        compiler_params=pltpu.CompilerParams(dimension_semantics=("arbitrary",), vmem_limit_bytes=VMEM_LIMIT),
        name="moe_dispatch",
    )(flat(piece_loc), flat(piece_glob), flat(tot), flat(gap), pos[0], pos[1], x1_groups)

    def blk(i, be, nu):
        return jnp.minimum(i, nu[0] - 1)

    blk_idx = jnp.arange(n_blocks, dtype=i32)
    first = ((blk_idx == 0) | (blk_expert != jnp.roll(blk_expert, 1))).astype(i32)
    w_slot = (jnp.cumsum(first) - 1) % 2
    e_idx = jnp.arange(N_EXPERTS, dtype=i32)
    later_used = (region > 0)[None, :] & (e_idx[None, :] > e_idx[:, None])
    nxt_of = jnp.min(jnp.where(later_used, e_idx[None, :], N_EXPERTS), axis=1)
    nxt_of = jnp.where(nxt_of < N_EXPERTS, nxt_of, -1)
    nxt_expert = jnp.sum(jnp.where(blk_expert[:, None] == e_idx[None, :], nxt_of[None, :], 0), axis=1)

    ys_hbm = pl.pallas_call(
        functools.partial(_expert_kernel, cast_rows=256),
        grid_spec=pltpu.PrefetchScalarGridSpec(
            num_scalar_prefetch=5,
            grid=(n_blocks,),
            in_specs=[
                pl.BlockSpec((EXP_ROWS, d), lambda i, be, nu, *_: (blk(i, be, nu), 0)),
                pl.BlockSpec(memory_space=pl.ANY),
                pl.BlockSpec(memory_space=pl.ANY),
            ],
            out_specs=pl.BlockSpec((EXP_ROWS, d), lambda i, be, nu, *_: (blk(i, be, nu), 0)),
            scratch_shapes=[
                pltpu.VMEM((2, d, 2 * EXPERT_HIDDEN), F32), pltpu.VMEM((2, EXPERT_HIDDEN, d), F32),
                pltpu.VMEM((d, 2 * EXPERT_HIDDEN), BF16), pltpu.VMEM((EXPERT_HIDDEN, d), BF16),
                pltpu.SemaphoreType.DMA((2, 2)),
            ],
        ),
        out_shape=jax.ShapeDtypeStruct((rows_max, d), F32),
        compiler_params=pltpu.CompilerParams(dimension_semantics=("arbitrary",), vmem_limit_bytes=VMEM_LIMIT),
        name="moe_experts",
    )(blk_expert, n_used, first, w_slot.astype(i32), nxt_expert.astype(i32), xs_hbm, w_gate_up, w_down)

    return pl.pallas_call(
        functools.partial(_combine_kernel, ln_rows=min(256, tile), max_big=max_big, max_small=max_small),
        grid_spec=pltpu.PrefetchScalarGridSpec(
            num_scalar_prefetch=3,
            grid=(n_tiles,),
            in_specs=[
                smem_spec(), smem_spec(), smem_spec(), smem_spec(),
                row_groups_spec,
                pl.BlockSpec(memory_space=pl.ANY),
                pl.BlockSpec((1, d), lambda t, *_: (0, 0)),
                pl.BlockSpec((1, d), lambda t, *_: (0, 0)),
            ],
            out_specs=row_groups_spec,
            scratch_shapes=[pltpu.VMEM((2, local_rows, d), F32), pltpu.SemaphoreType.DMA((2,))],
        ),
        out_shape=jax.ShapeDtypeStruct(x1_groups.shape, F32),
        compiler_params=pltpu.CompilerParams(dimension_semantics=("arbitrary",), vmem_limit_bytes=VMEM_LIMIT),
        name="moe_combine_ln2",
    )(flat(piece_loc), flat(piece_glob), flat(tot), pos[0], pos[1], r_wgt[0], r_wgt[1], x1_groups, ys_hbm,
      g2, b2).reshape(n, d)


def _block(x, w_in, b_in, lower_bound, hg_norm_g, rel_bias, w_proj_a, w_proj_b, w_out, ln1_g, ln1_b,
           w_group, b_group, w_expert, b_expert, w_gate_up, w_down, ln2_g, ln2_b, *, tm_proj, tm_merge):
    b, s, d = x.shape
    n = b * s
    n_hg = 4 * HG_HEADS * HG_DK
    n_mb = 3 * MB_HEADS * MB_DH
    n_gt = 2 * d
    x2 = x.reshape(n, d)
    hg, mb, gates = _in_proj(x2, w_in.astype(BF16), b_in.reshape(1, -1), n_hg, n_mb, n_gt, tm_proj)
    hg_o = _hgrn(hg.reshape(b, s, n_hg), lower_bound.reshape(1, -1), hg_norm_g.reshape(1, -1))
    mb_o = _moba(mb.reshape(b, s, n_mb), rel_bias)
    w_r = jnp.zeros((d, LANES), F32).at[:, :N_GROUPS].set(w_group).at[:, N_GROUPS:N_GROUPS + N_EXPERTS].set(w_expert)
    b_r = jnp.zeros((1, LANES), F32).at[0, :N_GROUPS].set(b_group).at[0, N_GROUPS:N_GROUPS + N_EXPERTS].set(b_expert)
    tm_merge = min(tm_merge, n)
    x1, route, cnt_run = _merge(hg_o.reshape(n, -1), mb_o.reshape(n, -1), gates, x2,
                                w_proj_a.astype(BF16), w_proj_b.astype(BF16), w_out.astype(BF16),
                                ln1_g.reshape(1, d), ln1_b.reshape(1, d), jnp.stack(_split2(w_r)), b_r,
                                tm_merge, _moe_tile(n))
    out = _moe(x1, route, cnt_run, w_gate_up, w_down, ln2_g.reshape(1, d), ln2_b.reshape(1, d))
    return out.reshape(b, s, d)


def kernel(x, w_in, b_in, lb_logits, hg_norm_g, rel_bias, w_proj_a, w_proj_b, w_out, ln1_g, ln1_b, w_group,
           b_group, w_expert, b_expert, w_gate_up, w_down, ln2_g, ln2_b):
    lower_bounds = jnp.cumsum(jax.nn.softmax(lb_logits.astype(F32), axis=0), axis=0)
    l = 0
    return _block(x, w_in[l], b_in[l], lower_bounds[l], hg_norm_g[l], rel_bias, w_proj_a[l], w_proj_b[l],
                  w_out[l], ln1_g[l], ln1_b[l], w_group[l], b_group[l], w_expert[l], b_expert[l],
                  w_gate_up[l], w_down[l], ln2_g[l], ln2_b[l], tm_proj=512, tm_merge=1024)
```
